```python
import jax, jax.numpy as jnp
from jax import lax
import numpy as np

D_MODEL = 1024
BATCH = 4
SEQ = 8192
DEPTH = 4

N_A = DEPTH // 2
N_B = DEPTH - N_A
RET_HEADS = 4
RET_QK_DIM = D_MODEL // RET_HEADS
RET_V_DIM = 2 * RET_QK_DIM
RET_PROJ = 2 * RET_HEADS * RET_QK_DIM + 2 * RET_HEADS * RET_V_DIM
RET_CHUNK = 128
GN_EPS = 1e-6
MLA_HEADS = 8
QK_NOPE = 128
QK_ROPE = 64
V_HEAD = 128
Q_LORA = 256
KV_LORA = 256
ATTN_BLOCK = 128
RMS_EPS = 1e-6
ROPE_THETA = 10000.0
MAX_POS_OFFSET = 4096
N_EXPERTS = 32
TOP_K = 4
D_FF = D_MODEL
SWIGLU_LIMIT = 7.0
SWIGLU_ALPHA = 1.702
MOE_BLOCK = 128
PLE_DIM = 256
DN_ALPHA = (2 * DEPTH) ** 0.25
DN_BETA = (8 * DEPTH) ** -0.25
LN_EPS = 1e-5

kernel_name = 'yoco_retnet_mla_moe_deepnorm_ple'


def layer_norm(x, g, b):
    xf = x.astype(jnp.float32)
    mu = xf.mean(-1, keepdims=True)
    var = jnp.square(xf - mu).mean(-1, keepdims=True)
    return ((xf - mu) * lax.rsqrt(var + LN_EPS) * g + b).astype(x.dtype)


def rms_norm(x, g):
    xf = x.astype(jnp.float32)
    return (xf * lax.rsqrt(jnp.square(xf).mean(-1, keepdims=True) + RMS_EPS) * g).astype(x.dtype)


def rope_tables(positions, dim):
    inv = ROPE_THETA ** (-jnp.arange(0, dim, 2, dtype=jnp.float32) / dim)
    ang = positions.astype(jnp.float32)[..., None] * inv
    return jnp.cos(ang), jnp.sin(ang)


def apply_rope(x, cos, sin):
    x1, x2 = jnp.split(x, 2, axis=-1)
    c = cos[:, :, None, :].astype(x.dtype)
    s = sin[:, :, None, :].astype(x.dtype)
    return jnp.concatenate([x1 * c - x2 * s, x1 * s + x2 * c], axis=-1)


def chunkwise_retention(q, k, v):
    B, S, H, dk = q.shape
    dv = v.shape[-1]
    C = RET_CHUNK
    N = S // C
    log_g = jnp.log(1.0 - 2.0 ** (-5.0 - jnp.arange(H, dtype=jnp.float32)))
    idx = jnp.arange(C, dtype=jnp.float32)
    diff = idx[:, None] - idx[None, :]
    decay_intra = jnp.where(diff >= 0, jnp.exp(log_g[:, None, None] * jnp.maximum(diff, 0.0)), 0.0)
    decay_q = jnp.exp(log_g[:, None] * (idx + 1.0))[None, :, :, None]
    decay_k = jnp.exp(log_g[:, None] * (C - 1.0 - idx))[None, :, :, None]
    decay_chunk = jnp.exp(log_g * C)[None, :, None, None]

    def to_chunks(t):
        return t.astype(jnp.float32).reshape(B, N, C, H, t.shape[-1]).transpose(1, 0, 3, 2, 4)

    qc, kc, vc = to_chunks(q), to_chunks(k), to_chunks(v)

    def step(state, inp):
        qi, ki, vi = inp
        inner = jnp.einsum('bhqd,bhkd->bhqk', qi, ki) * decay_intra
        y = jnp.einsum('bhqk,bhkv->bhqv', inner, vi)
        y = y + jnp.einsum('bhqd,bhdv->bhqv', qi, state) * decay_q
        state = state * decay_chunk + jnp.einsum('bhkd,bhkv->bhdv', ki * decay_k, vi)
        return state, y

    state0 = jnp.zeros((B, H, dk, dv), jnp.float32)
    _, ys = lax.scan(step, state0, (qc, kc, vc))
    return ys.transpose(1, 0, 3, 2, 4).reshape(B, S, H, dv)


def retention_mixer(x, cos, sin, w_in, gn_g, gn_b, w_out):
    B, S, _ = x.shape
    H, dk, dv = RET_HEADS, RET_QK_DIM, RET_V_DIM
    q, k, v, g = jnp.split(x @ w_in, [H * dk, 2 * H * dk, 2 * H * dk + H * dv], axis=-1)
    q = apply_rope(q.reshape(B, S, H, dk), cos, sin)
    k = apply_rope(k.reshape(B, S, H, dk), cos, sin) * (dk ** -0.5)
    y = chunkwise_retention(q, k, v.reshape(B, S, H, dv))
    mu = y.mean(-1, keepdims=True)
    var = jnp.square(y - mu).mean(-1, keepdims=True)
    y = ((y - mu) * lax.rsqrt(var + GN_EPS)).reshape(B, S, H * dv) * gn_g + gn_b
    y = y.astype(x.dtype)
    return (jax.nn.silu(g) * y) @ w_out


def mla_shared_kv(h, cos, sin, w_kv_a, kv_norm_g, w_kv_b):
    B, S, _ = h.shape
    c_kv, k_rope = jnp.split(h @ w_kv_a, [KV_LORA], axis=-1)
    c_kv = rms_norm(c_kv, kv_norm_g)
    kv = (c_kv @ w_kv_b).reshape(B, S, MLA_HEADS, QK_NOPE + V_HEAD)
    k_nope, v = jnp.split(kv, [QK_NOPE], axis=-1)
    k_rope = apply_rope(k_rope[:, :, None, :], cos, sin)[:, :, 0]
    return k_nope, k_rope, v


def mla_mixer(x, cos, sin, k_nope, k_rope, v, w_q_a, q_norm_g, w_q_b, w_o):
    B, S, _ = x.shape
    H = MLA_HEADS
    q = (rms_norm(x @ w_q_a, q_norm_g) @ w_q_b).reshape(B, S, H, QK_NOPE + QK_ROPE)
    q_nope, q_rope = jnp.split(q, [QK_NOPE], axis=-1)
    q_rope = apply_rope(q_rope, cos, sin)
    scale = (QK_NOPE + QK_ROPE) ** -0.5
    nblk = S // ATTN_BLOCK
    qn_b = q_nope.reshape(B, nblk, ATTN_BLOCK, H, QK_NOPE).transpose(1, 0, 2, 3, 4)
    qr_b = q_rope.reshape(B, nblk, ATTN_BLOCK, H, QK_ROPE).transpose(1, 0, 2, 3, 4)
    key_pos = jnp.arange(S)

    def block(args):
        i, qn, qr = args
        s = jnp.einsum('bqhd,bkhd->bhqk', qn, k_nope) + jnp.einsum('bqhr,bkr->bhqk', qr, k_rope)
        s = s.astype(jnp.float32) * scale
        q_pos = i * ATTN_BLOCK + jnp.arange(ATTN_BLOCK)
        s = jnp.where(key_pos[None, :] <= q_pos[:, None], s, -jnp.inf)
        pr = jax.nn.softmax(s, axis=-1).astype(v.dtype)
        return jnp.einsum('bhqk,bkhv->bqhv', pr, v)

    o = lax.map(block, (jnp.arange(nblk), qn_b, qr_b))
    o = o.transpose(1, 0, 2, 3, 4).reshape(B, S, H * V_HEAD)
    return o @ w_o


def moe_ffn(x, w_router, b_router, w_gate_up, b_gate_up, w_down, b_down):
    B, S, D = x.shape
    T = B * S
    xf = x.reshape(T, D)
    logits = (xf @ w_router + b_router).astype(jnp.float32)
    top_vals, top_idx = lax.top_k(logits, TOP_K)
    gates = jax.nn.softmax(top_vals, axis=-1).astype(x.dtype)
    n_assign = T * TOP_K
    e_flat = top_idx.reshape(-1).astype(jnp.int32)
    tok_flat = jnp.repeat(jnp.arange(T, dtype=jnp.int32), TOP_K)
    g_flat = gates.reshape(-1)
    order = jnp.argsort(e_flat)
    e_sorted, tok_sorted, g_sorted = e_flat[order], tok_flat[order], g_flat[order]
    counts = jnp.zeros((N_EXPERTS,), jnp.int32).at[e_flat].add(1)
    padded = (counts + MOE_BLOCK - 1) // MOE_BLOCK * MOE_BLOCK
    start = jnp.cumsum(counts) - counts
    pend = jnp.cumsum(padded)
    pstart = pend - padded
    dest = pstart[e_sorted] + jnp.arange(n_assign, dtype=jnp.int32) - start[e_sorted]
    n_slots = n_assign + N_EXPERTS * MOE_BLOCK
    n_blocks = n_slots // MOE_BLOCK
    slot_tok = jnp.zeros((n_slots,), jnp.int32).at[dest].set(tok_sorted)
    slot_gate = jnp.zeros((n_slots,), x.dtype).at[dest].set(g_sorted)
    block_start = jnp.arange(n_blocks, dtype=jnp.int32) * MOE_BLOCK
    block_expert = jnp.minimum(jnp.searchsorted(pend, block_start, side='right'), N_EXPERTS - 1)

    def expert_block(args):
        e, toks, gts = args
        hgu = xf[toks] @ w_gate_up[e] + b_gate_up[e]
        gate, up = jnp.split(hgu, 2, axis=-1)
        gate = jnp.minimum(gate, SWIGLU_LIMIT)
        up = jnp.clip(up, -SWIGLU_LIMIT, SWIGLU_LIMIT)
        act = gate * jax.nn.sigmoid(SWIGLU_ALPHA * gate) * (up + 1.0)
        return (act @ w_down[e] + b_down[e]) * gts[:, None]

    y_slots = lax.map(expert_block, (block_expert, slot_tok.reshape(n_blocks, MOE_BLOCK),
                                     slot_gate.reshape(n_blocks, MOE_BLOCK)))
    y = jnp.zeros((T, D), x.dtype).at[slot_tok].add(y_slots.reshape(n_slots, D))
    return y.reshape(B, S, D)


def setup_inputs(seed: int = 0) -> dict:
    key = jax.random.key(seed)
    ks = jax.random.split(key, 32)
    f32 = jnp.float32

    def nrm(k, shape, scale):
        return jax.random.normal(k, shape, f32) * scale

    def gain(k, shape):
        return 1.0 + 0.02 * jax.random.normal(k, shape, f32)

    H, dk, dv = RET_HEADS, RET_QK_DIM, RET_V_DIM
    x = nrm(ks[0], (BATCH, SEQ, D_MODEL), 1.0)
    p = nrm(ks[1], (DEPTH, BATCH, SEQ, PLE_DIM), 1.0)
    positions = (jax.random.randint(ks[2], (BATCH, 1), 0, MAX_POS_OFFSET, jnp.int32)
                 + jnp.arange(SEQ, dtype=jnp.int32)[None, :])
    ret_col = jnp.concatenate([jnp.ones((2 * H * dk,), f32), jnp.full((H * dv,), DN_BETA, f32),
                               jnp.ones((H * dv,), f32)])
    ret_w_in = nrm(ks[3], (N_A, D_MODEL, RET_PROJ), D_MODEL ** -0.5) * ret_col
    ret_gn_g = gain(ks[4], (N_A, H * dv))
    ret_gn_b = nrm(ks[5], (N_A, H * dv), 0.02)
    ret_w_out = nrm(ks[6], (N_A, H * dv, D_MODEL), (H * dv) ** -0.5 * DN_BETA)
    mla_w_kv_a = nrm(ks[7], (D_MODEL, KV_LORA + QK_ROPE), D_MODEL ** -0.5)
    mla_kv_norm_g = gain(ks[8], (KV_LORA,))
    kv_col = jnp.concatenate([jnp.ones((QK_NOPE,), f32), jnp.full((V_HEAD,), DN_BETA, f32)])
    mla_w_kv_b = (nrm(ks[9], (KV_LORA, MLA_HEADS, QK_NOPE + V_HEAD), KV_LORA ** -0.5) * kv_col
                  ).reshape(KV_LORA, MLA_HEADS * (QK_NOPE + V_HEAD))
    mla_w_q_a = nrm(ks[10], (N_B, D_MODEL, Q_LORA), D_MODEL ** -0.5)
    mla_q_norm_g = gain(ks[11], (N_B, Q_LORA))
    mla_w_q_b = nrm(ks[12], (N_B, Q_LORA, MLA_HEADS * (QK_NOPE + QK_ROPE)), Q_LORA ** -0.5)
    mla_w_o = nrm(ks[13], (N_B, MLA_HEADS * V_HEAD, D_MODEL), (MLA_HEADS * V_HEAD) ** -0.5 * DN_BETA)
    ln_mix_g = gain(ks[14], (DEPTH, D_MODEL))
    ln_mix_b = nrm(ks[15], (DEPTH, D_MODEL), 0.02)
    ln_ffn_g = gain(ks[16], (DEPTH, D_MODEL))
    ln_ffn_b = nrm(ks[17], (DEPTH, D_MODEL), 0.02)
    moe_w_router = nrm(ks[18], (DEPTH, D_MODEL, N_EXPERTS), D_MODEL ** -0.5)
    moe_b_router = nrm(ks[19], (DEPTH, N_EXPERTS), 0.01)
    moe_w_gate_up = nrm(ks[20], (DEPTH, N_EXPERTS, D_MODEL, 2 * D_FF), D_MODEL ** -0.5)
    moe_b_gate_up = nrm(ks[21], (DEPTH, N_EXPERTS, 2 * D_FF), 0.02)
    moe_w_down = nrm(ks[22], (DEPTH, N_EXPERTS, D_FF, D_MODEL), D_FF ** -0.5 * DN_BETA)
    moe_b_down = nrm(ks[23], (DEPTH, N_EXPERTS, D_MODEL), 0.02)
    ple_w_gate = nrm(ks[24], (DEPTH, D_MODEL, D_MODEL), D_MODEL ** -0.5)
    ple_w_proj = nrm(ks[25], (DEPTH, PLE_DIM, D_MODEL), PLE_DIM ** -0.5)
    return {'x': x, 'p': p, 'positions': positions,
            'ret_w_in': ret_w_in, 'ret_gn_g': ret_gn_g, 'ret_gn_b': ret_gn_b, 'ret_w_out': ret_w_out,
            'mla_w_kv_a': mla_w_kv_a, 'mla_kv_norm_g': mla_kv_norm_g, 'mla_w_kv_b': mla_w_kv_b,
            'mla_w_q_a': mla_w_q_a, 'mla_q_norm_g': mla_q_norm_g, 'mla_w_q_b': mla_w_q_b, 'mla_w_o': mla_w_o,
            'ln_mix_g': ln_mix_g, 'ln_mix_b': ln_mix_b, 'ln_ffn_g': ln_ffn_g, 'ln_ffn_b': ln_ffn_b,
            'moe_w_router': moe_w_router, 'moe_b_router': moe_b_router,
            'moe_w_gate_up': moe_w_gate_up, 'moe_b_gate_up': moe_b_gate_up,
            'moe_w_down': moe_w_down, 'moe_b_down': moe_b_down,
            'ple_w_gate': ple_w_gate, 'ple_w_proj': ple_w_proj}


def reference(x, p, positions, ret_w_in, ret_gn_g, ret_gn_b, ret_w_out,
              mla_w_kv_a, mla_kv_norm_g, mla_w_kv_b, mla_w_q_a, mla_q_norm_g, mla_w_q_b, mla_w_o,
              ln_mix_g, ln_mix_b, ln_ffn_g, ln_ffn_b, moe_w_router, moe_b_router,
              moe_w_gate_up, moe_b_gate_up, moe_w_down, moe_b_down, ple_w_gate, ple_w_proj):
    cos_r, sin_r = rope_tables(positions, RET_QK_DIM)
    cos_m, sin_m = rope_tables(positions, QK_ROPE)
    h = x
    shared_kv = None
    for i in range(DEPTH):
        if i < N_A:
            mix = retention_mixer(h, cos_r, sin_r, ret_w_in[i], ret_gn_g[i], ret_gn_b[i], ret_w_out[i])
        else:
            if i == N_A:
                shared_kv = mla_shared_kv(h, cos_m, sin_m, mla_w_kv_a, mla_kv_norm_g, mla_w_kv_b)
            j = i - N_A
            mix = mla_mixer(h, cos_m, sin_m, shared_kv[0], shared_kv[1], shared_kv[2],
                            mla_w_q_a[j], mla_q_norm_g[j], mla_w_q_b[j], mla_w_o[j])
        h = layer_norm(DN_ALPHA * h + mix, ln_mix_g[i], ln_mix_b[i])
        ffn = moe_ffn(h, moe_w_router[i], moe_b_router[i], moe_w_gate_up[i], moe_b_gate_up[i],
                      moe_w_down[i], moe_b_down[i])
        h = layer_norm(DN_ALPHA * h + ffn, ln_ffn_g[i], ln_ffn_b[i])
        h = h + jax.nn.sigmoid(h @ ple_w_gate[i]) * (p[i] @ ple_w_proj[i])
    return h
```

```python
import functools

import jax
import jax.numpy as jnp
from jax import lax
from jax.experimental import pallas as pl
from jax.experimental.pallas import tpu as pltpu

D_MODEL = 1024
DEPTH = 4
N_A = DEPTH // 2
RET_HEADS = 4
RET_QK_DIM = D_MODEL // RET_HEADS
RET_V_DIM = 2 * RET_QK_DIM
GN_EPS = 1e-6
MLA_HEADS = 8
QK_NOPE = 128
QK_ROPE = 64
V_HEAD = 128
Q_LORA = 256
KV_LORA = 256
RMS_EPS = 1e-6
ROPE_THETA = 10000.0
N_EXPERTS = 32
TOP_K = 4
D_FF = D_MODEL
SWIGLU_LIMIT = 7.0
SWIGLU_ALPHA = 1.702
PLE_DIM = 256
DN_ALPHA = (2 * DEPTH) ** 0.25
LN_EPS = 1e-5

LANES = 128
VMEM_LIMIT_BYTES = 56 * 1024 * 1024

RET_CHUNK = 256
RET_BLOCK = 1024
ATTN_BLOCK = 512
MOE_BLOCK = 512
DISPATCH_TOKENS = 1024

F32 = jnp.float32
BF16 = jnp.bfloat16
NT_DIMS = (((1,), (1,)), ((), ()))
TN_DIMS = (((0,), (0,)), ((), ()))


def _tile(n, pref):
    t = min(n, pref)
    assert n % t == 0, (n, pref)
    return t


def _params(*sem):
    return pltpu.CompilerParams(dimension_semantics=sem, vmem_limit_bytes=VMEM_LIMIT_BYTES)


def _layer_norm_rows(u, g, b):
    mu = jnp.mean(u, axis=-1, keepdims=True)
    d = u - mu
    var = jnp.mean(d * d, axis=-1, keepdims=True)
    return d * lax.rsqrt(var + LN_EPS) * g + b


def _rope_table_kernel(pos_ref, inv_r_ref, inv_m_ref, sign_m_ref, cr_ref, sr_ref, cm_ref, sm_ref):
    pos = pos_ref[...].astype(F32)
    ang_r = pos * inv_r_ref[...]
    cr_ref[...] = jnp.cos(ang_r)
    sr_ref[...] = jnp.sin(ang_r)
    ang_m = pos * inv_m_ref[...]
    cm_ref[...] = jnp.cos(ang_m)
    sm_ref[...] = jnp.sin(ang_m) * sign_m_ref[...]


def rope_tables(positions):
    T = positions.size
    tm = _tile(T, 1024)
    pos = positions.reshape(T, 1)
    half_r = RET_QK_DIM // 2
    inv_r = ROPE_THETA ** (-jnp.arange(0, RET_QK_DIM, 2, dtype=F32) / RET_QK_DIM)
    inv_m = ROPE_THETA ** (-jnp.arange(0, QK_ROPE, 2, dtype=F32) / QK_ROPE)
    half_m = QK_ROPE // 2
    lane = jnp.arange(LANES)
    inv_m = inv_m[lane % half_m]
    sign_m = jnp.where((lane % QK_ROPE) < half_m, -1.0, 1.0).astype(F32)
    assert half_r == LANES
    row = lambda v: v.reshape(1, LANES)
    vec_spec = pl.BlockSpec((1, LANES), lambda i: (0, 0))
    tab_spec = pl.BlockSpec((tm, LANES), lambda i: (i, 0))
    out = jax.ShapeDtypeStruct((T, LANES), F32)
    return pl.pallas_call(
        _rope_table_kernel,
        grid=(T // tm,),
        in_specs=[pl.BlockSpec((tm, 1), lambda i: (i, 0)), vec_spec, vec_spec, vec_spec],
        out_specs=[tab_spec] * 4,
        out_shape=[out] * 4,
        compiler_params=_params("arbitrary"),
        name="rope_tables",
    )(pos, row(inv_r), row(inv_m), row(sign_m))


def _ret_qk_kernel(x_ref, w_ref, cos_ref, sin_ref, o_ref):
    acc = jnp.dot(x_ref[...], w_ref[...], preferred_element_type=F32)
    scale = jnp.where(pl.program_id(0) == 1, RET_QK_DIM ** -0.5, 1.0).astype(F32)
    c = cos_ref[...] * scale
    s = sin_ref[...] * scale
    half = RET_QK_DIM // 2
    for h in range(RET_HEADS):
        lo = h * RET_QK_DIM
        x1 = acc[:, lo:lo + half]
        x2 = acc[:, lo + half:lo + 2 * half]
        o_ref[:, lo:lo + half] = (x1 * c - x2 * s).astype(o_ref.dtype)
        o_ref[:, lo + half:lo + 2 * half] = (x1 * s + x2 * c).astype(o_ref.dtype)


def _matmul_kernel(x_ref, w_ref, o_ref):
    o_ref[...] = jnp.dot(x_ref[...], w_ref[...], preferred_element_type=F32).astype(o_ref.dtype)


def ret_projections(hb, w_in_bf, cos_r, sin_r):
    T = hb.shape[0]
    tm = _tile(T, 1024)
    tn = D_MODEL
    n_qk = 2 * RET_HEADS * RET_QK_DIM // tn
    n_vg = 2 * RET_HEADS * RET_V_DIM // tn
    x_spec = pl.BlockSpec((tm, D_MODEL), lambda j, i: (i, 0))
    tab_spec = pl.BlockSpec((tm, LANES), lambda j, i: (i, 0))
    qk = pl.pallas_call(
        _ret_qk_kernel,
        grid=(n_qk, T // tm),
        in_specs=[x_spec, pl.BlockSpec((D_MODEL, tn), lambda j, i: (0, j)), tab_spec, tab_spec],
        out_specs=pl.BlockSpec((tm, tn), lambda j, i: (i, j)),
        out_shape=jax.ShapeDtypeStruct((T, n_qk * tn), BF16),
        compiler_params=_params("arbitrary", "arbitrary"),
        name="ret_qk_proj",
    )(hb, w_in_bf, cos_r, sin_r)
    vg = pl.pallas_call(
        _matmul_kernel,
        grid=(n_vg, T // tm),
        in_specs=[x_spec, pl.BlockSpec((D_MODEL, tn), lambda j, i: (0, j + n_qk))],
        out_specs=pl.BlockSpec((tm, tn), lambda j, i: (i, j)),
        out_shape=jax.ShapeDtypeStruct((T, n_vg * tn), BF16),
        compiler_params=_params("arbitrary", "arbitrary"),
        name="ret_vg_proj",
    )(hb, w_in_bf)
    return qk, vg


def _retention_kernel(q_ref, k_ref, v_ref, g_ref, dm_ref, dq_ref, dk_ref, dc_ref, gng_ref, gnb_ref,
                      z_ref, state_ref, *, chunk, n_chunks):
    @pl.when(pl.program_id(2) == 0)
    def _():
        state_ref[...] = jnp.zeros_like(state_ref)

    dm = dm_ref[0]
    dq = dq_ref[0]
    dk = dk_ref[0]
    dc = dc_ref[0]
    gng = gng_ref[...]
    gnb = gnb_ref[...]
    for c in range(n_chunks):
        rows = slice(c * chunk, (c + 1) * chunk)
        q = q_ref[rows, :]
        k = k_ref[rows, :]
        v = v_ref[rows, :]
        st = state_ref[...]
        inner = lax.dot_general(q, k, NT_DIMS, preferred_element_type=F32) * dm
        y = jnp.dot(inner.astype(BF16), v, preferred_element_type=F32)
        y = y + jnp.dot(q, st.astype(BF16), preferred_element_type=F32) * dq
        kd = (k.astype(F32) * dk).astype(BF16)
        state_ref[...] = st * dc + lax.dot_general(kd, v, TN_DIMS, preferred_element_type=F32)
        mu = jnp.mean(y, axis=-1, keepdims=True)
        d = y - mu
        var = jnp.mean(d * d, axis=-1, keepdims=True)
        yn = d * lax.rsqrt(var + GN_EPS) * gng + gnb
        g = g_ref[rows, :].astype(F32)
        z_ref[rows, :] = (g * jax.nn.sigmoid(g) * yn).astype(z_ref.dtype)


def retention(qk, vg, gn_g, gn_b, B, S):
    T = B * S
    H, dk, dv = RET_HEADS, RET_QK_DIM, RET_V_DIM
    C = _tile(S, RET_CHUNK)
    L = _tile(S, RET_BLOCK)
    nl = S // L
    log_g = jnp.log(1.0 - 2.0 ** (-5.0 - jnp.arange(H, dtype=F32)))
    idx = jnp.arange(C, dtype=F32)
    diff = idx[:, None] - idx[None, :]
    dm = jnp.where(diff >= 0, jnp.exp(log_g[:, None, None] * jnp.maximum(diff, 0.0)), 0.0)
    dq = jnp.exp(log_g[:, None] * (idx + 1.0))[:, :, None]
    dkk = jnp.exp(log_g[:, None] * (C - 1.0 - idx))[:, :, None]
    dc = jnp.broadcast_to(jnp.exp(log_g * C)[:, None, None], (H, 1, dv))
    kern = functools.partial(_retention_kernel, chunk=C, n_chunks=L // C)
    return pl.pallas_call(
        kern,
        grid=(B, H, nl),
        in_specs=[
            pl.BlockSpec((L, dk), lambda b, h, n: (b * nl + n, h)),
            pl.BlockSpec((L, dk), lambda b, h, n: (b * nl + n, H + h)),
            pl.BlockSpec((L, dv), lambda b, h, n: (b * nl + n, h)),
            pl.BlockSpec((L, dv), lambda b, h, n: (b * nl + n, H + h)),
            pl.BlockSpec((1, C, C), lambda b, h, n: (h, 0, 0)),
            pl.BlockSpec((1, C, 1), lambda b, h, n: (h, 0, 0)),
            pl.BlockSpec((1, C, 1), lambda b, h, n: (h, 0, 0)),
            pl.BlockSpec((1, 1, dv), lambda b, h, n: (h, 0, 0)),
            pl.BlockSpec((1, dv), lambda b, h, n: (0, h)),
            pl.BlockSpec((1, dv), lambda b, h, n: (0, h)),
        ],
        out_specs=pl.BlockSpec((L, dv), lambda b, h, n: (b * nl + n, h)),
        out_shape=jax.ShapeDtypeStruct((T, H * dv), BF16),
        scratch_shapes=[pltpu.VMEM((dk, dv), F32)],
        compiler_params=_params("arbitrary", "arbitrary", "arbitrary"),
        name="retention",
    )(qk, qk, vg, vg, dm, dq, dkk, dc, gn_g.reshape(1, H * dv), gn_b.reshape(1, H * dv))


def _out_ln_kernel(z_ref, w_ref, h_ref, g_ref, b_ref, o_ref):
    mix = jnp.dot(z_ref[...], w_ref[...], preferred_element_type=F32)
    u = DN_ALPHA * h_ref[...] + mix
    o_ref[...] = _layer_norm_rows(u, g_ref[...], b_ref[...])


def out_proj_ln(z, w_bf, h, ln_g, ln_b):
    T, K = z.shape
    tm = _tile(T, 512)
    row_spec = pl.BlockSpec((tm, D_MODEL), lambda i: (i, 0))
    vec_spec = pl.BlockSpec((1, D_MODEL), lambda i: (0, 0))
    return pl.pallas_call(
        _out_ln_kernel,
        grid=(T // tm,),
        in_specs=[pl.BlockSpec((tm, K), lambda i: (i, 0)), pl.BlockSpec((K, D_MODEL), lambda i: (0, 0)),
                  row_spec, vec_spec, vec_spec],
        out_specs=row_spec,
        out_shape=jax.ShapeDtypeStruct((T, D_MODEL), F32),
        compiler_params=_params("arbitrary"),
        name="out_proj_ln",
    )(z, w_bf, h, ln_g.reshape(1, D_MODEL), ln_b.reshape(1, D_MODEL))


def _router_kernel(h_ref, w_ref, b_ref, slab_ref, cnt_ref, carry_ref, *, tm):
    @pl.when(pl.program_id(0) == 0)
    def _():
        carry_ref[...] = jnp.zeros_like(carry_ref)

    logits = jnp.dot(h_ref[...], w_ref[...], preferred_element_type=F32,
                     precision=lax.Precision.HIGHEST) + b_ref[...]
    e_iota = lax.broadcasted_iota(jnp.int32, (tm, N_EXPERTS), 1).astype(F32)
    work = logits
    onehot = jnp.zeros((tm, N_EXPERTS), F32)
    top_vals, top_idx = [], []
    for _ in range(TOP_K):
        m = jnp.max(work, axis=-1, keepdims=True)
        idx = jnp.min(jnp.where(work == m, e_iota, float(N_EXPERTS)), axis=-1, keepdims=True)
        top_vals.append(m)
        top_idx.append(idx)
        hit = e_iota == idx
        work = jnp.where(hit, -jnp.inf, work)
        onehot = jnp.where(hit, 1.0, onehot)
    exps = [jnp.exp(v - top_vals[0]) for v in top_vals]
    denom = exps[0] + exps[1] + exps[2] + exps[3]
    gates = [e / denom for e in exps]

    r = lax.broadcasted_iota(jnp.int32, (tm, tm), 0)
    c = lax.broadcasted_iota(jnp.int32, (tm, tm), 1)
    tri = jnp.where(c < r, 1.0, 0.0).astype(BF16)
    before = jnp.dot(tri, onehot.astype(BF16), preferred_element_type=F32) + carry_ref[0:1, 0:N_EXPERTS]
    ranks = [jnp.sum(jnp.where(e_iota == idx, before, 0.0), axis=-1, keepdims=True) for idx in top_idx]

    lane = lax.broadcasted_iota(jnp.int32, (tm, LANES), 1)
    slab = jnp.zeros((tm, LANES), F32)
    for k in range(TOP_K):
        slab = jnp.where(lane == k, top_idx[k], slab)
        slab = jnp.where(lane == TOP_K + k, gates[k], slab)
        slab = jnp.where(lane == 2 * TOP_K + k, ranks[k], slab)
    slab_ref[...] = slab

    carry_ref[0:1, 0:N_EXPERTS] = carry_ref[0:1, 0:N_EXPERTS] + jnp.sum(onehot, axis=0, keepdims=True)
    cnt_ref[...] = carry_ref[...]


def router(h, w_router, b_router):
    T = h.shape[0]
    tm = _tile(T, 512)
    slab, cnt = pl.pallas_call(
        functools.partial(_router_kernel, tm=tm),
        grid=(T // tm,),
        in_specs=[pl.BlockSpec((tm, D_MODEL), lambda i: (i, 0)),
                  pl.BlockSpec((D_MODEL, N_EXPERTS), lambda i: (0, 0)),
                  pl.BlockSpec((1, N_EXPERTS), lambda i: (0, 0))],
        out_specs=[pl.BlockSpec((tm, LANES), lambda i: (i, 0)), pl.BlockSpec((8, LANES), lambda i: (0, 0))],
        out_shape=[jax.ShapeDtypeStruct((T, LANES), F32), jax.ShapeDtypeStruct((8, LANES), F32)],
        scratch_shapes=[pltpu.VMEM((8, LANES), F32)],
        compiler_params=_params("arbitrary"),
        name="router",
    )(h, w_router, b_router.reshape(1, N_EXPERTS))
    idx = slab[:, 0:TOP_K].astype(jnp.int32)
    gates = slab[:, TOP_K:2 * TOP_K]
    rank = slab[:, 2 * TOP_K:3 * TOP_K].astype(jnp.int32)
    counts = cnt[0, :N_EXPERTS].astype(jnp.int32)
    return idx, gates, rank, counts


def _dispatch_kernel(zfill_ref, dest_ref, h_ref, zeros_ref, xs_ref, sem, zsem, *, tt, bm):
    i = pl.program_id(0)

    @pl.when(i == 0)
    def _():
        for e in range(N_EXPERTS):
            @pl.when(zfill_ref[e] >= 0)
            def _():
                start = pl.multiple_of(zfill_ref[e], bm)
                cp = pltpu.make_async_copy(zeros_ref, xs_ref.at[pl.ds(start, bm)], zsem)
                cp.start()
                cp.wait()

    def issue(t, carry):
        for k in range(TOP_K):
            d = dest_ref[0, 0, t * TOP_K + k]
            pltpu.make_async_copy(h_ref.at[pl.ds(i * tt + t, 1)], xs_ref.at[pl.ds(d, 1)], sem).start()
        return carry

    lax.fori_loop(0, tt, issue, 0)

    def drain(t, carry):
        for k in range(TOP_K):
            pltpu.make_async_copy(h_ref.at[pl.ds(0, 1)], xs_ref.at[pl.ds(0, 1)], sem).wait()
        return carry

    lax.fori_loop(0, tt, drain, 0)


def dispatch_rows(h, dest, zfill, n_slots, bm):
    T = h.shape[0]
    tt = _tile(T, DISPATCH_TOKENS)
    dest3 = dest.reshape(T // tt, 1, tt * TOP_K)
    zeros = jnp.zeros((bm, D_MODEL), F32)
    return pl.pallas_call(
        functools.partial(_dispatch_kernel, tt=tt, bm=bm),
        grid_spec=pltpu.PrefetchScalarGridSpec(
            num_scalar_prefetch=1,
            grid=(T // tt,),
            in_specs=[pl.BlockSpec((1, 1, tt * TOP_K), lambda i, zf: (i, 0, 0), memory_space=pltpu.SMEM),
                      pl.BlockSpec(memory_space=pl.ANY),
                      pl.BlockSpec(memory_space=pl.ANY)],
            out_specs=pl.BlockSpec(memory_space=pl.ANY),
            scratch_shapes=[pltpu.SemaphoreType.DMA, pltpu.SemaphoreType.DMA],
        ),
        out_shape=jax.ShapeDtypeStruct((n_slots, D_MODEL), F32),
        compiler_params=_params("arbitrary"),
        name="moe_dispatch",
    )(zfill, dest3, h, zeros)


def _combine_kernel(dest_ref, ys_ref, yk_ref, sem, *, tt, n_tokens):
    i = pl.program_id(0)

    def issue(t, carry):
        for k in range(TOP_K):
            d = dest_ref[0, 0, t * TOP_K + k]
            pltpu.make_async_copy(ys_ref.at[pl.ds(d, 1)],
                                  yk_ref.at[pl.ds(k * n_tokens + i * tt + t, 1)], sem).start()
        return carry

    lax.fori_loop(0, tt, issue, 0)

    def drain(t, carry):
        for k in range(TOP_K):
            pltpu.make_async_copy(ys_ref.at[pl.ds(0, 1)], yk_ref.at[pl.ds(0, 1)], sem).wait()
        return carry

    lax.fori_loop(0, tt, drain, 0)


def combine_rows(ys, dest, T):
    tt = _tile(T, DISPATCH_TOKENS)
    dest3 = dest.reshape(T // tt, 1, tt * TOP_K)
    yk = pl.pallas_call(
        functools.partial(_combine_kernel, tt=tt, n_tokens=T),
        grid=(T // tt,),
        in_specs=[pl.BlockSpec((1, 1, tt * TOP_K), lambda i: (i, 0, 0), memory_space=pltpu.SMEM),
                  pl.BlockSpec(memory_space=pl.ANY)],
        out_specs=pl.BlockSpec(memory_space=pl.ANY),
        out_shape=jax.ShapeDtypeStruct((TOP_K * T, D_MODEL), F32),
        scratch_shapes=[pltpu.SemaphoreType.DMA],
        compiler_params=_params("arbitrary"),
        name="moe_combine",
    )(dest3, ys)
    return yk.reshape(TOP_K, T, D_MODEL)


def _moe_mlp_kernel(be_ref, na_ref, x_ref, wgu_ref, bgu_ref, wd_ref, bd_ref, y_ref, wgu_bf, wd_bf):
    i = pl.program_id(0)
    prev = be_ref[jnp.maximum(i - 1, 0)]
    active = i < na_ref[0]

    @pl.when(jnp.logical_and(active, jnp.logical_or(i == 0, be_ref[i] != prev)))
    def _():
        wgu_bf[...] = wgu_ref[0].astype(BF16)
        wd_bf[...] = wd_ref[0].astype(BF16)

    @pl.when(active)
    def _():
        x = x_ref[...].astype(BF16)
        hgu = jnp.dot(x, wgu_bf[...], preferred_element_type=F32) + bgu_ref[0]
        gate = jnp.minimum(hgu[:, :D_FF], SWIGLU_LIMIT)
        up = jnp.clip(hgu[:, D_FF:], -SWIGLU_LIMIT, SWIGLU_LIMIT)
        act = gate * jax.nn.sigmoid(SWIGLU_ALPHA * gate) * (up + 1.0)
        y_ref[...] = jnp.dot(act.astype(BF16), wd_bf[...], preferred_element_type=F32) + bd_ref[0]


def moe_mlp(xs, block_expert, n_active, w_gu, b_gu, w_d, b_d, bm):
    n_slots = xs.shape[0]
    n_blocks = n_slots // bm

    def row_map(i, be, na):
        return (jnp.minimum(i, na[0] - 1), 0)

    def exp_map(i, be, na):
        return (be[jnp.minimum(i, na[0] - 1)], 0, 0)

    return pl.pallas_call(
        _moe_mlp_kernel,
        grid_spec=pltpu.PrefetchScalarGridSpec(
            num_scalar_prefetch=2,
            grid=(n_blocks,),
            in_specs=[pl.BlockSpec((bm, D_MODEL), row_map),
                      pl.BlockSpec((1, D_MODEL, 2 * D_FF), exp_map),
                      pl.BlockSpec((1, 1, 2 * D_FF), exp_map),
                      pl.BlockSpec((1, D_FF, D_MODEL), exp_map),
                      pl.BlockSpec((1, 1, D_MODEL), exp_map)],
            out_specs=pl.BlockSpec((bm, D_MODEL), row_map),
            scratch_shapes=[pltpu.VMEM((D_MODEL, 2 * D_FF), BF16), pltpu.VMEM((D_FF, D_MODEL), BF16)],
        ),
        out_shape=jax.ShapeDtypeStruct((n_slots, D_MODEL), F32),
        compiler_params=_params("arbitrary"),
        name="moe_mlp",
    )(block_expert, n_active, xs, w_gu, b_gu.reshape(N_EXPERTS, 1, 2 * D_FF), w_d,
      b_d.reshape(N_EXPERTS, 1, D_MODEL))


def _ffn_ln_ple_kernel(yk_ref, gates_ref, h_ref, p_ref, g_ref, b_ref, wg_ref, wp_ref, o_ref, ob_ref):
    gates = gates_ref[...]
    ffn = yk_ref[0] * gates[:, 0:1]
    for k in range(1, TOP_K):
        ffn = ffn + yk_ref[k] * gates[:, k:k + 1]
    h2 = _layer_norm_rows(DN_ALPHA * h_ref[...] + ffn, g_ref[...], b_ref[...])
    gate = jax.nn.sigmoid(jnp.dot(h2.astype(BF16), wg_ref[...], preferred_element_type=F32))
    proj = jnp.dot(p_ref[...].astype(BF16), wp_ref[...], preferred_element_type=F32)
    out = h2 + gate * proj
    o_ref[...] = out
    ob_ref[...] = out.astype(BF16)


def ffn_ln_ple(yk, gates, h, p, ln_g, ln_b, wg_bf, wp_bf):
    T = h.shape[0]
    tm = _tile(T, 512)
    row_spec = pl.BlockSpec((tm, D_MODEL), lambda i: (i, 0))
    vec_spec = pl.BlockSpec((1, D_MODEL), lambda i: (0, 0))
    return pl.pallas_call(
        _ffn_ln_ple_kernel,
        grid=(T // tm,),
        in_specs=[pl.BlockSpec((TOP_K, tm, D_MODEL), lambda i: (0, i, 0)),
                  pl.BlockSpec((tm, TOP_K), lambda i: (i, 0)),
                  row_spec,
                  pl.BlockSpec((tm, PLE_DIM), lambda i: (i, 0)),
                  vec_spec, vec_spec,
                  pl.BlockSpec((D_MODEL, D_MODEL), lambda i: (0, 0)),
                  pl.BlockSpec((PLE_DIM, D_MODEL), lambda i: (0, 0))],
        out_specs=[row_spec, row_spec],
        out_shape=[jax.ShapeDtypeStruct((T, D_MODEL), F32), jax.ShapeDtypeStruct((T, D_MODEL), BF16)],
        compiler_params=_params("arbitrary"),
        name="ffn_ln_ple",
    )(yk, gates, h, p, ln_g.reshape(1, D_MODEL), ln_b.reshape(1, D_MODEL), wg_bf, wp_bf)


def moe_layer(h1, w_router, b_router, w_gu, b_gu, w_d, b_d):
    T = h1.shape[0]
    bm = MOE_BLOCK
    idx, gates, rank, counts = router(h1, w_router, b_router)
    padded = (counts + bm - 1) // bm * bm
    pend = jnp.cumsum(padded)
    pstart = pend - padded
    onehot = idx[..., None] == jnp.arange(N_EXPERTS, dtype=jnp.int32)
    dest = jnp.sum(jnp.where(onehot, pstart, 0), axis=-1) + rank
    n_slots = T * TOP_K + N_EXPERTS * bm
    n_blocks = n_slots // bm
    block_start = jnp.arange(n_blocks, dtype=jnp.int32) * bm
    block_expert = jnp.minimum(jnp.searchsorted(pend, block_start, side='right'),
                               N_EXPERTS - 1).astype(jnp.int32)
    n_active = (pend[-1:] // bm).astype(jnp.int32)
    zfill = jnp.where(padded > 0, pend - bm, -1).astype(jnp.int32)
    xs = dispatch_rows(h1, dest.astype(jnp.int32), zfill, n_slots, bm)
    ys = moe_mlp(xs, block_expert, n_active, w_gu, b_gu, w_d, b_d, bm)
    yk = combine_rows(ys, dest.astype(jnp.int32), T)
    return yk, gates


def _rms_rows(x, g):
    return x * lax.rsqrt(jnp.mean(x * x, axis=-1, keepdims=True) + RMS_EPS) * g


def _mla_kv_kernel(x_ref, wa_ref, g_ref, wkn_ref, wv_ref, cm_ref, sm_ref, k_ref, v_ref):
    a = jnp.dot(x_ref[...], wa_ref[...], preferred_element_type=F32)
    c_kv = _rms_rows(a[:, :KV_LORA], g_ref[...]).astype(BF16)
    kr = a[:, KV_LORA:KV_LORA + LANES] * cm_ref[...] + a[:, KV_LORA + LANES:] * sm_ref[...]
    kr = kr.astype(BF16)
    kn = jnp.dot(c_kv, wkn_ref[...], preferred_element_type=F32)
    v_ref[...] = jnp.dot(c_kv, wv_ref[...], preferred_element_type=F32).astype(BF16)
    for h in range(MLA_HEADS):
        k_ref[:, 2 * h * LANES:(2 * h + 1) * LANES] = kn[:, h * QK_NOPE:(h + 1) * QK_NOPE].astype(BF16)
        k_ref[:, (2 * h + 1) * LANES:(2 * h + 2) * LANES] = kr


def _pad_rope_cols(w, swap):
    K = w.shape[0]
    w = w.reshape(K, -1, QK_ROPE)
    if swap:
        w = jnp.concatenate([w[..., QK_ROPE // 2:], w[..., :QK_ROPE // 2]], axis=-1)
    w = jnp.concatenate([w, jnp.zeros_like(w)], axis=-1)
    return w.reshape(K, -1)


def mla_shared_kv(hb, w_kv_a, kv_norm_g, w_kv_b, cm, sm):
    T = hb.shape[0]
    tm = _tile(T, 512)
    w_r = w_kv_a[:, KV_LORA:]
    wa = jnp.concatenate([w_kv_a[:, :KV_LORA], _pad_rope_cols(w_r, False), _pad_rope_cols(w_r, True)],
                         axis=1).astype(BF16)
    wb = w_kv_b.reshape(KV_LORA, MLA_HEADS, QK_NOPE + V_HEAD)
    wkn = wb[:, :, :QK_NOPE].reshape(KV_LORA, MLA_HEADS * QK_NOPE).astype(BF16)
    wv = wb[:, :, QK_NOPE:].reshape(KV_LORA, MLA_HEADS * V_HEAD).astype(BF16)
    full = lambda shape: pl.BlockSpec(shape, lambda i: (0, 0))
    return pl.pallas_call(
        _mla_kv_kernel,
        grid=(T // tm,),
        in_specs=[pl.BlockSpec((tm, D_MODEL), lambda i: (i, 0)), full(wa.shape), full((1, KV_LORA)),
                  full(wkn.shape), full(wv.shape),
                  pl.BlockSpec((tm, LANES), lambda i: (i, 0)), pl.BlockSpec((tm, LANES), lambda i: (i, 0))],
        out_specs=[pl.BlockSpec((tm, MLA_HEADS * 2 * LANES), lambda i: (i, 0)),
                   pl.BlockSpec((tm, MLA_HEADS * V_HEAD), lambda i: (i, 0))],
        out_shape=[jax.ShapeDtypeStruct((T, MLA_HEADS * 2 * LANES), BF16),
                   jax.ShapeDtypeStruct((T, MLA_HEADS * V_HEAD), BF16)],
        compiler_params=_params("arbitrary"),
        name="mla_kv",
    )(hb, wa, kv_norm_g.reshape(1, KV_LORA), wkn, wv, cm, sm)


def _mla_q_kernel(x_ref, wa_ref, g_ref, wb_ref, cm_ref, sm_ref, q_ref):
    scale = (QK_NOPE + QK_ROPE) ** -0.5
    qa = jnp.dot(x_ref[...], wa_ref[...], preferred_element_type=F32)
    qc = _rms_rows(qa, g_ref[...]).astype(BF16)
    q = jnp.dot(qc, wb_ref[...], preferred_element_type=F32)
    cm = cm_ref[...] * scale
    sm = sm_ref[...] * scale
    n = MLA_HEADS * LANES
    for h in range(MLA_HEADS):
        lo = h * LANES
        q_ref[:, 2 * lo:2 * lo + LANES] = (q[:, lo:lo + LANES] * scale).astype(BF16)
        rope = q[:, n + lo:n + lo + LANES] * cm + q[:, 2 * n + lo:2 * n + lo + LANES] * sm
        q_ref[:, 2 * lo + LANES:2 * lo + 2 * LANES] = rope.astype(BF16)


def mla_q(hb, w_q_a, q_norm_g, w_q_b, cm, sm):
    T = hb.shape[0]
    tm = _tile(T, 512)
    wq = w_q_b.reshape(Q_LORA, MLA_HEADS, QK_NOPE + QK_ROPE)
    w_n = wq[:, :, :QK_NOPE].reshape(Q_LORA, MLA_HEADS * QK_NOPE)
    w_r = wq[:, :, QK_NOPE:].reshape(Q_LORA, MLA_HEADS * QK_ROPE)
    wb = jnp.concatenate([w_n, _pad_rope_cols(w_r, False), _pad_rope_cols(w_r, True)], axis=1).astype(BF16)
    full = lambda shape: pl.BlockSpec(shape, lambda i: (0, 0))
    return pl.pallas_call(
        _mla_q_kernel,
        grid=(T // tm,),
        in_specs=[pl.BlockSpec((tm, D_MODEL), lambda i: (i, 0)), full((D_MODEL, Q_LORA)), full((1, Q_LORA)),
                  full(wb.shape),
                  pl.BlockSpec((tm, LANES), lambda i: (i, 0)), pl.BlockSpec((tm, LANES), lambda i: (i, 0))],
        out_specs=pl.BlockSpec((tm, MLA_HEADS * 2 * LANES), lambda i: (i, 0)),
        out_shape=jax.ShapeDtypeStruct((T, MLA_HEADS * 2 * LANES), BF16),
        compiler_params=_params("arbitrary"),
        name="mla_q",
    )(hb, w_q_a.astype(BF16), q_norm_g.reshape(1, Q_LORA), wb, cm, sm)


def _flash_kernel(q_ref, k_ref, v_ref, o_ref, *, blk):
    i = pl.program_id(2)
    q = q_ref[...]

    def step(j, carry, masked):
        m, l, acc = carry
        start = pl.multiple_of(j * blk, blk)
        kj = k_ref[pl.ds(start, blk), :]
        vj = v_ref[pl.ds(start, blk), :]
        s = lax.dot_general(q, kj, NT_DIMS, preferred_element_type=F32)
        if masked:
            r = lax.broadcasted_iota(jnp.int32, (blk, blk), 0)
            c = lax.broadcasted_iota(jnp.int32, (blk, blk), 1)
            s = jnp.where(c <= r, s, -jnp.inf)
        m_new = jnp.maximum(m, jnp.max(s, axis=-1, keepdims=True))
        alpha = jnp.exp(m - m_new)
        p = jnp.exp(s - m_new)
        l = alpha * l + jnp.sum(p, axis=-1, keepdims=True)
        acc = alpha * acc + jnp.dot(p.astype(BF16), vj, preferred_element_type=F32)
        return m_new, l, acc

    init = (jnp.full((blk, 1), -jnp.inf, F32), jnp.zeros((blk, 1), F32), jnp.zeros((blk, V_HEAD), F32))
    carry = lax.fori_loop(0, i, lambda j, c: step(j, c, False), init)
    m, l, acc = step(i, carry, True)
    o_ref[...] = (acc / l).astype(o_ref.dtype)


def flash_attention(q, k, v, B, S):
    T = B * S
    blk = _tile(S, ATTN_BLOCK)
    nq = S // blk
    return pl.pallas_call(
        functools.partial(_flash_kernel, blk=blk),
        grid=(B, MLA_HEADS, nq),
        in_specs=[pl.BlockSpec((blk, 2 * LANES), lambda b, h, i: (b * nq + i, h)),
                  pl.BlockSpec((S, 2 * LANES), lambda b, h, i: (b, h)),
                  pl.BlockSpec((S, V_HEAD), lambda b, h, i: (b, h))],
        out_specs=pl.BlockSpec((blk, V_HEAD), lambda b, h, i: (b * nq + i, h)),
        out_shape=jax.ShapeDtypeStruct((T, MLA_HEADS * V_HEAD), BF16),
        compiler_params=_params("arbitrary", "arbitrary", "arbitrary"),
        name="mla_flash",
    )(q, k, v)


def kernel(x, p, positions, ret_w_in, ret_gn_g, ret_gn_b, ret_w_out, mla_w_kv_a, mla_kv_norm_g, mla_w_kv_b,
           mla_w_q_a, mla_q_norm_g, mla_w_q_b, mla_w_o, ln_mix_g, ln_mix_b, ln_ffn_g, ln_ffn_b,
           moe_w_router, moe_b_router, moe_w_gate_up, moe_b_gate_up, moe_w_down, moe_b_down,
           ple_w_gate, ple_w_proj):
    B, S, D = x.shape
    T = B * S
    cos_r, sin_r, cos_m, sin_m = rope_tables(positions)
    h = x.reshape(T, D)
    hb = h.astype(BF16)
    kv = None
    for i in range(DEPTH):
        if i < N_A:
            qk, vg = ret_projections(hb, ret_w_in[i].astype(BF16), cos_r, sin_r)
            z = retention(qk, vg, ret_gn_g[i], ret_gn_b[i], B, S)
            w_out = ret_w_out[i].astype(BF16)
        else:
            if kv is None:
                kv = mla_shared_kv(hb, mla_w_kv_a, mla_kv_norm_g, mla_w_kv_b, cos_m, sin_m)
            j = i - N_A
            q = mla_q(hb, mla_w_q_a[j], mla_q_norm_g[j], mla_w_q_b[j], cos_m, sin_m)
            z = flash_attention(q, kv[0], kv[1], B, S)
            w_out = mla_w_o[j].astype(BF16)
        h1 = out_proj_ln(z, w_out, h, ln_mix_g[i], ln_mix_b[i])
        yk, gates = moe_layer(h1, moe_w_router[i], moe_b_router[i], moe_w_gate_up[i], moe_b_gate_up[i],
                              moe_w_down[i], moe_b_down[i])
        h, hb = ffn_ln_ple(yk, gates, h1, p[i].reshape(T, PLE_DIM), ln_ffn_g[i], ln_ffn_b[i],
                           ple_w_gate[i].astype(BF16), ple_w_proj[i].astype(BF16))
    return h.reshape(B, S, D)
```

```python
import functools

import jax
import jax.numpy as jnp
from jax import lax
from jax.experimental import pallas as pl
from jax.experimental.pallas import tpu as pltpu

D_MODEL = 1024
DEPTH = 4
N_A = DEPTH // 2
RET_HEADS = 4
RET_QK_DIM = D_MODEL // RET_HEADS
RET_V_DIM = 2 * RET_QK_DIM
GN_EPS = 1e-6
MLA_HEADS = 8
QK_NOPE = 128
QK_ROPE = 64
V_HEAD = 128
Q_LORA = 256
KV_LORA = 256
RMS_EPS = 1e-6
ROPE_THETA = 10000.0
N_EXPERTS = 32
TOP_K = 4
D_FF = D_MODEL
SWIGLU_LIMIT = 7.0
SWIGLU_ALPHA = 1.702
PLE_DIM = 256
DN_ALPHA = (2 * DEPTH) ** 0.25
LN_EPS = 1e-5

LANES = 128
VMEM_LIMIT_BYTES = 56 * 1024 * 1024

RET_CHUNK = 256
RET_BLOCK = 1024
ATTN_BLOCK = 512
MOE_BLOCK = 512
DISPATCH_TOKENS = 1024

F32 = jnp.float32
BF16 = jnp.bfloat16
NT_DIMS = (((1,), (1,)), ((), ()))
TN_DIMS = (((0,), (0,)), ((), ()))


def _tile(n, pref):
    t = min(n, pref)
    assert n % t == 0, (n, pref)
    return t


def _params(*sem):
    return pltpu.CompilerParams(dimension_semantics=sem, vmem_limit_bytes=VMEM_LIMIT_BYTES)


def _layer_norm_rows(u, g, b):
    mu = jnp.mean(u, axis=-1, keepdims=True)
    d = u - mu
    var = jnp.mean(d * d, axis=-1, keepdims=True)
    return d * lax.rsqrt(var + LN_EPS) * g + b


def _rope_table_kernel(pos_ref, inv_r_ref, inv_m_ref, sign_m_ref, cr_ref, sr_ref, cm_ref, sm_ref):
    pos = pos_ref[...].astype(F32)
    ang_r = pos * inv_r_ref[...]
    cr_ref[...] = jnp.cos(ang_r)
    sr_ref[...] = jnp.sin(ang_r)
    ang_m = pos * inv_m_ref[...]
    cm_ref[...] = jnp.cos(ang_m)
    sm_ref[...] = jnp.sin(ang_m) * sign_m_ref[...]


def rope_tables(positions):
    T = positions.size
    tm = _tile(T, 1024)
    pos = positions.reshape(T, 1)
    half_r = RET_QK_DIM // 2
    inv_r = ROPE_THETA ** (-jnp.arange(0, RET_QK_DIM, 2, dtype=F32) / RET_QK_DIM)
    inv_m = ROPE_THETA ** (-jnp.arange(0, QK_ROPE, 2, dtype=F32) / QK_ROPE)
    half_m = QK_ROPE // 2
    lane = jnp.arange(LANES)
    inv_m = inv_m[lane % half_m]
    sign_m = jnp.where((lane % QK_ROPE) < half_m, -1.0, 1.0).astype(F32)
    assert half_r == LANES
    row = lambda v: v.reshape(1, LANES)
    vec_spec = pl.BlockSpec((1, LANES), lambda i: (0, 0))
    tab_spec = pl.BlockSpec((tm, LANES), lambda i: (i, 0))
    out = jax.ShapeDtypeStruct((T, LANES), F32)
    return pl.pallas_call(
        _rope_table_kernel,
        grid=(T // tm,),
        in_specs=[pl.BlockSpec((tm, 1), lambda i: (i, 0)), vec_spec, vec_spec, vec_spec],
        out_specs=[tab_spec] * 4,
        out_shape=[out] * 4,
        compiler_params=_params("arbitrary"),
        name="rope_tables",
    )(pos, row(inv_r), row(inv_m), row(sign_m))


def _ret_qk_kernel(x_ref, w_ref, cos_ref, sin_ref, o_ref):
    acc = jnp.dot(x_ref[...], w_ref[...], preferred_element_type=F32)
    scale = jnp.where(pl.program_id(0) == 1, RET_QK_DIM ** -0.5, 1.0).astype(F32)
    c = cos_ref[...] * scale
    s = sin_ref[...] * scale
    half = RET_QK_DIM // 2
    for h in range(RET_HEADS):
        lo = h * RET_QK_DIM
        x1 = acc[:, lo:lo + half]
        x2 = acc[:, lo + half:lo + 2 * half]
        o_ref[:, lo:lo + half] = (x1 * c - x2 * s).astype(o_ref.dtype)
        o_ref[:, lo + half:lo + 2 * half] = (x1 * s + x2 * c).astype(o_ref.dtype)


def _matmul_kernel(x_ref, w_ref, o_ref):
    o_ref[...] = jnp.dot(x_ref[...], w_ref[...], preferred_element_type=F32).astype(o_ref.dtype)


def ret_projections(hb, w_in_bf, cos_r, sin_r):
    T = hb.shape[0]
    tm = _tile(T, 1024)
    tn = D_MODEL
    n_qk = 2 * RET_HEADS * RET_QK_DIM // tn
    n_vg = 2 * RET_HEADS * RET_V_DIM // tn
    x_spec = pl.BlockSpec((tm, D_MODEL), lambda j, i: (i, 0))
    tab_spec = pl.BlockSpec((tm, LANES), lambda j, i: (i, 0))
    qk = pl.pallas_call(
        _ret_qk_kernel,
        grid=(n_qk, T // tm),
        in_specs=[x_spec, pl.BlockSpec((D_MODEL, tn), lambda j, i: (0, j)), tab_spec, tab_spec],
        out_specs=pl.BlockSpec((tm, tn), lambda j, i: (i, j)),
        out_shape=jax.ShapeDtypeStruct((T, n_qk * tn), BF16),
        compiler_params=_params("arbitrary", "arbitrary"),
        name="ret_qk_proj",
    )(hb, w_in_bf, cos_r, sin_r)
    vg = pl.pallas_call(
        _matmul_kernel,
        grid=(n_vg, T // tm),
        in_specs=[x_spec, pl.BlockSpec((D_MODEL, tn), lambda j, i: (0, j + n_qk))],
        out_specs=pl.BlockSpec((tm, tn), lambda j, i: (i, j)),
        out_shape=jax.ShapeDtypeStruct((T, n_vg * tn), BF16),
        compiler_params=_params("arbitrary", "arbitrary"),
        name="ret_vg_proj",
    )(hb, w_in_bf)
    return qk, vg


def _retention_kernel(q_ref, k_ref, v_ref, g_ref, dm_ref, dq_ref, dk_ref, dc_ref, gng_ref, gnb_ref,
                      z_ref, state_ref, *, chunk, n_chunks):
    @pl.when(pl.program_id(2) == 0)
    def _():
        state_ref[...] = jnp.zeros_like(state_ref)

    dm = dm_ref[0]
    dq = dq_ref[0]
    dk = dk_ref[0]
    dc = dc_ref[0]
    gng = gng_ref[...]
    gnb = gnb_ref[...]
    for c in range(n_chunks):
        rows = slice(c * chunk, (c + 1) * chunk)
        q = q_ref[rows, :]
        k = k_ref[rows, :]
        v = v_ref[rows, :]
        st = state_ref[...]
        inner = lax.dot_general(q, k, NT_DIMS, preferred_element_type=F32) * dm
        y = jnp.dot(inner.astype(BF16), v, preferred_element_type=F32)
        y = y + jnp.dot(q, st.astype(BF16), preferred_element_type=F32) * dq
        kd = (k.astype(F32) * dk).astype(BF16)
        state_ref[...] = st * dc + lax.dot_general(kd, v, TN_DIMS, preferred_element_type=F32)
        mu = jnp.mean(y, axis=-1, keepdims=True)
        d = y - mu
        var = jnp.mean(d * d, axis=-1, keepdims=True)
        yn = d * lax.rsqrt(var + GN_EPS) * gng + gnb
        g = g_ref[rows, :].astype(F32)
        z_ref[rows, :] = (g * jax.nn.sigmoid(g) * yn).astype(z_ref.dtype)


def retention(qk, vg, gn_g, gn_b, B, S):
    T = B * S
    H, dk, dv = RET_HEADS, RET_QK_DIM, RET_V_DIM
    C = _tile(S, RET_CHUNK)
    L = _tile(S, RET_BLOCK)
    nl = S // L
    log_g = jnp.log(1.0 - 2.0 ** (-5.0 - jnp.arange(H, dtype=F32)))
    idx = jnp.arange(C, dtype=F32)
    diff = idx[:, None] - idx[None, :]
    dm = jnp.where(diff >= 0, jnp.exp(log_g[:, None, None] * jnp.maximum(diff, 0.0)), 0.0)
    dq = jnp.exp(log_g[:, None] * (idx + 1.0))[:, :, None]
    dkk = jnp.exp(log_g[:, None] * (C - 1.0 - idx))[:, :, None]
    dc = jnp.broadcast_to(jnp.exp(log_g * C)[:, None, None], (H, 1, dv))
    kern = functools.partial(_retention_kernel, chunk=C, n_chunks=L // C)
    return pl.pallas_call(
        kern,
        grid=(B, H, nl),
        in_specs=[
            pl.BlockSpec((L, dk), lambda b, h, n: (b * nl + n, h)),
            pl.BlockSpec((L, dk), lambda b, h, n: (b * nl + n, H + h)),
            pl.BlockSpec((L, dv), lambda b, h, n: (b * nl + n, h)),
            pl.BlockSpec((L, dv), lambda b, h, n: (b * nl + n, H + h)),
            pl.BlockSpec((1, C, C), lambda b, h, n: (h, 0, 0)),
            pl.BlockSpec((1, C, 1), lambda b, h, n: (h, 0, 0)),
            pl.BlockSpec((1, C, 1), lambda b, h, n: (h, 0, 0)),
            pl.BlockSpec((1, 1, dv), lambda b, h, n: (h, 0, 0)),
            pl.BlockSpec((1, dv), lambda b, h, n: (0, h)),
            pl.BlockSpec((1, dv), lambda b, h, n: (0, h)),
        ],
        out_specs=pl.BlockSpec((L, dv), lambda b, h, n: (b * nl + n, h)),
        out_shape=jax.ShapeDtypeStruct((T, H * dv), BF16),
        scratch_shapes=[pltpu.VMEM((dk, dv), F32)],
        compiler_params=_params("arbitrary", "arbitrary", "arbitrary"),
        name="retention",
    )(qk, qk, vg, vg, dm, dq, dkk, dc, gn_g.reshape(1, H * dv), gn_b.reshape(1, H * dv))


def _out_ln_kernel(z_ref, w_ref, h_ref, g_ref, b_ref, o_ref):
    mix = jnp.dot(z_ref[...], w_ref[...], preferred_element_type=F32)
    u = DN_ALPHA * h_ref[...] + mix
    o_ref[...] = _layer_norm_rows(u, g_ref[...], b_ref[...])


def out_proj_ln(z, w_bf, h, ln_g, ln_b):
    T, K = z.shape
    tm = _tile(T, 512)
    row_spec = pl.BlockSpec((tm, D_MODEL), lambda i: (i, 0))
    vec_spec = pl.BlockSpec((1, D_MODEL), lambda i: (0, 0))
    return pl.pallas_call(
        _out_ln_kernel,
        grid=(T // tm,),
        in_specs=[pl.BlockSpec((tm, K), lambda i: (i, 0)), pl.BlockSpec((K, D_MODEL), lambda i: (0, 0)),
                  row_spec, vec_spec, vec_spec],
        out_specs=row_spec,
        out_shape=jax.ShapeDtypeStruct((T, D_MODEL), F32),
        compiler_params=_params("arbitrary"),
        name="out_proj_ln",
    )(z, w_bf, h, ln_g.reshape(1, D_MODEL), ln_b.reshape(1, D_MODEL))


def _router_kernel(h_ref, w_ref, b_ref, slab_ref, cnt_ref, carry_ref, *, tm):
    @pl.when(pl.program_id(0) == 0)
    def _():
        carry_ref[...] = jnp.zeros_like(carry_ref)

    logits = jnp.dot(h_ref[...], w_ref[...], preferred_element_type=F32,
                     precision=lax.Precision.HIGHEST) + b_ref[...]
    e_iota = lax.broadcasted_iota(jnp.int32, (tm, N_EXPERTS), 1).astype(F32)
    work = logits
    onehot = jnp.zeros((tm, N_EXPERTS), F32)
    top_vals, top_idx = [], []
    for _ in range(TOP_K):
        m = jnp.max(work, axis=-1, keepdims=True)
        idx = jnp.min(jnp.where(work == m, e_iota, float(N_EXPERTS)), axis=-1, keepdims=True)
        top_vals.append(m)
        top_idx.append(idx)
        hit = e_iota == idx
        work = jnp.where(hit, -jnp.inf, work)
        onehot = jnp.where(hit, 1.0, onehot)
    exps = [jnp.exp(v - top_vals[0]) for v in top_vals]
    denom = exps[0] + exps[1] + exps[2] + exps[3]
    gates = [e / denom for e in exps]

    r = lax.broadcasted_iota(jnp.int32, (tm, tm), 0)
    c = lax.broadcasted_iota(jnp.int32, (tm, tm), 1)
    tri = jnp.where(c < r, 1.0, 0.0).astype(BF16)
    before = jnp.dot(tri, onehot.astype(BF16), preferred_element_type=F32) + carry_ref[0:1, 0:N_EXPERTS]
    ranks = [jnp.sum(jnp.where(e_iota == idx, before, 0.0), axis=-1, keepdims=True) for idx in top_idx]

    lane = lax.broadcasted_iota(jnp.int32, (tm, LANES), 1)
    slab = jnp.zeros((tm, LANES), F32)
    for k in range(TOP_K):
        slab = jnp.where(lane == k, top_idx[k], slab)
        slab = jnp.where(lane == TOP_K + k, gates[k], slab)
        slab = jnp.where(lane == 2 * TOP_K + k, ranks[k], slab)
    slab_ref[...] = slab

    carry_ref[0:1, 0:N_EXPERTS] = carry_ref[0:1, 0:N_EXPERTS] + jnp.sum(onehot, axis=0, keepdims=True)
    cnt_ref[...] = carry_ref[...]


def router(h, w_router, b_router):
    T = h.shape[0]
    tm = _tile(T, 512)
    slab, cnt = pl.pallas_call(
        functools.partial(_router_kernel, tm=tm),
        grid=(T // tm,),
        in_specs=[pl.BlockSpec((tm, D_MODEL), lambda i: (i, 0)),
                  pl.BlockSpec((D_MODEL, N_EXPERTS), lambda i: (0, 0)),
                  pl.BlockSpec((1, N_EXPERTS), lambda i: (0, 0))],
        out_specs=[pl.BlockSpec((tm, LANES), lambda i: (i, 0)), pl.BlockSpec((8, LANES), lambda i: (0, 0))],
        out_shape=[jax.ShapeDtypeStruct((T, LANES), F32), jax.ShapeDtypeStruct((8, LANES), F32)],
        scratch_shapes=[pltpu.VMEM((8, LANES), F32)],
        compiler_params=_params("arbitrary"),
        name="router",
    )(h, w_router, b_router.reshape(1, N_EXPERTS))
    idx = slab[:, 0:TOP_K].astype(jnp.int32)
    gates = slab[:, TOP_K:2 * TOP_K]
    rank = slab[:, 2 * TOP_K:3 * TOP_K].astype(jnp.int32)
    counts = cnt[0, :N_EXPERTS].astype(jnp.int32)
    return idx, gates, rank, counts


def _dispatch_kernel(zfill_ref, dest_ref, h_ref, zeros_ref, xs_ref, sem, zsem, *, tt, bm):
    i = pl.program_id(0)

    @pl.when(i == 0)
    def _():
        for e in range(N_EXPERTS):
            @pl.when(zfill_ref[e] >= 0)
            def _():
                start = pl.multiple_of(zfill_ref[e], bm)
                cp = pltpu.make_async_copy(zeros_ref, xs_ref.at[pl.ds(start, bm)], zsem)
                cp.start()
                cp.wait()

    def issue(t, carry):
        for k in range(TOP_K):
            d = dest_ref[0, 0, t * TOP_K + k]
            pltpu.make_async_copy(h_ref.at[pl.ds(t, 1)], xs_ref.at[pl.ds(d, 1)], sem).start()
        return carry

    lax.fori_loop(0, tt, issue, 0)

    def drain(t, carry):
        for k in range(TOP_K):
            pltpu.make_async_copy(h_ref.at[pl.ds(0, 1)], xs_ref.at[pl.ds(0, 1)], sem).wait()
        return carry

    lax.fori_loop(0, tt, drain, 0)


def dispatch_rows(h, dest, zfill, n_slots, bm):
    T = h.shape[0]
    tt = _tile(T, DISPATCH_TOKENS)
    dest3 = dest.reshape(T // tt, 1, tt * TOP_K)
    zeros = jnp.zeros((bm, D_MODEL), F32)
    return pl.pallas_call(
        functools.partial(_dispatch_kernel, tt=tt, bm=bm),
        grid_spec=pltpu.PrefetchScalarGridSpec(
            num_scalar_prefetch=1,
            grid=(T // tt,),
            in_specs=[pl.BlockSpec((1, 1, tt * TOP_K), lambda i, zf: (i, 0, 0), memory_space=pltpu.SMEM),
                      pl.BlockSpec((tt, D_MODEL), lambda i, zf: (i, 0)),
                      pl.BlockSpec(memory_space=pl.ANY)],
            out_specs=pl.BlockSpec(memory_space=pl.ANY),
            scratch_shapes=[pltpu.SemaphoreType.DMA, pltpu.SemaphoreType.DMA],
        ),
        out_shape=jax.ShapeDtypeStruct((n_slots, D_MODEL), F32),
        compiler_params=_params("arbitrary"),
        name="moe_dispatch",
    )(zfill, dest3, h, zeros)


def _moe_mlp_kernel(be_ref, na_ref, x_ref, wgu_ref, bgu_ref, wd_ref, bd_ref, y_ref, wgu_bf, wd_bf):
    i = pl.program_id(0)
    prev = be_ref[jnp.maximum(i - 1, 0)]
    active = i < na_ref[0]

    @pl.when(jnp.logical_and(active, jnp.logical_or(i == 0, be_ref[i] != prev)))
    def _():
        wgu_bf[...] = wgu_ref[0].astype(BF16)
        wd_bf[...] = wd_ref[0].astype(BF16)

    @pl.when(active)
    def _():
        x = x_ref[...].astype(BF16)
        hgu = jnp.dot(x, wgu_bf[...], preferred_element_type=F32) + bgu_ref[0]
        gate = jnp.minimum(hgu[:, :D_FF], SWIGLU_LIMIT)
        up = jnp.clip(hgu[:, D_FF:], -SWIGLU_LIMIT, SWIGLU_LIMIT)
        act = gate * jax.nn.sigmoid(SWIGLU_ALPHA * gate) * (up + 1.0)
        y_ref[...] = jnp.dot(act.astype(BF16), wd_bf[...], preferred_element_type=F32) + bd_ref[0]


def moe_mlp(xs, block_expert, n_active, w_gu, b_gu, w_d, b_d, bm):
    n_slots = xs.shape[0]
    n_blocks = n_slots // bm

    def row_map(i, be, na):
        return (jnp.minimum(i, na[0] - 1), 0)

    def exp_map(i, be, na):
        return (be[jnp.minimum(i, na[0] - 1)], 0, 0)

    return pl.pallas_call(
        _moe_mlp_kernel,
        grid_spec=pltpu.PrefetchScalarGridSpec(
            num_scalar_prefetch=2,
            grid=(n_blocks,),
            in_specs=[pl.BlockSpec((bm, D_MODEL), row_map),
                      pl.BlockSpec((1, D_MODEL, 2 * D_FF), exp_map),
                      pl.BlockSpec((1, 1, 2 * D_FF), exp_map),
                      pl.BlockSpec((1, D_FF, D_MODEL), exp_map),
                      pl.BlockSpec((1, 1, D_MODEL), exp_map)],
            out_specs=pl.BlockSpec((bm, D_MODEL), row_map),
            scratch_shapes=[pltpu.VMEM((D_MODEL, 2 * D_FF), BF16), pltpu.VMEM((D_FF, D_MODEL), BF16)],
        ),
        out_shape=jax.ShapeDtypeStruct((n_slots, D_MODEL), F32),
        compiler_params=_params("arbitrary"),
        name="moe_mlp",
    )(block_expert, n_active, xs, w_gu, b_gu.reshape(N_EXPERTS, 1, 2 * D_FF), w_d,
      b_d.reshape(N_EXPERTS, 1, D_MODEL))


def _ffn_ln_ple_kernel(dcur_ref, dnext_ref, gates_ref, h_ref, p_ref, g_ref, b_ref, wg_ref, wp_ref, ys_ref,
                       o_ref, ob_ref, ybuf, sems, *, tm):
    i = pl.program_id(0)
    slot = i % 2

    def row_copy(dref, s, t, k):
        d = dref[0, 0, t * TOP_K + k]
        return pltpu.make_async_copy(ys_ref.at[pl.ds(d, 1)], ybuf.at[s, k, pl.ds(t, 1)], sems.at[s])

    def issue(dref, s):
        def body(t, carry):
            for k in range(TOP_K):
                row_copy(dref, s, t, k).start()
            return carry
        lax.fori_loop(0, tm, body, 0)

    @pl.when(i == 0)
    def _():
        issue(dcur_ref, 0)

    @pl.when(i + 1 < pl.num_programs(0))
    def _():
        issue(dnext_ref, 1 - slot)

    def drain(t, carry):
        for k in range(TOP_K):
            row_copy(dcur_ref, slot, t, k).wait()
        return carry
    lax.fori_loop(0, tm, drain, 0)

    gates = gates_ref[...]
    ffn = ybuf[slot, 0] * gates[:, 0:1]
    for k in range(1, TOP_K):
        ffn = ffn + ybuf[slot, k] * gates[:, k:k + 1]
    h2 = _layer_norm_rows(DN_ALPHA * h_ref[...] + ffn, g_ref[...], b_ref[...])
    gate = jax.nn.sigmoid(jnp.dot(h2.astype(BF16), wg_ref[...], preferred_element_type=F32))
    proj = jnp.dot(p_ref[...].astype(BF16), wp_ref[...], preferred_element_type=F32)
    out = h2 + gate * proj
    o_ref[...] = out
    ob_ref[...] = out.astype(BF16)


def ffn_ln_ple(ys, dest, gates, h, p, ln_g, ln_b, wg_bf, wp_bf):
    T = h.shape[0]
    tm = _tile(T, 512)
    n = T // tm
    dest3 = dest.reshape(n, 1, tm * TOP_K)
    row_spec = pl.BlockSpec((tm, D_MODEL), lambda i: (i, 0))
    vec_spec = pl.BlockSpec((1, D_MODEL), lambda i: (0, 0))
    return pl.pallas_call(
        functools.partial(_ffn_ln_ple_kernel, tm=tm),
        grid=(n,),
        in_specs=[pl.BlockSpec((1, 1, tm * TOP_K), lambda i: (i, 0, 0), memory_space=pltpu.SMEM),
                  pl.BlockSpec((1, 1, tm * TOP_K), lambda i: (jnp.minimum(i + 1, n - 1), 0, 0),
                               memory_space=pltpu.SMEM),
                  pl.BlockSpec((tm, TOP_K), lambda i: (i, 0)),
                  row_spec,
                  pl.BlockSpec((tm, PLE_DIM), lambda i: (i, 0)),
                  vec_spec, vec_spec,
                  pl.BlockSpec((D_MODEL, D_MODEL), lambda i: (0, 0)),
                  pl.BlockSpec((PLE_DIM, D_MODEL), lambda i: (0, 0)),
                  pl.BlockSpec(memory_space=pl.ANY)],
        out_specs=[row_spec, row_spec],
        out_shape=[jax.ShapeDtypeStruct((T, D_MODEL), F32), jax.ShapeDtypeStruct((T, D_MODEL), BF16)],
        scratch_shapes=[pltpu.VMEM((2, TOP_K, tm, D_MODEL), F32), pltpu.SemaphoreType.DMA((2,))],
        compiler_params=_params("arbitrary"),
        name="ffn_ln_ple",
    )(dest3, dest3, gates, h, p, ln_g.reshape(1, D_MODEL), ln_b.reshape(1, D_MODEL), wg_bf, wp_bf, ys)


def moe_layer(h1, w_router, b_router, w_gu, b_gu, w_d, b_d):
    T = h1.shape[0]
    bm = MOE_BLOCK
    idx, gates, rank, counts = router(h1, w_router, b_router)
    padded = (counts + bm - 1) // bm * bm
    pend = jnp.cumsum(padded)
    pstart = pend - padded
    onehot = idx[..., None] == jnp.arange(N_EXPERTS, dtype=jnp.int32)
    dest = (jnp.sum(jnp.where(onehot, pstart, 0), axis=-1) + rank).astype(jnp.int32)
    n_slots = T * TOP_K + N_EXPERTS * bm
    n_blocks = n_slots // bm
    block_start = jnp.arange(n_blocks, dtype=jnp.int32) * bm
    block_expert = jnp.minimum(jnp.searchsorted(pend, block_start, side='right'),
                               N_EXPERTS - 1).astype(jnp.int32)
    n_active = (pend[-1:] // bm).astype(jnp.int32)
    zfill = jnp.where(padded > 0, pend - bm, -1).astype(jnp.int32)
    xs = dispatch_rows(h1, dest, zfill, n_slots, bm)
    ys = moe_mlp(xs, block_expert, n_active, w_gu, b_gu, w_d, b_d, bm)
    return ys, dest, gates


def _rms_rows(x, g):
    return x * lax.rsqrt(jnp.mean(x * x, axis=-1, keepdims=True) + RMS_EPS) * g


def _mla_kv_kernel(x_ref, wa_ref, g_ref, wkn_ref, wv_ref, cm_ref, sm_ref, k_ref, v_ref):
    a = jnp.dot(x_ref[...], wa_ref[...], preferred_element_type=F32)
    c_kv = _rms_rows(a[:, :KV_LORA], g_ref[...]).astype(BF16)
    kr = a[:, KV_LORA:KV_LORA + LANES] * cm_ref[...] + a[:, KV_LORA + LANES:] * sm_ref[...]
    kr = kr.astype(BF16)
    kn = jnp.dot(c_kv, wkn_ref[...], preferred_element_type=F32)
    v_ref[...] = jnp.dot(c_kv, wv_ref[...], preferred_element_type=F32).astype(BF16)
    for h in range(MLA_HEADS):
        k_ref[:, 2 * h * LANES:(2 * h + 1) * LANES] = kn[:, h * QK_NOPE:(h + 1) * QK_NOPE].astype(BF16)
        k_ref[:, (2 * h + 1) * LANES:(2 * h + 2) * LANES] = kr


def _pad_rope_cols(w, swap):
    K = w.shape[0]
    w = w.reshape(K, -1, QK_ROPE)
    if swap:
        w = jnp.concatenate([w[..., QK_ROPE // 2:], w[..., :QK_ROPE // 2]], axis=-1)
    w = jnp.concatenate([w, jnp.zeros_like(w)], axis=-1)
    return w.reshape(K, -1)


def mla_shared_kv(hb, w_kv_a, kv_norm_g, w_kv_b, cm, sm):
    T = hb.shape[0]
    tm = _tile(T, 512)
    w_r = w_kv_a[:, KV_LORA:]
    wa = jnp.concatenate([w_kv_a[:, :KV_LORA], _pad_rope_cols(w_r, False), _pad_rope_cols(w_r, True)],
                         axis=1).astype(BF16)
    wb = w_kv_b.reshape(KV_LORA, MLA_HEADS, QK_NOPE + V_HEAD)
    wkn = wb[:, :, :QK_NOPE].reshape(KV_LORA, MLA_HEADS * QK_NOPE).astype(BF16)
    wv = wb[:, :, QK_NOPE:].reshape(KV_LORA, MLA_HEADS * V_HEAD).astype(BF16)
    full = lambda shape: pl.BlockSpec(shape, lambda i: (0, 0))
    return pl.pallas_call(
        _mla_kv_kernel,
        grid=(T // tm,),
        in_specs=[pl.BlockSpec((tm, D_MODEL), lambda i: (i, 0)), full(wa.shape), full((1, KV_LORA)),
                  full(wkn.shape), full(wv.shape),
                  pl.BlockSpec((tm, LANES), lambda i: (i, 0)), pl.BlockSpec((tm, LANES), lambda i: (i, 0))],
        out_specs=[pl.BlockSpec((tm, MLA_HEADS * 2 * LANES), lambda i: (i, 0)),
                   pl.BlockSpec((tm, MLA_HEADS * V_HEAD), lambda i: (i, 0))],
        out_shape=[jax.ShapeDtypeStruct((T, MLA_HEADS * 2 * LANES), BF16),
                   jax.ShapeDtypeStruct((T, MLA_HEADS * V_HEAD), BF16)],
        compiler_params=_params("arbitrary"),
        name="mla_kv",
    )(hb, wa, kv_norm_g.reshape(1, KV_LORA), wkn, wv, cm, sm)


def _mla_q_kernel(x_ref, wa_ref, g_ref, wb_ref, cm_ref, sm_ref, q_ref):
    scale = (QK_NOPE + QK_ROPE) ** -0.5
    qa = jnp.dot(x_ref[...], wa_ref[...], preferred_element_type=F32)
    qc = _rms_rows(qa, g_ref[...]).astype(BF16)
    q = jnp.dot(qc, wb_ref[...], preferred_element_type=F32)
    cm = cm_ref[...] * scale
    sm = sm_ref[...] * scale
    n = MLA_HEADS * LANES
    for h in range(MLA_HEADS):
        lo = h * LANES
        q_ref[:, 2 * lo:2 * lo + LANES] = (q[:, lo:lo + LANES] * scale).astype(BF16)
        rope = q[:, n + lo:n + lo + LANES] * cm + q[:, 2 * n + lo:2 * n + lo + LANES] * sm
        q_ref[:, 2 * lo + LANES:2 * lo + 2 * LANES] = rope.astype(BF16)


def mla_q(hb, w_q_a, q_norm_g, w_q_b, cm, sm):
    T = hb.shape[0]
    tm = _tile(T, 512)
    wq = w_q_b.reshape(Q_LORA, MLA_HEADS, QK_NOPE + QK_ROPE)
    w_n = wq[:, :, :QK_NOPE].reshape(Q_LORA, MLA_HEADS * QK_NOPE)
    w_r = wq[:, :, QK_NOPE:].reshape(Q_LORA, MLA_HEADS * QK_ROPE)
    wb = jnp.concatenate([w_n, _pad_rope_cols(w_r, False), _pad_rope_cols(w_r, True)], axis=1).astype(BF16)
    full = lambda shape: pl.BlockSpec(shape, lambda i: (0, 0))
    return pl.pallas_call(
        _mla_q_kernel,
        grid=(T // tm,),
        in_specs=[pl.BlockSpec((tm, D_MODEL), lambda i: (i, 0)), full((D_MODEL, Q_LORA)), full((1, Q_LORA)),
                  full(wb.shape),
                  pl.BlockSpec((tm, LANES), lambda i: (i, 0)), pl.BlockSpec((tm, LANES), lambda i: (i, 0))],
        out_specs=pl.BlockSpec((tm, MLA_HEADS * 2 * LANES), lambda i: (i, 0)),
        out_shape=jax.ShapeDtypeStruct((T, MLA_HEADS * 2 * LANES), BF16),
        compiler_params=_params("arbitrary"),
        name="mla_q",
    )(hb, w_q_a.astype(BF16), q_norm_g.reshape(1, Q_LORA), wb, cm, sm)


def _flash_kernel(q_ref, k_ref, v_ref, o_ref, *, blk):
    i = pl.program_id(2)
    q = q_ref[...]

    def step(j, carry, masked):
        m, l, acc = carry
        start = pl.multiple_of(j * blk, blk)
        kj = k_ref[pl.ds(start, blk), :]
        vj = v_ref[pl.ds(start, blk), :]
        s = lax.dot_general(q, kj, NT_DIMS, preferred_element_type=F32)
        if masked:
            r = lax.broadcasted_iota(jnp.int32, (blk, blk), 0)
            c = lax.broadcasted_iota(jnp.int32, (blk, blk), 1)
            s = jnp.where(c <= r, s, -jnp.inf)
        m_new = jnp.maximum(m, jnp.max(s, axis=-1, keepdims=True))
        alpha = jnp.exp(m - m_new)
        p = jnp.exp(s - m_new)
        l = alpha * l + jnp.sum(p, axis=-1, keepdims=True)
        acc = alpha * acc + jnp.dot(p.astype(BF16), vj, preferred_element_type=F32)
        return m_new, l, acc

    init = (jnp.full((blk, 1), -jnp.inf, F32), jnp.zeros((blk, 1), F32), jnp.zeros((blk, V_HEAD), F32))
    carry = lax.fori_loop(0, i, lambda j, c: step(j, c, False), init)
    m, l, acc = step(i, carry, True)
    o_ref[...] = (acc / l).astype(o_ref.dtype)


def flash_attention(q, k, v, B, S):
    T = B * S
    blk = _tile(S, ATTN_BLOCK)
    nq = S // blk
    return pl.pallas_call(
        functools.partial(_flash_kernel, blk=blk),
        grid=(B, MLA_HEADS, nq),
        in_specs=[pl.BlockSpec((blk, 2 * LANES), lambda b, h, i: (b * nq + i, h)),
                  pl.BlockSpec((S, 2 * LANES), lambda b, h, i: (b, h)),
                  pl.BlockSpec((S, V_HEAD), lambda b, h, i: (b, h))],
        out_specs=pl.BlockSpec((blk, V_HEAD), lambda b, h, i: (b * nq + i, h)),
        out_shape=jax.ShapeDtypeStruct((T, MLA_HEADS * V_HEAD), BF16),
        compiler_params=_params("arbitrary", "arbitrary", "arbitrary"),
        name="mla_flash",
    )(q, k, v)


def kernel(x, p, positions, ret_w_in, ret_gn_g, ret_gn_b, ret_w_out, mla_w_kv_a, mla_kv_norm_g, mla_w_kv_b,
           mla_w_q_a, mla_q_norm_g, mla_w_q_b, mla_w_o, ln_mix_g, ln_mix_b, ln_ffn_g, ln_ffn_b,
           moe_w_router, moe_b_router, moe_w_gate_up, moe_b_gate_up, moe_w_down, moe_b_down,
           ple_w_gate, ple_w_proj):
    B, S, D = x.shape
    T = B * S
    cos_r, sin_r, cos_m, sin_m = rope_tables(positions)
    h = x.reshape(T, D)
    hb = h.astype(BF16)
    kv = None
    for i in range(DEPTH):
        if i < N_A:
            qk, vg = ret_projections(hb, ret_w_in[i].astype(BF16), cos_r, sin_r)
            z = retention(qk, vg, ret_gn_g[i], ret_gn_b[i], B, S)
            w_out = ret_w_out[i].astype(BF16)
        else:
            if kv is None:
                kv = mla_shared_kv(hb, mla_w_kv_a, mla_kv_norm_g, mla_w_kv_b, cos_m, sin_m)
            j = i - N_A
            q = mla_q(hb, mla_w_q_a[j], mla_q_norm_g[j], mla_w_q_b[j], cos_m, sin_m)
            z = flash_attention(q, kv[0], kv[1], B, S)
            w_out = mla_w_o[j].astype(BF16)
        h1 = out_proj_ln(z, w_out, h, ln_mix_g[i], ln_mix_b[i])
        ys, dest, gates = moe_layer(h1, moe_w_router[i], moe_b_router[i], moe_w_gate_up[i], moe_b_gate_up[i],
                                    moe_w_down[i], moe_b_down[i])
        h, hb = ffn_ln_ple(ys, dest, gates, h1, p[i].reshape(T, PLE_DIM), ln_ffn_g[i], ln_ffn_b[i],
                           ple_w_gate[i].astype(BF16), ple_w_proj[i].astype(BF16))
    return h.reshape(B, S, D)
```

```python
import functools

import jax
import jax.numpy as jnp
from jax import lax
from jax.experimental import pallas as pl
from jax.experimental.pallas import tpu as pltpu

D_MODEL = 1024
DEPTH = 4
N_A = DEPTH // 2
RET_HEADS = 4
RET_QK_DIM = D_MODEL // RET_HEADS
RET_V_DIM = 2 * RET_QK_DIM
GN_EPS = 1e-6
MLA_HEADS = 8
QK_NOPE = 128
QK_ROPE = 64
V_HEAD = 128
Q_LORA = 256
KV_LORA = 256
RMS_EPS = 1e-6
ROPE_THETA = 10000.0
N_EXPERTS = 32
TOP_K = 4
D_FF = D_MODEL
SWIGLU_LIMIT = 7.0
SWIGLU_ALPHA = 1.702
PLE_DIM = 256
DN_ALPHA = (2 * DEPTH) ** 0.25
LN_EPS = 1e-5

LANES = 128
ROW_CHUNKS = D_MODEL // LANES
VMEM_LIMIT_BYTES = 56 * 1024 * 1024

RET_CHUNK = 256
RET_BLOCK = 1024
ATTN_BLOCK = 512
MOE_BLOCK = 512
DISPATCH_TOKENS = 1024

F32 = jnp.float32
BF16 = jnp.bfloat16
NT_DIMS = (((1,), (1,)), ((), ()))
TN_DIMS = (((0,), (0,)), ((), ()))


def _tile(n, pref):
    t = min(n, pref)
    assert n % t == 0, (n, pref)
    return t


def _params(*sem):
    return pltpu.CompilerParams(dimension_semantics=sem, vmem_limit_bytes=VMEM_LIMIT_BYTES)


def _layer_norm_rows(u, g, b):
    mu = jnp.mean(u, axis=-1, keepdims=True)
    d = u - mu
    var = jnp.mean(d * d, axis=-1, keepdims=True)
    return d * lax.rsqrt(var + LN_EPS) * g + b


def _rope_table_kernel(pos_ref, inv_r_ref, inv_m_ref, sign_m_ref, cr_ref, sr_ref, cm_ref, sm_ref):
    pos = pos_ref[...].astype(F32)
    ang_r = pos * inv_r_ref[...]
    cr_ref[...] = jnp.cos(ang_r)
    sr_ref[...] = jnp.sin(ang_r)
    ang_m = pos * inv_m_ref[...]
    cm_ref[...] = jnp.cos(ang_m)
    sm_ref[...] = jnp.sin(ang_m) * sign_m_ref[...]


def rope_tables(positions):
    T = positions.size
    tm = _tile(T, 1024)
    pos = positions.reshape(T, 1)
    half_r = RET_QK_DIM // 2
    inv_r = ROPE_THETA ** (-jnp.arange(0, RET_QK_DIM, 2, dtype=F32) / RET_QK_DIM)
    inv_m = ROPE_THETA ** (-jnp.arange(0, QK_ROPE, 2, dtype=F32) / QK_ROPE)
    half_m = QK_ROPE // 2
    lane = jnp.arange(LANES)
    inv_m = inv_m[lane % half_m]
    sign_m = jnp.where((lane % QK_ROPE) < half_m, -1.0, 1.0).astype(F32)
    assert half_r == LANES
    row = lambda v: v.reshape(1, LANES)
    vec_spec = pl.BlockSpec((1, LANES), lambda i: (0, 0))
    tab_spec = pl.BlockSpec((tm, LANES), lambda i: (i, 0))
    out = jax.ShapeDtypeStruct((T, LANES), F32)
    return pl.pallas_call(
        _rope_table_kernel,
        grid=(T // tm,),
        in_specs=[pl.BlockSpec((tm, 1), lambda i: (i, 0)), vec_spec, vec_spec, vec_spec],
        out_specs=[tab_spec] * 4,
        out_shape=[out] * 4,
        compiler_params=_params("arbitrary"),
        name="rope_tables",
    )(pos, row(inv_r), row(inv_m), row(sign_m))


def _ret_qk_kernel(x_ref, w_ref, cos_ref, sin_ref, o_ref):
    acc = jnp.dot(x_ref[...], w_ref[...], preferred_element_type=F32)
    scale = jnp.where(pl.program_id(0) == 1, RET_QK_DIM ** -0.5, 1.0).astype(F32)
    c = cos_ref[...] * scale
    s = sin_ref[...] * scale
    half = RET_QK_DIM // 2
    for h in range(RET_HEADS):
        lo = h * RET_QK_DIM
        x1 = acc[:, lo:lo + half]
        x2 = acc[:, lo + half:lo + 2 * half]
        o_ref[:, lo:lo + half] = (x1 * c - x2 * s).astype(o_ref.dtype)
        o_ref[:, lo + half:lo + 2 * half] = (x1 * s + x2 * c).astype(o_ref.dtype)


def _matmul_kernel(x_ref, w_ref, o_ref):
    o_ref[...] = jnp.dot(x_ref[...], w_ref[...], preferred_element_type=F32).astype(o_ref.dtype)


def ret_projections(hb, w_in_bf, cos_r, sin_r):
    T = hb.shape[0]
    tm = _tile(T, 1024)
    tn = D_MODEL
    n_qk = 2 * RET_HEADS * RET_QK_DIM // tn
    n_vg = 2 * RET_HEADS * RET_V_DIM // tn
    x_spec = pl.BlockSpec((tm, D_MODEL), lambda j, i: (i, 0))
    tab_spec = pl.BlockSpec((tm, LANES), lambda j, i: (i, 0))
    qk = pl.pallas_call(
        _ret_qk_kernel,
        grid=(n_qk, T // tm),
        in_specs=[x_spec, pl.BlockSpec((D_MODEL, tn), lambda j, i: (0, j)), tab_spec, tab_spec],
        out_specs=pl.BlockSpec((tm, tn), lambda j, i: (i, j)),
        out_shape=jax.ShapeDtypeStruct((T, n_qk * tn), BF16),
        compiler_params=_params("arbitrary", "arbitrary"),
        name="ret_qk_proj",
    )(hb, w_in_bf, cos_r, sin_r)
    vg = pl.pallas_call(
        _matmul_kernel,
        grid=(n_vg, T // tm),
        in_specs=[x_spec, pl.BlockSpec((D_MODEL, tn), lambda j, i: (0, j + n_qk))],
        out_specs=pl.BlockSpec((tm, tn), lambda j, i: (i, j)),
        out_shape=jax.ShapeDtypeStruct((T, n_vg * tn), BF16),
        compiler_params=_params("arbitrary", "arbitrary"),
        name="ret_vg_proj",
    )(hb, w_in_bf)
    return qk, vg


def _retention_kernel(q_ref, k_ref, v_ref, g_ref, dm_ref, dq_ref, dk_ref, dc_ref, gng_ref, gnb_ref,
                      z_ref, state_ref, *, chunk, n_chunks):
    @pl.when(pl.program_id(2) == 0)
    def _():
        state_ref[...] = jnp.zeros_like(state_ref)

    dm = dm_ref[0]
    dq = dq_ref[0]
    dk = dk_ref[0]
    dc = dc_ref[0]
    gng = gng_ref[...]
    gnb = gnb_ref[...]
    for c in range(n_chunks):
        rows = slice(c * chunk, (c + 1) * chunk)
        q = q_ref[rows, :]
        k = k_ref[rows, :]
        v = v_ref[rows, :]
        st = state_ref[...]
        inner = lax.dot_general(q, k, NT_DIMS, preferred_element_type=F32) * dm
        y = jnp.dot(inner.astype(BF16), v, preferred_element_type=F32)
        y = y + jnp.dot(q, st.astype(BF16), preferred_element_type=F32) * dq
        kd = (k.astype(F32) * dk).astype(BF16)
        state_ref[...] = st * dc + lax.dot_general(kd, v, TN_DIMS, preferred_element_type=F32)
        mu = jnp.mean(y, axis=-1, keepdims=True)
        d = y - mu
        var = jnp.mean(d * d, axis=-1, keepdims=True)
        yn = d * lax.rsqrt(var + GN_EPS) * gng + gnb
        g = g_ref[rows, :].astype(F32)
        z_ref[rows, :] = (g * jax.nn.sigmoid(g) * yn).astype(z_ref.dtype)


def retention(qk, vg, gn_g, gn_b, B, S):
    T = B * S
    H, dk, dv = RET_HEADS, RET_QK_DIM, RET_V_DIM
    C = _tile(S, RET_CHUNK)
    L = _tile(S, RET_BLOCK)
    nl = S // L
    log_g = jnp.log(1.0 - 2.0 ** (-5.0 - jnp.arange(H, dtype=F32)))
    idx = jnp.arange(C, dtype=F32)
    diff = idx[:, None] - idx[None, :]
    dm = jnp.where(diff >= 0, jnp.exp(log_g[:, None, None] * jnp.maximum(diff, 0.0)), 0.0)
    dq = jnp.exp(log_g[:, None] * (idx + 1.0))[:, :, None]
    dkk = jnp.exp(log_g[:, None] * (C - 1.0 - idx))[:, :, None]
    dc = jnp.broadcast_to(jnp.exp(log_g * C)[:, None, None], (H, 1, dv))
    kern = functools.partial(_retention_kernel, chunk=C, n_chunks=L // C)
    return pl.pallas_call(
        kern,
        grid=(B, H, nl),
        in_specs=[
            pl.BlockSpec((L, dk), lambda b, h, n: (b * nl + n, h)),
            pl.BlockSpec((L, dk), lambda b, h, n: (b * nl + n, H + h)),
            pl.BlockSpec((L, dv), lambda b, h, n: (b * nl + n, h)),
            pl.BlockSpec((L, dv), lambda b, h, n: (b * nl + n, H + h)),
            pl.BlockSpec((1, C, C), lambda b, h, n: (h, 0, 0)),
            pl.BlockSpec((1, C, 1), lambda b, h, n: (h, 0, 0)),
            pl.BlockSpec((1, C, 1), lambda b, h, n: (h, 0, 0)),
            pl.BlockSpec((1, 1, dv), lambda b, h, n: (h, 0, 0)),
            pl.BlockSpec((1, dv), lambda b, h, n: (0, h)),
            pl.BlockSpec((1, dv), lambda b, h, n: (0, h)),
        ],
        out_specs=pl.BlockSpec((L, dv), lambda b, h, n: (b * nl + n, h)),
        out_shape=jax.ShapeDtypeStruct((T, H * dv), BF16),
        scratch_shapes=[pltpu.VMEM((dk, dv), F32)],
        compiler_params=_params("arbitrary", "arbitrary", "arbitrary"),
        name="retention",
    )(qk, qk, vg, vg, dm, dq, dkk, dc, gn_g.reshape(1, H * dv), gn_b.reshape(1, H * dv))


def _out_ln_kernel(z_ref, w_ref, h_ref, g_ref, b_ref, o_ref):
    mix = jnp.dot(z_ref[...], w_ref[...], preferred_element_type=F32)
    u = DN_ALPHA * h_ref[...] + mix
    o_ref[...] = _layer_norm_rows(u, g_ref[...], b_ref[...])


def out_proj_ln(z, w_bf, h, ln_g, ln_b):
    T, K = z.shape
    tm = _tile(T, 512)
    row_spec = pl.BlockSpec((tm, D_MODEL), lambda i: (i, 0))
    vec_spec = pl.BlockSpec((1, D_MODEL), lambda i: (0, 0))
    return pl.pallas_call(
        _out_ln_kernel,
        grid=(T // tm,),
        in_specs=[pl.BlockSpec((tm, K), lambda i: (i, 0)), pl.BlockSpec((K, D_MODEL), lambda i: (0, 0)),
                  row_spec, vec_spec, vec_spec],
        out_specs=row_spec,
        out_shape=jax.ShapeDtypeStruct((T, D_MODEL), F32),
        compiler_params=_params("arbitrary"),
        name="out_proj_ln",
    )(z, w_bf, h, ln_g.reshape(1, D_MODEL), ln_b.reshape(1, D_MODEL))


def _router_kernel(h_ref, w_ref, b_ref, slab_ref, cnt_ref, carry_ref, *, tm):
    @pl.when(pl.program_id(0) == 0)
    def _():
        carry_ref[...] = jnp.zeros_like(carry_ref)

    logits = jnp.dot(h_ref[...], w_ref[...], preferred_element_type=F32,
                     precision=lax.Precision.HIGHEST) + b_ref[...]
    e_iota = lax.broadcasted_iota(jnp.int32, (tm, N_EXPERTS), 1).astype(F32)
    work = logits
    onehot = jnp.zeros((tm, N_EXPERTS), F32)
    top_vals, top_idx = [], []
    for _ in range(TOP_K):
        m = jnp.max(work, axis=-1, keepdims=True)
        idx = jnp.min(jnp.where(work == m, e_iota, float(N_EXPERTS)), axis=-1, keepdims=True)
        top_vals.append(m)
        top_idx.append(idx)
        hit = e_iota == idx
        work = jnp.where(hit, -jnp.inf, work)
        onehot = jnp.where(hit, 1.0, onehot)
    exps = [jnp.exp(v - top_vals[0]) for v in top_vals]
    denom = exps[0] + exps[1] + exps[2] + exps[3]
    gates = [e / denom for e in exps]

    r = lax.broadcasted_iota(jnp.int32, (tm, tm), 0)
    c = lax.broadcasted_iota(jnp.int32, (tm, tm), 1)
    tri = jnp.where(c < r, 1.0, 0.0).astype(BF16)
    before = jnp.dot(tri, onehot.astype(BF16), preferred_element_type=F32) + carry_ref[0:1, 0:N_EXPERTS]
    ranks = [jnp.sum(jnp.where(e_iota == idx, before, 0.0), axis=-1, keepdims=True) for idx in top_idx]

    lane = lax.broadcasted_iota(jnp.int32, (tm, LANES), 1)
    slab = jnp.zeros((tm, LANES), F32)
    for k in range(TOP_K):
        slab = jnp.where(lane == k, top_idx[k], slab)
        slab = jnp.where(lane == TOP_K + k, gates[k], slab)
        slab = jnp.where(lane == 2 * TOP_K + k, ranks[k], slab)
    slab_ref[...] = slab

    carry_ref[0:1, 0:N_EXPERTS] = carry_ref[0:1, 0:N_EXPERTS] + jnp.sum(onehot, axis=0, keepdims=True)
    cnt_ref[...] = carry_ref[...]


def router(h, w_router, b_router):
    T = h.shape[0]
    tm = _tile(T, 512)
    slab, cnt = pl.pallas_call(
        functools.partial(_router_kernel, tm=tm),
        grid=(T // tm,),
        in_specs=[pl.BlockSpec((tm, D_MODEL), lambda i: (i, 0)),
                  pl.BlockSpec((D_MODEL, N_EXPERTS), lambda i: (0, 0)),
                  pl.BlockSpec((1, N_EXPERTS), lambda i: (0, 0))],
        out_specs=[pl.BlockSpec((tm, LANES), lambda i: (i, 0)), pl.BlockSpec((8, LANES), lambda i: (0, 0))],
        out_shape=[jax.ShapeDtypeStruct((T, LANES), F32), jax.ShapeDtypeStruct((8, LANES), F32)],
        scratch_shapes=[pltpu.VMEM((8, LANES), F32)],
        compiler_params=_params("arbitrary"),
        name="router",
    )(h, w_router, b_router.reshape(1, N_EXPERTS))
    idx = slab[:, 0:TOP_K].astype(jnp.int32)
    gates = slab[:, TOP_K:2 * TOP_K]
    rank = slab[:, 2 * TOP_K:3 * TOP_K].astype(jnp.int32)
    counts = cnt[0, :N_EXPERTS].astype(jnp.int32)
    return idx, gates, rank, counts


def _rows_to_tiles(dst_ref, x, n):
    for c in range(ROW_CHUNKS):
        dst_ref[pl.ds(c, n, stride=ROW_CHUNKS), :] = x[:, c * LANES:(c + 1) * LANES]


def _tiles_to_rows(src_ref, n):
    return jnp.concatenate([src_ref[pl.ds(c, n, stride=ROW_CHUNKS), :] for c in range(ROW_CHUNKS)], axis=1)


def _dispatch_kernel(zfill_ref, dest_ref, h_ref, zeros_ref, xs_ref, stage, sem, zsem, *, tt, bm):
    i = pl.program_id(0)

    def drain():
        def body(t, carry):
            for k in range(TOP_K):
                pltpu.make_async_copy(stage.at[pl.ds(0, ROW_CHUNKS)], xs_ref.at[pl.ds(0, ROW_CHUNKS)], sem).wait()
            return carry
        lax.fori_loop(0, tt, body, 0)

    @pl.when(i == 0)
    def _():
        for e in range(N_EXPERTS):
            @pl.when(zfill_ref[e] >= 0)
            def _():
                start = pl.multiple_of(zfill_ref[e], bm * ROW_CHUNKS)
                cp = pltpu.make_async_copy(zeros_ref, xs_ref.at[pl.ds(start, bm * ROW_CHUNKS)], zsem)
                cp.start()
                cp.wait()

    @pl.when(i > 0)
    def _():
        drain()

    _rows_to_tiles(stage, h_ref[...], tt)

    def issue(t, carry):
        src = stage.at[pl.ds(pl.multiple_of(t * ROW_CHUNKS, ROW_CHUNKS), ROW_CHUNKS)]
        for k in range(TOP_K):
            d = pl.multiple_of(dest_ref[0, 0, t * TOP_K + k], ROW_CHUNKS)
            pltpu.make_async_copy(src, xs_ref.at[pl.ds(d, ROW_CHUNKS)], sem).start()
        return carry

    lax.fori_loop(0, tt, issue, 0)

    @pl.when(i == pl.num_programs(0) - 1)
    def _():
        drain()


def dispatch_rows(h, dest8, zfill8, n_slots, bm):
    T = h.shape[0]
    tt = _tile(T, DISPATCH_TOKENS)
    dest3 = dest8.reshape(T // tt, 1, tt * TOP_K)
    zeros = jnp.zeros((bm * ROW_CHUNKS, LANES), F32)
    return pl.pallas_call(
        functools.partial(_dispatch_kernel, tt=tt, bm=bm),
        grid_spec=pltpu.PrefetchScalarGridSpec(
            num_scalar_prefetch=1,
            grid=(T // tt,),
            in_specs=[pl.BlockSpec((1, 1, tt * TOP_K), lambda i, zf: (i, 0, 0), memory_space=pltpu.SMEM),
                      pl.BlockSpec((tt, D_MODEL), lambda i, zf: (i, 0)),
                      pl.BlockSpec(memory_space=pl.ANY)],
            out_specs=pl.BlockSpec(memory_space=pl.ANY),
            scratch_shapes=[pltpu.VMEM((tt * ROW_CHUNKS, LANES), F32),
                            pltpu.SemaphoreType.DMA, pltpu.SemaphoreType.DMA],
        ),
        out_shape=jax.ShapeDtypeStruct((n_slots * ROW_CHUNKS, LANES), F32),
        compiler_params=_params("arbitrary"),
        name="moe_dispatch",
    )(zfill8, dest3, h, zeros)


def _moe_mlp_kernel(be_ref, na_ref, x_ref, wgu_ref, bgu_ref, wd_ref, bd_ref, y_ref, wgu_bf, wd_bf, *, bm):
    i = pl.program_id(0)
    prev = be_ref[jnp.maximum(i - 1, 0)]
    active = i < na_ref[0]

    @pl.when(jnp.logical_and(active, jnp.logical_or(i == 0, be_ref[i] != prev)))
    def _():
        wgu_bf[...] = wgu_ref[...].astype(BF16)
        wd_bf[...] = wd_ref[...].astype(BF16)

    @pl.when(active)
    def _():
        x = _tiles_to_rows(x_ref, bm).astype(BF16)
        hgu = jnp.dot(x, wgu_bf[...], preferred_element_type=F32) + bgu_ref[...]
        gate = jnp.minimum(hgu[:, :D_FF], SWIGLU_LIMIT)
        up = jnp.clip(hgu[:, D_FF:], -SWIGLU_LIMIT, SWIGLU_LIMIT)
        act = gate * jax.nn.sigmoid(SWIGLU_ALPHA * gate) * (up + 1.0)
        y = jnp.dot(act.astype(BF16), wd_bf[...], preferred_element_type=F32) + bd_ref[...]
        _rows_to_tiles(y_ref, y, bm)


def moe_mlp(xs, block_expert, n_active, layer, w_gu, b_gu, w_d, b_d, bm):
    n_blocks = xs.shape[0] // (bm * ROW_CHUNKS)

    def row_map(i, be, na):
        return (jnp.minimum(i, na[0] - 1), 0)

    def w_map(i, be, na):
        return (layer, be[jnp.minimum(i, na[0] - 1)], 0, 0)

    def b_map(i, be, na):
        return (layer * N_EXPERTS + be[jnp.minimum(i, na[0] - 1)], 0, 0)

    return pl.pallas_call(
        functools.partial(_moe_mlp_kernel, bm=bm),
        grid_spec=pltpu.PrefetchScalarGridSpec(
            num_scalar_prefetch=2,
            grid=(n_blocks,),
            in_specs=[pl.BlockSpec((bm * ROW_CHUNKS, LANES), row_map),
                      pl.BlockSpec((None, None, D_MODEL, 2 * D_FF), w_map),
                      pl.BlockSpec((None, 1, 2 * D_FF), b_map),
                      pl.BlockSpec((None, None, D_FF, D_MODEL), w_map),
                      pl.BlockSpec((None, 1, D_MODEL), b_map)],
            out_specs=pl.BlockSpec((bm * ROW_CHUNKS, LANES), row_map),
            scratch_shapes=[pltpu.VMEM((D_MODEL, 2 * D_FF), BF16), pltpu.VMEM((D_FF, D_MODEL), BF16)],
        ),
        out_shape=jax.ShapeDtypeStruct(xs.shape, F32),
        compiler_params=_params("arbitrary"),
        name="moe_mlp",
    )(block_expert, n_active, xs, w_gu, b_gu.reshape(DEPTH * N_EXPERTS, 1, 2 * D_FF), w_d,
      b_d.reshape(DEPTH * N_EXPERTS, 1, D_MODEL))


def _ffn_ln_ple_kernel(dcur_ref, dnext_ref, gates_ref, h_ref, p_ref, g_ref, b_ref, wg_ref, wp_ref, ys_ref,
                       o_ref, ob_ref, ybuf, sems, *, tm):
    i = pl.program_id(0)
    slot = i % 2

    def row_copy(dref, s, t, k):
        d = pl.multiple_of(dref[0, 0, t * TOP_K + k], ROW_CHUNKS)
        dst = ybuf.at[s, k, pl.ds(pl.multiple_of(t * ROW_CHUNKS, ROW_CHUNKS), ROW_CHUNKS)]
        return pltpu.make_async_copy(ys_ref.at[pl.ds(d, ROW_CHUNKS)], dst, sems.at[s])

    def issue(dref, s):
        def body(t, carry):
            for k in range(TOP_K):
                row_copy(dref, s, t, k).start()
            return carry
        lax.fori_loop(0, tm, body, 0)

    @pl.when(i == 0)
    def _():
        issue(dcur_ref, 0)

    @pl.when(i + 1 < pl.num_programs(0))
    def _():
        issue(dnext_ref, 1 - slot)

    def drain(t, carry):
        for k in range(TOP_K):
            row_copy(dcur_ref, slot, t, k).wait()
        return carry
    lax.fori_loop(0, tm, drain, 0)

    gates = gates_ref[...]
    ffn = _tiles_to_rows(ybuf.at[slot, 0], tm) * gates[:, 0:1]
    for k in range(1, TOP_K):
        ffn = ffn + _tiles_to_rows(ybuf.at[slot, k], tm) * gates[:, k:k + 1]
    h2 = _layer_norm_rows(DN_ALPHA * h_ref[...] + ffn, g_ref[...], b_ref[...])
    gate = jax.nn.sigmoid(jnp.dot(h2.astype(BF16), wg_ref[...], preferred_element_type=F32))
    proj = jnp.dot(p_ref[...].astype(BF16), wp_ref[...], preferred_element_type=F32)
    out = h2 + gate * proj
    o_ref[...] = out
    ob_ref[...] = out.astype(BF16)


def ffn_ln_ple(ys, dest8, gates, h, p, ln_g, ln_b, wg_bf, wp_bf):
    T = h.shape[0]
    tm = _tile(T, 512)
    n = T // tm
    dest3 = dest8.reshape(n, 1, tm * TOP_K)
    row_spec = pl.BlockSpec((tm, D_MODEL), lambda i: (i, 0))
    vec_spec = pl.BlockSpec((1, D_MODEL), lambda i: (0, 0))
    return pl.pallas_call(
        functools.partial(_ffn_ln_ple_kernel, tm=tm),
        grid=(n,),
        in_specs=[pl.BlockSpec((1, 1, tm * TOP_K), lambda i: (i, 0, 0), memory_space=pltpu.SMEM),
                  pl.BlockSpec((1, 1, tm * TOP_K), lambda i: (jnp.minimum(i + 1, n - 1), 0, 0),
                               memory_space=pltpu.SMEM),
                  pl.BlockSpec((tm, TOP_K), lambda i: (i, 0)),
                  row_spec,
                  pl.BlockSpec((tm, PLE_DIM), lambda i: (i, 0)),
                  vec_spec, vec_spec,
                  pl.BlockSpec((D_MODEL, D_MODEL), lambda i: (0, 0)),
                  pl.BlockSpec((PLE_DIM, D_MODEL), lambda i: (0, 0)),
                  pl.BlockSpec(memory_space=pl.ANY)],
        out_specs=[row_spec, row_spec],
        out_shape=[jax.ShapeDtypeStruct((T, D_MODEL), F32), jax.ShapeDtypeStruct((T, D_MODEL), BF16)],
        scratch_shapes=[pltpu.VMEM((2, TOP_K, tm * ROW_CHUNKS, LANES), F32), pltpu.SemaphoreType.DMA((2,))],
        compiler_params=_params("arbitrary"),
        name="ffn_ln_ple",
    )(dest3, dest3, gates, h, p, ln_g.reshape(1, D_MODEL), ln_b.reshape(1, D_MODEL), wg_bf, wp_bf, ys)


def moe_layer(h1, layer, w_router, b_router, w_gu, b_gu, w_d, b_d):
    T = h1.shape[0]
    bm = MOE_BLOCK
    idx, gates, rank, counts = router(h1, w_router, b_router)
    padded = (counts + bm - 1) // bm * bm
    pend = jnp.cumsum(padded)
    pstart = pend - padded
    onehot = idx[..., None] == jnp.arange(N_EXPERTS, dtype=jnp.int32)
    dest = (jnp.sum(jnp.where(onehot, pstart, 0), axis=-1) + rank).astype(jnp.int32)
    dest8 = dest * ROW_CHUNKS
    n_slots = T * TOP_K + N_EXPERTS * bm
    n_blocks = n_slots // bm
    block_start = jnp.arange(n_blocks, dtype=jnp.int32) * bm
    block_expert = jnp.minimum(jnp.sum(block_start[:, None] >= pend[None, :], axis=1),
                               N_EXPERTS - 1).astype(jnp.int32)
    n_active = (pend[-1:] // bm).astype(jnp.int32)
    zfill8 = jnp.where(padded > 0, (pend - bm) * ROW_CHUNKS, -1).astype(jnp.int32)
    xs = dispatch_rows(h1, dest8, zfill8, n_slots, bm)
    ys = moe_mlp(xs, block_expert, n_active, layer, w_gu, b_gu, w_d, b_d, bm)
    return ys, dest8, gates


def _rms_rows(x, g):
    return x * lax.rsqrt(jnp.mean(x * x, axis=-1, keepdims=True) + RMS_EPS) * g


def _mla_kv_kernel(x_ref, wa_ref, g_ref, wkn_ref, wv_ref, cm_ref, sm_ref, k_ref, v_ref):
    a = jnp.dot(x_ref[...], wa_ref[...], preferred_element_type=F32)
    c_kv = _rms_rows(a[:, :KV_LORA], g_ref[...]).astype(BF16)
    kr = a[:, KV_LORA:KV_LORA + LANES] * cm_ref[...] + a[:, KV_LORA + LANES:] * sm_ref[...]
    kr = kr.astype(BF16)
    kn = jnp.dot(c_kv, wkn_ref[...], preferred_element_type=F32)
    v_ref[...] = jnp.dot(c_kv, wv_ref[...], preferred_element_type=F32).astype(BF16)
    for h in range(MLA_HEADS):
        k_ref[:, 2 * h * LANES:(2 * h + 1) * LANES] = kn[:, h * QK_NOPE:(h + 1) * QK_NOPE].astype(BF16)
        k_ref[:, (2 * h + 1) * LANES:(2 * h + 2) * LANES] = kr


def _pad_rope_cols(w, swap):
    K = w.shape[0]
    w = w.reshape(K, -1, QK_ROPE)
    if swap:
        w = jnp.concatenate([w[..., QK_ROPE // 2:], w[..., :QK_ROPE // 2]], axis=-1)
    w = jnp.concatenate([w, jnp.zeros_like(w)], axis=-1)
    return w.reshape(K, -1)


def mla_shared_kv(hb, w_kv_a, kv_norm_g, w_kv_b, cm, sm):
    T = hb.shape[0]
    tm = _tile(T, 512)
    w_r = w_kv_a[:, KV_LORA:]
    wa = jnp.concatenate([w_kv_a[:, :KV_LORA], _pad_rope_cols(w_r, False), _pad_rope_cols(w_r, True)],
                         axis=1).astype(BF16)
    wb = w_kv_b.reshape(KV_LORA, MLA_HEADS, QK_NOPE + V_HEAD)
    wkn = wb[:, :, :QK_NOPE].reshape(KV_LORA, MLA_HEADS * QK_NOPE).astype(BF16)
    wv = wb[:, :, QK_NOPE:].reshape(KV_LORA, MLA_HEADS * V_HEAD).astype(BF16)
    full = lambda shape: pl.BlockSpec(shape, lambda i: (0, 0))
    return pl.pallas_call(
        _mla_kv_kernel,
        grid=(T // tm,),
        in_specs=[pl.BlockSpec((tm, D_MODEL), lambda i: (i, 0)), full(wa.shape), full((1, KV_LORA)),
                  full(wkn.shape), full(wv.shape),
                  pl.BlockSpec((tm, LANES), lambda i: (i, 0)), pl.BlockSpec((tm, LANES), lambda i: (i, 0))],
        out_specs=[pl.BlockSpec((tm, MLA_HEADS * 2 * LANES), lambda i: (i, 0)),
                   pl.BlockSpec((tm, MLA_HEADS * V_HEAD), lambda i: (i, 0))],
        out_shape=[jax.ShapeDtypeStruct((T, MLA_HEADS * 2 * LANES), BF16),
                   jax.ShapeDtypeStruct((T, MLA_HEADS * V_HEAD), BF16)],
        compiler_params=_params("arbitrary"),
        name="mla_kv",
    )(hb, wa, kv_norm_g.reshape(1, KV_LORA), wkn, wv, cm, sm)


def _mla_q_kernel(x_ref, wa_ref, g_ref, wb_ref, cm_ref, sm_ref, q_ref):
    scale = (QK_NOPE + QK_ROPE) ** -0.5
    qa = jnp.dot(x_ref[...], wa_ref[...], preferred_element_type=F32)
    qc = _rms_rows(qa, g_ref[...]).astype(BF16)
    q = jnp.dot(qc, wb_ref[...], preferred_element_type=F32)
    cm = cm_ref[...] * scale
    sm = sm_ref[...] * scale
    n = MLA_HEADS * LANES
    for h in range(MLA_HEADS):
        lo = h * LANES
        q_ref[:, 2 * lo:2 * lo + LANES] = (q[:, lo:lo + LANES] * scale).astype(BF16)
        rope = q[:, n + lo:n + lo + LANES] * cm + q[:, 2 * n + lo:2 * n + lo + LANES] * sm
        q_ref[:, 2 * lo + LANES:2 * lo + 2 * LANES] = rope.astype(BF16)


def mla_q(hb, w_q_a, q_norm_g, w_q_b, cm, sm):
    T = hb.shape[0]
    tm = _tile(T, 512)
    wq = w_q_b.reshape(Q_LORA, MLA_HEADS, QK_NOPE + QK_ROPE)
    w_n = wq[:, :, :QK_NOPE].reshape(Q_LORA, MLA_HEADS * QK_NOPE)
    w_r = wq[:, :, QK_NOPE:].reshape(Q_LORA, MLA_HEADS * QK_ROPE)
    wb = jnp.concatenate([w_n, _pad_rope_cols(w_r, False), _pad_rope_cols(w_r, True)], axis=1).astype(BF16)
    full = lambda shape: pl.BlockSpec(shape, lambda i: (0, 0))
    return pl.pallas_call(
        _mla_q_kernel,
        grid=(T // tm,),
        in_specs=[pl.BlockSpec((tm, D_MODEL), lambda i: (i, 0)), full((D_MODEL, Q_LORA)), full((1, Q_LORA)),
                  full(wb.shape),
                  pl.BlockSpec((tm, LANES), lambda i: (i, 0)), pl.BlockSpec((tm, LANES), lambda i: (i, 0))],
        out_specs=pl.BlockSpec((tm, MLA_HEADS * 2 * LANES), lambda i: (i, 0)),
        out_shape=jax.ShapeDtypeStruct((T, MLA_HEADS * 2 * LANES), BF16),
        compiler_params=_params("arbitrary"),
        name="mla_q",
    )(hb, w_q_a.astype(BF16), q_norm_g.reshape(1, Q_LORA), wb, cm, sm)


def _flash_kernel(q_ref, k_ref, v_ref, o_ref, *, blk):
    i = pl.program_id(2)
    q = q_ref[...]

    def step(j, carry, masked):
        m, l, acc = carry
        start = pl.multiple_of(j * blk, blk)
        kj = k_ref[pl.ds(start, blk), :]
        vj = v_ref[pl.ds(start, blk), :]
        s = lax.dot_general(q, kj, NT_DIMS, preferred_element_type=F32)
        if masked:
            r = lax.broadcasted_iota(jnp.int32, (blk, blk), 0)
            c = lax.broadcasted_iota(jnp.int32, (blk, blk), 1)
            s = jnp.where(c <= r, s, -jnp.inf)
        m_new = jnp.maximum(m, jnp.max(s, axis=-1, keepdims=True))
        alpha = jnp.exp(m - m_new)
        p = jnp.exp(s - m_new)
        l = alpha * l + jnp.sum(p, axis=-1, keepdims=True)
        acc = alpha * acc + jnp.dot(p.astype(BF16), vj, preferred_element_type=F32)
        return m_new, l, acc

    init = (jnp.full((blk, 1), -jnp.inf, F32), jnp.zeros((blk, 1), F32), jnp.zeros((blk, V_HEAD), F32))
    carry = lax.fori_loop(0, i, lambda j, c: step(j, c, False), init)
    m, l, acc = step(i, carry, True)
    o_ref[...] = (acc / l).astype(o_ref.dtype)


def flash_attention(q, k, v, B, S):
    T = B * S
    blk = _tile(S, ATTN_BLOCK)
    nq = S // blk
    return pl.pallas_call(
        functools.partial(_flash_kernel, blk=blk),
        grid=(B, MLA_HEADS, nq),
        in_specs=[pl.BlockSpec((blk, 2 * LANES), lambda b, h, i: (b * nq + i, h)),
                  pl.BlockSpec((S, 2 * LANES), lambda b, h, i: (b, h)),
                  pl.BlockSpec((S, V_HEAD), lambda b, h, i: (b, h))],
        out_specs=pl.BlockSpec((blk, V_HEAD), lambda b, h, i: (b * nq + i, h)),
        out_shape=jax.ShapeDtypeStruct((T, MLA_HEADS * V_HEAD), BF16),
        compiler_params=_params("arbitrary", "arbitrary", "arbitrary"),
        name="mla_flash",
    )(q, k, v)


def kernel(x, p, positions, ret_w_in, ret_gn_g, ret_gn_b, ret_w_out, mla_w_kv_a, mla_kv_norm_g, mla_w_kv_b,
           mla_w_q_a, mla_q_norm_g, mla_w_q_b, mla_w_o, ln_mix_g, ln_mix_b, ln_ffn_g, ln_ffn_b,
           moe_w_router, moe_b_router, moe_w_gate_up, moe_b_gate_up, moe_w_down, moe_b_down,
           ple_w_gate, ple_w_proj):
    B, S, D = x.shape
    T = B * S
    cos_r, sin_r, cos_m, sin_m = rope_tables(positions)
    h = x.reshape(T, D)
    hb = h.astype(BF16)
    kv = None
    for i in range(DEPTH):
        if i < N_A:
            qk, vg = ret_projections(hb, ret_w_in[i].astype(BF16), cos_r, sin_r)
            z = retention(qk, vg, ret_gn_g[i], ret_gn_b[i], B, S)
            w_out = ret_w_out[i].astype(BF16)
        else:
            if kv is None:
                kv = mla_shared_kv(hb, mla_w_kv_a, mla_kv_norm_g, mla_w_kv_b, cos_m, sin_m)
            j = i - N_A
            q = mla_q(hb, mla_w_q_a[j], mla_q_norm_g[j], mla_w_q_b[j], cos_m, sin_m)
            z = flash_attention(q, kv[0], kv[1], B, S)
            w_out = mla_w_o[j].astype(BF16)
        h1 = out_proj_ln(z, w_out, h, ln_mix_g[i], ln_mix_b[i])
        ys, dest8, gates = moe_layer(h1, i, moe_w_router[i], moe_b_router[i], moe_w_gate_up, moe_b_gate_up,
                                     moe_w_down, moe_b_down)
        h, hb = ffn_ln_ple(ys, dest8, gates, h1, p[i].reshape(T, PLE_DIM), ln_ffn_g[i], ln_ffn_b[i],
                           ple_w_gate[i].astype(BF16), ple_w_proj[i].astype(BF16))
    return h.reshape(B, S, D)
```

```python
import functools

import jax
import jax.numpy as jnp
from jax import lax
from jax.experimental import pallas as pl
from jax.experimental.pallas import tpu as pltpu

D_MODEL = 1024
DEPTH = 4
N_A = DEPTH // 2
RET_HEADS = 4
RET_QK_DIM = D_MODEL // RET_HEADS
RET_V_DIM = 2 * RET_QK_DIM
GN_EPS = 1e-6
MLA_HEADS = 8
QK_NOPE = 128
QK_ROPE = 64
V_HEAD = 128
Q_LORA = 256
KV_LORA = 256
RMS_EPS = 1e-6
ROPE_THETA = 10000.0
N_EXPERTS = 32
TOP_K = 4
D_FF = D_MODEL
SWIGLU_LIMIT = 7.0
SWIGLU_ALPHA = 1.702
PLE_DIM = 256
DN_ALPHA = (2 * DEPTH) ** 0.25
LN_EPS = 1e-5
LOG2_E = 1.4426950408889634

LANES = 128
ROW_CHUNKS = D_MODEL // LANES
VMEM_LIMIT_BYTES = 56 * 1024 * 1024

RET_CHUNK = 256
RET_BLOCK = 1024
ATTN_BLOCK = 512
MOE_BLOCK = 512
DISPATCH_TOKENS = 1024

F32 = jnp.float32
BF16 = jnp.bfloat16
NT_DIMS = (((1,), (1,)), ((), ()))
TN_DIMS = (((0,), (0,)), ((), ()))


def _tile(n, pref):
    t = min(n, pref)
    assert n % t == 0, (n, pref)
    return t


def _params(*sem):
    return pltpu.CompilerParams(dimension_semantics=sem, vmem_limit_bytes=VMEM_LIMIT_BYTES)


def _layer_norm_rows(u, g, b):
    mu = jnp.mean(u, axis=-1, keepdims=True)
    d = u - mu
    var = jnp.mean(d * d, axis=-1, keepdims=True)
    return d * lax.rsqrt(var + LN_EPS) * g + b


def _rope_table_kernel(pos_ref, inv_r_ref, inv_m_ref, sign_m_ref, cr_ref, sr_ref, cm_ref, sm_ref):
    pos = pos_ref[...].astype(F32)
    ang_r = pos * inv_r_ref[...]
    cr_ref[...] = jnp.cos(ang_r)
    sr_ref[...] = jnp.sin(ang_r)
    ang_m = pos * inv_m_ref[...]
    cm_ref[...] = jnp.cos(ang_m)
    sm_ref[...] = jnp.sin(ang_m) * sign_m_ref[...]


def rope_tables(positions):
    T = positions.size
    tm = _tile(T, 1024)
    pos = positions.reshape(T, 1)
    half_r = RET_QK_DIM // 2
    inv_r = ROPE_THETA ** (-jnp.arange(0, RET_QK_DIM, 2, dtype=F32) / RET_QK_DIM)
    inv_m = ROPE_THETA ** (-jnp.arange(0, QK_ROPE, 2, dtype=F32) / QK_ROPE)
    half_m = QK_ROPE // 2
    lane = jnp.arange(LANES)
    inv_m = inv_m[lane % half_m]
    sign_m = jnp.where((lane % QK_ROPE) < half_m, -1.0, 1.0).astype(F32)
    assert half_r == LANES
    row = lambda v: v.reshape(1, LANES)
    vec_spec = pl.BlockSpec((1, LANES), lambda i: (0, 0))
    tab_spec = pl.BlockSpec((tm, LANES), lambda i: (i, 0))
    out = jax.ShapeDtypeStruct((T, LANES), F32)
    return pl.pallas_call(
        _rope_table_kernel,
        grid=(T // tm,),
        in_specs=[pl.BlockSpec((tm, 1), lambda i: (i, 0)), vec_spec, vec_spec, vec_spec],
        out_specs=[tab_spec] * 4,
        out_shape=[out] * 4,
        compiler_params=_params("arbitrary"),
        name="rope_tables",
    )(pos, row(inv_r), row(inv_m), row(sign_m))


def _ret_qk_kernel(x_ref, w_ref, cos_ref, sin_ref, o_ref):
    acc = jnp.dot(x_ref[...], w_ref[...], preferred_element_type=F32)
    scale = jnp.where(pl.program_id(0) == 1, RET_QK_DIM ** -0.5, 1.0).astype(F32)
    c = cos_ref[...] * scale
    s = sin_ref[...] * scale
    half = RET_QK_DIM // 2
    for h in range(RET_HEADS):
        lo = h * RET_QK_DIM
        x1 = acc[:, lo:lo + half]
        x2 = acc[:, lo + half:lo + 2 * half]
        o_ref[:, lo:lo + half] = (x1 * c - x2 * s).astype(o_ref.dtype)
        o_ref[:, lo + half:lo + 2 * half] = (x1 * s + x2 * c).astype(o_ref.dtype)


def _matmul_kernel(x_ref, w_ref, o_ref):
    o_ref[...] = jnp.dot(x_ref[...], w_ref[...], preferred_element_type=F32).astype(o_ref.dtype)


def ret_projections(hb, w_in_bf, cos_r, sin_r):
    T = hb.shape[0]
    tm = _tile(T, 1024)
    tn = D_MODEL
    n_qk = 2 * RET_HEADS * RET_QK_DIM // tn
    n_vg = 2 * RET_HEADS * RET_V_DIM // tn
    x_spec = pl.BlockSpec((tm, D_MODEL), lambda j, i: (i, 0))
    tab_spec = pl.BlockSpec((tm, LANES), lambda j, i: (i, 0))
    qk = pl.pallas_call(
        _ret_qk_kernel,
        grid=(n_qk, T // tm),
        in_specs=[x_spec, pl.BlockSpec((D_MODEL, tn), lambda j, i: (0, j)), tab_spec, tab_spec],
        out_specs=pl.BlockSpec((tm, tn), lambda j, i: (i, j)),
        out_shape=jax.ShapeDtypeStruct((T, n_qk * tn), BF16),
        compiler_params=_params("arbitrary", "arbitrary"),
        name="ret_qk_proj",
    )(hb, w_in_bf, cos_r, sin_r)
    vg = pl.pallas_call(
        _matmul_kernel,
        grid=(n_vg, T // tm),
        in_specs=[x_spec, pl.BlockSpec((D_MODEL, tn), lambda j, i: (0, j + n_qk))],
        out_specs=pl.BlockSpec((tm, tn), lambda j, i: (i, j)),
        out_shape=jax.ShapeDtypeStruct((T, n_vg * tn), BF16),
        compiler_params=_params("arbitrary", "arbitrary"),
        name="ret_vg_proj",
    )(hb, w_in_bf)
    return qk, vg


def _retention_kernel(q_ref, k_ref, v_ref, g_ref, dm_ref, dq_ref, dk_ref, dc_ref, gng_ref, gnb_ref,
                      z_ref, state_ref, *, chunk, n_chunks):
    @pl.when(pl.program_id(2) == 0)
    def _():
        state_ref[...] = jnp.zeros_like(state_ref)

    dm = dm_ref[0]
    dq = dq_ref[0]
    dk = dk_ref[0]
    dc = dc_ref[0]
    gng = gng_ref[...]
    gnb = gnb_ref[...]
    for c in range(n_chunks):
        rows = slice(c * chunk, (c + 1) * chunk)
        q = q_ref[rows, :]
        k = k_ref[rows, :]
        v = v_ref[rows, :]
        st = state_ref[...]
        inner = lax.dot_general(q, k, NT_DIMS, preferred_element_type=F32) * dm
        y = jnp.dot(inner.astype(BF16), v, preferred_element_type=F32)
        y = y + jnp.dot(q, st.astype(BF16), preferred_element_type=F32) * dq
        kd = (k.astype(F32) * dk).astype(BF16)
        state_ref[...] = st * dc + lax.dot_general(kd, v, TN_DIMS, preferred_element_type=F32)
        mu = jnp.mean(y, axis=-1, keepdims=True)
        d = y - mu
        var = jnp.mean(d * d, axis=-1, keepdims=True)
        yn = d * lax.rsqrt(var + GN_EPS) * gng + gnb
        g = g_ref[rows, :].astype(F32)
        z_ref[rows, :] = (g * jax.nn.sigmoid(g) * yn).astype(z_ref.dtype)


def retention(qk, vg, gn_g, gn_b, B, S):
    T = B * S
    H, dk, dv = RET_HEADS, RET_QK_DIM, RET_V_DIM
    C = _tile(S, RET_CHUNK)
    L = _tile(S, RET_BLOCK)
    nl = S // L
    log_g = jnp.log(1.0 - 2.0 ** (-5.0 - jnp.arange(H, dtype=F32)))
    idx = jnp.arange(C, dtype=F32)
    diff = idx[:, None] - idx[None, :]
    dm = jnp.where(diff >= 0, jnp.exp(log_g[:, None, None] * jnp.maximum(diff, 0.0)), 0.0)
    dq = jnp.exp(log_g[:, None] * (idx + 1.0))[:, :, None]
    dkk = jnp.exp(log_g[:, None] * (C - 1.0 - idx))[:, :, None]
    dc = jnp.broadcast_to(jnp.exp(log_g * C)[:, None, None], (H, 1, dv))
    kern = functools.partial(_retention_kernel, chunk=C, n_chunks=L // C)
    return pl.pallas_call(
        kern,
        grid=(B, H, nl),
        in_specs=[
            pl.BlockSpec((L, dk), lambda b, h, n: (b * nl + n, h)),
            pl.BlockSpec((L, dk), lambda b, h, n: (b * nl + n, H + h)),
            pl.BlockSpec((L, dv), lambda b, h, n: (b * nl + n, h)),
            pl.BlockSpec((L, dv), lambda b, h, n: (b * nl + n, H + h)),
            pl.BlockSpec((1, C, C), lambda b, h, n: (h, 0, 0)),
            pl.BlockSpec((1, C, 1), lambda b, h, n: (h, 0, 0)),
            pl.BlockSpec((1, C, 1), lambda b, h, n: (h, 0, 0)),
            pl.BlockSpec((1, 1, dv), lambda b, h, n: (h, 0, 0)),
            pl.BlockSpec((1, dv), lambda b, h, n: (0, h)),
            pl.BlockSpec((1, dv), lambda b, h, n: (0, h)),
        ],
        out_specs=pl.BlockSpec((L, dv), lambda b, h, n: (b * nl + n, h)),
        out_shape=jax.ShapeDtypeStruct((T, H * dv), BF16),
        scratch_shapes=[pltpu.VMEM((dk, dv), F32)],
        compiler_params=_params("arbitrary", "arbitrary", "arbitrary"),
        name="retention",
    )(qk, qk, vg, vg, dm, dq, dkk, dc, gn_g.reshape(1, H * dv), gn_b.reshape(1, H * dv))


def _out_ln_kernel(z_ref, w_ref, h_ref, g_ref, b_ref, o_ref):
    mix = jnp.dot(z_ref[...], w_ref[...], preferred_element_type=F32)
    u = DN_ALPHA * h_ref[...] + mix
    o_ref[...] = _layer_norm_rows(u, g_ref[...], b_ref[...])


def out_proj_ln(z, w_bf, h, ln_g, ln_b):
    T, K = z.shape
    tm = _tile(T, 512)
    row_spec = pl.BlockSpec((tm, D_MODEL), lambda i: (i, 0))
    vec_spec = pl.BlockSpec((1, D_MODEL), lambda i: (0, 0))
    return pl.pallas_call(
        _out_ln_kernel,
        grid=(T // tm,),
        in_specs=[pl.BlockSpec((tm, K), lambda i: (i, 0)), pl.BlockSpec((K, D_MODEL), lambda i: (0, 0)),
                  row_spec, vec_spec, vec_spec],
        out_specs=row_spec,
        out_shape=jax.ShapeDtypeStruct((T, D_MODEL), F32),
        compiler_params=_params("arbitrary"),
        name="out_proj_ln",
    )(z, w_bf, h, ln_g.reshape(1, D_MODEL), ln_b.reshape(1, D_MODEL))


def _router_kernel(h_ref, w_ref, b_ref, slab_ref, cnt_ref, carry_ref, *, tm):
    @pl.when(pl.program_id(0) == 0)
    def _():
        carry_ref[...] = jnp.zeros_like(carry_ref)

    logits = jnp.dot(h_ref[...], w_ref[...], preferred_element_type=F32,
                     precision=lax.Precision.HIGHEST) + b_ref[...]
    e_iota = lax.broadcasted_iota(jnp.int32, (tm, N_EXPERTS), 1).astype(F32)
    work = logits
    onehot = jnp.zeros((tm, N_EXPERTS), F32)
    top_vals, top_idx = [], []
    for _ in range(TOP_K):
        m = jnp.max(work, axis=-1, keepdims=True)
        idx = jnp.min(jnp.where(work == m, e_iota, float(N_EXPERTS)), axis=-1, keepdims=True)
        top_vals.append(m)
        top_idx.append(idx)
        hit = e_iota == idx
        work = jnp.where(hit, -jnp.inf, work)
        onehot = jnp.where(hit, 1.0, onehot)
    exps = [jnp.exp(v - top_vals[0]) for v in top_vals]
    denom = exps[0] + exps[1] + exps[2] + exps[3]
    gates = [e / denom for e in exps]

    r = lax.broadcasted_iota(jnp.int32, (tm, tm), 0)
    c = lax.broadcasted_iota(jnp.int32, (tm, tm), 1)
    tri = jnp.where(c < r, 1.0, 0.0).astype(BF16)
    before = jnp.dot(tri, onehot.astype(BF16), preferred_element_type=F32) + carry_ref[0:1, 0:N_EXPERTS]
    ranks = [jnp.sum(jnp.where(e_iota == idx, before, 0.0), axis=-1, keepdims=True) for idx in top_idx]

    lane = lax.broadcasted_iota(jnp.int32, (tm, LANES), 1)
    slab = jnp.zeros((tm, LANES), F32)
    for k in range(TOP_K):
        slab = jnp.where(lane == k, top_idx[k], slab)
        slab = jnp.where(lane == TOP_K + k, gates[k], slab)
        slab = jnp.where(lane == 2 * TOP_K + k, ranks[k], slab)
    slab_ref[...] = slab

    carry_ref[0:1, 0:N_EXPERTS] = carry_ref[0:1, 0:N_EXPERTS] + jnp.sum(onehot, axis=0, keepdims=True)
    cnt_ref[...] = carry_ref[...]


def router(h, w_router, b_router):
    T = h.shape[0]
    tm = _tile(T, 512)
    slab, cnt = pl.pallas_call(
        functools.partial(_router_kernel, tm=tm),
        grid=(T // tm,),
        in_specs=[pl.BlockSpec((tm, D_MODEL), lambda i: (i, 0)),
                  pl.BlockSpec((D_MODEL, N_EXPERTS), lambda i: (0, 0)),
                  pl.BlockSpec((1, N_EXPERTS), lambda i: (0, 0))],
        out_specs=[pl.BlockSpec((tm, LANES), lambda i: (i, 0)), pl.BlockSpec((8, LANES), lambda i: (0, 0))],
        out_shape=[jax.ShapeDtypeStruct((T, LANES), F32), jax.ShapeDtypeStruct((8, LANES), F32)],
        scratch_shapes=[pltpu.VMEM((8, LANES), F32)],
        compiler_params=_params("arbitrary"),
        name="router",
    )(h, w_router, b_router.reshape(1, N_EXPERTS))
    idx = slab[:, 0:TOP_K].astype(jnp.int32)
    gates = slab[:, TOP_K:2 * TOP_K]
    rank = slab[:, 2 * TOP_K:3 * TOP_K].astype(jnp.int32)
    counts = cnt[0, :N_EXPERTS].astype(jnp.int32)
    return idx, gates, rank, counts


def _rows_to_tiles(dst_ref, x, n):
    for c in range(ROW_CHUNKS):
        dst_ref[pl.ds(c, n, stride=ROW_CHUNKS), :] = x[:, c * LANES:(c + 1) * LANES]


def _tiles_to_rows(src_ref, n):
    return jnp.concatenate([src_ref[pl.ds(c, n, stride=ROW_CHUNKS), :] for c in range(ROW_CHUNKS)], axis=1)


def _dispatch_kernel(zfill_ref, rsrc_ref, rdst_ref, rlen_ref, q_ref, h_ref, zeros_ref, xs_ref,
                     rows, stage, sem, zsem, *, tt, bm):
    i = pl.program_id(0)

    def for_each_run_piece(step, fn):
        def per_expert(e, carry):
            n = rlen_ref[step * N_EXPERTS + e]
            src = rsrc_ref[step * N_EXPERTS + e]
            dst = rdst_ref[step * N_EXPERTS + e]
            done = jnp.int32(0)
            piece = tt
            while piece >= 1:
                take = (n & piece) != 0
                size = piece * ROW_CHUNKS

                @pl.when(take)
                def _(done=done, size=size):
                    s = pl.multiple_of(src + done, ROW_CHUNKS)
                    d = pl.multiple_of(dst + done, ROW_CHUNKS)
                    fn(pltpu.make_async_copy(stage.at[pl.ds(s, size)], xs_ref.at[pl.ds(d, size)], sem))

                done = done + jnp.where(take, size, 0)
                piece //= 2
            return carry
        lax.fori_loop(0, N_EXPERTS, per_expert, 0)

    @pl.when(i == 0)
    def _():
        for e in range(N_EXPERTS):
            @pl.when(zfill_ref[e] >= 0)
            def _():
                start = pl.multiple_of(zfill_ref[e], bm * ROW_CHUNKS)
                cp = pltpu.make_async_copy(zeros_ref, xs_ref.at[pl.ds(start, bm * ROW_CHUNKS)], zsem)
                cp.start()
                cp.wait()

    @pl.when(i > 0)
    def _():
        for_each_run_piece(i - 1, lambda cp: cp.wait())

    _rows_to_tiles(rows, h_ref[...], tt)

    def sort_rows(t, carry):
        row = rows[pl.ds(pl.multiple_of(t * ROW_CHUNKS, ROW_CHUNKS), ROW_CHUNKS), :]
        for k in range(TOP_K):
            q = pl.multiple_of(q_ref[0, 0, t * TOP_K + k], ROW_CHUNKS)
            stage[pl.ds(q, ROW_CHUNKS), :] = row
        return carry

    lax.fori_loop(0, tt, sort_rows, 0)
    for_each_run_piece(i, lambda cp: cp.start())

    @pl.when(i == pl.num_programs(0) - 1)
    def _():
        for_each_run_piece(i, lambda cp: cp.wait())


def dispatch_rows(h, stage_pos8, run_src8, run_dst8, run_len, zfill8, n_slots, bm):
    T = h.shape[0]
    tt = _tile(T, DISPATCH_TOKENS)
    pos3 = stage_pos8.reshape(T // tt, 1, tt * TOP_K)
    zeros = jnp.zeros((bm * ROW_CHUNKS, LANES), F32)
    return pl.pallas_call(
        functools.partial(_dispatch_kernel, tt=tt, bm=bm),
        grid_spec=pltpu.PrefetchScalarGridSpec(
            num_scalar_prefetch=4,
            grid=(T // tt,),
            in_specs=[pl.BlockSpec((1, 1, tt * TOP_K), lambda i, *_: (i, 0, 0), memory_space=pltpu.SMEM),
                      pl.BlockSpec((tt, D_MODEL), lambda i, *_: (i, 0)),
                      pl.BlockSpec(memory_space=pl.ANY)],
            out_specs=pl.BlockSpec(memory_space=pl.ANY),
            scratch_shapes=[pltpu.VMEM((tt * ROW_CHUNKS, LANES), F32),
                            pltpu.VMEM((tt * TOP_K * ROW_CHUNKS, LANES), F32),
                            pltpu.SemaphoreType.DMA, pltpu.SemaphoreType.DMA],
        ),
        out_shape=jax.ShapeDtypeStruct((n_slots * ROW_CHUNKS, LANES), F32),
        compiler_params=_params("arbitrary"),
        name="moe_dispatch",
    )(zfill8, run_src8, run_dst8, run_len, pos3, h, zeros)


def _moe_mlp_kernel(be_ref, na_ref, x_ref, wgu_ref, bgu_ref, wd_ref, bd_ref, y_ref, wgu_bf, wd_bf, *, bm):
    i = pl.program_id(0)
    prev = be_ref[jnp.maximum(i - 1, 0)]
    active = i < na_ref[0]

    @pl.when(jnp.logical_and(active, jnp.logical_or(i == 0, be_ref[i] != prev)))
    def _():
        wgu_bf[...] = wgu_ref[...].astype(BF16)
        wd_bf[...] = wd_ref[...].astype(BF16)

    @pl.when(active)
    def _():
        x = _tiles_to_rows(x_ref, bm).astype(BF16)
        hgu = jnp.dot(x, wgu_bf[...], preferred_element_type=F32) + bgu_ref[...]
        gate = jnp.minimum(hgu[:, :D_FF], SWIGLU_LIMIT)
        up = jnp.clip(hgu[:, D_FF:], -SWIGLU_LIMIT, SWIGLU_LIMIT)
        act = gate * jax.nn.sigmoid(SWIGLU_ALPHA * gate) * (up + 1.0)
        y = jnp.dot(act.astype(BF16), wd_bf[...], preferred_element_type=F32) + bd_ref[...]
        _rows_to_tiles(y_ref, y, bm)


def moe_mlp(xs, block_expert, n_active, layer, w_gu, b_gu, w_d, b_d, bm):
    n_blocks = xs.shape[0] // (bm * ROW_CHUNKS)

    def row_map(i, be, na):
        return (jnp.minimum(i, na[0] - 1), 0)

    def w_map(i, be, na):
        return (layer, be[jnp.minimum(i, na[0] - 1)], 0, 0)

    def b_map(i, be, na):
        return (layer * N_EXPERTS + be[jnp.minimum(i, na[0] - 1)], 0, 0)

    return pl.pallas_call(
        functools.partial(_moe_mlp_kernel, bm=bm),
        grid_spec=pltpu.PrefetchScalarGridSpec(
            num_scalar_prefetch=2,
            grid=(n_blocks,),
            in_specs=[pl.BlockSpec((bm * ROW_CHUNKS, LANES), row_map),
                      pl.BlockSpec((None, None, D_MODEL, 2 * D_FF), w_map),
                      pl.BlockSpec((None, 1, 2 * D_FF), b_map),
                      pl.BlockSpec((None, None, D_FF, D_MODEL), w_map),
                      pl.BlockSpec((None, 1, D_MODEL), b_map)],
            out_specs=pl.BlockSpec((bm * ROW_CHUNKS, LANES), row_map),
            scratch_shapes=[pltpu.VMEM((D_MODEL, 2 * D_FF), BF16), pltpu.VMEM((D_FF, D_MODEL), BF16)],
        ),
        out_shape=jax.ShapeDtypeStruct(xs.shape, F32),
        compiler_params=_params("arbitrary"),
        name="moe_mlp",
    )(block_expert, n_active, xs, w_gu, b_gu.reshape(DEPTH * N_EXPERTS, 1, 2 * D_FF), w_d,
      b_d.reshape(DEPTH * N_EXPERTS, 1, D_MODEL))


def _ffn_ln_ple_kernel(dcur_ref, dnext_ref, gates_ref, h_ref, p_ref, g_ref, b_ref, wg_ref, wp_ref, ys_ref,
                       o_ref, ob_ref, ybuf, sems, *, tm):
    i = pl.program_id(0)
    slot = i % 2

    def row_copy(dref, s, t, k):
        d = pl.multiple_of(dref[0, 0, t * TOP_K + k], ROW_CHUNKS)
        dst = ybuf.at[s, k, pl.ds(pl.multiple_of(t * ROW_CHUNKS, ROW_CHUNKS), ROW_CHUNKS)]
        return pltpu.make_async_copy(ys_ref.at[pl.ds(d, ROW_CHUNKS)], dst, sems.at[s])

    def issue(dref, s):
        def body(t, carry):
            for k in range(TOP_K):
                row_copy(dref, s, t, k).start()
            return carry
        lax.fori_loop(0, tm, body, 0)

    @pl.when(i == 0)
    def _():
        issue(dcur_ref, 0)

    @pl.when(i + 1 < pl.num_programs(0))
    def _():
        issue(dnext_ref, 1 - slot)

    def drain(t, carry):
        for k in range(TOP_K):
            row_copy(dcur_ref, slot, t, k).wait()
        return carry
    lax.fori_loop(0, tm, drain, 0)

    gates = gates_ref[...]
    ffn = _tiles_to_rows(ybuf.at[slot, 0], tm) * gates[:, 0:1]
    for k in range(1, TOP_K):
        ffn = ffn + _tiles_to_rows(ybuf.at[slot, k], tm) * gates[:, k:k + 1]
    h2 = _layer_norm_rows(DN_ALPHA * h_ref[...] + ffn, g_ref[...], b_ref[...])
    gate = jax.nn.sigmoid(jnp.dot(h2.astype(BF16), wg_ref[...], preferred_element_type=F32))
    proj = jnp.dot(p_ref[...].astype(BF16), wp_ref[...], preferred_element_type=F32)
    out = h2 + gate * proj
    o_ref[...] = out
    ob_ref[...] = out.astype(BF16)


def ffn_ln_ple(ys, dest8, gates, h, p, ln_g, ln_b, wg_bf, wp_bf):
    T = h.shape[0]
    tm = _tile(T, 512)
    n = T // tm
    dest3 = dest8.reshape(n, 1, tm * TOP_K)
    row_spec = pl.BlockSpec((tm, D_MODEL), lambda i: (i, 0))
    vec_spec = pl.BlockSpec((1, D_MODEL), lambda i: (0, 0))
    return pl.pallas_call(
        functools.partial(_ffn_ln_ple_kernel, tm=tm),
        grid=(n,),
        in_specs=[pl.BlockSpec((1, 1, tm * TOP_K), lambda i: (i, 0, 0), memory_space=pltpu.SMEM),
                  pl.BlockSpec((1, 1, tm * TOP_K), lambda i: (jnp.minimum(i + 1, n - 1), 0, 0),
                               memory_space=pltpu.SMEM),
                  pl.BlockSpec((tm, TOP_K), lambda i: (i, 0)),
                  row_spec,
                  pl.BlockSpec((tm, PLE_DIM), lambda i: (i, 0)),
                  vec_spec, vec_spec,
                  pl.BlockSpec((D_MODEL, D_MODEL), lambda i: (0, 0)),
                  pl.BlockSpec((PLE_DIM, D_MODEL), lambda i: (0, 0)),
                  pl.BlockSpec(memory_space=pl.ANY)],
        out_specs=[row_spec, row_spec],
        out_shape=[jax.ShapeDtypeStruct((T, D_MODEL), F32), jax.ShapeDtypeStruct((T, D_MODEL), BF16)],
        scratch_shapes=[pltpu.VMEM((2, TOP_K, tm * ROW_CHUNKS, LANES), F32), pltpu.SemaphoreType.DMA((2,))],
        compiler_params=_params("arbitrary"),
        name="ffn_ln_ple",
    )(dest3, dest3, gates, h, p, ln_g.reshape(1, D_MODEL), ln_b.reshape(1, D_MODEL), wg_bf, wp_bf, ys)


def moe_layer(h1, layer, w_router, b_router, w_gu, b_gu, w_d, b_d):
    T = h1.shape[0]
    bm = MOE_BLOCK
    idx, gates, rank, counts = router(h1, w_router, b_router)
    padded = (counts + bm - 1) // bm * bm
    pend = jnp.cumsum(padded)
    pstart = pend - padded
    onehot = idx[..., None] == jnp.arange(N_EXPERTS, dtype=jnp.int32)
    dest = (jnp.sum(jnp.where(onehot, pstart, 0), axis=-1) + rank).astype(jnp.int32)
    dest8 = dest * ROW_CHUNKS
    n_slots = T * TOP_K + N_EXPERTS * bm
    n_blocks = n_slots // bm
    block_start = jnp.arange(n_blocks, dtype=jnp.int32) * bm
    block_expert = jnp.minimum(jnp.sum(block_start[:, None] >= pend[None, :], axis=1),
                               N_EXPERTS - 1).astype(jnp.int32)
    n_active = (pend[-1:] // bm).astype(jnp.int32)
    zfill8 = jnp.where(padded > 0, (pend - bm) * ROW_CHUNKS, -1).astype(jnp.int32)
    tt = _tile(T, DISPATCH_TOKENS)
    nt = T // tt
    oh_t = onehot.reshape(nt, tt * TOP_K, N_EXPERTS)
    tile_cnt = jnp.sum(oh_t, axis=1, dtype=jnp.int32)
    tile_base = jnp.cumsum(tile_cnt, axis=0) - tile_cnt
    stage_off = jnp.cumsum(tile_cnt, axis=1) - tile_cnt
    shift = jnp.sum(jnp.where(oh_t, (stage_off - tile_base)[:, None, :], 0), axis=-1)
    stage_pos8 = (shift + rank.reshape(nt, tt * TOP_K)).astype(jnp.int32) * ROW_CHUNKS
    run_src8 = (stage_off * ROW_CHUNKS).reshape(-1).astype(jnp.int32)
    run_dst8 = ((pstart[None, :] + tile_base) * ROW_CHUNKS).reshape(-1).astype(jnp.int32)
    xs = dispatch_rows(h1, stage_pos8, run_src8, run_dst8, tile_cnt.reshape(-1), zfill8, n_slots, bm)
    ys = moe_mlp(xs, block_expert, n_active, layer, w_gu, b_gu, w_d, b_d, bm)
    return ys, dest8, gates


def _rms_rows(x, g):
    return x * lax.rsqrt(jnp.mean(x * x, axis=-1, keepdims=True) + RMS_EPS) * g


def _mla_kv_kernel(x_ref, wa_ref, g_ref, wkn_ref, wv_ref, cm_ref, sm_ref, k_ref, v_ref):
    a = jnp.dot(x_ref[...], wa_ref[...], preferred_element_type=F32)
    c_kv = _rms_rows(a[:, :KV_LORA], g_ref[...]).astype(BF16)
    kr = a[:, KV_LORA:KV_LORA + LANES] * cm_ref[...] + a[:, KV_LORA + LANES:] * sm_ref[...]
    kr = kr.astype(BF16)
    kn = jnp.dot(c_kv, wkn_ref[...], preferred_element_type=F32)
    v_ref[...] = jnp.dot(c_kv, wv_ref[...], preferred_element_type=F32).astype(BF16)
    for h in range(MLA_HEADS):
        k_ref[:, 2 * h * LANES:(2 * h + 1) * LANES] = kn[:, h * QK_NOPE:(h + 1) * QK_NOPE].astype(BF16)
        k_ref[:, (2 * h + 1) * LANES:(2 * h + 2) * LANES] = kr


def _pad_rope_cols(w, swap):
    K = w.shape[0]
    w = w.reshape(K, -1, QK_ROPE)
    if swap:
        w = jnp.concatenate([w[..., QK_ROPE // 2:], w[..., :QK_ROPE // 2]], axis=-1)
    w = jnp.concatenate([w, jnp.zeros_like(w)], axis=-1)
    return w.reshape(K, -1)


def mla_shared_kv(hb, w_kv_a, kv_norm_g, w_kv_b, cm, sm):
    T = hb.shape[0]
    tm = _tile(T, 512)
    w_r = w_kv_a[:, KV_LORA:]
    wa = jnp.concatenate([w_kv_a[:, :KV_LORA], _pad_rope_cols(w_r, False), _pad_rope_cols(w_r, True)],
                         axis=1).astype(BF16)
    wb = w_kv_b.reshape(KV_LORA, MLA_HEADS, QK_NOPE + V_HEAD)
    wkn = wb[:, :, :QK_NOPE].reshape(KV_LORA, MLA_HEADS * QK_NOPE).astype(BF16)
    wv = wb[:, :, QK_NOPE:].reshape(KV_LORA, MLA_HEADS * V_HEAD).astype(BF16)
    full = lambda shape: pl.BlockSpec(shape, lambda i: (0, 0))
    return pl.pallas_call(
        _mla_kv_kernel,
        grid=(T // tm,),
        in_specs=[pl.BlockSpec((tm, D_MODEL), lambda i: (i, 0)), full(wa.shape), full((1, KV_LORA)),
                  full(wkn.shape), full(wv.shape),
                  pl.BlockSpec((tm, LANES), lambda i: (i, 0)), pl.BlockSpec((tm, LANES), lambda i: (i, 0))],
        out_specs=[pl.BlockSpec((tm, MLA_HEADS * 2 * LANES), lambda i: (i, 0)),
                   pl.BlockSpec((tm, MLA_HEADS * V_HEAD), lambda i: (i, 0))],
        out_shape=[jax.ShapeDtypeStruct((T, MLA_HEADS * 2 * LANES), BF16),
                   jax.ShapeDtypeStruct((T, MLA_HEADS * V_HEAD), BF16)],
        compiler_params=_params("arbitrary"),
        name="mla_kv",
    )(hb, wa, kv_norm_g.reshape(1, KV_LORA), wkn, wv, cm, sm)


def _mla_q_kernel(x_ref, wa_ref, g_ref, wb_ref, cm_ref, sm_ref, q_ref):
    scale = (QK_NOPE + QK_ROPE) ** -0.5 * LOG2_E
    qa = jnp.dot(x_ref[...], wa_ref[...], preferred_element_type=F32)
    qc = _rms_rows(qa, g_ref[...]).astype(BF16)
    q = jnp.dot(qc, wb_ref[...], preferred_element_type=F32)
    cm = cm_ref[...] * scale
    sm = sm_ref[...] * scale
    n = MLA_HEADS * LANES
    for h in range(MLA_HEADS):
        lo = h * LANES
        q_ref[:, 2 * lo:2 * lo + LANES] = (q[:, lo:lo + LANES] * scale).astype(BF16)
        rope = q[:, n + lo:n + lo + LANES] * cm + q[:, 2 * n + lo:2 * n + lo + LANES] * sm
        q_ref[:, 2 * lo + LANES:2 * lo + 2 * LANES] = rope.astype(BF16)


def mla_q(hb, w_q_a, q_norm_g, w_q_b, cm, sm):
    T = hb.shape[0]
    tm = _tile(T, 512)
    wq = w_q_b.reshape(Q_LORA, MLA_HEADS, QK_NOPE + QK_ROPE)
    w_n = wq[:, :, :QK_NOPE].reshape(Q_LORA, MLA_HEADS * QK_NOPE)
    w_r = wq[:, :, QK_NOPE:].reshape(Q_LORA, MLA_HEADS * QK_ROPE)
    wb = jnp.concatenate([w_n, _pad_rope_cols(w_r, False), _pad_rope_cols(w_r, True)], axis=1).astype(BF16)
    full = lambda shape: pl.BlockSpec(shape, lambda i: (0, 0))
    return pl.pallas_call(
        _mla_q_kernel,
        grid=(T // tm,),
        in_specs=[pl.BlockSpec((tm, D_MODEL), lambda i: (i, 0)), full((D_MODEL, Q_LORA)), full((1, Q_LORA)),
                  full(wb.shape),
                  pl.BlockSpec((tm, LANES), lambda i: (i, 0)), pl.BlockSpec((tm, LANES), lambda i: (i, 0))],
        out_specs=pl.BlockSpec((tm, MLA_HEADS * 2 * LANES), lambda i: (i, 0)),
        out_shape=jax.ShapeDtypeStruct((T, MLA_HEADS * 2 * LANES), BF16),
        compiler_params=_params("arbitrary"),
        name="mla_q",
    )(hb, w_q_a.astype(BF16), q_norm_g.reshape(1, Q_LORA), wb, cm, sm)


def _flash_kernel(q_ref, k_ref, v_ref, o_ref, m_ref, l_ref, acc_ref, *, blk):
    i = pl.program_id(2)
    q = q_ref[...]
    m_ref[...] = jnp.full_like(m_ref, -jnp.inf)
    l_ref[...] = jnp.zeros_like(l_ref)
    acc_ref[...] = jnp.zeros_like(acc_ref)

    def step(j, masked):
        start = pl.multiple_of(j * blk, blk)
        kj = k_ref[pl.ds(start, blk), :]
        vj = v_ref[pl.ds(start, blk), :]
        s = lax.dot_general(q, kj, NT_DIMS, preferred_element_type=F32)
        if masked:
            r = lax.broadcasted_iota(jnp.int32, (blk, blk), 0)
            c = lax.broadcasted_iota(jnp.int32, (blk, blk), 1)
            s = jnp.where(c <= r, s, -jnp.inf)
        m_prev = m_ref[...]
        m_new = jnp.maximum(m_prev, jnp.max(s, axis=-1, keepdims=True))
        alpha = jnp.exp2(m_prev - m_new)
        p = jnp.exp2(s - pltpu.repeat(m_new, blk // LANES, axis=1))
        l_ref[...] = alpha * l_ref[...] + jnp.sum(p, axis=-1, keepdims=True)
        acc_ref[...] = alpha * acc_ref[...] + jnp.dot(p.astype(BF16), vj, preferred_element_type=F32)
        m_ref[...] = m_new

    def pair(j, carry):
        step(2 * j, False)
        step(2 * j + 1, False)
        return carry

    lax.fori_loop(0, i // 2, pair, 0)

    @pl.when(i % 2 == 1)
    def _():
        step(i - 1, False)

    step(i, True)
    o_ref[...] = (acc_ref[...] / l_ref[...]).astype(o_ref.dtype)


def flash_attention(q, k, v, B, S):
    T = B * S
    blk = _tile(S, ATTN_BLOCK)
    nq = S // blk
    return pl.pallas_call(
        functools.partial(_flash_kernel, blk=blk),
        grid=(B, MLA_HEADS, nq),
        in_specs=[pl.BlockSpec((blk, 2 * LANES), lambda b, h, i: (b * nq + i, h)),
                  pl.BlockSpec((S, 2 * LANES), lambda b, h, i: (b, h)),
                  pl.BlockSpec((S, V_HEAD), lambda b, h, i: (b, h))],
        out_specs=pl.BlockSpec((blk, V_HEAD), lambda b, h, i: (b * nq + i, h)),
        out_shape=jax.ShapeDtypeStruct((T, MLA_HEADS * V_HEAD), BF16),
        scratch_shapes=[pltpu.VMEM((blk, LANES), F32), pltpu.VMEM((blk, LANES), F32),
                        pltpu.VMEM((blk, V_HEAD), F32)],
        compiler_params=_params("arbitrary", "arbitrary", "arbitrary"),
        name="mla_flash",
    )(q, k, v)


def kernel(x, p, positions, ret_w_in, ret_gn_g, ret_gn_b, ret_w_out, mla_w_kv_a, mla_kv_norm_g, mla_w_kv_b,
           mla_w_q_a, mla_q_norm_g, mla_w_q_b, mla_w_o, ln_mix_g, ln_mix_b, ln_ffn_g, ln_ffn_b,
           moe_w_router, moe_b_router, moe_w_gate_up, moe_b_gate_up, moe_w_down, moe_b_down,
           ple_w_gate, ple_w_proj):
    B, S, D = x.shape
    T = B * S
    cos_r, sin_r, cos_m, sin_m = rope_tables(positions)
    h = x.reshape(T, D)
    hb = h.astype(BF16)
    kv = None
    for i in range(DEPTH):
        if i < N_A:
            qk, vg = ret_projections(hb, ret_w_in[i].astype(BF16), cos_r, sin_r)
            z = retention(qk, vg, ret_gn_g[i], ret_gn_b[i], B, S)
            w_out = ret_w_out[i].astype(BF16)
        else:
            if kv is None:
                kv = mla_shared_kv(hb, mla_w_kv_a, mla_kv_norm_g, mla_w_kv_b, cos_m, sin_m)
            j = i - N_A
            q = mla_q(hb, mla_w_q_a[j], mla_q_norm_g[j], mla_w_q_b[j], cos_m, sin_m)
            z = flash_attention(q, kv[0], kv[1], B, S)
            w_out = mla_w_o[j].astype(BF16)
        h1 = out_proj_ln(z, w_out, h, ln_mix_g[i], ln_mix_b[i])
        ys, dest8, gates = moe_layer(h1, i, moe_w_router[i], moe_b_router[i], moe_w_gate_up, moe_b_gate_up,
                                     moe_w_down, moe_b_down)
        h, hb = ffn_ln_ple(ys, dest8, gates, h1, p[i].reshape(T, PLE_DIM), ln_ffn_g[i], ln_ffn_b[i],
                           ple_w_gate[i].astype(BF16), ple_w_proj[i].astype(BF16))
    return h.reshape(B, S, D)
```

```python
import functools

import jax
import jax.numpy as jnp
from jax import lax
from jax.experimental import pallas as pl
from jax.experimental.pallas import tpu as pltpu

D_MODEL = 1024
DEPTH = 4
N_A = DEPTH // 2
RET_HEADS = 4
RET_QK_DIM = D_MODEL // RET_HEADS
RET_V_DIM = 2 * RET_QK_DIM
GN_EPS = 1e-6
MLA_HEADS = 8
QK_NOPE = 128
QK_ROPE = 64
V_HEAD = 128
Q_LORA = 256
KV_LORA = 256
RMS_EPS = 1e-6
ROPE_THETA = 10000.0
N_EXPERTS = 32
TOP_K = 4
D_FF = D_MODEL
SWIGLU_LIMIT = 7.0
SWIGLU_ALPHA = 1.702
PLE_DIM = 256
DN_ALPHA = (2 * DEPTH) ** 0.25
LN_EPS = 1e-5
LOG2_E = 1.4426950408889634

LANES = 128
ROW_CHUNKS = D_MODEL // LANES
VMEM_LIMIT_BYTES = 56 * 1024 * 1024

RET_CHUNK = 256
RET_BLOCK = 1024
ATTN_BLOCK = 512
MOE_BLOCK = 512
DISPATCH_TOKENS = 1024

F32 = jnp.float32
BF16 = jnp.bfloat16
NT_DIMS = (((1,), (1,)), ((), ()))
TN_DIMS = (((0,), (0,)), ((), ()))


def _tile(n, pref):
    t = min(n, pref)
    assert n % t == 0, (n, pref)
    return t


def _params(*sem):
    return pltpu.CompilerParams(dimension_semantics=sem, vmem_limit_bytes=VMEM_LIMIT_BYTES)


def _layer_norm_rows(u, g, b):
    mu = jnp.mean(u, axis=-1, keepdims=True)
    d = u - mu
    var = jnp.mean(d * d, axis=-1, keepdims=True)
    return d * lax.rsqrt(var + LN_EPS) * g + b


def _rope_table_kernel(pos_ref, inv_r_ref, inv_m_ref, sign_m_ref, cr_ref, sr_ref, cm_ref, sm_ref):
    pos = pos_ref[...].astype(F32)
    ang_r = pos * inv_r_ref[...]
    cr_ref[...] = jnp.cos(ang_r)
    sr_ref[...] = jnp.sin(ang_r)
    ang_m = pos * inv_m_ref[...]
    cm_ref[...] = jnp.cos(ang_m)
    sm_ref[...] = jnp.sin(ang_m) * sign_m_ref[...]


def rope_tables(positions):
    T = positions.size
    tm = _tile(T, 1024)
    pos = positions.reshape(T, 1)
    half_r = RET_QK_DIM // 2
    inv_r = ROPE_THETA ** (-jnp.arange(0, RET_QK_DIM, 2, dtype=F32) / RET_QK_DIM)
    inv_m = ROPE_THETA ** (-jnp.arange(0, QK_ROPE, 2, dtype=F32) / QK_ROPE)
    half_m = QK_ROPE // 2
    lane = jnp.arange(LANES)
    inv_m = inv_m[lane % half_m]
    sign_m = jnp.where((lane % QK_ROPE) < half_m, -1.0, 1.0).astype(F32)
    assert half_r == LANES
    row = lambda v: v.reshape(1, LANES)
    vec_spec = pl.BlockSpec((1, LANES), lambda i: (0, 0))
    tab_spec = pl.BlockSpec((tm, LANES), lambda i: (i, 0))
    out = jax.ShapeDtypeStruct((T, LANES), F32)
    return pl.pallas_call(
        _rope_table_kernel,
        grid=(T // tm,),
        in_specs=[pl.BlockSpec((tm, 1), lambda i: (i, 0)), vec_spec, vec_spec, vec_spec],
        out_specs=[tab_spec] * 4,
        out_shape=[out] * 4,
        compiler_params=_params("arbitrary"),
        name="rope_tables",
    )(pos, row(inv_r), row(inv_m), row(sign_m))


def _ret_qk_kernel(x_ref, w_ref, cos_ref, sin_ref, o_ref):
    acc = jnp.dot(x_ref[...], w_ref[...], preferred_element_type=F32)
    scale = jnp.where(pl.program_id(0) == 1, RET_QK_DIM ** -0.5, 1.0).astype(F32)
    c = cos_ref[...] * scale
    s = sin_ref[...] * scale
    half = RET_QK_DIM // 2
    for h in range(RET_HEADS):
        lo = h * RET_QK_DIM
        x1 = acc[:, lo:lo + half]
        x2 = acc[:, lo + half:lo + 2 * half]
        o_ref[:, lo:lo + half] = (x1 * c - x2 * s).astype(o_ref.dtype)
        o_ref[:, lo + half:lo + 2 * half] = (x1 * s + x2 * c).astype(o_ref.dtype)


def _matmul_kernel(x_ref, w_ref, o_ref):
    o_ref[...] = jnp.dot(x_ref[...], w_ref[...], preferred_element_type=F32).astype(o_ref.dtype)


def ret_projections(hb, w_in_bf, cos_r, sin_r):
    T = hb.shape[0]
    tm = _tile(T, 1024)
    tn = D_MODEL
    n_qk = 2 * RET_HEADS * RET_QK_DIM // tn
    n_vg = 2 * RET_HEADS * RET_V_DIM // tn
    x_spec = pl.BlockSpec((tm, D_MODEL), lambda j, i: (i, 0))
    tab_spec = pl.BlockSpec((tm, LANES), lambda j, i: (i, 0))
    qk = pl.pallas_call(
        _ret_qk_kernel,
        grid=(n_qk, T // tm),
        in_specs=[x_spec, pl.BlockSpec((D_MODEL, tn), lambda j, i: (0, j)), tab_spec, tab_spec],
        out_specs=pl.BlockSpec((tm, tn), lambda j, i: (i, j)),
        out_shape=jax.ShapeDtypeStruct((T, n_qk * tn), BF16),
        compiler_params=_params("arbitrary", "arbitrary"),
        name="ret_qk_proj",
    )(hb, w_in_bf, cos_r, sin_r)
    vg = pl.pallas_call(
        _matmul_kernel,
        grid=(n_vg, T // tm),
        in_specs=[x_spec, pl.BlockSpec((D_MODEL, tn), lambda j, i: (0, j + n_qk))],
        out_specs=pl.BlockSpec((tm, tn), lambda j, i: (i, j)),
        out_shape=jax.ShapeDtypeStruct((T, n_vg * tn), BF16),
        compiler_params=_params("arbitrary", "arbitrary"),
        name="ret_vg_proj",
    )(hb, w_in_bf)
    return qk, vg


def _retention_kernel(q_ref, k_ref, v_ref, g_ref, dm_ref, dq_ref, dk_ref, dc_ref, gng_ref, gnb_ref,
                      z_ref, state_ref, *, chunk, n_chunks):
    @pl.when(pl.program_id(2) == 0)
    def _():
        state_ref[...] = jnp.zeros_like(state_ref)

    dm = dm_ref[0]
    dq = dq_ref[0]
    dk = dk_ref[0]
    dc = dc_ref[0]
    gng = gng_ref[...]
    gnb = gnb_ref[...]
    for c in range(n_chunks):
        rows = slice(c * chunk, (c + 1) * chunk)
        q = q_ref[rows, :]
        k = k_ref[rows, :]
        v = v_ref[rows, :]
        st = state_ref[...]
        inner = lax.dot_general(q, k, NT_DIMS, preferred_element_type=F32) * dm
        y = jnp.dot(inner.astype(BF16), v, preferred_element_type=F32)
        y = y + jnp.dot(q, st.astype(BF16), preferred_element_type=F32) * dq
        kd = (k.astype(F32) * dk).astype(BF16)
        state_ref[...] = st * dc + lax.dot_general(kd, v, TN_DIMS, preferred_element_type=F32)
        mu = jnp.mean(y, axis=-1, keepdims=True)
        d = y - mu
        var = jnp.mean(d * d, axis=-1, keepdims=True)
        yn = d * lax.rsqrt(var + GN_EPS) * gng + gnb
        g = g_ref[rows, :].astype(F32)
        z_ref[rows, :] = (g * jax.nn.sigmoid(g) * yn).astype(z_ref.dtype)


def retention(qk, vg, gn_g, gn_b, B, S):
    T = B * S
    H, dk, dv = RET_HEADS, RET_QK_DIM, RET_V_DIM
    C = _tile(S, RET_CHUNK)
    L = _tile(S, RET_BLOCK)
    nl = S // L
    log_g = jnp.log(1.0 - 2.0 ** (-5.0 - jnp.arange(H, dtype=F32)))
    idx = jnp.arange(C, dtype=F32)
    diff = idx[:, None] - idx[None, :]
    dm = jnp.where(diff >= 0, jnp.exp(log_g[:, None, None] * jnp.maximum(diff, 0.0)), 0.0)
    dq = jnp.exp(log_g[:, None] * (idx + 1.0))[:, :, None]
    dkk = jnp.exp(log_g[:, None] * (C - 1.0 - idx))[:, :, None]
    dc = jnp.broadcast_to(jnp.exp(log_g * C)[:, None, None], (H, 1, dv))
    kern = functools.partial(_retention_kernel, chunk=C, n_chunks=L // C)
    return pl.pallas_call(
        kern,
        grid=(B, H, nl),
        in_specs=[
            pl.BlockSpec((L, dk), lambda b, h, n: (b * nl + n, h)),
            pl.BlockSpec((L, dk), lambda b, h, n: (b * nl + n, H + h)),
            pl.BlockSpec((L, dv), lambda b, h, n: (b * nl + n, h)),
            pl.BlockSpec((L, dv), lambda b, h, n: (b * nl + n, H + h)),
            pl.BlockSpec((1, C, C), lambda b, h, n: (h, 0, 0)),
            pl.BlockSpec((1, C, 1), lambda b, h, n: (h, 0, 0)),
            pl.BlockSpec((1, C, 1), lambda b, h, n: (h, 0, 0)),
            pl.BlockSpec((1, 1, dv), lambda b, h, n: (h, 0, 0)),
            pl.BlockSpec((1, dv), lambda b, h, n: (0, h)),
            pl.BlockSpec((1, dv), lambda b, h, n: (0, h)),
        ],
        out_specs=pl.BlockSpec((L, dv), lambda b, h, n: (b * nl + n, h)),
        out_shape=jax.ShapeDtypeStruct((T, H * dv), BF16),
        scratch_shapes=[pltpu.VMEM((dk, dv), F32)],
        compiler_params=_params("arbitrary", "arbitrary", "arbitrary"),
        name="retention",
    )(qk, qk, vg, vg, dm, dq, dkk, dc, gn_g.reshape(1, H * dv), gn_b.reshape(1, H * dv))


def _out_ln_kernel(z_ref, w_ref, h_ref, g_ref, b_ref, o_ref):
    mix = jnp.dot(z_ref[...], w_ref[...], preferred_element_type=F32)
    u = DN_ALPHA * h_ref[...] + mix
    o_ref[...] = _layer_norm_rows(u, g_ref[...], b_ref[...])


def out_proj_ln(z, w_bf, h, ln_g, ln_b):
    T, K = z.shape
    tm = _tile(T, 512)
    row_spec = pl.BlockSpec((tm, D_MODEL), lambda i: (i, 0))
    vec_spec = pl.BlockSpec((1, D_MODEL), lambda i: (0, 0))
    return pl.pallas_call(
        _out_ln_kernel,
        grid=(T // tm,),
        in_specs=[pl.BlockSpec((tm, K), lambda i: (i, 0)), pl.BlockSpec((K, D_MODEL), lambda i: (0, 0)),
                  row_spec, vec_spec, vec_spec],
        out_specs=row_spec,
        out_shape=jax.ShapeDtypeStruct((T, D_MODEL), F32),
        compiler_params=_params("arbitrary"),
        name="out_proj_ln",
    )(z, w_bf, h, ln_g.reshape(1, D_MODEL), ln_b.reshape(1, D_MODEL))


def _router_kernel(h_ref, w_ref, b_ref, slab_ref, cnt_ref, carry_ref, *, tm):
    @pl.when(pl.program_id(0) == 0)
    def _():
        carry_ref[...] = jnp.zeros_like(carry_ref)

    logits = jnp.dot(h_ref[...], w_ref[...], preferred_element_type=F32,
                     precision=lax.Precision.HIGHEST) + b_ref[...]
    e_iota = lax.broadcasted_iota(jnp.int32, (tm, N_EXPERTS), 1).astype(F32)
    work = logits
    onehot = jnp.zeros((tm, N_EXPERTS), F32)
    top_vals, top_idx = [], []
    for _ in range(TOP_K):
        m = jnp.max(work, axis=-1, keepdims=True)
        idx = jnp.min(jnp.where(work == m, e_iota, float(N_EXPERTS)), axis=-1, keepdims=True)
        top_vals.append(m)
        top_idx.append(idx)
        hit = e_iota == idx
        work = jnp.where(hit, -jnp.inf, work)
        onehot = jnp.where(hit, 1.0, onehot)
    exps = [jnp.exp(v - top_vals[0]) for v in top_vals]
    denom = exps[0] + exps[1] + exps[2] + exps[3]
    gates = [e / denom for e in exps]

    r = lax.broadcasted_iota(jnp.int32, (tm, tm), 0)
    c = lax.broadcasted_iota(jnp.int32, (tm, tm), 1)
    tri = jnp.where(c < r, 1.0, 0.0).astype(BF16)
    before = jnp.dot(tri, onehot.astype(BF16), preferred_element_type=F32) + carry_ref[0:1, 0:N_EXPERTS]
    ranks = [jnp.sum(jnp.where(e_iota == idx, before, 0.0), axis=-1, keepdims=True) for idx in top_idx]

    lane = lax.broadcasted_iota(jnp.int32, (tm, LANES), 1)
    slab = jnp.zeros((tm, LANES), F32)
    for k in range(TOP_K):
        slab = jnp.where(lane == k, top_idx[k], slab)
        slab = jnp.where(lane == TOP_K + k, gates[k], slab)
        slab = jnp.where(lane == 2 * TOP_K + k, ranks[k], slab)
    slab_ref[...] = slab

    carry_ref[0:1, 0:N_EXPERTS] = carry_ref[0:1, 0:N_EXPERTS] + jnp.sum(onehot, axis=0, keepdims=True)
    cnt_ref[...] = carry_ref[...]


def router(h, w_router, b_router):
    T = h.shape[0]
    tm = _tile(T, 512)
    slab, cnt = pl.pallas_call(
        functools.partial(_router_kernel, tm=tm),
        grid=(T // tm,),
        in_specs=[pl.BlockSpec((tm, D_MODEL), lambda i: (i, 0)),
                  pl.BlockSpec((D_MODEL, N_EXPERTS), lambda i: (0, 0)),
                  pl.BlockSpec((1, N_EXPERTS), lambda i: (0, 0))],
        out_specs=[pl.BlockSpec((tm, LANES), lambda i: (i, 0)), pl.BlockSpec((8, LANES), lambda i: (0, 0))],
        out_shape=[jax.ShapeDtypeStruct((T, LANES), F32), jax.ShapeDtypeStruct((8, LANES), F32)],
        scratch_shapes=[pltpu.VMEM((8, LANES), F32)],
        compiler_params=_params("arbitrary"),
        name="router",
    )(h, w_router, b_router.reshape(1, N_EXPERTS))
    idx = slab[:, 0:TOP_K].astype(jnp.int32)
    gates = slab[:, TOP_K:2 * TOP_K]
    rank = slab[:, 2 * TOP_K:3 * TOP_K].astype(jnp.int32)
    counts = cnt[0, :N_EXPERTS].astype(jnp.int32)
    return idx, gates, rank, counts


def _rows_to_tiles(dst_ref, x, n):
    for c in range(ROW_CHUNKS):
        dst_ref[pl.ds(c, n, stride=ROW_CHUNKS), :] = x[:, c * LANES:(c + 1) * LANES]


def _tiles_to_rows(src_ref, n):
    return jnp.concatenate([src_ref[pl.ds(c, n, stride=ROW_CHUNKS), :] for c in range(ROW_CHUNKS)], axis=1)


def _dispatch_kernel(zfill_ref, rsrc_ref, rdst_ref, rlen_ref, q_ref, h_ref, xs_ref,
                     rows, stage, sem, zsem, *, tt, bm):
    i = pl.program_id(0)

    def for_each_run_piece(step, fn):
        def per_expert(e, carry):
            n = rlen_ref[step * N_EXPERTS + e]
            src = rsrc_ref[step * N_EXPERTS + e]
            dst = rdst_ref[step * N_EXPERTS + e]
            done = jnp.int32(0)
            piece = tt
            while piece >= 1:
                take = (n & piece) != 0
                size = piece * ROW_CHUNKS

                @pl.when(take)
                def _(done=done, size=size):
                    s = pl.multiple_of(src + done, ROW_CHUNKS)
                    d = pl.multiple_of(dst + done, ROW_CHUNKS)
                    fn(pltpu.make_async_copy(stage.at[pl.ds(s, size)], xs_ref.at[pl.ds(d, size)], sem))

                done = done + jnp.where(take, size, 0)
                piece //= 2
            return carry
        lax.fori_loop(0, N_EXPERTS, per_expert, 0)

    @pl.when(i == 0)
    def _():
        zeros = stage.at[pl.ds(0, bm * ROW_CHUNKS)]
        zeros[...] = jnp.zeros_like(zeros)

        def zero_copy(e):
            start = pl.multiple_of(zfill_ref[e], bm * ROW_CHUNKS)
            return pltpu.make_async_copy(zeros, xs_ref.at[pl.ds(start, bm * ROW_CHUNKS)], zsem)

        for e in range(N_EXPERTS):
            @pl.when(zfill_ref[e] >= 0)
            def _():
                zero_copy(e).start()
        for e in range(N_EXPERTS):
            @pl.when(zfill_ref[e] >= 0)
            def _():
                zero_copy(e).wait()

    @pl.when(i > 0)
    def _():
        for_each_run_piece(i - 1, lambda cp: cp.wait())

    _rows_to_tiles(rows, h_ref[...], tt)

    def sort_rows(t, carry):
        row = rows[pl.ds(pl.multiple_of(t * ROW_CHUNKS, ROW_CHUNKS), ROW_CHUNKS), :]
        for k in range(TOP_K):
            q = pl.multiple_of(q_ref[0, 0, t * TOP_K + k], ROW_CHUNKS)
            stage[pl.ds(q, ROW_CHUNKS), :] = row
        return carry

    lax.fori_loop(0, tt, sort_rows, 0)
    for_each_run_piece(i, lambda cp: cp.start())

    @pl.when(i == pl.num_programs(0) - 1)
    def _():
        for_each_run_piece(i, lambda cp: cp.wait())


def dispatch_rows(h, stage_pos8, run_src8, run_dst8, run_len, zfill8, n_slots, bm):
    T = h.shape[0]
    tt = _tile(T, DISPATCH_TOKENS)
    pos3 = stage_pos8.reshape(T // tt, 1, tt * TOP_K)
    assert tt * TOP_K >= bm
    return pl.pallas_call(
        functools.partial(_dispatch_kernel, tt=tt, bm=bm),
        grid_spec=pltpu.PrefetchScalarGridSpec(
            num_scalar_prefetch=4,
            grid=(T // tt,),
            in_specs=[pl.BlockSpec((1, 1, tt * TOP_K), lambda i, *_: (i, 0, 0), memory_space=pltpu.SMEM),
                      pl.BlockSpec((tt, D_MODEL), lambda i, *_: (i, 0))],
            out_specs=pl.BlockSpec(memory_space=pl.ANY),
            scratch_shapes=[pltpu.VMEM((tt * ROW_CHUNKS, LANES), F32),
                            pltpu.VMEM((tt * TOP_K * ROW_CHUNKS, LANES), F32),
                            pltpu.SemaphoreType.DMA, pltpu.SemaphoreType.DMA],
        ),
        out_shape=jax.ShapeDtypeStruct((n_slots * ROW_CHUNKS, LANES), F32),
        compiler_params=_params("arbitrary"),
        name="moe_dispatch",
    )(zfill8, run_src8, run_dst8, run_len, pos3, h)


def _moe_mlp_kernel(be_ref, na_ref, x_ref, wgu_ref, bgu_ref, wd_ref, bd_ref, y_ref, wgu_bf, wd_bf, *, bm):
    i = pl.program_id(0)
    prev = be_ref[jnp.maximum(i - 1, 0)]
    active = i < na_ref[0]

    @pl.when(jnp.logical_and(active, jnp.logical_or(i == 0, be_ref[i] != prev)))
    def _():
        wgu_bf[...] = wgu_ref[...].astype(BF16)
        wd_bf[...] = wd_ref[...].astype(BF16)

    @pl.when(active)
    def _():
        x = _tiles_to_rows(x_ref, bm).astype(BF16)
        hgu = jnp.dot(x, wgu_bf[...], preferred_element_type=F32) + bgu_ref[...]
        gate = jnp.minimum(hgu[:, :D_FF], SWIGLU_LIMIT)
        up = jnp.clip(hgu[:, D_FF:], -SWIGLU_LIMIT, SWIGLU_LIMIT)
        act = gate * jax.nn.sigmoid(SWIGLU_ALPHA * gate) * (up + 1.0)
        y = jnp.dot(act.astype(BF16), wd_bf[...], preferred_element_type=F32) + bd_ref[...]
        _rows_to_tiles(y_ref, y, bm)


def moe_mlp(xs, block_expert, n_active, layer, w_gu, b_gu, w_d, b_d, bm):
    n_blocks = xs.shape[0] // (bm * ROW_CHUNKS)

    def row_map(i, be, na):
        return (jnp.minimum(i, na[0] - 1), 0)

    def w_map(i, be, na):
        return (layer, be[jnp.minimum(i, na[0] - 1)], 0, 0)

    def b_map(i, be, na):
        return (layer * N_EXPERTS + be[jnp.minimum(i, na[0] - 1)], 0, 0)

    return pl.pallas_call(
        functools.partial(_moe_mlp_kernel, bm=bm),
        grid_spec=pltpu.PrefetchScalarGridSpec(
            num_scalar_prefetch=2,
            grid=(n_blocks,),
            in_specs=[pl.BlockSpec((bm * ROW_CHUNKS, LANES), row_map),
                      pl.BlockSpec((None, None, D_MODEL, 2 * D_FF), w_map),
                      pl.BlockSpec((None, 1, 2 * D_FF), b_map),
                      pl.BlockSpec((None, None, D_FF, D_MODEL), w_map),
                      pl.BlockSpec((None, 1, D_MODEL), b_map)],
            out_specs=pl.BlockSpec((bm * ROW_CHUNKS, LANES), row_map),
            scratch_shapes=[pltpu.VMEM((D_MODEL, 2 * D_FF), BF16), pltpu.VMEM((D_FF, D_MODEL), BF16)],
        ),
        out_shape=jax.ShapeDtypeStruct(xs.shape, F32),
        compiler_params=_params("arbitrary"),
        name="moe_mlp",
    )(block_expert, n_active, xs, w_gu, b_gu.reshape(DEPTH * N_EXPERTS, 1, 2 * D_FF), w_d,
      b_d.reshape(DEPTH * N_EXPERTS, 1, D_MODEL))


def _ffn_ln_ple_kernel(dcur_ref, dnext_ref, gates_ref, h_ref, p_ref, g_ref, b_ref, wg_ref, wp_ref, ys_ref,
                       o_ref, ob_ref, ybuf, sems, *, tm):
    i = pl.program_id(0)
    slot = i % 2

    def row_copy(dref, s, t, k):
        d = pl.multiple_of(dref[0, 0, t * TOP_K + k], ROW_CHUNKS)
        dst = ybuf.at[s, k, pl.ds(pl.multiple_of(t * ROW_CHUNKS, ROW_CHUNKS), ROW_CHUNKS)]
        return pltpu.make_async_copy(ys_ref.at[pl.ds(d, ROW_CHUNKS)], dst, sems.at[s])

    def issue(dref, s):
        def body(t, carry):
            for k in range(TOP_K):
                row_copy(dref, s, t, k).start()
            return carry
        lax.fori_loop(0, tm, body, 0)

    @pl.when(i == 0)
    def _():
        issue(dcur_ref, 0)

    @pl.when(i + 1 < pl.num_programs(0))
    def _():
        issue(dnext_ref, 1 - slot)

    for k in range(TOP_K):
        pltpu.make_async_copy(ys_ref.at[pl.ds(0, tm * ROW_CHUNKS)], ybuf.at[slot, k], sems.at[slot]).wait()

    gates = gates_ref[...]
    ffn = _tiles_to_rows(ybuf.at[slot, 0], tm) * gates[:, 0:1]
    for k in range(1, TOP_K):
        ffn = ffn + _tiles_to_rows(ybuf.at[slot, k], tm) * gates[:, k:k + 1]
    h2 = _layer_norm_rows(DN_ALPHA * h_ref[...] + ffn, g_ref[...], b_ref[...])
    gate = jax.nn.sigmoid(jnp.dot(h2.astype(BF16), wg_ref[...], preferred_element_type=F32))
    proj = jnp.dot(p_ref[...].astype(BF16), wp_ref[...], preferred_element_type=F32)
    out = h2 + gate * proj
    o_ref[...] = out
    ob_ref[...] = out.astype(BF16)


def ffn_ln_ple(ys, dest8, gates, h, p, ln_g, ln_b, wg_bf, wp_bf):
    T = h.shape[0]
    tm = _tile(T, 512)
    n = T // tm
    dest3 = dest8.reshape(n, 1, tm * TOP_K)
    row_spec = pl.BlockSpec((tm, D_MODEL), lambda i: (i, 0))
    vec_spec = pl.BlockSpec((1, D_MODEL), lambda i: (0, 0))
    return pl.pallas_call(
        functools.partial(_ffn_ln_ple_kernel, tm=tm),
        grid=(n,),
        in_specs=[pl.BlockSpec((1, 1, tm * TOP_K), lambda i: (i, 0, 0), memory_space=pltpu.SMEM),
                  pl.BlockSpec((1, 1, tm * TOP_K), lambda i: (jnp.minimum(i + 1, n - 1), 0, 0),
                               memory_space=pltpu.SMEM),
                  pl.BlockSpec((tm, TOP_K), lambda i: (i, 0)),
                  row_spec,
                  pl.BlockSpec((tm, PLE_DIM), lambda i: (i, 0)),
                  vec_spec, vec_spec,
                  pl.BlockSpec((D_MODEL, D_MODEL), lambda i: (0, 0)),
                  pl.BlockSpec((PLE_DIM, D_MODEL), lambda i: (0, 0)),
                  pl.BlockSpec(memory_space=pl.ANY)],
        out_specs=[row_spec, row_spec],
        out_shape=[jax.ShapeDtypeStruct((T, D_MODEL), F32), jax.ShapeDtypeStruct((T, D_MODEL), BF16)],
        scratch_shapes=[pltpu.VMEM((2, TOP_K, tm * ROW_CHUNKS, LANES), F32), pltpu.SemaphoreType.DMA((2,))],
        compiler_params=_params("arbitrary"),
        name="ffn_ln_ple",
    )(dest3, dest3, gates, h, p, ln_g.reshape(1, D_MODEL), ln_b.reshape(1, D_MODEL), wg_bf, wp_bf, ys)


def moe_layer(h1, layer, w_router, b_router, w_gu, b_gu, w_d, b_d):
    T = h1.shape[0]
    bm = MOE_BLOCK
    idx, gates, rank, counts = router(h1, w_router, b_router)
    padded = (counts + bm - 1) // bm * bm
    pend = jnp.cumsum(padded)
    pstart = pend - padded
    onehot = idx[..., None] == jnp.arange(N_EXPERTS, dtype=jnp.int32)
    dest = (jnp.sum(jnp.where(onehot, pstart, 0), axis=-1) + rank).astype(jnp.int32)
    dest8 = dest * ROW_CHUNKS
    n_slots = T * TOP_K + N_EXPERTS * bm
    n_blocks = n_slots // bm
    block_start = jnp.arange(n_blocks, dtype=jnp.int32) * bm
    block_expert = jnp.minimum(jnp.sum(block_start[:, None] >= pend[None, :], axis=1),
                               N_EXPERTS - 1).astype(jnp.int32)
    n_active = (pend[-1:] // bm).astype(jnp.int32)
    zfill8 = jnp.where(padded > 0, (pend - bm) * ROW_CHUNKS, -1).astype(jnp.int32)
    tt = _tile(T, DISPATCH_TOKENS)
    nt = T // tt
    oh_t = onehot.reshape(nt, tt * TOP_K, N_EXPERTS)
    tile_cnt = jnp.sum(oh_t, axis=1, dtype=jnp.int32)
    tile_base = jnp.cumsum(tile_cnt, axis=0) - tile_cnt
    stage_off = jnp.cumsum(tile_cnt, axis=1) - tile_cnt
    shift = jnp.sum(jnp.where(oh_t, (stage_off - tile_base)[:, None, :], 0), axis=-1)
    stage_pos8 = (shift + rank.reshape(nt, tt * TOP_K)).astype(jnp.int32) * ROW_CHUNKS
    run_src8 = (stage_off * ROW_CHUNKS).reshape(-1).astype(jnp.int32)
    run_dst8 = ((pstart[None, :] + tile_base) * ROW_CHUNKS).reshape(-1).astype(jnp.int32)
    xs = dispatch_rows(h1, stage_pos8, run_src8, run_dst8, tile_cnt.reshape(-1), zfill8, n_slots, bm)
    ys = moe_mlp(xs, block_expert, n_active, layer, w_gu, b_gu, w_d, b_d, bm)
    return ys, dest8, gates


def _rms_rows(x, g):
    return x * lax.rsqrt(jnp.mean(x * x, axis=-1, keepdims=True) + RMS_EPS) * g


def _mla_kv_kernel(x_ref, wa_ref, g_ref, wkn_ref, wv_ref, cm_ref, sm_ref, k_ref, v_ref):
    a = jnp.dot(x_ref[...], wa_ref[...], preferred_element_type=F32)
    c_kv = _rms_rows(a[:, :KV_LORA], g_ref[...]).astype(BF16)
    kr = a[:, KV_LORA:KV_LORA + LANES] * cm_ref[...] + a[:, KV_LORA + LANES:] * sm_ref[...]
    kr = kr.astype(BF16)
    kn = jnp.dot(c_kv, wkn_ref[...], preferred_element_type=F32)
    v_ref[...] = jnp.dot(c_kv, wv_ref[...], preferred_element_type=F32).astype(BF16)
    for h in range(MLA_HEADS):
        k_ref[:, 2 * h * LANES:(2 * h + 1) * LANES] = kn[:, h * QK_NOPE:(h + 1) * QK_NOPE].astype(BF16)
        k_ref[:, (2 * h + 1) * LANES:(2 * h + 2) * LANES] = kr


def _pad_rope_cols(w, swap):
    K = w.shape[0]
    w = w.reshape(K, -1, QK_ROPE)
    if swap:
        w = jnp.concatenate([w[..., QK_ROPE // 2:], w[..., :QK_ROPE // 2]], axis=-1)
    w = jnp.concatenate([w, jnp.zeros_like(w)], axis=-1)
    return w.reshape(K, -1)


def mla_shared_kv(hb, w_kv_a, kv_norm_g, w_kv_b, cm, sm):
    T = hb.shape[0]
    tm = _tile(T, 512)
    w_r = w_kv_a[:, KV_LORA:]
    wa = jnp.concatenate([w_kv_a[:, :KV_LORA], _pad_rope_cols(w_r, False), _pad_rope_cols(w_r, True)],
                         axis=1).astype(BF16)
    wb = w_kv_b.reshape(KV_LORA, MLA_HEADS, QK_NOPE + V_HEAD)
    wkn = wb[:, :, :QK_NOPE].reshape(KV_LORA, MLA_HEADS * QK_NOPE).astype(BF16)
    wv = wb[:, :, QK_NOPE:].reshape(KV_LORA, MLA_HEADS * V_HEAD).astype(BF16)
    full = lambda shape: pl.BlockSpec(shape, lambda i: (0, 0))
    return pl.pallas_call(
        _mla_kv_kernel,
        grid=(T // tm,),
        in_specs=[pl.BlockSpec((tm, D_MODEL), lambda i: (i, 0)), full(wa.shape), full((1, KV_LORA)),
                  full(wkn.shape), full(wv.shape),
                  pl.BlockSpec((tm, LANES), lambda i: (i, 0)), pl.BlockSpec((tm, LANES), lambda i: (i, 0))],
        out_specs=[pl.BlockSpec((tm, MLA_HEADS * 2 * LANES), lambda i: (i, 0)),
                   pl.BlockSpec((tm, MLA_HEADS * V_HEAD), lambda i: (i, 0))],
        out_shape=[jax.ShapeDtypeStruct((T, MLA_HEADS * 2 * LANES), BF16),
                   jax.ShapeDtypeStruct((T, MLA_HEADS * V_HEAD), BF16)],
        compiler_params=_params("arbitrary"),
        name="mla_kv",
    )(hb, wa, kv_norm_g.reshape(1, KV_LORA), wkn, wv, cm, sm)


def _mla_q_kernel(x_ref, wa_ref, g_ref, wb_ref, cm_ref, sm_ref, q_ref):
    scale = (QK_NOPE + QK_ROPE) ** -0.5 * LOG2_E
    qa = jnp.dot(x_ref[...], wa_ref[...], preferred_element_type=F32)
    qc = _rms_rows(qa, g_ref[...]).astype(BF16)
    q = jnp.dot(qc, wb_ref[...], preferred_element_type=F32)
    cm = cm_ref[...] * scale
    sm = sm_ref[...] * scale
    n = MLA_HEADS * LANES
    for h in range(MLA_HEADS):
        lo = h * LANES
        q_ref[:, 2 * lo:2 * lo + LANES] = (q[:, lo:lo + LANES] * scale).astype(BF16)
        rope = q[:, n + lo:n + lo + LANES] * cm + q[:, 2 * n + lo:2 * n + lo + LANES] * sm
        q_ref[:, 2 * lo + LANES:2 * lo + 2 * LANES] = rope.astype(BF16)


def mla_q(hb, w_q_a, q_norm_g, w_q_b, cm, sm):
    T = hb.shape[0]
    tm = _tile(T, 512)
    wq = w_q_b.reshape(Q_LORA, MLA_HEADS, QK_NOPE + QK_ROPE)
    w_n = wq[:, :, :QK_NOPE].reshape(Q_LORA, MLA_HEADS * QK_NOPE)
    w_r = wq[:, :, QK_NOPE:].reshape(Q_LORA, MLA_HEADS * QK_ROPE)
    wb = jnp.concatenate([w_n, _pad_rope_cols(w_r, False), _pad_rope_cols(w_r, True)], axis=1).astype(BF16)
    full = lambda shape: pl.BlockSpec(shape, lambda i: (0, 0))
    return pl.pallas_call(
        _mla_q_kernel,
        grid=(T // tm,),
        in_specs=[pl.BlockSpec((tm, D_MODEL), lambda i: (i, 0)), full((D_MODEL, Q_LORA)), full((1, Q_LORA)),
                  full(wb.shape),
                  pl.BlockSpec((tm, LANES), lambda i: (i, 0)), pl.BlockSpec((tm, LANES), lambda i: (i, 0))],
        out_specs=pl.BlockSpec((tm, MLA_HEADS * 2 * LANES), lambda i: (i, 0)),
        out_shape=jax.ShapeDtypeStruct((T, MLA_HEADS * 2 * LANES), BF16),
        compiler_params=_params("arbitrary"),
        name="mla_q",
    )(hb, w_q_a.astype(BF16), q_norm_g.reshape(1, Q_LORA), wb, cm, sm)


def _flash_kernel(q_ref, k_ref, v_ref, o_ref, m_ref, l_ref, acc_ref, *, blk):
    i = pl.program_id(2)
    q = q_ref[...]
    m_ref[...] = jnp.full_like(m_ref, -jnp.inf)
    l_ref[...] = jnp.zeros_like(l_ref)
    acc_ref[...] = jnp.zeros_like(acc_ref)

    def step(j, masked):
        start = pl.multiple_of(j * blk, blk)
        kj = k_ref[pl.ds(start, blk), :]
        vj = v_ref[pl.ds(start, blk), :]
        s = lax.dot_general(q, kj, NT_DIMS, preferred_element_type=F32)
        if masked:
            r = lax.broadcasted_iota(jnp.int32, (blk, blk), 0)
            c = lax.broadcasted_iota(jnp.int32, (blk, blk), 1)
            s = jnp.where(c <= r, s, -jnp.inf)
        m_prev = m_ref[...]
        m_new = jnp.maximum(m_prev, jnp.max(s, axis=-1, keepdims=True))
        alpha = jnp.exp2(m_prev - m_new)
        p = jnp.exp2(s - pltpu.repeat(m_new, blk // LANES, axis=1))
        l_ref[...] = alpha * l_ref[...] + jnp.sum(p, axis=-1, keepdims=True)
        acc_ref[...] = alpha * acc_ref[...] + jnp.dot(p.astype(BF16), vj, preferred_element_type=F32)
        m_ref[...] = m_new

    def pair(j, carry):
        step(2 * j, False)
        step(2 * j + 1, False)
        return carry

    lax.fori_loop(0, i // 2, pair, 0)

    @pl.when(i % 2 == 1)
    def _():
        step(i - 1, False)

    step(i, True)
    o_ref[...] = (acc_ref[...] / l_ref[...]).astype(o_ref.dtype)


def flash_attention(q, k, v, B, S):
    T = B * S
    blk = _tile(S, ATTN_BLOCK)
    nq = S // blk
    return pl.pallas_call(
        functools.partial(_flash_kernel, blk=blk),
        grid=(B, MLA_HEADS, nq),
        in_specs=[pl.BlockSpec((blk, 2 * LANES), lambda b, h, i: (b * nq + i, h)),
                  pl.BlockSpec((S, 2 * LANES), lambda b, h, i: (b, h)),
                  pl.BlockSpec((S, V_HEAD), lambda b, h, i: (b, h))],
        out_specs=pl.BlockSpec((blk, V_HEAD), lambda b, h, i: (b * nq + i, h)),
        out_shape=jax.ShapeDtypeStruct((T, MLA_HEADS * V_HEAD), BF16),
        scratch_shapes=[pltpu.VMEM((blk, LANES), F32), pltpu.VMEM((blk, LANES), F32),
                        pltpu.VMEM((blk, V_HEAD), F32)],
        compiler_params=_params("arbitrary", "arbitrary", "arbitrary"),
        name="mla_flash",
    )(q, k, v)


def kernel(x, p, positions, ret_w_in, ret_gn_g, ret_gn_b, ret_w_out, mla_w_kv_a, mla_kv_norm_g, mla_w_kv_b,
           mla_w_q_a, mla_q_norm_g, mla_w_q_b, mla_w_o, ln_mix_g, ln_mix_b, ln_ffn_g, ln_ffn_b,
           moe_w_router, moe_b_router, moe_w_gate_up, moe_b_gate_up, moe_w_down, moe_b_down,
           ple_w_gate, ple_w_proj):
    B, S, D = x.shape
    T = B * S
    cos_r, sin_r, cos_m, sin_m = rope_tables(positions)
    h = x.reshape(T, D)
    hb = h.astype(BF16)
    kv = None
    for i in range(DEPTH):
        if i < N_A:
            qk, vg = ret_projections(hb, ret_w_in[i].astype(BF16), cos_r, sin_r)
            z = retention(qk, vg, ret_gn_g[i], ret_gn_b[i], B, S)
            w_out = ret_w_out[i].astype(BF16)
        else:
            if kv is None:
                kv = mla_shared_kv(hb, mla_w_kv_a, mla_kv_norm_g, mla_w_kv_b, cos_m, sin_m)
            j = i - N_A
            q = mla_q(hb, mla_w_q_a[j], mla_q_norm_g[j], mla_w_q_b[j], cos_m, sin_m)
            z = flash_attention(q, kv[0], kv[1], B, S)
            w_out = mla_w_o[j].astype(BF16)
        h1 = out_proj_ln(z, w_out, h, ln_mix_g[i], ln_mix_b[i])
        ys, dest8, gates = moe_layer(h1, i, moe_w_router[i], moe_b_router[i], moe_w_gate_up, moe_b_gate_up,
                                     moe_w_down, moe_b_down)
        h, hb = ffn_ln_ple(ys, dest8, gates, h1, p[i].reshape(T, PLE_DIM), ln_ffn_g[i], ln_ffn_b[i],
                           ple_w_gate[i].astype(BF16), ple_w_proj[i].astype(BF16))
    return h.reshape(B, S, D)
```

```python
import functools

import jax
import jax.numpy as jnp
from jax import lax
from jax.experimental import pallas as pl
from jax.experimental.pallas import tpu as pltpu

D_MODEL = 1024
DEPTH = 4
N_A = DEPTH // 2
RET_HEADS = 4
RET_QK_DIM = D_MODEL // RET_HEADS
RET_V_DIM = 2 * RET_QK_DIM
GN_EPS = 1e-6
MLA_HEADS = 8
QK_NOPE = 128
QK_ROPE = 64
V_HEAD = 128
Q_LORA = 256
KV_LORA = 256
RMS_EPS = 1e-6
ROPE_THETA = 10000.0
N_EXPERTS = 32
TOP_K = 4
D_FF = D_MODEL
SWIGLU_LIMIT = 7.0
SWIGLU_ALPHA = 1.702
PLE_DIM = 256
DN_ALPHA = (2 * DEPTH) ** 0.25
LN_EPS = 1e-5
LOG2_E = 1.4426950408889634

LANES = 128
ROW_CHUNKS = D_MODEL // LANES
VMEM_LIMIT_BYTES = 56 * 1024 * 1024

RET_CHUNK = 256
RET_BLOCK = 1024
ATTN_BLOCK = 512
MOE_BLOCK = 512
DISPATCH_TOKENS = 512

F32 = jnp.float32
BF16 = jnp.bfloat16
NT_DIMS = (((1,), (1,)), ((), ()))
TN_DIMS = (((0,), (0,)), ((), ()))


def _tile(n, pref):
    t = min(n, pref)
    assert n % t == 0, (n, pref)
    return t


def _params(*sem):
    return pltpu.CompilerParams(dimension_semantics=sem, vmem_limit_bytes=VMEM_LIMIT_BYTES)


def _layer_norm_rows(u, g, b):
    mu = jnp.mean(u, axis=-1, keepdims=True)
    d = u - mu
    var = jnp.mean(d * d, axis=-1, keepdims=True)
    return d * lax.rsqrt(var + LN_EPS) * g + b


def _rope_table_kernel(pos_ref, inv_r_ref, inv_m_ref, sign_m_ref, cr_ref, sr_ref, cm_ref, sm_ref):
    pos = pos_ref[...].astype(F32)
    ang_r = pos * inv_r_ref[...]
    cr_ref[...] = jnp.cos(ang_r)
    sr_ref[...] = jnp.sin(ang_r)
    ang_m = pos * inv_m_ref[...]
    cm_ref[...] = jnp.cos(ang_m)
    sm_ref[...] = jnp.sin(ang_m) * sign_m_ref[...]


def rope_tables(positions):
    T = positions.size
    tm = _tile(T, 1024)
    pos = positions.reshape(T, 1)
    half_r = RET_QK_DIM // 2
    inv_r = ROPE_THETA ** (-jnp.arange(0, RET_QK_DIM, 2, dtype=F32) / RET_QK_DIM)
    inv_m = ROPE_THETA ** (-jnp.arange(0, QK_ROPE, 2, dtype=F32) / QK_ROPE)
    half_m = QK_ROPE // 2
    lane = jnp.arange(LANES)
    inv_m = inv_m[lane % half_m]
    sign_m = jnp.where((lane % QK_ROPE) < half_m, -1.0, 1.0).astype(F32)
    assert half_r == LANES
    row = lambda v: v.reshape(1, LANES)
    vec_spec = pl.BlockSpec((1, LANES), lambda i: (0, 0))
    tab_spec = pl.BlockSpec((tm, LANES), lambda i: (i, 0))
    out = jax.ShapeDtypeStruct((T, LANES), F32)
    return pl.pallas_call(
        _rope_table_kernel,
        grid=(T // tm,),
        in_specs=[pl.BlockSpec((tm, 1), lambda i: (i, 0)), vec_spec, vec_spec, vec_spec],
        out_specs=[tab_spec] * 4,
        out_shape=[out] * 4,
        compiler_params=_params("arbitrary"),
        name="rope_tables",
    )(pos, row(inv_r), row(inv_m), row(sign_m))


def _ret_qk_kernel(x_ref, w_ref, cos_ref, sin_ref, o_ref):
    acc = jnp.dot(x_ref[...], w_ref[...], preferred_element_type=F32)
    scale = jnp.where(pl.program_id(0) == 1, RET_QK_DIM ** -0.5, 1.0).astype(F32)
    c = cos_ref[...] * scale
    s = sin_ref[...] * scale
    half = RET_QK_DIM // 2
    for h in range(RET_HEADS):
        lo = h * RET_QK_DIM
        x1 = acc[:, lo:lo + half]
        x2 = acc[:, lo + half:lo + 2 * half]
        o_ref[:, lo:lo + half] = (x1 * c - x2 * s).astype(o_ref.dtype)
        o_ref[:, lo + half:lo + 2 * half] = (x1 * s + x2 * c).astype(o_ref.dtype)


def _matmul_kernel(x_ref, w_ref, o_ref):
    o_ref[...] = jnp.dot(x_ref[...], w_ref[...], preferred_element_type=F32).astype(o_ref.dtype)


def ret_projections(hb, w_in_bf, cos_r, sin_r):
    T = hb.shape[0]
    tm = _tile(T, 1024)
    tn = D_MODEL
    n_qk = 2 * RET_HEADS * RET_QK_DIM // tn
    n_vg = 2 * RET_HEADS * RET_V_DIM // tn
    x_spec = pl.BlockSpec((tm, D_MODEL), lambda j, i: (i, 0))
    tab_spec = pl.BlockSpec((tm, LANES), lambda j, i: (i, 0))
    qk = pl.pallas_call(
        _ret_qk_kernel,
        grid=(n_qk, T // tm),
        in_specs=[x_spec, pl.BlockSpec((D_MODEL, tn), lambda j, i: (0, j)), tab_spec, tab_spec],
        out_specs=pl.BlockSpec((tm, tn), lambda j, i: (i, j)),
        out_shape=jax.ShapeDtypeStruct((T, n_qk * tn), BF16),
        compiler_params=_params("arbitrary", "arbitrary"),
        name="ret_qk_proj",
    )(hb, w_in_bf, cos_r, sin_r)
    vg = pl.pallas_call(
        _matmul_kernel,
        grid=(n_vg, T // tm),
        in_specs=[x_spec, pl.BlockSpec((D_MODEL, tn), lambda j, i: (0, j + n_qk))],
        out_specs=pl.BlockSpec((tm, tn), lambda j, i: (i, j)),
        out_shape=jax.ShapeDtypeStruct((T, n_vg * tn), BF16),
        compiler_params=_params("arbitrary", "arbitrary"),
        name="ret_vg_proj",
    )(hb, w_in_bf)
    return qk, vg


def _retention_kernel(q_ref, k_ref, v_ref, g_ref, dm_ref, dq_ref, dk_ref, dc_ref, gng_ref, gnb_ref,
                      z_ref, state_ref, *, chunk, n_chunks):
    @pl.when(pl.program_id(2) == 0)
    def _():
        state_ref[...] = jnp.zeros_like(state_ref)

    dm = dm_ref[0]
    dq = dq_ref[0]
    dk = dk_ref[0]
    dc = dc_ref[0]
    gng = gng_ref[...]
    gnb = gnb_ref[...]
    for c in range(n_chunks):
        rows = slice(c * chunk, (c + 1) * chunk)
        q = q_ref[rows, :]
        k = k_ref[rows, :]
        v = v_ref[rows, :]
        st = state_ref[...]
        inner = lax.dot_general(q, k, NT_DIMS, preferred_element_type=F32) * dm
        y = jnp.dot(inner.astype(BF16), v, preferred_element_type=F32)
        y = y + jnp.dot(q, st.astype(BF16), preferred_element_type=F32) * dq
        kd = (k.astype(F32) * dk).astype(BF16)
        state_ref[...] = st * dc + lax.dot_general(kd, v, TN_DIMS, preferred_element_type=F32)
        mu = jnp.mean(y, axis=-1, keepdims=True)
        d = y - mu
        var = jnp.mean(d * d, axis=-1, keepdims=True)
        yn = d * lax.rsqrt(var + GN_EPS) * gng + gnb
        g = g_ref[rows, :].astype(F32)
        z_ref[rows, :] = (g * jax.nn.sigmoid(g) * yn).astype(z_ref.dtype)


def retention(qk, vg, gn_g, gn_b, B, S):
    T = B * S
    H, dk, dv = RET_HEADS, RET_QK_DIM, RET_V_DIM
    C = _tile(S, RET_CHUNK)
    L = _tile(S, RET_BLOCK)
    nl = S // L
    log_g = jnp.log(1.0 - 2.0 ** (-5.0 - jnp.arange(H, dtype=F32)))
    idx = jnp.arange(C, dtype=F32)
    diff = idx[:, None] - idx[None, :]
    dm = jnp.where(diff >= 0, jnp.exp(log_g[:, None, None] * jnp.maximum(diff, 0.0)), 0.0)
    dq = jnp.exp(log_g[:, None] * (idx + 1.0))[:, :, None]
    dkk = jnp.exp(log_g[:, None] * (C - 1.0 - idx))[:, :, None]
    dc = jnp.broadcast_to(jnp.exp(log_g * C)[:, None, None], (H, 1, dv))
    kern = functools.partial(_retention_kernel, chunk=C, n_chunks=L // C)
    return pl.pallas_call(
        kern,
        grid=(B, H, nl),
        in_specs=[
            pl.BlockSpec((L, dk), lambda b, h, n: (b * nl + n, h)),
            pl.BlockSpec((L, dk), lambda b, h, n: (b * nl + n, H + h)),
            pl.BlockSpec((L, dv), lambda b, h, n: (b * nl + n, h)),
            pl.BlockSpec((L, dv), lambda b, h, n: (b * nl + n, H + h)),
            pl.BlockSpec((1, C, C), lambda b, h, n: (h, 0, 0)),
            pl.BlockSpec((1, C, 1), lambda b, h, n: (h, 0, 0)),
            pl.BlockSpec((1, C, 1), lambda b, h, n: (h, 0, 0)),
            pl.BlockSpec((1, 1, dv), lambda b, h, n: (h, 0, 0)),
            pl.BlockSpec((1, dv), lambda b, h, n: (0, h)),
            pl.BlockSpec((1, dv), lambda b, h, n: (0, h)),
        ],
        out_specs=pl.BlockSpec((L, dv), lambda b, h, n: (b * nl + n, h)),
        out_shape=jax.ShapeDtypeStruct((T, H * dv), BF16),
        scratch_shapes=[pltpu.VMEM((dk, dv), F32)],
        compiler_params=_params("arbitrary", "arbitrary", "arbitrary"),
        name="retention",
    )(qk, qk, vg, vg, dm, dq, dkk, dc, gn_g.reshape(1, H * dv), gn_b.reshape(1, H * dv))


def _out_ln_kernel(z_ref, w_ref, h_ref, g_ref, b_ref, o_ref):
    mix = jnp.dot(z_ref[...], w_ref[...], preferred_element_type=F32)
    u = DN_ALPHA * h_ref[...] + mix
    o_ref[...] = _layer_norm_rows(u, g_ref[...], b_ref[...])


def out_proj_ln(z, w_bf, h, ln_g, ln_b):
    T, K = z.shape
    tm = _tile(T, 512)
    row_spec = pl.BlockSpec((tm, D_MODEL), lambda i: (i, 0))
    vec_spec = pl.BlockSpec((1, D_MODEL), lambda i: (0, 0))
    return pl.pallas_call(
        _out_ln_kernel,
        grid=(T // tm,),
        in_specs=[pl.BlockSpec((tm, K), lambda i: (i, 0)), pl.BlockSpec((K, D_MODEL), lambda i: (0, 0)),
                  row_spec, vec_spec, vec_spec],
        out_specs=row_spec,
        out_shape=jax.ShapeDtypeStruct((T, D_MODEL), F32),
        compiler_params=_params("arbitrary"),
        name="out_proj_ln",
    )(z, w_bf, h, ln_g.reshape(1, D_MODEL), ln_b.reshape(1, D_MODEL))


def _router_kernel(h_ref, w_ref, b_ref, slab_ref, cnt_ref, carry_ref, *, tm):
    @pl.when(pl.program_id(0) == 0)
    def _():
        carry_ref[...] = jnp.zeros_like(carry_ref)

    x = h_ref[...]
    w = w_ref[...]
    x_hi = x.astype(BF16)
    x_lo = (x - x_hi.astype(F32)).astype(BF16)
    w_hi = w.astype(BF16)
    w_lo = (w - w_hi.astype(F32)).astype(BF16)
    logits = (jnp.dot(x_hi, w_hi, preferred_element_type=F32)
              + (jnp.dot(x_hi, w_lo, preferred_element_type=F32)
                 + jnp.dot(x_lo, w_hi, preferred_element_type=F32))) + b_ref[...]
    e_iota = lax.broadcasted_iota(jnp.int32, (tm, N_EXPERTS), 1).astype(F32)
    work = logits
    onehot = jnp.zeros((tm, N_EXPERTS), F32)
    top_vals, top_idx = [], []
    for _ in range(TOP_K):
        m = jnp.max(work, axis=-1, keepdims=True)
        idx = jnp.min(jnp.where(work == m, e_iota, float(N_EXPERTS)), axis=-1, keepdims=True)
        top_vals.append(m)
        top_idx.append(idx)
        hit = e_iota == idx
        work = jnp.where(hit, -jnp.inf, work)
        onehot = jnp.where(hit, 1.0, onehot)
    exps = [jnp.exp(v - top_vals[0]) for v in top_vals]
    denom = exps[0] + exps[1] + exps[2] + exps[3]
    gates = [e / denom for e in exps]

    r = lax.broadcasted_iota(jnp.int32, (tm, tm), 0)
    c = lax.broadcasted_iota(jnp.int32, (tm, tm), 1)
    tri = jnp.where(c < r, 1.0, 0.0).astype(BF16)
    before = jnp.dot(tri, onehot.astype(BF16), preferred_element_type=F32) + carry_ref[0:1, 0:N_EXPERTS]
    ranks = [jnp.sum(jnp.where(e_iota == idx, before, 0.0), axis=-1, keepdims=True) for idx in top_idx]

    lane = lax.broadcasted_iota(jnp.int32, (tm, LANES), 1)
    slab = jnp.zeros((tm, LANES), F32)
    for k in range(TOP_K):
        slab = jnp.where(lane == k, top_idx[k], slab)
        slab = jnp.where(lane == TOP_K + k, gates[k], slab)
        slab = jnp.where(lane == 2 * TOP_K + k, ranks[k], slab)
    slab_ref[...] = slab

    carry_ref[0:1, 0:N_EXPERTS] = carry_ref[0:1, 0:N_EXPERTS] + jnp.sum(onehot, axis=0, keepdims=True)
    cnt_ref[...] = carry_ref[...]


def router(h, w_router, b_router):
    T = h.shape[0]
    tm = _tile(T, 512)
    slab, cnt = pl.pallas_call(
        functools.partial(_router_kernel, tm=tm),
        grid=(T // tm,),
        in_specs=[pl.BlockSpec((tm, D_MODEL), lambda i: (i, 0)),
                  pl.BlockSpec((D_MODEL, N_EXPERTS), lambda i: (0, 0)),
                  pl.BlockSpec((1, N_EXPERTS), lambda i: (0, 0))],
        out_specs=[pl.BlockSpec((tm, LANES), lambda i: (i, 0)), pl.BlockSpec((8, LANES), lambda i: (0, 0))],
        out_shape=[jax.ShapeDtypeStruct((T, LANES), F32), jax.ShapeDtypeStruct((8, LANES), F32)],
        scratch_shapes=[pltpu.VMEM((8, LANES), F32)],
        compiler_params=_params("arbitrary"),
        name="router",
    )(h, w_router, b_router.reshape(1, N_EXPERTS))
    idx = slab[:, 0:TOP_K].astype(jnp.int32)
    gates = slab[:, TOP_K:2 * TOP_K]
    rank = slab[:, 2 * TOP_K:3 * TOP_K].astype(jnp.int32)
    counts = cnt[0, :N_EXPERTS].astype(jnp.int32)
    return idx, gates, rank, counts


def _rows_to_tiles(dst_ref, x, n):
    for c in range(ROW_CHUNKS):
        dst_ref[pl.ds(c, n, stride=ROW_CHUNKS), :] = x[:, c * LANES:(c + 1) * LANES]


def _tiles_to_rows(src_ref, n):
    return jnp.concatenate([src_ref[pl.ds(c, n, stride=ROW_CHUNKS), :] for c in range(ROW_CHUNKS)], axis=1)


def _dispatch_kernel(zfill_ref, rsrc_ref, rdst_ref, rlen_ref, q_ref, h_ref, xs_ref,
                     rows, stage, sems, zsem, *, tt, bm):
    i = pl.program_id(0)
    last = pl.num_programs(0) - 1
    slot = i % 2

    def for_each_run_piece(step, fn):
        buf = stage.at[step % 2]
        sem = sems.at[step % 2]

        def per_expert(e, carry):
            n = rlen_ref[step * N_EXPERTS + e]
            src = rsrc_ref[step * N_EXPERTS + e]
            dst = rdst_ref[step * N_EXPERTS + e]
            done = jnp.int32(0)
            piece = tt
            while piece >= 1:
                take = (n & piece) != 0
                size = piece * ROW_CHUNKS

                @pl.when(take)
                def _(done=done, size=size):
                    s = pl.multiple_of(src + done, ROW_CHUNKS)
                    d = pl.multiple_of(dst + done, ROW_CHUNKS)
                    fn(pltpu.make_async_copy(buf.at[pl.ds(s, size)], xs_ref.at[pl.ds(d, size)], sem))

                done = done + jnp.where(take, size, 0)
                piece //= 2
            return carry
        lax.fori_loop(0, N_EXPERTS, per_expert, 0)

    @pl.when(i == 0)
    def _():
        zeros = stage.at[0, pl.ds(0, bm * ROW_CHUNKS)]
        zeros[...] = jnp.zeros_like(zeros)

        def zero_copy(e):
            start = pl.multiple_of(zfill_ref[e], bm * ROW_CHUNKS)
            return pltpu.make_async_copy(zeros, xs_ref.at[pl.ds(start, bm * ROW_CHUNKS)], zsem)

        for e in range(N_EXPERTS):
            @pl.when(zfill_ref[e] >= 0)
            def _():
                zero_copy(e).start()
        for e in range(N_EXPERTS):
            @pl.when(zfill_ref[e] >= 0)
            def _():
                zero_copy(e).wait()

    @pl.when(i >= 2)
    def _():
        for_each_run_piece(i - 2, lambda cp: cp.wait())

    _rows_to_tiles(rows, h_ref[...], tt)

    def sort_rows(t, carry):
        row = rows[pl.ds(pl.multiple_of(t * ROW_CHUNKS, ROW_CHUNKS), ROW_CHUNKS), :]
        for k in range(TOP_K):
            q = pl.multiple_of(q_ref[0, 0, t * TOP_K + k], ROW_CHUNKS)
            stage[slot, pl.ds(q, ROW_CHUNKS), :] = row
        return carry

    lax.fori_loop(0, tt, sort_rows, 0, unroll=8)
    for_each_run_piece(i, lambda cp: cp.start())

    @pl.when(jnp.logical_and(i == last, i >= 1))
    def _():
        for_each_run_piece(i - 1, lambda cp: cp.wait())

    @pl.when(i == last)
    def _():
        for_each_run_piece(i, lambda cp: cp.wait())


def dispatch_rows(h, stage_pos8, run_src8, run_dst8, run_len, zfill8, n_slots, bm):
    T = h.shape[0]
    tt = _tile(T, DISPATCH_TOKENS)
    pos3 = stage_pos8.reshape(T // tt, 1, tt * TOP_K)
    assert tt * TOP_K >= bm
    return pl.pallas_call(
        functools.partial(_dispatch_kernel, tt=tt, bm=bm),
        grid_spec=pltpu.PrefetchScalarGridSpec(
            num_scalar_prefetch=4,
            grid=(T // tt,),
            in_specs=[pl.BlockSpec((1, 1, tt * TOP_K), lambda i, *_: (i, 0, 0), memory_space=pltpu.SMEM),
                      pl.BlockSpec((tt, D_MODEL), lambda i, *_: (i, 0))],
            out_specs=pl.BlockSpec(memory_space=pl.ANY),
            scratch_shapes=[pltpu.VMEM((tt * ROW_CHUNKS, LANES), F32),
                            pltpu.VMEM((2, tt * TOP_K * ROW_CHUNKS, LANES), F32),
                            pltpu.SemaphoreType.DMA((2,)), pltpu.SemaphoreType.DMA],
        ),
        out_shape=jax.ShapeDtypeStruct((n_slots * ROW_CHUNKS, LANES), F32),
        compiler_params=_params("arbitrary"),
        name="moe_dispatch",
    )(zfill8, run_src8, run_dst8, run_len, pos3, h)


def _moe_mlp_kernel(be_ref, na_ref, x_ref, wgu_ref, bgu_ref, wd_ref, bd_ref, y_ref, wgu_bf, wd_bf, *, bm):
    i = pl.program_id(0)
    prev = be_ref[jnp.maximum(i - 1, 0)]
    active = i < na_ref[0]

    @pl.when(jnp.logical_and(active, jnp.logical_or(i == 0, be_ref[i] != prev)))
    def _():
        wgu_bf[...] = wgu_ref[...].astype(BF16)
        wd_bf[...] = wd_ref[...].astype(BF16)

    @pl.when(active)
    def _():
        x = _tiles_to_rows(x_ref, bm).astype(BF16)
        hgu = jnp.dot(x, wgu_bf[...], preferred_element_type=F32) + bgu_ref[...]
        gate = jnp.minimum(hgu[:, :D_FF], SWIGLU_LIMIT)
        up = jnp.clip(hgu[:, D_FF:], -SWIGLU_LIMIT, SWIGLU_LIMIT)
        act = gate * jax.nn.sigmoid(SWIGLU_ALPHA * gate) * (up + 1.0)
        y = jnp.dot(act.astype(BF16), wd_bf[...], preferred_element_type=F32) + bd_ref[...]
        _rows_to_tiles(y_ref, y, bm)


def moe_mlp(xs, block_expert, n_active, layer, w_gu, b_gu, w_d, b_d, bm):
    n_blocks = xs.shape[0] // (bm * ROW_CHUNKS)

    def row_map(i, be, na):
        return (jnp.minimum(i, na[0] - 1), 0)

    def w_map(i, be, na):
        return (layer, be[jnp.minimum(i, na[0] - 1)], 0, 0)

    def b_map(i, be, na):
        return (layer * N_EXPERTS + be[jnp.minimum(i, na[0] - 1)], 0, 0)

    return pl.pallas_call(
        functools.partial(_moe_mlp_kernel, bm=bm),
        grid_spec=pltpu.PrefetchScalarGridSpec(
            num_scalar_prefetch=2,
            grid=(n_blocks,),
            in_specs=[pl.BlockSpec((bm * ROW_CHUNKS, LANES), row_map),
                      pl.BlockSpec((None, None, D_MODEL, 2 * D_FF), w_map),
                      pl.BlockSpec((None, 1, 2 * D_FF), b_map),
                      pl.BlockSpec((None, None, D_FF, D_MODEL), w_map),
                      pl.BlockSpec((None, 1, D_MODEL), b_map)],
            out_specs=pl.BlockSpec((bm * ROW_CHUNKS, LANES), row_map),
            scratch_shapes=[pltpu.VMEM((D_MODEL, 2 * D_FF), BF16), pltpu.VMEM((D_FF, D_MODEL), BF16)],
        ),
        out_shape=jax.ShapeDtypeStruct(xs.shape, F32),
        compiler_params=_params("arbitrary"),
        name="moe_mlp",
    )(block_expert, n_active, xs, w_gu, b_gu.reshape(DEPTH * N_EXPERTS, 1, 2 * D_FF), w_d,
      b_d.reshape(DEPTH * N_EXPERTS, 1, D_MODEL))


def _ffn_ln_ple_kernel(dcur_ref, dnext_ref, gates_ref, h_ref, p_ref, g_ref, b_ref, wg_ref, wp_ref, ys_ref,
                       o_ref, ob_ref, ybuf, sems, *, tm):
    i = pl.program_id(0)
    slot = i % 2

    def row_copy(dref, s, t, k):
        d = pl.multiple_of(dref[0, 0, t * TOP_K + k], ROW_CHUNKS)
        dst = ybuf.at[s, k, pl.ds(pl.multiple_of(t * ROW_CHUNKS, ROW_CHUNKS), ROW_CHUNKS)]
        return pltpu.make_async_copy(ys_ref.at[pl.ds(d, ROW_CHUNKS)], dst, sems.at[s])

    def issue(dref, s):
        def body(t, carry):
            for k in range(TOP_K):
                row_copy(dref, s, t, k).start()
            return carry
        lax.fori_loop(0, tm, body, 0)

    @pl.when(i == 0)
    def _():
        issue(dcur_ref, 0)

    @pl.when(i + 1 < pl.num_programs(0))
    def _():
        issue(dnext_ref, 1 - slot)

    for k in range(TOP_K):
        pltpu.make_async_copy(ys_ref.at[pl.ds(0, tm * ROW_CHUNKS)], ybuf.at[slot, k], sems.at[slot]).wait()

    gates = gates_ref[...]
    ffn = _tiles_to_rows(ybuf.at[slot, 0], tm) * gates[:, 0:1]
    for k in range(1, TOP_K):
        ffn = ffn + _tiles_to_rows(ybuf.at[slot, k], tm) * gates[:, k:k + 1]
    h2 = _layer_norm_rows(DN_ALPHA * h_ref[...] + ffn, g_ref[...], b_ref[...])
    gate = jax.nn.sigmoid(jnp.dot(h2.astype(BF16), wg_ref[...], preferred_element_type=F32))
    proj = jnp.dot(p_ref[...].astype(BF16), wp_ref[...], preferred_element_type=F32)
    out = h2 + gate * proj
    o_ref[...] = out
    ob_ref[...] = out.astype(BF16)


def ffn_ln_ple(ys, dest8, gates, h, p, ln_g, ln_b, wg_bf, wp_bf):
    T = h.shape[0]
    tm = _tile(T, 512)
    n = T // tm
    dest3 = dest8.reshape(n, 1, tm * TOP_K)
    row_spec = pl.BlockSpec((tm, D_MODEL), lambda i: (i, 0))
    vec_spec = pl.BlockSpec((1, D_MODEL), lambda i: (0, 0))
    return pl.pallas_call(
        functools.partial(_ffn_ln_ple_kernel, tm=tm),
        grid=(n,),
        in_specs=[pl.BlockSpec((1, 1, tm * TOP_K), lambda i: (i, 0, 0), memory_space=pltpu.SMEM),
                  pl.BlockSpec((1, 1, tm * TOP_K), lambda i: (jnp.minimum(i + 1, n - 1), 0, 0),
                               memory_space=pltpu.SMEM),
                  pl.BlockSpec((tm, TOP_K), lambda i: (i, 0)),
                  row_spec,
                  pl.BlockSpec((tm, PLE_DIM), lambda i: (i, 0)),
                  vec_spec, vec_spec,
                  pl.BlockSpec((D_MODEL, D_MODEL), lambda i: (0, 0)),
                  pl.BlockSpec((PLE_DIM, D_MODEL), lambda i: (0, 0)),
                  pl.BlockSpec(memory_space=pl.ANY)],
        out_specs=[row_spec, row_spec],
        out_shape=[jax.ShapeDtypeStruct((T, D_MODEL), F32), jax.ShapeDtypeStruct((T, D_MODEL), BF16)],
        scratch_shapes=[pltpu.VMEM((2, TOP_K, tm * ROW_CHUNKS, LANES), F32), pltpu.SemaphoreType.DMA((2,))],
        compiler_params=_params("arbitrary"),
        name="ffn_ln_ple",
    )(dest3, dest3, gates, h, p, ln_g.reshape(1, D_MODEL), ln_b.reshape(1, D_MODEL), wg_bf, wp_bf, ys)


def moe_layer(h1, layer, w_router, b_router, w_gu, b_gu, w_d, b_d):
    T = h1.shape[0]
    bm = MOE_BLOCK
    idx, gates, rank, counts = router(h1, w_router, b_router)
    padded = (counts + bm - 1) // bm * bm
    pend = jnp.cumsum(padded)
    pstart = pend - padded
    onehot = idx[..., None] == jnp.arange(N_EXPERTS, dtype=jnp.int32)
    dest = (jnp.sum(jnp.where(onehot, pstart, 0), axis=-1) + rank).astype(jnp.int32)
    dest8 = dest * ROW_CHUNKS
    n_slots = T * TOP_K + N_EXPERTS * bm
    n_blocks = n_slots // bm
    block_start = jnp.arange(n_blocks, dtype=jnp.int32) * bm
    block_expert = jnp.minimum(jnp.sum(block_start[:, None] >= pend[None, :], axis=1),
                               N_EXPERTS - 1).astype(jnp.int32)
    n_active = (pend[-1:] // bm).astype(jnp.int32)
    zfill8 = jnp.where(padded > 0, (pend - bm) * ROW_CHUNKS, -1).astype(jnp.int32)
    tt = _tile(T, DISPATCH_TOKENS)
    nt = T // tt
    oh_t = onehot.reshape(nt, tt * TOP_K, N_EXPERTS)
    tile_cnt = jnp.sum(oh_t, axis=1, dtype=jnp.int32)
    tile_base = jnp.cumsum(tile_cnt, axis=0) - tile_cnt
    stage_off = jnp.cumsum(tile_cnt, axis=1) - tile_cnt
    shift = jnp.sum(jnp.where(oh_t, (stage_off - tile_base)[:, None, :], 0), axis=-1)
    stage_pos8 = (shift + rank.reshape(nt, tt * TOP_K)).astype(jnp.int32) * ROW_CHUNKS
    run_src8 = (stage_off * ROW_CHUNKS).reshape(-1).astype(jnp.int32)
    run_dst8 = ((pstart[None, :] + tile_base) * ROW_CHUNKS).reshape(-1).astype(jnp.int32)
    xs = dispatch_rows(h1, stage_pos8, run_src8, run_dst8, tile_cnt.reshape(-1), zfill8, n_slots, bm)
    ys = moe_mlp(xs, block_expert, n_active, layer, w_gu, b_gu, w_d, b_d, bm)
    return ys, dest8, gates


def _rms_rows(x, g):
    return x * lax.rsqrt(jnp.mean(x * x, axis=-1, keepdims=True) + RMS_EPS) * g


def _mla_kv_kernel(x_ref, wa_ref, g_ref, wkn_ref, wv_ref, cm_ref, sm_ref, k_ref, v_ref):
    a = jnp.dot(x_ref[...], wa_ref[...], preferred_element_type=F32)
    c_kv = _rms_rows(a[:, :KV_LORA], g_ref[...]).astype(BF16)
    kr = a[:, KV_LORA:KV_LORA + LANES] * cm_ref[...] + a[:, KV_LORA + LANES:] * sm_ref[...]
    kr = kr.astype(BF16)
    kn = jnp.dot(c_kv, wkn_ref[...], preferred_element_type=F32)
    v_ref[...] = jnp.dot(c_kv, wv_ref[...], preferred_element_type=F32).astype(BF16)
    for h in range(MLA_HEADS):
        k_ref[:, 2 * h * LANES:(2 * h + 1) * LANES] = kn[:, h * QK_NOPE:(h + 1) * QK_NOPE].astype(BF16)
        k_ref[:, (2 * h + 1) * LANES:(2 * h + 2) * LANES] = kr


def _pad_rope_cols(w, swap):
    K = w.shape[0]
    w = w.reshape(K, -1, QK_ROPE)
    if swap:
        w = jnp.concatenate([w[..., QK_ROPE // 2:], w[..., :QK_ROPE // 2]], axis=-1)
    w = jnp.concatenate([w, jnp.zeros_like(w)], axis=-1)
    return w.reshape(K, -1)


def mla_shared_kv(hb, w_kv_a, kv_norm_g, w_kv_b, cm, sm):
    T = hb.shape[0]
    tm = _tile(T, 512)
    w_r = w_kv_a[:, KV_LORA:]
    wa = jnp.concatenate([w_kv_a[:, :KV_LORA], _pad_rope_cols(w_r, False), _pad_rope_cols(w_r, True)],
                         axis=1).astype(BF16)
    wb = w_kv_b.reshape(KV_LORA, MLA_HEADS, QK_NOPE + V_HEAD)
    wkn = wb[:, :, :QK_NOPE].reshape(KV_LORA, MLA_HEADS * QK_NOPE).astype(BF16)
    wv = wb[:, :, QK_NOPE:].reshape(KV_LORA, MLA_HEADS * V_HEAD).astype(BF16)
    full = lambda shape: pl.BlockSpec(shape, lambda i: (0, 0))
    return pl.pallas_call(
        _mla_kv_kernel,
        grid=(T // tm,),
        in_specs=[pl.BlockSpec((tm, D_MODEL), lambda i: (i, 0)), full(wa.shape), full((1, KV_LORA)),
                  full(wkn.shape), full(wv.shape),
                  pl.BlockSpec((tm, LANES), lambda i: (i, 0)), pl.BlockSpec((tm, LANES), lambda i: (i, 0))],
        out_specs=[pl.BlockSpec((tm, MLA_HEADS * 2 * LANES), lambda i: (i, 0)),
                   pl.BlockSpec((tm, MLA_HEADS * V_HEAD), lambda i: (i, 0))],
        out_shape=[jax.ShapeDtypeStruct((T, MLA_HEADS * 2 * LANES), BF16),
                   jax.ShapeDtypeStruct((T, MLA_HEADS * V_HEAD), BF16)],
        compiler_params=_params("arbitrary"),
        name="mla_kv",
    )(hb, wa, kv_norm_g.reshape(1, KV_LORA), wkn, wv, cm, sm)


def _mla_q_kernel(x_ref, wa_ref, g_ref, wb_ref, cm_ref, sm_ref, q_ref):
    scale = (QK_NOPE + QK_ROPE) ** -0.5 * LOG2_E
    qa = jnp.dot(x_ref[...], wa_ref[...], preferred_element_type=F32)
    qc = _rms_rows(qa, g_ref[...]).astype(BF16)
    q = jnp.dot(qc, wb_ref[...], preferred_element_type=F32)
    cm = cm_ref[...] * scale
    sm = sm_ref[...] * scale
    n = MLA_HEADS * LANES
    for h in range(MLA_HEADS):
        lo = h * LANES
        q_ref[:, 2 * lo:2 * lo + LANES] = (q[:, lo:lo + LANES] * scale).astype(BF16)
        rope = q[:, n + lo:n + lo + LANES] * cm + q[:, 2 * n + lo:2 * n + lo + LANES] * sm
        q_ref[:, 2 * lo + LANES:2 * lo + 2 * LANES] = rope.astype(BF16)


def mla_q(hb, w_q_a, q_norm_g, w_q_b, cm, sm):
    T = hb.shape[0]
    tm = _tile(T, 512)
    wq = w_q_b.reshape(Q_LORA, MLA_HEADS, QK_NOPE + QK_ROPE)
    w_n = wq[:, :, :QK_NOPE].reshape(Q_LORA, MLA_HEADS * QK_NOPE)
    w_r = wq[:, :, QK_NOPE:].reshape(Q_LORA, MLA_HEADS * QK_ROPE)
    wb = jnp.concatenate([w_n, _pad_rope_cols(w_r, False), _pad_rope_cols(w_r, True)], axis=1).astype(BF16)
    full = lambda shape: pl.BlockSpec(shape, lambda i: (0, 0))
    return pl.pallas_call(
        _mla_q_kernel,
        grid=(T // tm,),
        in_specs=[pl.BlockSpec((tm, D_MODEL), lambda i: (i, 0)), full((D_MODEL, Q_LORA)), full((1, Q_LORA)),
                  full(wb.shape),
                  pl.BlockSpec((tm, LANES), lambda i: (i, 0)), pl.BlockSpec((tm, LANES), lambda i: (i, 0))],
        out_specs=pl.BlockSpec((tm, MLA_HEADS * 2 * LANES), lambda i: (i, 0)),
        out_shape=jax.ShapeDtypeStruct((T, MLA_HEADS * 2 * LANES), BF16),
        compiler_params=_params("arbitrary"),
        name="mla_q",
    )(hb, w_q_a.astype(BF16), q_norm_g.reshape(1, Q_LORA), wb, cm, sm)


def _flash_kernel(q_ref, k_ref, v_ref, o_ref, m_ref, l_ref, acc_ref, *, blk):
    i = pl.program_id(2)
    q = q_ref[...]
    m_ref[...] = jnp.full_like(m_ref, -jnp.inf)
    l_ref[...] = jnp.zeros_like(l_ref)
    acc_ref[...] = jnp.zeros_like(acc_ref)

    def step(j, masked):
        start = pl.multiple_of(j * blk, blk)
        kj = k_ref[pl.ds(start, blk), :]
        vj = v_ref[pl.ds(start, blk), :]
        s = lax.dot_general(q, kj, NT_DIMS, preferred_element_type=F32)
        if masked:
            r = lax.broadcasted_iota(jnp.int32, (blk, blk), 0)
            c = lax.broadcasted_iota(jnp.int32, (blk, blk), 1)
            s = jnp.where(c <= r, s, -jnp.inf)
        m_prev = m_ref[...]
        m_new = jnp.maximum(m_prev, jnp.max(s, axis=-1, keepdims=True))
        alpha = jnp.exp2(m_prev - m_new)
        p = jnp.exp2(s - pltpu.repeat(m_new, blk // LANES, axis=1))
        l_ref[...] = alpha * l_ref[...] + jnp.sum(p, axis=-1, keepdims=True)
        acc_ref[...] = alpha * acc_ref[...] + jnp.dot(p.astype(BF16), vj, preferred_element_type=F32)
        m_ref[...] = m_new

    def pair(j, carry):
        step(2 * j, False)
        step(2 * j + 1, False)
        return carry

    lax.fori_loop(0, i // 2, pair, 0)

    @pl.when(i % 2 == 1)
    def _():
        step(i - 1, False)

    step(i, True)
    o_ref[...] = (acc_ref[...] / l_ref[...]).astype(o_ref.dtype)


def flash_attention(q, k, v, B, S):
    T = B * S
    blk = _tile(S, ATTN_BLOCK)
    nq = S // blk
    return pl.pallas_call(
        functools.partial(_flash_kernel, blk=blk),
        grid=(B, MLA_HEADS, nq),
        in_specs=[pl.BlockSpec((blk, 2 * LANES), lambda b, h, i: (b * nq + i, h)),
                  pl.BlockSpec((S, 2 * LANES), lambda b, h, i: (b, h)),
                  pl.BlockSpec((S, V_HEAD), lambda b, h, i: (b, h))],
        out_specs=pl.BlockSpec((blk, V_HEAD), lambda b, h, i: (b * nq + i, h)),
        out_shape=jax.ShapeDtypeStruct((T, MLA_HEADS * V_HEAD), BF16),
        scratch_shapes=[pltpu.VMEM((blk, LANES), F32), pltpu.VMEM((blk, LANES), F32),
                        pltpu.VMEM((blk, V_HEAD), F32)],
        compiler_params=_params("arbitrary", "arbitrary", "arbitrary"),
        name="mla_flash",
    )(q, k, v)


def kernel(x, p, positions, ret_w_in, ret_gn_g, ret_gn_b, ret_w_out, mla_w_kv_a, mla_kv_norm_g, mla_w_kv_b,
           mla_w_q_a, mla_q_norm_g, mla_w_q_b, mla_w_o, ln_mix_g, ln_mix_b, ln_ffn_g, ln_ffn_b,
           moe_w_router, moe_b_router, moe_w_gate_up, moe_b_gate_up, moe_w_down, moe_b_down,
           ple_w_gate, ple_w_proj):
    B, S, D = x.shape
    T = B * S
    cos_r, sin_r, cos_m, sin_m = rope_tables(positions)
    h = x.reshape(T, D)
    hb = h.astype(BF16)
    kv = None
    for i in range(DEPTH):
        if i < N_A:
            qk, vg = ret_projections(hb, ret_w_in[i].astype(BF16), cos_r, sin_r)
            z = retention(qk, vg, ret_gn_g[i], ret_gn_b[i], B, S)
            w_out = ret_w_out[i].astype(BF16)
        else:
            if kv is None:
                kv = mla_shared_kv(hb, mla_w_kv_a, mla_kv_norm_g, mla_w_kv_b, cos_m, sin_m)
            j = i - N_A
            q = mla_q(hb, mla_w_q_a[j], mla_q_norm_g[j], mla_w_q_b[j], cos_m, sin_m)
            z = flash_attention(q, kv[0], kv[1], B, S)
            w_out = mla_w_o[j].astype(BF16)
        h1 = out_proj_ln(z, w_out, h, ln_mix_g[i], ln_mix_b[i])
        ys, dest8, gates = moe_layer(h1, i, moe_w_router[i], moe_b_router[i], moe_w_gate_up, moe_b_gate_up,
                                     moe_w_down, moe_b_down)
        h, hb = ffn_ln_ple(ys, dest8, gates, h1, p[i].reshape(T, PLE_DIM), ln_ffn_g[i], ln_ffn_b[i],
                           ple_w_gate[i].astype(BF16), ple_w_proj[i].astype(BF16))
    return h.reshape(B, S, D)
```

```python
import functools

import jax
import jax.numpy as jnp
from jax import lax
from jax.experimental import pallas as pl
from jax.experimental.pallas import tpu as pltpu

D_MODEL = 1024
DEPTH = 4
N_A = DEPTH // 2
RET_HEADS = 4
RET_QK_DIM = D_MODEL // RET_HEADS
RET_V_DIM = 2 * RET_QK_DIM
GN_EPS = 1e-6
MLA_HEADS = 8
QK_NOPE = 128
QK_ROPE = 64
V_HEAD = 128
Q_LORA = 256
KV_LORA = 256
RMS_EPS = 1e-6
ROPE_THETA = 10000.0
N_EXPERTS = 32
TOP_K = 4
D_FF = D_MODEL
SWIGLU_LIMIT = 7.0
SWIGLU_ALPHA = 1.702
PLE_DIM = 256
DN_ALPHA = (2 * DEPTH) ** 0.25
LN_EPS = 1e-5
LOG2_E = 1.4426950408889634

LANES = 128
ROW_CHUNKS = D_MODEL // LANES
VMEM_LIMIT_BYTES = 56 * 1024 * 1024

RET_CHUNK = 256
RET_BLOCK = 1024
ATTN_BLOCK = 512
MOE_BLOCK = 512
DISPATCH_TOKENS = 512
FFN_ROW_CHUNKS = 2

F32 = jnp.float32
BF16 = jnp.bfloat16
NT_DIMS = (((1,), (1,)), ((), ()))
TN_DIMS = (((0,), (0,)), ((), ()))


def _tile(n, pref):
    t = min(n, pref)
    assert n % t == 0, (n, pref)
    return t


def _params(*sem):
    return pltpu.CompilerParams(dimension_semantics=sem, vmem_limit_bytes=VMEM_LIMIT_BYTES)


def _layer_norm_rows(u, g, b):
    mu = jnp.mean(u, axis=-1, keepdims=True)
    d = u - mu
    var = jnp.mean(d * d, axis=-1, keepdims=True)
    return d * lax.rsqrt(var + LN_EPS) * g + b


def _rope_table_kernel(pos_ref, inv_r_ref, inv_m_ref, sign_m_ref, cr_ref, sr_ref, cm_ref, sm_ref):
    pos = pos_ref[...].astype(F32)
    ang_r = pos * inv_r_ref[...]
    cr_ref[...] = jnp.cos(ang_r)
    sr_ref[...] = jnp.sin(ang_r)
    ang_m = pos * inv_m_ref[...]
    cm_ref[...] = jnp.cos(ang_m)
    sm_ref[...] = jnp.sin(ang_m) * sign_m_ref[...]


def rope_tables(positions):
    T = positions.size
    tm = _tile(T, 1024)
    pos = positions.reshape(T, 1)
    half_r = RET_QK_DIM // 2
    inv_r = ROPE_THETA ** (-jnp.arange(0, RET_QK_DIM, 2, dtype=F32) / RET_QK_DIM)
    inv_m = ROPE_THETA ** (-jnp.arange(0, QK_ROPE, 2, dtype=F32) / QK_ROPE)
    half_m = QK_ROPE // 2
    lane = jnp.arange(LANES)
    inv_m = inv_m[lane % half_m]
    sign_m = jnp.where((lane % QK_ROPE) < half_m, -1.0, 1.0).astype(F32)
    assert half_r == LANES
    row = lambda v: v.reshape(1, LANES)
    vec_spec = pl.BlockSpec((1, LANES), lambda i: (0, 0))
    tab_spec = pl.BlockSpec((tm, LANES), lambda i: (i, 0))
    out = jax.ShapeDtypeStruct((T, LANES), F32)
    return pl.pallas_call(
        _rope_table_kernel,
        grid=(T // tm,),
        in_specs=[pl.BlockSpec((tm, 1), lambda i: (i, 0)), vec_spec, vec_spec, vec_spec],
        out_specs=[tab_spec] * 4,
        out_shape=[out] * 4,
        compiler_params=_params("arbitrary"),
        name="rope_tables",
    )(pos, row(inv_r), row(inv_m), row(sign_m))


def _ret_qk_kernel(x_ref, w_ref, cos_ref, sin_ref, o_ref):
    acc = jnp.dot(x_ref[...], w_ref[...], preferred_element_type=F32)
    scale = jnp.where(pl.program_id(0) == 1, RET_QK_DIM ** -0.5, 1.0).astype(F32)
    c = cos_ref[...] * scale
    s = sin_ref[...] * scale
    half = RET_QK_DIM // 2
    for h in range(RET_HEADS):
        lo = h * RET_QK_DIM
        x1 = acc[:, lo:lo + half]
        x2 = acc[:, lo + half:lo + 2 * half]
        o_ref[:, lo:lo + half] = (x1 * c - x2 * s).astype(o_ref.dtype)
        o_ref[:, lo + half:lo + 2 * half] = (x1 * s + x2 * c).astype(o_ref.dtype)


def _matmul_kernel(x_ref, w_ref, o_ref):
    o_ref[...] = jnp.dot(x_ref[...], w_ref[...], preferred_element_type=F32).astype(o_ref.dtype)


def ret_projections(hb, w_in_bf, cos_r, sin_r):
    T = hb.shape[0]
    tm = _tile(T, 1024)
    tn = D_MODEL
    n_qk = 2 * RET_HEADS * RET_QK_DIM // tn
    n_vg = 2 * RET_HEADS * RET_V_DIM // tn
    x_spec = pl.BlockSpec((tm, D_MODEL), lambda j, i: (i, 0))
    tab_spec = pl.BlockSpec((tm, LANES), lambda j, i: (i, 0))
    qk = pl.pallas_call(
        _ret_qk_kernel,
        grid=(n_qk, T // tm),
        in_specs=[x_spec, pl.BlockSpec((D_MODEL, tn), lambda j, i: (0, j)), tab_spec, tab_spec],
        out_specs=pl.BlockSpec((tm, tn), lambda j, i: (i, j)),
        out_shape=jax.ShapeDtypeStruct((T, n_qk * tn), BF16),
        compiler_params=_params("arbitrary", "arbitrary"),
        name="ret_qk_proj",
    )(hb, w_in_bf, cos_r, sin_r)
    vg = pl.pallas_call(
        _matmul_kernel,
        grid=(n_vg, T // tm),
        in_specs=[x_spec, pl.BlockSpec((D_MODEL, tn), lambda j, i: (0, j + n_qk))],
        out_specs=pl.BlockSpec((tm, tn), lambda j, i: (i, j)),
        out_shape=jax.ShapeDtypeStruct((T, n_vg * tn), BF16),
        compiler_params=_params("arbitrary", "arbitrary"),
        name="ret_vg_proj",
    )(hb, w_in_bf)
    return qk, vg


def _retention_kernel(q_ref, k_ref, v_ref, g_ref, dm_ref, dq_ref, dk_ref, dc_ref, gng_ref, gnb_ref,
                      z_ref, state_ref, *, chunk, n_chunks):
    @pl.when(pl.program_id(2) == 0)
    def _():
        state_ref[...] = jnp.zeros_like(state_ref)

    dm = dm_ref[0]
    dq = dq_ref[0]
    dk = dk_ref[0]
    dc = dc_ref[0]
    gng = gng_ref[...]
    gnb = gnb_ref[...]
    for c in range(n_chunks):
        rows = slice(c * chunk, (c + 1) * chunk)
        q = q_ref[rows, :]
        k = k_ref[rows, :]
        v = v_ref[rows, :]
        st = state_ref[...]
        inner = lax.dot_general(q, k, NT_DIMS, preferred_element_type=F32) * dm
        y = jnp.dot(inner.astype(BF16), v, preferred_element_type=F32)
        y = y + jnp.dot(q, st.astype(BF16), preferred_element_type=F32) * dq
        kd = (k.astype(F32) * dk).astype(BF16)
        state_ref[...] = st * dc + lax.dot_general(kd, v, TN_DIMS, preferred_element_type=F32)
        mu = jnp.mean(y, axis=-1, keepdims=True)
        d = y - mu
        var = jnp.mean(d * d, axis=-1, keepdims=True)
        yn = d * lax.rsqrt(var + GN_EPS) * gng + gnb
        g = g_ref[rows, :].astype(F32)
        z_ref[rows, :] = (g * jax.nn.sigmoid(g) * yn).astype(z_ref.dtype)


def retention(qk, vg, gn_g, gn_b, B, S):
    T = B * S
    H, dk, dv = RET_HEADS, RET_QK_DIM, RET_V_DIM
    C = _tile(S, RET_CHUNK)
    L = _tile(S, RET_BLOCK)
    nl = S // L
    log_g = jnp.log(1.0 - 2.0 ** (-5.0 - jnp.arange(H, dtype=F32)))
    idx = jnp.arange(C, dtype=F32)
    diff = idx[:, None] - idx[None, :]
    dm = jnp.where(diff >= 0, jnp.exp(log_g[:, None, None] * jnp.maximum(diff, 0.0)), 0.0)
    dq = jnp.exp(log_g[:, None] * (idx + 1.0))[:, :, None]
    dkk = jnp.exp(log_g[:, None] * (C - 1.0 - idx))[:, :, None]
    dc = jnp.broadcast_to(jnp.exp(log_g * C)[:, None, None], (H, 1, dv))
    kern = functools.partial(_retention_kernel, chunk=C, n_chunks=L // C)
    return pl.pallas_call(
        kern,
        grid=(B, H, nl),
        in_specs=[
            pl.BlockSpec((L, dk), lambda b, h, n: (b * nl + n, h)),
            pl.BlockSpec((L, dk), lambda b, h, n: (b * nl + n, H + h)),
            pl.BlockSpec((L, dv), lambda b, h, n: (b * nl + n, h)),
            pl.BlockSpec((L, dv), lambda b, h, n: (b * nl + n, H + h)),
            pl.BlockSpec((1, C, C), lambda b, h, n: (h, 0, 0)),
            pl.BlockSpec((1, C, 1), lambda b, h, n: (h, 0, 0)),
            pl.BlockSpec((1, C, 1), lambda b, h, n: (h, 0, 0)),
            pl.BlockSpec((1, 1, dv), lambda b, h, n: (h, 0, 0)),
            pl.BlockSpec((1, dv), lambda b, h, n: (0, h)),
            pl.BlockSpec((1, dv), lambda b, h, n: (0, h)),
        ],
        out_specs=pl.BlockSpec((L, dv), lambda b, h, n: (b * nl + n, h)),
        out_shape=jax.ShapeDtypeStruct((T, H * dv), BF16),
        scratch_shapes=[pltpu.VMEM((dk, dv), F32)],
        compiler_params=_params("arbitrary", "arbitrary", "arbitrary"),
        name="retention",
    )(qk, qk, vg, vg, dm, dq, dkk, dc, gn_g.reshape(1, H * dv), gn_b.reshape(1, H * dv))


def _out_ln_kernel(z_ref, w_ref, h_ref, g_ref, b_ref, o_ref):
    mix = jnp.dot(z_ref[...], w_ref[...], preferred_element_type=F32)
    u = DN_ALPHA * h_ref[...] + mix
    o_ref[...] = _layer_norm_rows(u, g_ref[...], b_ref[...])


def out_proj_ln(z, w_bf, h, ln_g, ln_b):
    T, K = z.shape
    tm = _tile(T, 512)
    row_spec = pl.BlockSpec((tm, D_MODEL), lambda i: (i, 0))
    vec_spec = pl.BlockSpec((1, D_MODEL), lambda i: (0, 0))
    return pl.pallas_call(
        _out_ln_kernel,
        grid=(T // tm,),
        in_specs=[pl.BlockSpec((tm, K), lambda i: (i, 0)), pl.BlockSpec((K, D_MODEL), lambda i: (0, 0)),
                  row_spec, vec_spec, vec_spec],
        out_specs=row_spec,
        out_shape=jax.ShapeDtypeStruct((T, D_MODEL), F32),
        compiler_params=_params("arbitrary"),
        name="out_proj_ln",
    )(z, w_bf, h, ln_g.reshape(1, D_MODEL), ln_b.reshape(1, D_MODEL))


def _router_kernel(h_ref, w_ref, b_ref, slab_ref, cnt_ref, carry_ref, *, tm):
    @pl.when(pl.program_id(0) == 0)
    def _():
        carry_ref[...] = jnp.zeros_like(carry_ref)

    x = h_ref[...]
    w = w_ref[...]
    x_hi = x.astype(BF16)
    x_lo = (x - x_hi.astype(F32)).astype(BF16)
    w_hi = w.astype(BF16)
    w_lo = (w - w_hi.astype(F32)).astype(BF16)
    logits = (jnp.dot(x_hi, w_hi, preferred_element_type=F32)
              + (jnp.dot(x_hi, w_lo, preferred_element_type=F32)
                 + jnp.dot(x_lo, w_hi, preferred_element_type=F32))) + b_ref[...]
    e_iota = lax.broadcasted_iota(jnp.int32, (tm, N_EXPERTS), 1).astype(F32)
    work = logits
    onehot = jnp.zeros((tm, N_EXPERTS), F32)
    top_vals, top_idx = [], []
    for _ in range(TOP_K):
        m = jnp.max(work, axis=-1, keepdims=True)
        idx = jnp.min(jnp.where(work == m, e_iota, float(N_EXPERTS)), axis=-1, keepdims=True)
        top_vals.append(m)
        top_idx.append(idx)
        hit = e_iota == idx
        work = jnp.where(hit, -jnp.inf, work)
        onehot = jnp.where(hit, 1.0, onehot)
    exps = [jnp.exp(v - top_vals[0]) for v in top_vals]
    denom = exps[0] + exps[1] + exps[2] + exps[3]
    gates = [e / denom for e in exps]

    r = lax.broadcasted_iota(jnp.int32, (tm, tm), 0)
    c = lax.broadcasted_iota(jnp.int32, (tm, tm), 1)
    tri = jnp.where(c < r, 1.0, 0.0).astype(BF16)
    before = jnp.dot(tri, onehot.astype(BF16), preferred_element_type=F32) + carry_ref[0:1, 0:N_EXPERTS]
    ranks = [jnp.sum(jnp.where(e_iota == idx, before, 0.0), axis=-1, keepdims=True) for idx in top_idx]

    lane = lax.broadcasted_iota(jnp.int32, (tm, LANES), 1)
    slab = jnp.zeros((tm, LANES), F32)
    for k in range(TOP_K):
        slab = jnp.where(lane == k, top_idx[k], slab)
        slab = jnp.where(lane == TOP_K + k, gates[k], slab)
        slab = jnp.where(lane == 2 * TOP_K + k, ranks[k], slab)
    slab_ref[...] = slab

    carry_ref[0:1, 0:N_EXPERTS] = carry_ref[0:1, 0:N_EXPERTS] + jnp.sum(onehot, axis=0, keepdims=True)
    cnt_ref[...] = carry_ref[...]


def router(h, w_router, b_router):
    T = h.shape[0]
    tm = _tile(T, 512)
    slab, cnt = pl.pallas_call(
        functools.partial(_router_kernel, tm=tm),
        grid=(T // tm,),
        in_specs=[pl.BlockSpec((tm, D_MODEL), lambda i: (i, 0)),
                  pl.BlockSpec((D_MODEL, N_EXPERTS), lambda i: (0, 0)),
                  pl.BlockSpec((1, N_EXPERTS), lambda i: (0, 0))],
        out_specs=[pl.BlockSpec((tm, LANES), lambda i: (i, 0)), pl.BlockSpec((8, LANES), lambda i: (0, 0))],
        out_shape=[jax.ShapeDtypeStruct((T, LANES), F32), jax.ShapeDtypeStruct((8, LANES), F32)],
        scratch_shapes=[pltpu.VMEM((8, LANES), F32)],
        compiler_params=_params("arbitrary"),
        name="router",
    )(h, w_router, b_router.reshape(1, N_EXPERTS))
    idx = slab[:, 0:TOP_K].astype(jnp.int32)
    gates = slab[:, TOP_K:2 * TOP_K]
    rank = slab[:, 2 * TOP_K:3 * TOP_K].astype(jnp.int32)
    counts = cnt[0, :N_EXPERTS].astype(jnp.int32)
    return idx, gates, rank, counts


def _rows_to_tiles(dst_ref, x, n):
    for c in range(ROW_CHUNKS):
        dst_ref[pl.ds(c, n, stride=ROW_CHUNKS), :] = x[:, c * LANES:(c + 1) * LANES]


def _tiles_to_rows(src_ref, n):
    return jnp.concatenate([src_ref[pl.ds(c, n, stride=ROW_CHUNKS), :] for c in range(ROW_CHUNKS)], axis=1)


def _dispatch_kernel(zfill_ref, rsrc_ref, rdst_ref, rlen_ref, q_ref, h_ref, xs_ref,
                     rows, stage, sems, zsem, *, tt, bm):
    i = pl.program_id(0)
    last = pl.num_programs(0) - 1
    slot = i % 2

    def for_each_run_piece(step, fn):
        buf = stage.at[step % 2]
        sem = sems.at[step % 2]

        def per_expert(e, carry):
            n = rlen_ref[step * N_EXPERTS + e]
            src = rsrc_ref[step * N_EXPERTS + e]
            dst = rdst_ref[step * N_EXPERTS + e]
            done = jnp.int32(0)
            piece = tt
            while piece >= 1:
                take = (n & piece) != 0
                size = piece * ROW_CHUNKS

                @pl.when(take)
                def _(done=done, size=size):
                    s = pl.multiple_of(src + done, ROW_CHUNKS)
                    d = pl.multiple_of(dst + done, ROW_CHUNKS)
                    fn(pltpu.make_async_copy(buf.at[pl.ds(s, size)], xs_ref.at[pl.ds(d, size)], sem))

                done = done + jnp.where(take, size, 0)
                piece //= 2
            return carry
        lax.fori_loop(0, N_EXPERTS, per_expert, 0)

    @pl.when(i == 0)
    def _():
        zeros = stage.at[0, pl.ds(0, bm * ROW_CHUNKS)]
        zeros[...] = jnp.zeros_like(zeros)

        def zero_copy(e):
            start = pl.multiple_of(zfill_ref[e], bm * ROW_CHUNKS)
            return pltpu.make_async_copy(zeros, xs_ref.at[pl.ds(start, bm * ROW_CHUNKS)], zsem)

        for e in range(N_EXPERTS):
            @pl.when(zfill_ref[e] >= 0)
            def _():
                zero_copy(e).start()
        for e in range(N_EXPERTS):
            @pl.when(zfill_ref[e] >= 0)
            def _():
                zero_copy(e).wait()

    @pl.when(i >= 2)
    def _():
        for_each_run_piece(i - 2, lambda cp: cp.wait())

    _rows_to_tiles(rows, h_ref[...], tt)

    def sort_rows(t, carry):
        row = rows[pl.ds(pl.multiple_of(t * ROW_CHUNKS, ROW_CHUNKS), ROW_CHUNKS), :]
        for k in range(TOP_K):
            q = pl.multiple_of(q_ref[0, 0, t * TOP_K + k], ROW_CHUNKS)
            stage[slot, pl.ds(q, ROW_CHUNKS), :] = row
        return carry

    lax.fori_loop(0, tt, sort_rows, 0, unroll=8)
    for_each_run_piece(i, lambda cp: cp.start())

    @pl.when(jnp.logical_and(i == last, i >= 1))
    def _():
        for_each_run_piece(i - 1, lambda cp: cp.wait())

    @pl.when(i == last)
    def _():
        for_each_run_piece(i, lambda cp: cp.wait())


def dispatch_rows(h, stage_pos8, run_src8, run_dst8, run_len, zfill8, n_slots, bm):
    T = h.shape[0]
    tt = _tile(T, DISPATCH_TOKENS)
    pos3 = stage_pos8.reshape(T // tt, 1, tt * TOP_K)
    assert tt * TOP_K >= bm
    return pl.pallas_call(
        functools.partial(_dispatch_kernel, tt=tt, bm=bm),
        grid_spec=pltpu.PrefetchScalarGridSpec(
            num_scalar_prefetch=4,
            grid=(T // tt,),
            in_specs=[pl.BlockSpec((1, 1, tt * TOP_K), lambda i, *_: (i, 0, 0), memory_space=pltpu.SMEM),
                      pl.BlockSpec((tt, D_MODEL), lambda i, *_: (i, 0))],
            out_specs=pl.BlockSpec(memory_space=pl.ANY),
            scratch_shapes=[pltpu.VMEM((tt * ROW_CHUNKS, LANES), F32),
                            pltpu.VMEM((2, tt * TOP_K * ROW_CHUNKS, LANES), F32),
                            pltpu.SemaphoreType.DMA((2,)), pltpu.SemaphoreType.DMA],
        ),
        out_shape=jax.ShapeDtypeStruct((n_slots * ROW_CHUNKS, LANES), F32),
        compiler_params=_params("arbitrary"),
        name="moe_dispatch",
    )(zfill8, run_src8, run_dst8, run_len, pos3, h)


def _moe_mlp_kernel(be_ref, na_ref, x_ref, wgu_ref, bgu_ref, wd_ref, bd_ref, y_ref, wgu_bf, wd_bf, *, bm):
    i = pl.program_id(0)
    prev = be_ref[jnp.maximum(i - 1, 0)]
    active = i < na_ref[0]

    @pl.when(jnp.logical_and(active, jnp.logical_or(i == 0, be_ref[i] != prev)))
    def _():
        wgu_bf[...] = wgu_ref[...].astype(BF16)
        wd_bf[...] = wd_ref[...].astype(BF16)

    @pl.when(active)
    def _():
        x = _tiles_to_rows(x_ref, bm).astype(BF16)
        hgu = jnp.dot(x, wgu_bf[...], preferred_element_type=F32) + bgu_ref[...]
        gate = jnp.minimum(hgu[:, :D_FF], SWIGLU_LIMIT)
        up = jnp.clip(hgu[:, D_FF:], -SWIGLU_LIMIT, SWIGLU_LIMIT)
        act = gate * jax.nn.sigmoid(SWIGLU_ALPHA * gate) * (up + 1.0)
        y = jnp.dot(act.astype(BF16), wd_bf[...], preferred_element_type=F32) + bd_ref[...]
        _rows_to_tiles(y_ref, y, bm)


def moe_mlp(xs, block_expert, n_active, layer, w_gu, b_gu, w_d, b_d, bm):
    n_blocks = xs.shape[0] // (bm * ROW_CHUNKS)

    def row_map(i, be, na):
        return (jnp.minimum(i, na[0] - 1), 0)

    def w_map(i, be, na):
        return (layer, be[jnp.minimum(i, na[0] - 1)], 0, 0)

    def b_map(i, be, na):
        return (layer * N_EXPERTS + be[jnp.minimum(i, na[0] - 1)], 0, 0)

    return pl.pallas_call(
        functools.partial(_moe_mlp_kernel, bm=bm),
        grid_spec=pltpu.PrefetchScalarGridSpec(
            num_scalar_prefetch=2,
            grid=(n_blocks,),
            in_specs=[pl.BlockSpec((bm * ROW_CHUNKS, LANES), row_map),
                      pl.BlockSpec((None, None, D_MODEL, 2 * D_FF), w_map),
                      pl.BlockSpec((None, 1, 2 * D_FF), b_map),
                      pl.BlockSpec((None, None, D_FF, D_MODEL), w_map),
                      pl.BlockSpec((None, 1, D_MODEL), b_map)],
            out_specs=pl.BlockSpec((bm * ROW_CHUNKS, LANES), row_map),
            scratch_shapes=[pltpu.VMEM((D_MODEL, 2 * D_FF), BF16), pltpu.VMEM((D_FF, D_MODEL), BF16)],
        ),
        out_shape=jax.ShapeDtypeStruct(xs.shape, F32),
        compiler_params=_params("arbitrary"),
        name="moe_mlp",
    )(block_expert, n_active, xs, w_gu, b_gu.reshape(DEPTH * N_EXPERTS, 1, 2 * D_FF), w_d,
      b_d.reshape(DEPTH * N_EXPERTS, 1, D_MODEL))


def _ffn_ln_ple_kernel(dcur_ref, dnext_ref, gates_ref, h_ref, p_ref, g_ref, b_ref, wg_ref, wp_ref, ys_ref,
                       o_ref, ob_ref, ybuf, sems, *, tm):
    i = pl.program_id(0)
    slot = i % 2

    def row_copy(dref, s, t, k):
        d = pl.multiple_of(dref[0, 0, t * TOP_K + k], ROW_CHUNKS)
        dst = ybuf.at[s, k, pl.ds(pl.multiple_of(t * ROW_CHUNKS, ROW_CHUNKS), ROW_CHUNKS)]
        return pltpu.make_async_copy(ys_ref.at[pl.ds(d, ROW_CHUNKS)], dst, sems.at[s])

    def issue(dref, s):
        def body(t, carry):
            for k in range(TOP_K):
                row_copy(dref, s, t, k).start()
            return carry
        lax.fori_loop(0, tm, body, 0, unroll=4)

    @pl.when(i == 0)
    def _():
        issue(dcur_ref, 0)

    @pl.when(i + 1 < pl.num_programs(0))
    def _():
        issue(dnext_ref, 1 - slot)

    for k in range(TOP_K):
        pltpu.make_async_copy(ys_ref.at[pl.ds(0, tm * ROW_CHUNKS)], ybuf.at[slot, k], sems.at[slot]).wait()

    sub = tm // FFN_ROW_CHUNKS
    for c in range(FFN_ROW_CHUNKS):
        rows = slice(c * sub, (c + 1) * sub)
        tiles = pl.ds(c * sub * ROW_CHUNKS, sub * ROW_CHUNKS)
        gates = gates_ref[rows, :]
        ffn = _tiles_to_rows(ybuf.at[slot, 0, tiles], sub) * gates[:, 0:1]
        for k in range(1, TOP_K):
            ffn = ffn + _tiles_to_rows(ybuf.at[slot, k, tiles], sub) * gates[:, k:k + 1]
        h2 = _layer_norm_rows(DN_ALPHA * h_ref[rows, :] + ffn, g_ref[...], b_ref[...])
        gate = jax.nn.sigmoid(jnp.dot(h2.astype(BF16), wg_ref[...], preferred_element_type=F32))
        proj = jnp.dot(p_ref[rows, :].astype(BF16), wp_ref[...], preferred_element_type=F32)
        out = h2 + gate * proj
        o_ref[rows, :] = out
        ob_ref[rows, :] = out.astype(BF16)


def ffn_ln_ple(ys, dest8, gates, h, p, ln_g, ln_b, wg_bf, wp_bf):
    T = h.shape[0]
    tm = _tile(T, 512)
    n = T // tm
    dest3 = dest8.reshape(n, 1, tm * TOP_K)
    row_spec = pl.BlockSpec((tm, D_MODEL), lambda i: (i, 0))
    vec_spec = pl.BlockSpec((1, D_MODEL), lambda i: (0, 0))
    return pl.pallas_call(
        functools.partial(_ffn_ln_ple_kernel, tm=tm),
        grid=(n,),
        in_specs=[pl.BlockSpec((1, 1, tm * TOP_K), lambda i: (i, 0, 0), memory_space=pltpu.SMEM),
                  pl.BlockSpec((1, 1, tm * TOP_K), lambda i: (jnp.minimum(i + 1, n - 1), 0, 0),
                               memory_space=pltpu.SMEM),
                  pl.BlockSpec((tm, TOP_K), lambda i: (i, 0)),
                  row_spec,
                  pl.BlockSpec((tm, PLE_DIM), lambda i: (i, 0)),
                  vec_spec, vec_spec,
                  pl.BlockSpec((D_MODEL, D_MODEL), lambda i: (0, 0)),
                  pl.BlockSpec((PLE_DIM, D_MODEL), lambda i: (0, 0)),
                  pl.BlockSpec(memory_space=pl.ANY)],
        out_specs=[row_spec, row_spec],
        out_shape=[jax.ShapeDtypeStruct((T, D_MODEL), F32), jax.ShapeDtypeStruct((T, D_MODEL), BF16)],
        scratch_shapes=[pltpu.VMEM((2, TOP_K, tm * ROW_CHUNKS, LANES), F32), pltpu.SemaphoreType.DMA((2,))],
        compiler_params=_params("arbitrary"),
        name="ffn_ln_ple",
    )(dest3, dest3, gates, h, p, ln_g.reshape(1, D_MODEL), ln_b.reshape(1, D_MODEL), wg_bf, wp_bf, ys)


def moe_layer(h1, layer, w_router, b_router, w_gu, b_gu, w_d, b_d):
    T = h1.shape[0]
    bm = MOE_BLOCK
    idx, gates, rank, counts = router(h1, w_router, b_router)
    padded = (counts + bm - 1) // bm * bm
    pend = jnp.cumsum(padded)
    pstart = pend - padded
    onehot = idx[..., None] == jnp.arange(N_EXPERTS, dtype=jnp.int32)
    dest = (jnp.sum(jnp.where(onehot, pstart, 0), axis=-1) + rank).astype(jnp.int32)
    dest8 = dest * ROW_CHUNKS
    n_slots = T * TOP_K + N_EXPERTS * bm
    n_blocks = n_slots // bm
    block_start = jnp.arange(n_blocks, dtype=jnp.int32) * bm
    block_expert = jnp.minimum(jnp.sum(block_start[:, None] >= pend[None, :], axis=1),
                               N_EXPERTS - 1).astype(jnp.int32)
    n_active = (pend[-1:] // bm).astype(jnp.int32)
    zfill8 = jnp.where(padded > 0, (pend - bm) * ROW_CHUNKS, -1).astype(jnp.int32)
    tt = _tile(T, DISPATCH_TOKENS)
    nt = T // tt
    oh_t = onehot.reshape(nt, tt * TOP_K, N_EXPERTS)
    tile_cnt = jnp.sum(oh_t, axis=1, dtype=jnp.int32)
    tile_base = jnp.cumsum(tile_cnt, axis=0) - tile_cnt
    stage_off = jnp.cumsum(tile_cnt, axis=1) - tile_cnt
    shift = jnp.sum(jnp.where(oh_t, (stage_off - tile_base)[:, None, :], 0), axis=-1)
    stage_pos8 = (shift + rank.reshape(nt, tt * TOP_K)).astype(jnp.int32) * ROW_CHUNKS
    run_src8 = (stage_off * ROW_CHUNKS).reshape(-1).astype(jnp.int32)
    run_dst8 = ((pstart[None, :] + tile_base) * ROW_CHUNKS).reshape(-1).astype(jnp.int32)
    xs = dispatch_rows(h1, stage_pos8, run_src8, run_dst8, tile_cnt.reshape(-1), zfill8, n_slots, bm)
    ys = moe_mlp(xs, block_expert, n_active, layer, w_gu, b_gu, w_d, b_d, bm)
    return ys, dest8, gates


def _rms_rows(x, g):
    return x * lax.rsqrt(jnp.mean(x * x, axis=-1, keepdims=True) + RMS_EPS) * g


def _mla_kv_kernel(x_ref, wa_ref, g_ref, wkn_ref, wv_ref, cm_ref, sm_ref, k_ref, v_ref):
    a = jnp.dot(x_ref[...], wa_ref[...], preferred_element_type=F32)
    c_kv = _rms_rows(a[:, :KV_LORA], g_ref[...]).astype(BF16)
    kr = a[:, KV_LORA:KV_LORA + LANES] * cm_ref[...] + a[:, KV_LORA + LANES:] * sm_ref[...]
    kr = kr.astype(BF16)
    kn = jnp.dot(c_kv, wkn_ref[...], preferred_element_type=F32)
    v_ref[...] = jnp.dot(c_kv, wv_ref[...], preferred_element_type=F32).astype(BF16)
    for h in range(MLA_HEADS):
        k_ref[:, 2 * h * LANES:(2 * h + 1) * LANES] = kn[:, h * QK_NOPE:(h + 1) * QK_NOPE].astype(BF16)
        k_ref[:, (2 * h + 1) * LANES:(2 * h + 2) * LANES] = kr


def _pad_rope_cols(w, swap):
    K = w.shape[0]
    w = w.reshape(K, -1, QK_ROPE)
    if swap:
        w = jnp.concatenate([w[..., QK_ROPE // 2:], w[..., :QK_ROPE // 2]], axis=-1)
    w = jnp.concatenate([w, jnp.zeros_like(w)], axis=-1)
    return w.reshape(K, -1)


def mla_shared_kv(hb, w_kv_a, kv_norm_g, w_kv_b, cm, sm):
    T = hb.shape[0]
    tm = _tile(T, 512)
    w_r = w_kv_a[:, KV_LORA:]
    wa = jnp.concatenate([w_kv_a[:, :KV_LORA], _pad_rope_cols(w_r, False), _pad_rope_cols(w_r, True)],
                         axis=1).astype(BF16)
    wb = w_kv_b.reshape(KV_LORA, MLA_HEADS, QK_NOPE + V_HEAD)
    wkn = wb[:, :, :QK_NOPE].reshape(KV_LORA, MLA_HEADS * QK_NOPE).astype(BF16)
    wv = wb[:, :, QK_NOPE:].reshape(KV_LORA, MLA_HEADS * V_HEAD).astype(BF16)
    full = lambda shape: pl.BlockSpec(shape, lambda i: (0, 0))
    return pl.pallas_call(
        _mla_kv_kernel,
        grid=(T // tm,),
        in_specs=[pl.BlockSpec((tm, D_MODEL), lambda i: (i, 0)), full(wa.shape), full((1, KV_LORA)),
                  full(wkn.shape), full(wv.shape),
                  pl.BlockSpec((tm, LANES), lambda i: (i, 0)), pl.BlockSpec((tm, LANES), lambda i: (i, 0))],
        out_specs=[pl.BlockSpec((tm, MLA_HEADS * 2 * LANES), lambda i: (i, 0)),
                   pl.BlockSpec((tm, MLA_HEADS * V_HEAD), lambda i: (i, 0))],
        out_shape=[jax.ShapeDtypeStruct((T, MLA_HEADS * 2 * LANES), BF16),
                   jax.ShapeDtypeStruct((T, MLA_HEADS * V_HEAD), BF16)],
        compiler_params=_params("arbitrary"),
        name="mla_kv",
    )(hb, wa, kv_norm_g.reshape(1, KV_LORA), wkn, wv, cm, sm)


def _mla_q_kernel(x_ref, wa_ref, g_ref, wb_ref, cm_ref, sm_ref, q_ref):
    scale = (QK_NOPE + QK_ROPE) ** -0.5 * LOG2_E
    qa = jnp.dot(x_ref[...], wa_ref[...], preferred_element_type=F32)
    qc = _rms_rows(qa, g_ref[...]).astype(BF16)
    q = jnp.dot(qc, wb_ref[...], preferred_element_type=F32)
    cm = cm_ref[...] * scale
    sm = sm_ref[...] * scale
    n = MLA_HEADS * LANES
    for h in range(MLA_HEADS):
        lo = h * LANES
        q_ref[:, 2 * lo:2 * lo + LANES] = (q[:, lo:lo + LANES] * scale).astype(BF16)
        rope = q[:, n + lo:n + lo + LANES] * cm + q[:, 2 * n + lo:2 * n + lo + LANES] * sm
        q_ref[:, 2 * lo + LANES:2 * lo + 2 * LANES] = rope.astype(BF16)


def mla_q(hb, w_q_a, q_norm_g, w_q_b, cm, sm):
    T = hb.shape[0]
    tm = _tile(T, 512)
    wq = w_q_b.reshape(Q_LORA, MLA_HEADS, QK_NOPE + QK_ROPE)
    w_n = wq[:, :, :QK_NOPE].reshape(Q_LORA, MLA_HEADS * QK_NOPE)
    w_r = wq[:, :, QK_NOPE:].reshape(Q_LORA, MLA_HEADS * QK_ROPE)
    wb = jnp.concatenate([w_n, _pad_rope_cols(w_r, False), _pad_rope_cols(w_r, True)], axis=1).astype(BF16)
    full = lambda shape: pl.BlockSpec(shape, lambda i: (0, 0))
    return pl.pallas_call(
        _mla_q_kernel,
        grid=(T // tm,),
        in_specs=[pl.BlockSpec((tm, D_MODEL), lambda i: (i, 0)), full((D_MODEL, Q_LORA)), full((1, Q_LORA)),
                  full(wb.shape),
                  pl.BlockSpec((tm, LANES), lambda i: (i, 0)), pl.BlockSpec((tm, LANES), lambda i: (i, 0))],
        out_specs=pl.BlockSpec((tm, MLA_HEADS * 2 * LANES), lambda i: (i, 0)),
        out_shape=jax.ShapeDtypeStruct((T, MLA_HEADS * 2 * LANES), BF16),
        compiler_params=_params("arbitrary"),
        name="mla_q",
    )(hb, w_q_a.astype(BF16), q_norm_g.reshape(1, Q_LORA), wb, cm, sm)


def _flash_kernel(q_ref, k_ref, v_ref, o_ref, m_ref, l_ref, acc_ref, *, blk):
    i = pl.program_id(2)
    q = q_ref[...]
    m_ref[...] = jnp.full_like(m_ref, -jnp.inf)
    l_ref[...] = jnp.zeros_like(l_ref)
    acc_ref[...] = jnp.zeros_like(acc_ref)

    def step(j, masked):
        start = pl.multiple_of(j * blk, blk)
        kj = k_ref[pl.ds(start, blk), :]
        vj = v_ref[pl.ds(start, blk), :]
        s = lax.dot_general(q, kj, NT_DIMS, preferred_element_type=F32)
        if masked:
            r = lax.broadcasted_iota(jnp.int32, (blk, blk), 0)
            c = lax.broadcasted_iota(jnp.int32, (blk, blk), 1)
            s = jnp.where(c <= r, s, -jnp.inf)
        m_prev = m_ref[...]
        m_new = jnp.maximum(m_prev, jnp.max(s, axis=-1, keepdims=True))
        alpha = jnp.exp2(m_prev - m_new)
        p = jnp.exp2(s - pltpu.repeat(m_new, blk // LANES, axis=1))
        l_ref[...] = alpha * l_ref[...] + jnp.sum(p, axis=-1, keepdims=True)
        acc_ref[...] = alpha * acc_ref[...] + jnp.dot(p.astype(BF16), vj, preferred_element_type=F32)
        m_ref[...] = m_new

    def pair(j, carry):
        step(2 * j, False)
        step(2 * j + 1, False)
        return carry

    lax.fori_loop(0, i // 2, pair, 0)

    @pl.when(i % 2 == 1)
    def _():
        step(i - 1, False)

    step(i, True)
    o_ref[...] = (acc_ref[...] / l_ref[...]).astype(o_ref.dtype)


def flash_attention(q, k, v, B, S):
    T = B * S
    blk = _tile(S, ATTN_BLOCK)
    nq = S // blk
    return pl.pallas_call(
        functools.partial(_flash_kernel, blk=blk),
        grid=(B, MLA_HEADS, nq),
        in_specs=[pl.BlockSpec((blk, 2 * LANES), lambda b, h, i: (b * nq + i, h)),
                  pl.BlockSpec((S, 2 * LANES), lambda b, h, i: (b, h)),
                  pl.BlockSpec((S, V_HEAD), lambda b, h, i: (b, h))],
        out_specs=pl.BlockSpec((blk, V_HEAD), lambda b, h, i: (b * nq + i, h)),
        out_shape=jax.ShapeDtypeStruct((T, MLA_HEADS * V_HEAD), BF16),
        scratch_shapes=[pltpu.VMEM((blk, LANES), F32), pltpu.VMEM((blk, LANES), F32),
                        pltpu.VMEM((blk, V_HEAD), F32)],
        compiler_params=_params("arbitrary", "arbitrary", "arbitrary"),
        name="mla_flash",
    )(q, k, v)


def kernel(x, p, positions, ret_w_in, ret_gn_g, ret_gn_b, ret_w_out, mla_w_kv_a, mla_kv_norm_g, mla_w_kv_b,
           mla_w_q_a, mla_q_norm_g, mla_w_q_b, mla_w_o, ln_mix_g, ln_mix_b, ln_ffn_g, ln_ffn_b,
           moe_w_router, moe_b_router, moe_w_gate_up, moe_b_gate_up, moe_w_down, moe_b_down,
           ple_w_gate, ple_w_proj):
    B, S, D = x.shape
    T = B * S
    cos_r, sin_r, cos_m, sin_m = rope_tables(positions)
    h = x.reshape(T, D)
    hb = h.astype(BF16)
    kv = None
    for i in range(DEPTH):
        if i < N_A:
            qk, vg = ret_projections(hb, ret_w_in[i].astype(BF16), cos_r, sin_r)
            z = retention(qk, vg, ret_gn_g[i], ret_gn_b[i], B, S)
            w_out = ret_w_out[i].astype(BF16)
        else:
            if kv is None:
                kv = mla_shared_kv(hb, mla_w_kv_a, mla_kv_norm_g, mla_w_kv_b, cos_m, sin_m)
            j = i - N_A
            q = mla_q(hb, mla_w_q_a[j], mla_q_norm_g[j], mla_w_q_b[j], cos_m, sin_m)
            z = flash_attention(q, kv[0], kv[1], B, S)
            w_out = mla_w_o[j].astype(BF16)
        h1 = out_proj_ln(z, w_out, h, ln_mix_g[i], ln_mix_b[i])
        ys, dest8, gates = moe_layer(h1, i, moe_w_router[i], moe_b_router[i], moe_w_gate_up, moe_b_gate_up,
                                     moe_w_down, moe_b_down)
        h, hb = ffn_ln_ple(ys, dest8, gates, h1, p[i].reshape(T, PLE_DIM), ln_ffn_g[i], ln_ffn_b[i],
                           ple_w_gate[i].astype(BF16), ple_w_proj[i].astype(BF16))
    return h.reshape(B, S, D)
```

```python
import functools

import jax
import jax.numpy as jnp
from jax import lax
from jax.experimental import pallas as pl
from jax.experimental.pallas import tpu as pltpu

D_MODEL = 1024
DEPTH = 4
N_A = DEPTH // 2
RET_HEADS = 4
RET_QK_DIM = D_MODEL // RET_HEADS
RET_V_DIM = 2 * RET_QK_DIM
GN_EPS = 1e-6
MLA_HEADS = 8
QK_NOPE = 128
QK_ROPE = 64
V_HEAD = 128
Q_LORA = 256
KV_LORA = 256
RMS_EPS = 1e-6
ROPE_THETA = 10000.0
N_EXPERTS = 32
TOP_K = 4
D_FF = D_MODEL
SWIGLU_LIMIT = 7.0
SWIGLU_ALPHA = 1.702
PLE_DIM = 256
DN_ALPHA = (2 * DEPTH) ** 0.25
LN_EPS = 1e-5
LOG2_E = 1.4426950408889634

LANES = 128
ROW_CHUNKS = D_MODEL // LANES
VMEM_LIMIT_BYTES = 56 * 1024 * 1024

RET_CHUNK = 256
RET_BLOCK = 1024
ATTN_BLOCK = 512
MOE_BLOCK = 512
DISPATCH_TOKENS = 512
FFN_ROW_CHUNKS = 2

F32 = jnp.float32
BF16 = jnp.bfloat16
NT_DIMS = (((1,), (1,)), ((), ()))
TN_DIMS = (((0,), (0,)), ((), ()))


def _tile(n, pref):
    t = min(n, pref)
    assert n % t == 0, (n, pref)
    return t


def _params(*sem):
    return pltpu.CompilerParams(dimension_semantics=sem, vmem_limit_bytes=VMEM_LIMIT_BYTES)


def _layer_norm_rows(u, g, b):
    mu = jnp.mean(u, axis=-1, keepdims=True)
    d = u - mu
    var = jnp.mean(d * d, axis=-1, keepdims=True)
    return d * lax.rsqrt(var + LN_EPS) * g + b


def _rope_table_kernel(pos_ref, inv_r_ref, inv_m_ref, sign_m_ref, cr_ref, sr_ref, cm_ref, sm_ref):
    pos = pos_ref[...].astype(F32)
    ang_r = pos * inv_r_ref[...]
    cr_ref[...] = jnp.cos(ang_r)
    sr_ref[...] = jnp.sin(ang_r)
    ang_m = pos * inv_m_ref[...]
    cm_ref[...] = jnp.cos(ang_m)
    sm_ref[...] = jnp.sin(ang_m) * sign_m_ref[...]


def rope_tables(positions):
    T = positions.size
    tm = _tile(T, 1024)
    pos = positions.reshape(T, 1)
    half_r = RET_QK_DIM // 2
    inv_r = ROPE_THETA ** (-jnp.arange(0, RET_QK_DIM, 2, dtype=F32) / RET_QK_DIM)
    inv_m = ROPE_THETA ** (-jnp.arange(0, QK_ROPE, 2, dtype=F32) / QK_ROPE)
    half_m = QK_ROPE // 2
    lane = jnp.arange(LANES)
    inv_m = inv_m[lane % half_m]
    sign_m = jnp.where((lane % QK_ROPE) < half_m, -1.0, 1.0).astype(F32)
    assert half_r == LANES
    row = lambda v: v.reshape(1, LANES)
    vec_spec = pl.BlockSpec((1, LANES), lambda i: (0, 0))
    tab_spec = pl.BlockSpec((tm, LANES), lambda i: (i, 0))
    out = jax.ShapeDtypeStruct((T, LANES), F32)
    return pl.pallas_call(
        _rope_table_kernel,
        grid=(T // tm,),
        in_specs=[pl.BlockSpec((tm, 1), lambda i: (i, 0)), vec_spec, vec_spec, vec_spec],
        out_specs=[tab_spec] * 4,
        out_shape=[out] * 4,
        compiler_params=_params("arbitrary"),
        name="rope_tables",
    )(pos, row(inv_r), row(inv_m), row(sign_m))


def _ret_qk_kernel(x_ref, w_ref, cos_ref, sin_ref, o_ref):
    acc = jnp.dot(x_ref[...], w_ref[...], preferred_element_type=F32)
    scale = jnp.where(pl.program_id(0) == 1, RET_QK_DIM ** -0.5, 1.0).astype(F32)
    c = cos_ref[...] * scale
    s = sin_ref[...] * scale
    half = RET_QK_DIM // 2
    for h in range(RET_HEADS):
        lo = h * RET_QK_DIM
        x1 = acc[:, lo:lo + half]
        x2 = acc[:, lo + half:lo + 2 * half]
        o_ref[:, lo:lo + half] = (x1 * c - x2 * s).astype(o_ref.dtype)
        o_ref[:, lo + half:lo + 2 * half] = (x1 * s + x2 * c).astype(o_ref.dtype)


def _matmul_kernel(x_ref, w_ref, o_ref):
    o_ref[...] = jnp.dot(x_ref[...], w_ref[...], preferred_element_type=F32).astype(o_ref.dtype)


def ret_projections(hb, w_in_bf, cos_r, sin_r):
    T = hb.shape[0]
    tm = _tile(T, 1024)
    tn = D_MODEL
    n_qk = 2 * RET_HEADS * RET_QK_DIM // tn
    n_vg = 2 * RET_HEADS * RET_V_DIM // tn
    x_spec = pl.BlockSpec((tm, D_MODEL), lambda j, i: (i, 0))
    tab_spec = pl.BlockSpec((tm, LANES), lambda j, i: (i, 0))
    qk = pl.pallas_call(
        _ret_qk_kernel,
        grid=(n_qk, T // tm),
        in_specs=[x_spec, pl.BlockSpec((D_MODEL, tn), lambda j, i: (0, j)), tab_spec, tab_spec],
        out_specs=pl.BlockSpec((tm, tn), lambda j, i: (i, j)),
        out_shape=jax.ShapeDtypeStruct((T, n_qk * tn), BF16),
        compiler_params=_params("arbitrary", "arbitrary"),
        name="ret_qk_proj",
    )(hb, w_in_bf, cos_r, sin_r)
    vg = pl.pallas_call(
        _matmul_kernel,
        grid=(n_vg, T // tm),
        in_specs=[x_spec, pl.BlockSpec((D_MODEL, tn), lambda j, i: (0, j + n_qk))],
        out_specs=pl.BlockSpec((tm, tn), lambda j, i: (i, j)),
        out_shape=jax.ShapeDtypeStruct((T, n_vg * tn), BF16),
        compiler_params=_params("arbitrary", "arbitrary"),
        name="ret_vg_proj",
    )(hb, w_in_bf)
    return qk, vg


def _retention_kernel(q_ref, k_ref, v_ref, g_ref, dm_ref, dq_ref, dk_ref, dc_ref, gng_ref, gnb_ref,
                      z_ref, state_ref, *, chunk, n_chunks):
    @pl.when(pl.program_id(2) == 0)
    def _():
        state_ref[...] = jnp.zeros_like(state_ref)

    dm = dm_ref[0]
    dq = dq_ref[0]
    dk = dk_ref[0]
    dc = dc_ref[0]
    gng = gng_ref[...]
    gnb = gnb_ref[...]
    for c in range(n_chunks):
        rows = slice(c * chunk, (c + 1) * chunk)
        q = q_ref[rows, :]
        k = k_ref[rows, :]
        v = v_ref[rows, :]
        st = state_ref[...]
        inner = lax.dot_general(q, k, NT_DIMS, preferred_element_type=F32) * dm
        y = jnp.dot(inner.astype(BF16), v, preferred_element_type=F32)
        y = y + jnp.dot(q, st.astype(BF16), preferred_element_type=F32) * dq
        kd = (k.astype(F32) * dk).astype(BF16)
        state_ref[...] = st * dc + lax.dot_general(kd, v, TN_DIMS, preferred_element_type=F32)
        mu = jnp.mean(y, axis=-1, keepdims=True)
        d = y - mu
        var = jnp.mean(d * d, axis=-1, keepdims=True)
        yn = d * lax.rsqrt(var + GN_EPS) * gng + gnb
        g = g_ref[rows, :].astype(F32)
        z_ref[rows, :] = (g * jax.nn.sigmoid(g) * yn).astype(z_ref.dtype)


def retention(qk, vg, gn_g, gn_b, B, S):
    T = B * S
    H, dk, dv = RET_HEADS, RET_QK_DIM, RET_V_DIM
    C = _tile(S, RET_CHUNK)
    L = _tile(S, RET_BLOCK)
    nl = S // L
    log_g = jnp.log(1.0 - 2.0 ** (-5.0 - jnp.arange(H, dtype=F32)))
    idx = jnp.arange(C, dtype=F32)
    diff = idx[:, None] - idx[None, :]
    dm = jnp.where(diff >= 0, jnp.exp(log_g[:, None, None] * jnp.maximum(diff, 0.0)), 0.0)
    dq = jnp.exp(log_g[:, None] * (idx + 1.0))[:, :, None]
    dkk = jnp.exp(log_g[:, None] * (C - 1.0 - idx))[:, :, None]
    dc = jnp.broadcast_to(jnp.exp(log_g * C)[:, None, None], (H, 1, dv))
    kern = functools.partial(_retention_kernel, chunk=C, n_chunks=L // C)
    return pl.pallas_call(
        kern,
        grid=(B, H, nl),
        in_specs=[
            pl.BlockSpec((L, dk), lambda b, h, n: (b * nl + n, h)),
            pl.BlockSpec((L, dk), lambda b, h, n: (b * nl + n, H + h)),
            pl.BlockSpec((L, dv), lambda b, h, n: (b * nl + n, h)),
            pl.BlockSpec((L, dv), lambda b, h, n: (b * nl + n, H + h)),
            pl.BlockSpec((1, C, C), lambda b, h, n: (h, 0, 0)),
            pl.BlockSpec((1, C, 1), lambda b, h, n: (h, 0, 0)),
            pl.BlockSpec((1, C, 1), lambda b, h, n: (h, 0, 0)),
            pl.BlockSpec((1, 1, dv), lambda b, h, n: (h, 0, 0)),
            pl.BlockSpec((1, dv), lambda b, h, n: (0, h)),
            pl.BlockSpec((1, dv), lambda b, h, n: (0, h)),
        ],
        out_specs=pl.BlockSpec((L, dv), lambda b, h, n: (b * nl + n, h)),
        out_shape=jax.ShapeDtypeStruct((T, H * dv), BF16),
        scratch_shapes=[pltpu.VMEM((dk, dv), F32)],
        compiler_params=_params("arbitrary", "arbitrary", "arbitrary"),
        name="retention",
    )(qk, qk, vg, vg, dm, dq, dkk, dc, gn_g.reshape(1, H * dv), gn_b.reshape(1, H * dv))


def _out_ln_kernel(z_ref, w_ref, h_ref, g_ref, b_ref, o_ref):
    mix = jnp.dot(z_ref[...], w_ref[...], preferred_element_type=F32)
    u = DN_ALPHA * h_ref[...] + mix
    o_ref[...] = _layer_norm_rows(u, g_ref[...], b_ref[...])


def out_proj_ln(z, w_bf, h, ln_g, ln_b):
    T, K = z.shape
    tm = _tile(T, 512)
    row_spec = pl.BlockSpec((tm, D_MODEL), lambda i: (i, 0))
    vec_spec = pl.BlockSpec((1, D_MODEL), lambda i: (0, 0))
    return pl.pallas_call(
        _out_ln_kernel,
        grid=(T // tm,),
        in_specs=[pl.BlockSpec((tm, K), lambda i: (i, 0)), pl.BlockSpec((K, D_MODEL), lambda i: (0, 0)),
                  row_spec, vec_spec, vec_spec],
        out_specs=row_spec,
        out_shape=jax.ShapeDtypeStruct((T, D_MODEL), F32),
        compiler_params=_params("arbitrary"),
        name="out_proj_ln",
    )(z, w_bf, h, ln_g.reshape(1, D_MODEL), ln_b.reshape(1, D_MODEL))


def _router_kernel(h_ref, w_ref, b_ref, slab_ref, cnt_ref, carry_ref, *, tm):
    @pl.when(pl.program_id(0) == 0)
    def _():
        carry_ref[...] = jnp.zeros_like(carry_ref)

    x = h_ref[...]
    w = w_ref[...]
    x_hi = x.astype(BF16)
    x_lo = (x - x_hi.astype(F32)).astype(BF16)
    w_hi = w.astype(BF16)
    w_lo = (w - w_hi.astype(F32)).astype(BF16)
    logits = (jnp.dot(x_hi, w_hi, preferred_element_type=F32)
              + (jnp.dot(x_hi, w_lo, preferred_element_type=F32)
                 + jnp.dot(x_lo, w_hi, preferred_element_type=F32))) + b_ref[...]
    e_iota = lax.broadcasted_iota(jnp.int32, (tm, N_EXPERTS), 1).astype(F32)
    work = logits
    onehot = jnp.zeros((tm, N_EXPERTS), F32)
    top_vals, top_idx = [], []
    for _ in range(TOP_K):
        m = jnp.max(work, axis=-1, keepdims=True)
        idx = jnp.min(jnp.where(work == m, e_iota, float(N_EXPERTS)), axis=-1, keepdims=True)
        top_vals.append(m)
        top_idx.append(idx)
        hit = e_iota == idx
        work = jnp.where(hit, -jnp.inf, work)
        onehot = jnp.where(hit, 1.0, onehot)
    exps = [jnp.exp(v - top_vals[0]) for v in top_vals]
    denom = exps[0] + exps[1] + exps[2] + exps[3]
    gates = [e / denom for e in exps]

    r = lax.broadcasted_iota(jnp.int32, (tm, tm), 0)
    c = lax.broadcasted_iota(jnp.int32, (tm, tm), 1)
    tri = jnp.where(c < r, 1.0, 0.0).astype(BF16)
    before = jnp.dot(tri, onehot.astype(BF16), preferred_element_type=F32) + carry_ref[0:1, 0:N_EXPERTS]
    ranks = [jnp.sum(jnp.where(e_iota == idx, before, 0.0), axis=-1, keepdims=True) for idx in top_idx]

    lane = lax.broadcasted_iota(jnp.int32, (tm, LANES), 1)
    slab = jnp.zeros((tm, LANES), F32)
    for k in range(TOP_K):
        slab = jnp.where(lane == k, top_idx[k], slab)
        slab = jnp.where(lane == TOP_K + k, gates[k], slab)
        slab = jnp.where(lane == 2 * TOP_K + k, ranks[k], slab)
    slab_ref[...] = slab

    carry_ref[0:1, 0:N_EXPERTS] = carry_ref[0:1, 0:N_EXPERTS] + jnp.sum(onehot, axis=0, keepdims=True)
    cnt_ref[...] = carry_ref[...]


def router(h, w_router, b_router):
    T = h.shape[0]
    tm = _tile(T, 512)
    slab, cnt = pl.pallas_call(
        functools.partial(_router_kernel, tm=tm),
        grid=(T // tm,),
        in_specs=[pl.BlockSpec((tm, D_MODEL), lambda i: (i, 0)),
                  pl.BlockSpec((D_MODEL, N_EXPERTS), lambda i: (0, 0)),
                  pl.BlockSpec((1, N_EXPERTS), lambda i: (0, 0))],
        out_specs=[pl.BlockSpec((tm, LANES), lambda i: (i, 0)), pl.BlockSpec((8, LANES), lambda i: (0, 0))],
        out_shape=[jax.ShapeDtypeStruct((T, LANES), F32), jax.ShapeDtypeStruct((8, LANES), F32)],
        scratch_shapes=[pltpu.VMEM((8, LANES), F32)],
        compiler_params=_params("arbitrary"),
        name="router",
    )(h, w_router, b_router.reshape(1, N_EXPERTS))
    idx = slab[:, 0:TOP_K].astype(jnp.int32)
    gates = slab[:, TOP_K:2 * TOP_K]
    rank = slab[:, 2 * TOP_K:3 * TOP_K].astype(jnp.int32)
    counts = cnt[0, :N_EXPERTS].astype(jnp.int32)
    return idx, gates, rank, counts


def _rows_to_tiles(dst_ref, x, n):
    for c in range(ROW_CHUNKS):
        dst_ref[pl.ds(c, n, stride=ROW_CHUNKS), :] = x[:, c * LANES:(c + 1) * LANES]


def _tiles_to_rows(src_ref, n):
    return jnp.concatenate([src_ref[pl.ds(c, n, stride=ROW_CHUNKS), :] for c in range(ROW_CHUNKS)], axis=1)


def _dispatch_kernel(zfill_ref, rsrc_ref, rdst_ref, rlen_ref, q_ref, h_ref, xs_ref,
                     rows, stage, sems, zsem, *, tt, bm):
    i = pl.program_id(0)
    last = pl.num_programs(0) - 1
    slot = i % 2

    def for_each_run_piece(step, fn):
        buf = stage.at[step % 2]
        sem = sems.at[step % 2]

        def per_expert(e, carry):
            n = rlen_ref[step * N_EXPERTS + e]
            src = rsrc_ref[step * N_EXPERTS + e]
            dst = rdst_ref[step * N_EXPERTS + e]
            done = jnp.int32(0)
            piece = tt
            prio = 0
            while piece >= 1:
                take = (n & piece) != 0
                size = piece * ROW_CHUNKS

                @pl.when(take)
                def _(done=done, size=size, prio=prio):
                    s = pl.multiple_of(src + done, ROW_CHUNKS)
                    d = pl.multiple_of(dst + done, ROW_CHUNKS)
                    fn(pltpu.make_async_copy(buf.at[pl.ds(s, size)], xs_ref.at[pl.ds(d, size)], sem), prio)

                done = done + jnp.where(take, size, 0)
                piece //= 2
                prio = 1 - prio
            return carry
        lax.fori_loop(0, N_EXPERTS, per_expert, 0)

    @pl.when(i == 0)
    def _():
        zeros = stage.at[0, pl.ds(0, bm * ROW_CHUNKS)]
        zeros[...] = jnp.zeros_like(zeros)

        def zero_copy(e):
            start = pl.multiple_of(zfill_ref[e], bm * ROW_CHUNKS)
            return pltpu.make_async_copy(zeros, xs_ref.at[pl.ds(start, bm * ROW_CHUNKS)], zsem)

        for e in range(N_EXPERTS):
            @pl.when(zfill_ref[e] >= 0)
            def _():
                zero_copy(e).start()
        for e in range(N_EXPERTS):
            @pl.when(zfill_ref[e] >= 0)
            def _():
                zero_copy(e).wait()

    @pl.when(i >= 2)
    def _():
        for_each_run_piece(i - 2, lambda cp, prio: cp.wait())

    _rows_to_tiles(rows, h_ref[...], tt)

    def sort_rows(t, carry):
        row = rows[pl.ds(pl.multiple_of(t * ROW_CHUNKS, ROW_CHUNKS), ROW_CHUNKS), :]
        for k in range(TOP_K):
            q = pl.multiple_of(q_ref[0, 0, t * TOP_K + k], ROW_CHUNKS)
            stage[slot, pl.ds(q, ROW_CHUNKS), :] = row
        return carry

    lax.fori_loop(0, tt, sort_rows, 0, unroll=8)
    for_each_run_piece(i, lambda cp, prio: cp.start(priority=prio))

    @pl.when(jnp.logical_and(i == last, i >= 1))
    def _():
        for_each_run_piece(i - 1, lambda cp, prio: cp.wait())

    @pl.when(i == last)
    def _():
        for_each_run_piece(i, lambda cp, prio: cp.wait())


def dispatch_rows(h, stage_pos8, run_src8, run_dst8, run_len, zfill8, n_slots, bm):
    T = h.shape[0]
    tt = _tile(T, DISPATCH_TOKENS)
    pos3 = stage_pos8.reshape(T // tt, 1, tt * TOP_K)
    assert tt * TOP_K >= bm
    return pl.pallas_call(
        functools.partial(_dispatch_kernel, tt=tt, bm=bm),
        grid_spec=pltpu.PrefetchScalarGridSpec(
            num_scalar_prefetch=4,
            grid=(T // tt,),
            in_specs=[pl.BlockSpec((1, 1, tt * TOP_K), lambda i, *_: (i, 0, 0), memory_space=pltpu.SMEM),
                      pl.BlockSpec((tt, D_MODEL), lambda i, *_: (i, 0))],
            out_specs=pl.BlockSpec(memory_space=pl.ANY),
            scratch_shapes=[pltpu.VMEM((tt * ROW_CHUNKS, LANES), F32),
                            pltpu.VMEM((2, tt * TOP_K * ROW_CHUNKS, LANES), F32),
                            pltpu.SemaphoreType.DMA((2,)), pltpu.SemaphoreType.DMA],
        ),
        out_shape=jax.ShapeDtypeStruct((n_slots * ROW_CHUNKS, LANES), F32),
        compiler_params=_params("arbitrary"),
        name="moe_dispatch",
    )(zfill8, run_src8, run_dst8, run_len, pos3, h)


def _moe_mlp_kernel(be_ref, na_ref, x_ref, wgu_ref, bgu_ref, wd_ref, bd_ref, y_ref, wgu_bf, wd_bf, *, bm):
    i = pl.program_id(0)
    prev = be_ref[jnp.maximum(i - 1, 0)]
    active = i < na_ref[0]

    @pl.when(jnp.logical_and(active, jnp.logical_or(i == 0, be_ref[i] != prev)))
    def _():
        wgu_bf[...] = wgu_ref[...].astype(BF16)
        wd_bf[...] = wd_ref[...].astype(BF16)

    @pl.when(active)
    def _():
        x = _tiles_to_rows(x_ref, bm).astype(BF16)
        hgu = jnp.dot(x, wgu_bf[...], preferred_element_type=F32) + bgu_ref[...]
        gate = jnp.minimum(hgu[:, :D_FF], SWIGLU_LIMIT)
        up = jnp.clip(hgu[:, D_FF:], -SWIGLU_LIMIT, SWIGLU_LIMIT)
        act = gate * jax.nn.sigmoid(SWIGLU_ALPHA * gate) * (up + 1.0)
        y = jnp.dot(act.astype(BF16), wd_bf[...], preferred_element_type=F32) + bd_ref[...]
        _rows_to_tiles(y_ref, y, bm)


def moe_mlp(xs, block_expert, n_active, layer, w_gu, b_gu, w_d, b_d, bm):
    n_blocks = xs.shape[0] // (bm * ROW_CHUNKS)

    def row_map(i, be, na):
        return (jnp.minimum(i, na[0] - 1), 0)

    def w_map(i, be, na):
        return (layer, be[jnp.minimum(i, na[0] - 1)], 0, 0)

    def b_map(i, be, na):
        return (layer * N_EXPERTS + be[jnp.minimum(i, na[0] - 1)], 0, 0)

    return pl.pallas_call(
        functools.partial(_moe_mlp_kernel, bm=bm),
        grid_spec=pltpu.PrefetchScalarGridSpec(
            num_scalar_prefetch=2,
            grid=(n_blocks,),
            in_specs=[pl.BlockSpec((bm * ROW_CHUNKS, LANES), row_map),
                      pl.BlockSpec((None, None, D_MODEL, 2 * D_FF), w_map),
                      pl.BlockSpec((None, 1, 2 * D_FF), b_map),
                      pl.BlockSpec((None, None, D_FF, D_MODEL), w_map),
                      pl.BlockSpec((None, 1, D_MODEL), b_map)],
            out_specs=pl.BlockSpec((bm * ROW_CHUNKS, LANES), row_map),
            scratch_shapes=[pltpu.VMEM((D_MODEL, 2 * D_FF), BF16), pltpu.VMEM((D_FF, D_MODEL), BF16)],
        ),
        out_shape=jax.ShapeDtypeStruct(xs.shape, F32),
        compiler_params=_params("arbitrary"),
        name="moe_mlp",
    )(block_expert, n_active, xs, w_gu, b_gu.reshape(DEPTH * N_EXPERTS, 1, 2 * D_FF), w_d,
      b_d.reshape(DEPTH * N_EXPERTS, 1, D_MODEL))


def _ffn_ln_ple_kernel(dcur_ref, dnext_ref, gates_ref, h_ref, p_ref, g_ref, b_ref, wg_ref, wp_ref, ys_ref,
                       o_ref, ob_ref, ybuf, sems, *, tm):
    i = pl.program_id(0)
    slot = i % 2

    def row_copy(dref, s, t, k):
        d = pl.multiple_of(dref[0, 0, t * TOP_K + k], ROW_CHUNKS)
        dst = ybuf.at[s, k, pl.ds(pl.multiple_of(t * ROW_CHUNKS, ROW_CHUNKS), ROW_CHUNKS)]
        return pltpu.make_async_copy(ys_ref.at[pl.ds(d, ROW_CHUNKS)], dst, sems.at[s])

    def issue(dref, s):
        def body(t, carry):
            for k in range(TOP_K):
                row_copy(dref, s, t, k).start(priority=k % 2)
            return carry
        lax.fori_loop(0, tm, body, 0, unroll=4)

    @pl.when(i == 0)
    def _():
        issue(dcur_ref, 0)

    @pl.when(i + 1 < pl.num_programs(0))
    def _():
        issue(dnext_ref, 1 - slot)

    for k in range(TOP_K):
        pltpu.make_async_copy(ys_ref.at[pl.ds(0, tm * ROW_CHUNKS)], ybuf.at[slot, k], sems.at[slot]).wait()

    sub = tm // FFN_ROW_CHUNKS
    for c in range(FFN_ROW_CHUNKS):
        rows = slice(c * sub, (c + 1) * sub)
        tiles = pl.ds(c * sub * ROW_CHUNKS, sub * ROW_CHUNKS)
        gates = gates_ref[rows, :]
        ffn = _tiles_to_rows(ybuf.at[slot, 0, tiles], sub) * gates[:, 0:1]
        for k in range(1, TOP_K):
            ffn = ffn + _tiles_to_rows(ybuf.at[slot, k, tiles], sub) * gates[:, k:k + 1]
        h2 = _layer_norm_rows(DN_ALPHA * h_ref[rows, :] + ffn, g_ref[...], b_ref[...])
        gate = jax.nn.sigmoid(jnp.dot(h2.astype(BF16), wg_ref[...], preferred_element_type=F32))
        proj = jnp.dot(p_ref[rows, :].astype(BF16), wp_ref[...], preferred_element_type=F32)
        out = h2 + gate * proj
        o_ref[rows, :] = out
        ob_ref[rows, :] = out.astype(BF16)


def ffn_ln_ple(ys, dest8, gates, h, p, ln_g, ln_b, wg_bf, wp_bf):
    T = h.shape[0]
    tm = _tile(T, 512)
    n = T // tm
    dest3 = dest8.reshape(n, 1, tm * TOP_K)
    row_spec = pl.BlockSpec((tm, D_MODEL), lambda i: (i, 0))
    vec_spec = pl.BlockSpec((1, D_MODEL), lambda i: (0, 0))
    return pl.pallas_call(
        functools.partial(_ffn_ln_ple_kernel, tm=tm),
        grid=(n,),
        in_specs=[pl.BlockSpec((1, 1, tm * TOP_K), lambda i: (i, 0, 0), memory_space=pltpu.SMEM),
                  pl.BlockSpec((1, 1, tm * TOP_K), lambda i: (jnp.minimum(i + 1, n - 1), 0, 0),
                               memory_space=pltpu.SMEM),
                  pl.BlockSpec((tm, TOP_K), lambda i: (i, 0)),
                  row_spec,
                  pl.BlockSpec((tm, PLE_DIM), lambda i: (i, 0)),
                  vec_spec, vec_spec,
                  pl.BlockSpec((D_MODEL, D_MODEL), lambda i: (0, 0)),
                  pl.BlockSpec((PLE_DIM, D_MODEL), lambda i: (0, 0)),
                  pl.BlockSpec(memory_space=pl.ANY)],
        out_specs=[row_spec, row_spec],
        out_shape=[jax.ShapeDtypeStruct((T, D_MODEL), F32), jax.ShapeDtypeStruct((T, D_MODEL), BF16)],
        scratch_shapes=[pltpu.VMEM((2, TOP_K, tm * ROW_CHUNKS, LANES), F32), pltpu.SemaphoreType.DMA((2,))],
        compiler_params=_params("arbitrary"),
        name="ffn_ln_ple",
    )(dest3, dest3, gates, h, p, ln_g.reshape(1, D_MODEL), ln_b.reshape(1, D_MODEL), wg_bf, wp_bf, ys)


def moe_layer(h1, layer, w_router, b_router, w_gu, b_gu, w_d, b_d):
    T = h1.shape[0]
    bm = MOE_BLOCK
    idx, gates, rank, counts = router(h1, w_router, b_router)
    padded = (counts + bm - 1) // bm * bm
    pend = jnp.cumsum(padded)
    pstart = pend - padded
    onehot = idx[..., None] == jnp.arange(N_EXPERTS, dtype=jnp.int32)
    dest = (jnp.sum(jnp.where(onehot, pstart, 0), axis=-1) + rank).astype(jnp.int32)
    dest8 = dest * ROW_CHUNKS
    n_slots = T * TOP_K + N_EXPERTS * bm
    n_blocks = n_slots // bm
    block_start = jnp.arange(n_blocks, dtype=jnp.int32) * bm
    block_expert = jnp.minimum(jnp.sum(block_start[:, None] >= pend[None, :], axis=1),
                               N_EXPERTS - 1).astype(jnp.int32)
    n_active = (pend[-1:] // bm).astype(jnp.int32)
    zfill8 = jnp.where(padded > 0, (pend - bm) * ROW_CHUNKS, -1).astype(jnp.int32)
    tt = _tile(T, DISPATCH_TOKENS)
    nt = T // tt
    oh_t = onehot.reshape(nt, tt * TOP_K, N_EXPERTS)
    tile_cnt = jnp.sum(oh_t, axis=1, dtype=jnp.int32)
    tile_base = jnp.cumsum(tile_cnt, axis=0) - tile_cnt
    stage_off = jnp.cumsum(tile_cnt, axis=1) - tile_cnt
    shift = jnp.sum(jnp.where(oh_t, (stage_off - tile_base)[:, None, :], 0), axis=-1)
    stage_pos8 = (shift + rank.reshape(nt, tt * TOP_K)).astype(jnp.int32) * ROW_CHUNKS
    run_src8 = (stage_off * ROW_CHUNKS).reshape(-1).astype(jnp.int32)
    run_dst8 = ((pstart[None, :] + tile_base) * ROW_CHUNKS).reshape(-1).astype(jnp.int32)
    xs = dispatch_rows(h1, stage_pos8, run_src8, run_dst8, tile_cnt.reshape(-1), zfill8, n_slots, bm)
    ys = moe_mlp(xs, block_expert, n_active, layer, w_gu, b_gu, w_d, b_d, bm)
    return ys, dest8, gates


def _rms_rows(x, g):
    return x * lax.rsqrt(jnp.mean(x * x, axis=-1, keepdims=True) + RMS_EPS) * g


def _mla_kv_kernel(x_ref, wa_ref, g_ref, wkn_ref, wv_ref, cm_ref, sm_ref, k_ref, v_ref):
    a = jnp.dot(x_ref[...], wa_ref[...], preferred_element_type=F32)
    c_kv = _rms_rows(a[:, :KV_LORA], g_ref[...]).astype(BF16)
    kr = a[:, KV_LORA:KV_LORA + LANES] * cm_ref[...] + a[:, KV_LORA + LANES:] * sm_ref[...]
    kr = kr.astype(BF16)
    kn = jnp.dot(c_kv, wkn_ref[...], preferred_element_type=F32)
    v_ref[...] = jnp.dot(c_kv, wv_ref[...], preferred_element_type=F32).astype(BF16)
    for h in range(MLA_HEADS):
        k_ref[:, 2 * h * LANES:(2 * h + 1) * LANES] = kn[:, h * QK_NOPE:(h + 1) * QK_NOPE].astype(BF16)
        k_ref[:, (2 * h + 1) * LANES:(2 * h + 2) * LANES] = kr


def _pad_rope_cols(w, swap):
    K = w.shape[0]
    w = w.reshape(K, -1, QK_ROPE)
    if swap:
        w = jnp.concatenate([w[..., QK_ROPE // 2:], w[..., :QK_ROPE // 2]], axis=-1)
    w = jnp.concatenate([w, jnp.zeros_like(w)], axis=-1)
    return w.reshape(K, -1)


def mla_shared_kv(hb, w_kv_a, kv_norm_g, w_kv_b, cm, sm):
    T = hb.shape[0]
    tm = _tile(T, 512)
    w_r = w_kv_a[:, KV_LORA:]
    wa = jnp.concatenate([w_kv_a[:, :KV_LORA], _pad_rope_cols(w_r, False), _pad_rope_cols(w_r, True)],
                         axis=1).astype(BF16)
    wb = w_kv_b.reshape(KV_LORA, MLA_HEADS, QK_NOPE + V_HEAD)
    wkn = wb[:, :, :QK_NOPE].reshape(KV_LORA, MLA_HEADS * QK_NOPE).astype(BF16)
    wv = wb[:, :, QK_NOPE:].reshape(KV_LORA, MLA_HEADS * V_HEAD).astype(BF16)
    full = lambda shape: pl.BlockSpec(shape, lambda i: (0, 0))
    return pl.pallas_call(
        _mla_kv_kernel,
        grid=(T // tm,),
        in_specs=[pl.BlockSpec((tm, D_MODEL), lambda i: (i, 0)), full(wa.shape), full((1, KV_LORA)),
                  full(wkn.shape), full(wv.shape),
                  pl.BlockSpec((tm, LANES), lambda i: (i, 0)), pl.BlockSpec((tm, LANES), lambda i: (i, 0))],
        out_specs=[pl.BlockSpec((tm, MLA_HEADS * 2 * LANES), lambda i: (i, 0)),
                   pl.BlockSpec((tm, MLA_HEADS * V_HEAD), lambda i: (i, 0))],
        out_shape=[jax.ShapeDtypeStruct((T, MLA_HEADS * 2 * LANES), BF16),
                   jax.ShapeDtypeStruct((T, MLA_HEADS * V_HEAD), BF16)],
        compiler_params=_params("arbitrary"),
        name="mla_kv",
    )(hb, wa, kv_norm_g.reshape(1, KV_LORA), wkn, wv, cm, sm)


def _mla_q_kernel(x_ref, wa_ref, g_ref, wb_ref, cm_ref, sm_ref, q_ref):
    scale = (QK_NOPE + QK_ROPE) ** -0.5 * LOG2_E
    qa = jnp.dot(x_ref[...], wa_ref[...], preferred_element_type=F32)
    qc = _rms_rows(qa, g_ref[...]).astype(BF16)
    q = jnp.dot(qc, wb_ref[...], preferred_element_type=F32)
    cm = cm_ref[...] * scale
    sm = sm_ref[...] * scale
    n = MLA_HEADS * LANES
    for h in range(MLA_HEADS):
        lo = h * LANES
        q_ref[:, 2 * lo:2 * lo + LANES] = (q[:, lo:lo + LANES] * scale).astype(BF16)
        rope = q[:, n + lo:n + lo + LANES] * cm + q[:, 2 * n + lo:2 * n + lo + LANES] * sm
        q_ref[:, 2 * lo + LANES:2 * lo + 2 * LANES] = rope.astype(BF16)


def mla_q(hb, w_q_a, q_norm_g, w_q_b, cm, sm):
    T = hb.shape[0]
    tm = _tile(T, 512)
    wq = w_q_b.reshape(Q_LORA, MLA_HEADS, QK_NOPE + QK_ROPE)
    w_n = wq[:, :, :QK_NOPE].reshape(Q_LORA, MLA_HEADS * QK_NOPE)
    w_r = wq[:, :, QK_NOPE:].reshape(Q_LORA, MLA_HEADS * QK_ROPE)
    wb = jnp.concatenate([w_n, _pad_rope_cols(w_r, False), _pad_rope_cols(w_r, True)], axis=1).astype(BF16)
    full = lambda shape: pl.BlockSpec(shape, lambda i: (0, 0))
    return pl.pallas_call(
        _mla_q_kernel,
        grid=(T // tm,),
        in_specs=[pl.BlockSpec((tm, D_MODEL), lambda i: (i, 0)), full((D_MODEL, Q_LORA)), full((1, Q_LORA)),
                  full(wb.shape),
                  pl.BlockSpec((tm, LANES), lambda i: (i, 0)), pl.BlockSpec((tm, LANES), lambda i: (i, 0))],
        out_specs=pl.BlockSpec((tm, MLA_HEADS * 2 * LANES), lambda i: (i, 0)),
        out_shape=jax.ShapeDtypeStruct((T, MLA_HEADS * 2 * LANES), BF16),
        compiler_params=_params("arbitrary"),
        name="mla_q",
    )(hb, w_q_a.astype(BF16), q_norm_g.reshape(1, Q_LORA), wb, cm, sm)


def _flash_kernel(q_ref, k_ref, v_ref, o_ref, m_ref, l_ref, acc_ref, *, blk):
    i = pl.program_id(2)
    q = q_ref[...]
    m_ref[...] = jnp.full_like(m_ref, -jnp.inf)
    l_ref[...] = jnp.zeros_like(l_ref)
    acc_ref[...] = jnp.zeros_like(acc_ref)

    def step(j, masked):
        start = pl.multiple_of(j * blk, blk)
        kj = k_ref[pl.ds(start, blk), :]
        vj = v_ref[pl.ds(start, blk), :]
        s = lax.dot_general(q, kj, NT_DIMS, preferred_element_type=F32)
        if masked:
            r = lax.broadcasted_iota(jnp.int32, (blk, blk), 0)
            c = lax.broadcasted_iota(jnp.int32, (blk, blk), 1)
            s = jnp.where(c <= r, s, -jnp.inf)
        m_prev = m_ref[...]
        m_new = jnp.maximum(m_prev, jnp.max(s, axis=-1, keepdims=True))
        alpha = jnp.exp2(m_prev - m_new)
        p = jnp.exp2(s - pltpu.repeat(m_new, blk // LANES, axis=1))
        l_ref[...] = alpha * l_ref[...] + jnp.sum(p, axis=-1, keepdims=True)
        acc_ref[...] = alpha * acc_ref[...] + jnp.dot(p.astype(BF16), vj, preferred_element_type=F32)
        m_ref[...] = m_new

    def pair(j, carry):
        step(2 * j, False)
        step(2 * j + 1, False)
        return carry

    lax.fori_loop(0, i // 2, pair, 0)

    @pl.when(i % 2 == 1)
    def _():
        step(i - 1, False)

    step(i, True)
    o_ref[...] = (acc_ref[...] / l_ref[...]).astype(o_ref.dtype)


def flash_attention(q, k, v, B, S):
    T = B * S
    blk = _tile(S, ATTN_BLOCK)
    nq = S // blk
    return pl.pallas_call(
        functools.partial(_flash_kernel, blk=blk),
        grid=(B, MLA_HEADS, nq),
        in_specs=[pl.BlockSpec((blk, 2 * LANES), lambda b, h, i: (b * nq + i, h)),
                  pl.BlockSpec((S, 2 * LANES), lambda b, h, i: (b, h)),
                  pl.BlockSpec((S, V_HEAD), lambda b, h, i: (b, h))],
        out_specs=pl.BlockSpec((blk, V_HEAD), lambda b, h, i: (b * nq + i, h)),
        out_shape=jax.ShapeDtypeStruct((T, MLA_HEADS * V_HEAD), BF16),
        scratch_shapes=[pltpu.VMEM((blk, LANES), F32), pltpu.VMEM((blk, LANES), F32),
                        pltpu.VMEM((blk, V_HEAD), F32)],
        compiler_params=_params("arbitrary", "arbitrary", "arbitrary"),
        name="mla_flash",
    )(q, k, v)


def kernel(x, p, positions, ret_w_in, ret_gn_g, ret_gn_b, ret_w_out, mla_w_kv_a, mla_kv_norm_g, mla_w_kv_b,
           mla_w_q_a, mla_q_norm_g, mla_w_q_b, mla_w_o, ln_mix_g, ln_mix_b, ln_ffn_g, ln_ffn_b,
           moe_w_router, moe_b_router, moe_w_gate_up, moe_b_gate_up, moe_w_down, moe_b_down,
           ple_w_gate, ple_w_proj):
    B, S, D = x.shape
    T = B * S
    cos_r, sin_r, cos_m, sin_m = rope_tables(positions)
    h = x.reshape(T, D)
    hb = h.astype(BF16)
    kv = None
    for i in range(DEPTH):
        if i < N_A:
            qk, vg = ret_projections(hb, ret_w_in[i].astype(BF16), cos_r, sin_r)
            z = retention(qk, vg, ret_gn_g[i], ret_gn_b[i], B, S)
            w_out = ret_w_out[i].astype(BF16)
        else:
            if kv is None:
                kv = mla_shared_kv(hb, mla_w_kv_a, mla_kv_norm_g, mla_w_kv_b, cos_m, sin_m)
            j = i - N_A
            q = mla_q(hb, mla_w_q_a[j], mla_q_norm_g[j], mla_w_q_b[j], cos_m, sin_m)
            z = flash_attention(q, kv[0], kv[1], B, S)
            w_out = mla_w_o[j].astype(BF16)
        h1 = out_proj_ln(z, w_out, h, ln_mix_g[i], ln_mix_b[i])
        ys, dest8, gates = moe_layer(h1, i, moe_w_router[i], moe_b_router[i], moe_w_gate_up, moe_b_gate_up,
                                     moe_w_down, moe_b_down)
        h, hb = ffn_ln_ple(ys, dest8, gates, h1, p[i].reshape(T, PLE_DIM), ln_ffn_g[i], ln_ffn_b[i],
                           ple_w_gate[i].astype(BF16), ple_w_proj[i].astype(BF16))
    return h.reshape(B, S, D)
```

```python
import functools

import jax
import jax.numpy as jnp
from jax import lax
from jax.experimental import pallas as pl
from jax.experimental.pallas import tpu as pltpu

D_MODEL = 1024
DEPTH = 4
N_A = DEPTH // 2
RET_HEADS = 4
RET_QK_DIM = D_MODEL // RET_HEADS
RET_V_DIM = 2 * RET_QK_DIM
GN_EPS = 1e-6
MLA_HEADS = 8
QK_NOPE = 128
QK_ROPE = 64
V_HEAD = 128
Q_LORA = 256
KV_LORA = 256
RMS_EPS = 1e-6
ROPE_THETA = 10000.0
N_EXPERTS = 32
TOP_K = 4
D_FF = D_MODEL
SWIGLU_LIMIT = 7.0
SWIGLU_ALPHA = 1.702
PLE_DIM = 256
DN_ALPHA = (2 * DEPTH) ** 0.25
LN_EPS = 1e-5
LOG2_E = 1.4426950408889634

LANES = 128
ROW_CHUNKS = D_MODEL // LANES
VMEM_LIMIT_BYTES = 56 * 1024 * 1024

RET_CHUNK = 256
RET_BLOCK = 1024
ATTN_BLOCK_Q = 1024
MOE_BLOCK = 512
DISPATCH_TOKENS = 512
FFN_ROW_CHUNKS = 2

F32 = jnp.float32
BF16 = jnp.bfloat16
NT_DIMS = (((1,), (1,)), ((), ()))
TN_DIMS = (((0,), (0,)), ((), ()))


def _tile(n, pref):
    t = min(n, pref)
    assert n % t == 0, (n, pref)
    return t


def _params(*sem):
    return pltpu.CompilerParams(dimension_semantics=sem, vmem_limit_bytes=VMEM_LIMIT_BYTES)


def _layer_norm_rows(u, g, b):
    mu = jnp.mean(u, axis=-1, keepdims=True)
    d = u - mu
    var = jnp.mean(d * d, axis=-1, keepdims=True)
    return d * lax.rsqrt(var + LN_EPS) * g + b


def _rope_table_kernel(pos_ref, inv_r_ref, inv_m_ref, sign_m_ref, cr_ref, sr_ref, cm_ref, sm_ref):
    pos = pos_ref[...].astype(F32)
    ang_r = pos * inv_r_ref[...]
    cr_ref[...] = jnp.cos(ang_r)
    sr_ref[...] = jnp.sin(ang_r)
    ang_m = pos * inv_m_ref[...]
    cm_ref[...] = jnp.cos(ang_m)
    sm_ref[...] = jnp.sin(ang_m) * sign_m_ref[...]


def rope_tables(positions):
    T = positions.size
    tm = _tile(T, 1024)
    pos = positions.reshape(T, 1)
    half_r = RET_QK_DIM // 2
    inv_r = ROPE_THETA ** (-jnp.arange(0, RET_QK_DIM, 2, dtype=F32) / RET_QK_DIM)
    inv_m = ROPE_THETA ** (-jnp.arange(0, QK_ROPE, 2, dtype=F32) / QK_ROPE)
    half_m = QK_ROPE // 2
    lane = jnp.arange(LANES)
    inv_m = inv_m[lane % half_m]
    sign_m = jnp.where((lane % QK_ROPE) < half_m, -1.0, 1.0).astype(F32)
    assert half_r == LANES
    row = lambda v: v.reshape(1, LANES)
    vec_spec = pl.BlockSpec((1, LANES), lambda i: (0, 0))
    tab_spec = pl.BlockSpec((tm, LANES), lambda i: (i, 0))
    out = jax.ShapeDtypeStruct((T, LANES), F32)
    return pl.pallas_call(
        _rope_table_kernel,
        grid=(T // tm,),
        in_specs=[pl.BlockSpec((tm, 1), lambda i: (i, 0)), vec_spec, vec_spec, vec_spec],
        out_specs=[tab_spec] * 4,
        out_shape=[out] * 4,
        compiler_params=_params("arbitrary"),
        name="rope_tables",
    )(pos, row(inv_r), row(inv_m), row(sign_m))


def _ret_qk_kernel(x_ref, w_ref, cos_ref, sin_ref, o_ref):
    acc = jnp.dot(x_ref[...], w_ref[...], preferred_element_type=F32)
    scale = jnp.where(pl.program_id(0) == 1, RET_QK_DIM ** -0.5, 1.0).astype(F32)
    c = cos_ref[...] * scale
    s = sin_ref[...] * scale
    half = RET_QK_DIM // 2
    for h in range(RET_HEADS):
        lo = h * RET_QK_DIM
        x1 = acc[:, lo:lo + half]
        x2 = acc[:, lo + half:lo + 2 * half]
        o_ref[:, lo:lo + half] = (x1 * c - x2 * s).astype(o_ref.dtype)
        o_ref[:, lo + half:lo + 2 * half] = (x1 * s + x2 * c).astype(o_ref.dtype)


def _matmul_kernel(x_ref, w_ref, o_ref):
    o_ref[...] = jnp.dot(x_ref[...], w_ref[...], preferred_element_type=F32).astype(o_ref.dtype)


def ret_projections(hb, w_in_bf, cos_r, sin_r):
    T = hb.shape[0]
    tm = _tile(T, 1024)
    tn = D_MODEL
    n_qk = 2 * RET_HEADS * RET_QK_DIM // tn
    n_vg = 2 * RET_HEADS * RET_V_DIM // tn
    x_spec = pl.BlockSpec((tm, D_MODEL), lambda j, i: (i, 0))
    tab_spec = pl.BlockSpec((tm, LANES), lambda j, i: (i, 0))
    qk = pl.pallas_call(
        _ret_qk_kernel,
        grid=(n_qk, T // tm),
        in_specs=[x_spec, pl.BlockSpec((D_MODEL, tn), lambda j, i: (0, j)), tab_spec, tab_spec],
        out_specs=pl.BlockSpec((tm, tn), lambda j, i: (i, j)),
        out_shape=jax.ShapeDtypeStruct((T, n_qk * tn), BF16),
        compiler_params=_params("arbitrary", "arbitrary"),
        name="ret_qk_proj",
    )(hb, w_in_bf, cos_r, sin_r)
    vg = pl.pallas_call(
        _matmul_kernel,
        grid=(n_vg, T // tm),
        in_specs=[x_spec, pl.BlockSpec((D_MODEL, tn), lambda j, i: (0, j + n_qk))],
        out_specs=pl.BlockSpec((tm, tn), lambda j, i: (i, j)),
        out_shape=jax.ShapeDtypeStruct((T, n_vg * tn), BF16),
        compiler_params=_params("arbitrary", "arbitrary"),
        name="ret_vg_proj",
    )(hb, w_in_bf)
    return qk, vg


def _retention_kernel(q_ref, k_ref, v_ref, g_ref, dm_ref, dq_ref, dk_ref, dc_ref, gng_ref, gnb_ref,
                      z_ref, state_ref, *, chunk, n_chunks):
    @pl.when(pl.program_id(2) == 0)
    def _():
        state_ref[...] = jnp.zeros_like(state_ref)

    dm = dm_ref[0]
    dq = dq_ref[0]
    dk = dk_ref[0]
    dc = dc_ref[0]
    gng = gng_ref[...]
    gnb = gnb_ref[...]
    for c in range(n_chunks):
        rows = slice(c * chunk, (c + 1) * chunk)
        q = q_ref[rows, :]
        k = k_ref[rows, :]
        v = v_ref[rows, :]
        st = state_ref[...]
        inner = lax.dot_general(q, k, NT_DIMS, preferred_element_type=F32) * dm
        y = jnp.dot(inner.astype(BF16), v, preferred_element_type=F32)
        y = y + jnp.dot(q, st.astype(BF16), preferred_element_type=F32) * dq
        kd = (k.astype(F32) * dk).astype(BF16)
        state_ref[...] = st * dc + lax.dot_general(kd, v, TN_DIMS, preferred_element_type=F32)
        mu = jnp.mean(y, axis=-1, keepdims=True)
        d = y - mu
        var = jnp.mean(d * d, axis=-1, keepdims=True)
        yn = d * lax.rsqrt(var + GN_EPS) * gng + gnb
        g = g_ref[rows, :].astype(F32)
        z_ref[rows, :] = (g * jax.nn.sigmoid(g) * yn).astype(z_ref.dtype)


def retention(qk, vg, gn_g, gn_b, B, S):
    T = B * S
    H, dk, dv = RET_HEADS, RET_QK_DIM, RET_V_DIM
    C = _tile(S, RET_CHUNK)
    L = _tile(S, RET_BLOCK)
    nl = S // L
    log_g = jnp.log(1.0 - 2.0 ** (-5.0 - jnp.arange(H, dtype=F32)))
    idx = jnp.arange(C, dtype=F32)
    diff = idx[:, None] - idx[None, :]
    dm = jnp.where(diff >= 0, jnp.exp(log_g[:, None, None] * jnp.maximum(diff, 0.0)), 0.0)
    dq = jnp.exp(log_g[:, None] * (idx + 1.0))[:, :, None]
    dkk = jnp.exp(log_g[:, None] * (C - 1.0 - idx))[:, :, None]
    dc = jnp.broadcast_to(jnp.exp(log_g * C)[:, None, None], (H, 1, dv))
    kern = functools.partial(_retention_kernel, chunk=C, n_chunks=L // C)
    return pl.pallas_call(
        kern,
        grid=(B, H, nl),
        in_specs=[
            pl.BlockSpec((L, dk), lambda b, h, n: (b * nl + n, h)),
            pl.BlockSpec((L, dk), lambda b, h, n: (b * nl + n, H + h)),
            pl.BlockSpec((L, dv), lambda b, h, n: (b * nl + n, h)),
            pl.BlockSpec((L, dv), lambda b, h, n: (b * nl + n, H + h)),
            pl.BlockSpec((1, C, C), lambda b, h, n: (h, 0, 0)),
            pl.BlockSpec((1, C, 1), lambda b, h, n: (h, 0, 0)),
            pl.BlockSpec((1, C, 1), lambda b, h, n: (h, 0, 0)),
            pl.BlockSpec((1, 1, dv), lambda b, h, n: (h, 0, 0)),
            pl.BlockSpec((1, dv), lambda b, h, n: (0, h)),
            pl.BlockSpec((1, dv), lambda b, h, n: (0, h)),
        ],
        out_specs=pl.BlockSpec((L, dv), lambda b, h, n: (b * nl + n, h)),
        out_shape=jax.ShapeDtypeStruct((T, H * dv), BF16),
        scratch_shapes=[pltpu.VMEM((dk, dv), F32)],
        compiler_params=_params("arbitrary", "arbitrary", "arbitrary"),
        name="retention",
    )(qk, qk, vg, vg, dm, dq, dkk, dc, gn_g.reshape(1, H * dv), gn_b.reshape(1, H * dv))


def _out_ln_kernel(z_ref, w_ref, h_ref, g_ref, b_ref, o_ref):
    mix = jnp.dot(z_ref[...], w_ref[...], preferred_element_type=F32)
    u = DN_ALPHA * h_ref[...] + mix
    o_ref[...] = _layer_norm_rows(u, g_ref[...], b_ref[...])


def out_proj_ln(z, w_bf, h, ln_g, ln_b):
    T, K = z.shape
    tm = _tile(T, 512)
    row_spec = pl.BlockSpec((tm, D_MODEL), lambda i: (i, 0))
    vec_spec = pl.BlockSpec((1, D_MODEL), lambda i: (0, 0))
    return pl.pallas_call(
        _out_ln_kernel,
        grid=(T // tm,),
        in_specs=[pl.BlockSpec((tm, K), lambda i: (i, 0)), pl.BlockSpec((K, D_MODEL), lambda i: (0, 0)),
                  row_spec, vec_spec, vec_spec],
        out_specs=row_spec,
        out_shape=jax.ShapeDtypeStruct((T, D_MODEL), F32),
        compiler_params=_params("arbitrary"),
        name="out_proj_ln",
    )(z, w_bf, h, ln_g.reshape(1, D_MODEL), ln_b.reshape(1, D_MODEL))


def _router_kernel(h_ref, w_ref, b_ref, slab_ref, cnt_ref, carry_ref, *, tm):
    @pl.when(pl.program_id(0) == 0)
    def _():
        carry_ref[...] = jnp.zeros_like(carry_ref)

    x = h_ref[...]
    w = w_ref[...]
    x_hi = x.astype(BF16)
    x_lo = (x - x_hi.astype(F32)).astype(BF16)
    w_hi = w.astype(BF16)
    w_lo = (w - w_hi.astype(F32)).astype(BF16)
    logits = (jnp.dot(x_hi, w_hi, preferred_element_type=F32)
              + (jnp.dot(x_hi, w_lo, preferred_element_type=F32)
                 + jnp.dot(x_lo, w_hi, preferred_element_type=F32))) + b_ref[...]
    e_iota = lax.broadcasted_iota(jnp.int32, (tm, N_EXPERTS), 1).astype(F32)
    work = logits
    onehot = jnp.zeros((tm, N_EXPERTS), F32)
    top_vals, top_idx = [], []
    for _ in range(TOP_K):
        m = jnp.max(work, axis=-1, keepdims=True)
        idx = jnp.min(jnp.where(work == m, e_iota, float(N_EXPERTS)), axis=-1, keepdims=True)
        top_vals.append(m)
        top_idx.append(idx)
        hit = e_iota == idx
        work = jnp.where(hit, -jnp.inf, work)
        onehot = jnp.where(hit, 1.0, onehot)
    exps = [jnp.exp(v - top_vals[0]) for v in top_vals]
    denom = exps[0] + exps[1] + exps[2] + exps[3]
    gates = [e / denom for e in exps]

    r = lax.broadcasted_iota(jnp.int32, (tm, tm), 0)
    c = lax.broadcasted_iota(jnp.int32, (tm, tm), 1)
    tri = jnp.where(c < r, 1.0, 0.0).astype(BF16)
    before = jnp.dot(tri, onehot.astype(BF16), preferred_element_type=F32) + carry_ref[0:1, 0:N_EXPERTS]
    ranks = [jnp.sum(jnp.where(e_iota == idx, before, 0.0), axis=-1, keepdims=True) for idx in top_idx]

    lane = lax.broadcasted_iota(jnp.int32, (tm, LANES), 1)
    slab = jnp.zeros((tm, LANES), F32)
    for k in range(TOP_K):
        slab = jnp.where(lane == k, top_idx[k], slab)
        slab = jnp.where(lane == TOP_K + k, gates[k], slab)
        slab = jnp.where(lane == 2 * TOP_K + k, ranks[k], slab)
    slab_ref[...] = slab

    carry_ref[0:1, 0:N_EXPERTS] = carry_ref[0:1, 0:N_EXPERTS] + jnp.sum(onehot, axis=0, keepdims=True)
    cnt_ref[...] = carry_ref[...]


def router(h, w_router, b_router):
    T = h.shape[0]
    tm = _tile(T, 512)
    slab, cnt = pl.pallas_call(
        functools.partial(_router_kernel, tm=tm),
        grid=(T // tm,),
        in_specs=[pl.BlockSpec((tm, D_MODEL), lambda i: (i, 0)),
                  pl.BlockSpec((D_MODEL, N_EXPERTS), lambda i: (0, 0)),
                  pl.BlockSpec((1, N_EXPERTS), lambda i: (0, 0))],
        out_specs=[pl.BlockSpec((tm, LANES), lambda i: (i, 0)), pl.BlockSpec((8, LANES), lambda i: (0, 0))],
        out_shape=[jax.ShapeDtypeStruct((T, LANES), F32), jax.ShapeDtypeStruct((8, LANES), F32)],
        scratch_shapes=[pltpu.VMEM((8, LANES), F32)],
        compiler_params=_params("arbitrary"),
        name="router",
    )(h, w_router, b_router.reshape(1, N_EXPERTS))
    idx = slab[:, 0:TOP_K].astype(jnp.int32)
    gates = slab[:, TOP_K:2 * TOP_K]
    rank = slab[:, 2 * TOP_K:3 * TOP_K].astype(jnp.int32)
    counts = cnt[0, :N_EXPERTS].astype(jnp.int32)
    return idx, gates, rank, counts


def _rows_to_tiles(dst_ref, x, n):
    for c in range(ROW_CHUNKS):
        dst_ref[pl.ds(c, n, stride=ROW_CHUNKS), :] = x[:, c * LANES:(c + 1) * LANES]


def _tiles_to_rows(src_ref, n):
    return jnp.concatenate([src_ref[pl.ds(c, n, stride=ROW_CHUNKS), :] for c in range(ROW_CHUNKS)], axis=1)


def _dispatch_kernel(zfill_ref, rsrc_ref, rdst_ref, rlen_ref, q_ref, h_ref, xs_ref,
                     rows, stage, sems, zsem, *, tt, bm):
    i = pl.program_id(0)
    last = pl.num_programs(0) - 1
    slot = i % 2

    def for_each_run_piece(step, fn):
        buf = stage.at[step % 2]
        sem = sems.at[step % 2]

        def per_expert(e, carry):
            n = rlen_ref[step * N_EXPERTS + e]
            src = rsrc_ref[step * N_EXPERTS + e]
            dst = rdst_ref[step * N_EXPERTS + e]
            done = jnp.int32(0)
            piece = tt
            while piece >= 1:
                take = (n & piece) != 0
                size = piece * ROW_CHUNKS

                @pl.when(take)
                def _(done=done, size=size):
                    s = pl.multiple_of(src + done, ROW_CHUNKS)
                    d = pl.multiple_of(dst + done, ROW_CHUNKS)
                    fn(pltpu.make_async_copy(buf.at[pl.ds(s, size)], xs_ref.at[pl.ds(d, size)], sem))

                done = done + jnp.where(take, size, 0)
                piece //= 2
            return carry
        lax.fori_loop(0, N_EXPERTS, per_expert, 0)

    @pl.when(i == 0)
    def _():
        zeros = stage.at[0, pl.ds(0, bm * ROW_CHUNKS)]
        zeros[...] = jnp.zeros_like(zeros)

        def zero_copy(e):
            start = pl.multiple_of(zfill_ref[e], bm * ROW_CHUNKS)
            return pltpu.make_async_copy(zeros, xs_ref.at[pl.ds(start, bm * ROW_CHUNKS)], zsem)

        for e in range(N_EXPERTS):
            @pl.when(zfill_ref[e] >= 0)
            def _():
                zero_copy(e).start()
        for e in range(N_EXPERTS):
            @pl.when(zfill_ref[e] >= 0)
            def _():
                zero_copy(e).wait()

    @pl.when(i >= 2)
    def _():
        for_each_run_piece(i - 2, lambda cp: cp.wait())

    _rows_to_tiles(rows, h_ref[...], tt)

    def sort_rows(t, carry):
        row = rows[pl.ds(pl.multiple_of(t * ROW_CHUNKS, ROW_CHUNKS), ROW_CHUNKS), :]
        for k in range(TOP_K):
            q = pl.multiple_of(q_ref[0, 0, t * TOP_K + k], ROW_CHUNKS)
            stage[slot, pl.ds(q, ROW_CHUNKS), :] = row
        return carry

    lax.fori_loop(0, tt, sort_rows, 0, unroll=8)
    for_each_run_piece(i, lambda cp: cp.start())

    @pl.when(jnp.logical_and(i == last, i >= 1))
    def _():
        for_each_run_piece(i - 1, lambda cp: cp.wait())

    @pl.when(i == last)
    def _():
        for_each_run_piece(i, lambda cp: cp.wait())


def dispatch_rows(h, stage_pos8, run_src8, run_dst8, run_len, zfill8, n_slots, bm):
    T = h.shape[0]
    tt = _tile(T, DISPATCH_TOKENS)
    pos3 = stage_pos8.reshape(T // tt, 1, tt * TOP_K)
    assert tt * TOP_K >= bm
    return pl.pallas_call(
        functools.partial(_dispatch_kernel, tt=tt, bm=bm),
        grid_spec=pltpu.PrefetchScalarGridSpec(
            num_scalar_prefetch=4,
            grid=(T // tt,),
            in_specs=[pl.BlockSpec((1, 1, tt * TOP_K), lambda i, *_: (i, 0, 0), memory_space=pltpu.SMEM),
                      pl.BlockSpec((tt, D_MODEL), lambda i, *_: (i, 0))],
            out_specs=pl.BlockSpec(memory_space=pl.ANY),
            scratch_shapes=[pltpu.VMEM((tt * ROW_CHUNKS, LANES), F32),
                            pltpu.VMEM((2, tt * TOP_K * ROW_CHUNKS, LANES), F32),
                            pltpu.SemaphoreType.DMA((2,)), pltpu.SemaphoreType.DMA],
        ),
        out_shape=jax.ShapeDtypeStruct((n_slots * ROW_CHUNKS, LANES), F32),
        compiler_params=_params("arbitrary"),
        name="moe_dispatch",
    )(zfill8, run_src8, run_dst8, run_len, pos3, h)


def _moe_mlp_kernel(be_ref, na_ref, x_ref, wgu_ref, bgu_ref, wd_ref, bd_ref, y_ref, wgu_bf, wd_bf, *, bm):
    i = pl.program_id(0)
    prev = be_ref[jnp.maximum(i - 1, 0)]
    active = i < na_ref[0]

    @pl.when(jnp.logical_and(active, jnp.logical_or(i == 0, be_ref[i] != prev)))
    def _():
        wgu_bf[...] = wgu_ref[...].astype(BF16)
        wd_bf[...] = wd_ref[...].astype(BF16)

    @pl.when(active)
    def _():
        x = _tiles_to_rows(x_ref, bm).astype(BF16)
        hgu = jnp.dot(x, wgu_bf[...], preferred_element_type=F32) + bgu_ref[...]
        gate = jnp.minimum(hgu[:, :D_FF], SWIGLU_LIMIT)
        up = jnp.clip(hgu[:, D_FF:], -SWIGLU_LIMIT, SWIGLU_LIMIT)
        act = gate * jax.nn.sigmoid(SWIGLU_ALPHA * gate) * (up + 1.0)
        y = jnp.dot(act.astype(BF16), wd_bf[...], preferred_element_type=F32) + bd_ref[...]
        _rows_to_tiles(y_ref, y, bm)


def moe_mlp(xs, block_expert, n_active, layer, w_gu, b_gu, w_d, b_d, bm):
    n_blocks = xs.shape[0] // (bm * ROW_CHUNKS)

    def row_map(i, be, na):
        return (jnp.minimum(i, na[0] - 1), 0)

    def w_map(i, be, na):
        return (layer, be[jnp.minimum(i, na[0] - 1)], 0, 0)

    def b_map(i, be, na):
        return (layer * N_EXPERTS + be[jnp.minimum(i, na[0] - 1)], 0, 0)

    return pl.pallas_call(
        functools.partial(_moe_mlp_kernel, bm=bm),
        grid_spec=pltpu.PrefetchScalarGridSpec(
            num_scalar_prefetch=2,
            grid=(n_blocks,),
            in_specs=[pl.BlockSpec((bm * ROW_CHUNKS, LANES), row_map),
                      pl.BlockSpec((None, None, D_MODEL, 2 * D_FF), w_map),
                      pl.BlockSpec((None, 1, 2 * D_FF), b_map),
                      pl.BlockSpec((None, None, D_FF, D_MODEL), w_map),
                      pl.BlockSpec((None, 1, D_MODEL), b_map)],
            out_specs=pl.BlockSpec((bm * ROW_CHUNKS, LANES), row_map),
            scratch_shapes=[pltpu.VMEM((D_MODEL, 2 * D_FF), BF16), pltpu.VMEM((D_FF, D_MODEL), BF16)],
        ),
        out_shape=jax.ShapeDtypeStruct(xs.shape, F32),
        compiler_params=_params("arbitrary"),
        name="moe_mlp",
    )(block_expert, n_active, xs, w_gu, b_gu.reshape(DEPTH * N_EXPERTS, 1, 2 * D_FF), w_d,
      b_d.reshape(DEPTH * N_EXPERTS, 1, D_MODEL))


def _ffn_ln_ple_kernel(dcur_ref, dnext_ref, gates_ref, h_ref, p_ref, g_ref, b_ref, wg_ref, wp_ref, ys_ref,
                       o_ref, ob_ref, ybuf, sems, *, tm):
    i = pl.program_id(0)
    slot = i % 2

    def row_copy(dref, s, t, k):
        d = pl.multiple_of(dref[0, 0, t * TOP_K + k], ROW_CHUNKS)
        dst = ybuf.at[s, k, pl.ds(pl.multiple_of(t * ROW_CHUNKS, ROW_CHUNKS), ROW_CHUNKS)]
        return pltpu.make_async_copy(ys_ref.at[pl.ds(d, ROW_CHUNKS)], dst, sems.at[s])

    def issue(dref, s):
        def body(t, carry):
            for k in range(TOP_K):
                row_copy(dref, s, t, k).start(priority=k % 2)
            return carry
        lax.fori_loop(0, tm, body, 0, unroll=4)

    @pl.when(i == 0)
    def _():
        issue(dcur_ref, 0)

    @pl.when(i + 1 < pl.num_programs(0))
    def _():
        issue(dnext_ref, 1 - slot)

    for k in range(TOP_K):
        pltpu.make_async_copy(ys_ref.at[pl.ds(0, tm * ROW_CHUNKS)], ybuf.at[slot, k], sems.at[slot]).wait()

    sub = tm // FFN_ROW_CHUNKS
    for c in range(FFN_ROW_CHUNKS):
        rows = slice(c * sub, (c + 1) * sub)
        tiles = pl.ds(c * sub * ROW_CHUNKS, sub * ROW_CHUNKS)
        gates = gates_ref[rows, :]
        ffn = _tiles_to_rows(ybuf.at[slot, 0, tiles], sub) * gates[:, 0:1]
        for k in range(1, TOP_K):
            ffn = ffn + _tiles_to_rows(ybuf.at[slot, k, tiles], sub) * gates[:, k:k + 1]
        h2 = _layer_norm_rows(DN_ALPHA * h_ref[rows, :] + ffn, g_ref[...], b_ref[...])
        gate = jax.nn.sigmoid(jnp.dot(h2.astype(BF16), wg_ref[...], preferred_element_type=F32))
        proj = jnp.dot(p_ref[rows, :].astype(BF16), wp_ref[...], preferred_element_type=F32)
        out = h2 + gate * proj
        o_ref[rows, :] = out
        ob_ref[rows, :] = out.astype(BF16)


def ffn_ln_ple(ys, dest8, gates, h, p, ln_g, ln_b, wg_bf, wp_bf):
    T = h.shape[0]
    tm = _tile(T, 512)
    n = T // tm
    dest3 = dest8.reshape(n, 1, tm * TOP_K)
    row_spec = pl.BlockSpec((tm, D_MODEL), lambda i: (i, 0))
    vec_spec = pl.BlockSpec((1, D_MODEL), lambda i: (0, 0))
    return pl.pallas_call(
        functools.partial(_ffn_ln_ple_kernel, tm=tm),
        grid=(n,),
        in_specs=[pl.BlockSpec((1, 1, tm * TOP_K), lambda i: (i, 0, 0), memory_space=pltpu.SMEM),
                  pl.BlockSpec((1, 1, tm * TOP_K), lambda i: (jnp.minimum(i + 1, n - 1), 0, 0),
                               memory_space=pltpu.SMEM),
                  pl.BlockSpec((tm, TOP_K), lambda i: (i, 0)),
                  row_spec,
                  pl.BlockSpec((tm, PLE_DIM), lambda i: (i, 0)),
                  vec_spec, vec_spec,
                  pl.BlockSpec((D_MODEL, D_MODEL), lambda i: (0, 0)),
                  pl.BlockSpec((PLE_DIM, D_MODEL), lambda i: (0, 0)),
                  pl.BlockSpec(memory_space=pl.ANY)],
        out_specs=[row_spec, row_spec],
        out_shape=[jax.ShapeDtypeStruct((T, D_MODEL), F32), jax.ShapeDtypeStruct((T, D_MODEL), BF16)],
        scratch_shapes=[pltpu.VMEM((2, TOP_K, tm * ROW_CHUNKS, LANES), F32), pltpu.SemaphoreType.DMA((2,))],
        compiler_params=_params("arbitrary"),
        name="ffn_ln_ple",
    )(dest3, dest3, gates, h, p, ln_g.reshape(1, D_MODEL), ln_b.reshape(1, D_MODEL), wg_bf, wp_bf, ys)


def moe_layer(h1, layer, w_router, b_router, w_gu, b_gu, w_d, b_d):
    T = h1.shape[0]
    bm = MOE_BLOCK
    idx, gates, rank, counts = router(h1, w_router, b_router)
    padded = (counts + bm - 1) // bm * bm
    pend = jnp.cumsum(padded)
    pstart = pend - padded
    onehot = idx[..., None] == jnp.arange(N_EXPERTS, dtype=jnp.int32)
    dest = (jnp.sum(jnp.where(onehot, pstart, 0), axis=-1) + rank).astype(jnp.int32)
    dest8 = dest * ROW_CHUNKS
    n_slots = T * TOP_K + N_EXPERTS * bm
    n_blocks = n_slots // bm
    block_start = jnp.arange(n_blocks, dtype=jnp.int32) * bm
    block_expert = jnp.minimum(jnp.sum(block_start[:, None] >= pend[None, :], axis=1),
                               N_EXPERTS - 1).astype(jnp.int32)
    n_active = (pend[-1:] // bm).astype(jnp.int32)
    zfill8 = jnp.where(padded > 0, (pend - bm) * ROW_CHUNKS, -1).astype(jnp.int32)
    tt = _tile(T, DISPATCH_TOKENS)
    nt = T // tt
    oh_t = onehot.reshape(nt, tt * TOP_K, N_EXPERTS)
    tile_cnt = jnp.sum(oh_t, axis=1, dtype=jnp.int32)
    tile_base = jnp.cumsum(tile_cnt, axis=0) - tile_cnt
    stage_off = jnp.cumsum(tile_cnt, axis=1) - tile_cnt
    shift = jnp.sum(jnp.where(oh_t, (stage_off - tile_base)[:, None, :], 0), axis=-1)
    stage_pos8 = (shift + rank.reshape(nt, tt * TOP_K)).astype(jnp.int32) * ROW_CHUNKS
    run_src8 = (stage_off * ROW_CHUNKS).reshape(-1).astype(jnp.int32)
    run_dst8 = ((pstart[None, :] + tile_base) * ROW_CHUNKS).reshape(-1).astype(jnp.int32)
    xs = dispatch_rows(h1, stage_pos8, run_src8, run_dst8, tile_cnt.reshape(-1), zfill8, n_slots, bm)
    ys = moe_mlp(xs, block_expert, n_active, layer, w_gu, b_gu, w_d, b_d, bm)
    return ys, dest8, gates


def _rms_rows(x, g):
    return x * lax.rsqrt(jnp.mean(x * x, axis=-1, keepdims=True) + RMS_EPS) * g


def _mla_kv_kernel(x_ref, wa_ref, g_ref, wkn_ref, wv_ref, cm_ref, sm_ref, k_ref, v_ref):
    a = jnp.dot(x_ref[...], wa_ref[...], preferred_element_type=F32)
    c_kv = _rms_rows(a[:, :KV_LORA], g_ref[...]).astype(BF16)
    kr = a[:, KV_LORA:KV_LORA + LANES] * cm_ref[...] + a[:, KV_LORA + LANES:] * sm_ref[...]
    kr = kr.astype(BF16)
    kn = jnp.dot(c_kv, wkn_ref[...], preferred_element_type=F32)
    v_ref[...] = jnp.dot(c_kv, wv_ref[...], preferred_element_type=F32).astype(BF16)
    for h in range(MLA_HEADS):
        k_ref[:, 2 * h * LANES:(2 * h + 1) * LANES] = kn[:, h * QK_NOPE:(h + 1) * QK_NOPE].astype(BF16)
        k_ref[:, (2 * h + 1) * LANES:(2 * h + 2) * LANES] = kr


def _pad_rope_cols(w, swap):
    K = w.shape[0]
    w = w.reshape(K, -1, QK_ROPE)
    if swap:
        w = jnp.concatenate([w[..., QK_ROPE // 2:], w[..., :QK_ROPE // 2]], axis=-1)
    w = jnp.concatenate([w, jnp.zeros_like(w)], axis=-1)
    return w.reshape(K, -1)


def mla_shared_kv(hb, w_kv_a, kv_norm_g, w_kv_b, cm, sm):
    T = hb.shape[0]
    tm = _tile(T, 512)
    w_r = w_kv_a[:, KV_LORA:]
    wa = jnp.concatenate([w_kv_a[:, :KV_LORA], _pad_rope_cols(w_r, False), _pad_rope_cols(w_r, True)],
                         axis=1).astype(BF16)
    wb = w_kv_b.reshape(KV_LORA, MLA_HEADS, QK_NOPE + V_HEAD)
    wkn = wb[:, :, :QK_NOPE].reshape(KV_LORA, MLA_HEADS * QK_NOPE).astype(BF16)
    wv = wb[:, :, QK_NOPE:].reshape(KV_LORA, MLA_HEADS * V_HEAD).astype(BF16)
    full = lambda shape: pl.BlockSpec(shape, lambda i: (0, 0))
    return pl.pallas_call(
        _mla_kv_kernel,
        grid=(T // tm,),
        in_specs=[pl.BlockSpec((tm, D_MODEL), lambda i: (i, 0)), full(wa.shape), full((1, KV_LORA)),
                  full(wkn.shape), full(wv.shape),
                  pl.BlockSpec((tm, LANES), lambda i: (i, 0)), pl.BlockSpec((tm, LANES), lambda i: (i, 0))],
        out_specs=[pl.BlockSpec((tm, MLA_HEADS * 2 * LANES), lambda i: (i, 0)),
                   pl.BlockSpec((tm, MLA_HEADS * V_HEAD), lambda i: (i, 0))],
        out_shape=[jax.ShapeDtypeStruct((T, MLA_HEADS * 2 * LANES), BF16),
                   jax.ShapeDtypeStruct((T, MLA_HEADS * V_HEAD), BF16)],
        compiler_params=_params("arbitrary"),
        name="mla_kv",
    )(hb, wa, kv_norm_g.reshape(1, KV_LORA), wkn, wv, cm, sm)


def _mla_q_kernel(x_ref, wa_ref, g_ref, wb_ref, cm_ref, sm_ref, q_ref):
    scale = (QK_NOPE + QK_ROPE) ** -0.5 * LOG2_E
    qa = jnp.dot(x_ref[...], wa_ref[...], preferred_element_type=F32)
    qc = _rms_rows(qa, g_ref[...]).astype(BF16)
    q = jnp.dot(qc, wb_ref[...], preferred_element_type=F32)
    cm = cm_ref[...] * scale
    sm = sm_ref[...] * scale
    n = MLA_HEADS * LANES
    for h in range(MLA_HEADS):
        lo = h * LANES
        q_ref[:, 2 * lo:2 * lo + LANES] = (q[:, lo:lo + LANES] * scale).astype(BF16)
        rope = q[:, n + lo:n + lo + LANES] * cm + q[:, 2 * n + lo:2 * n + lo + LANES] * sm
        q_ref[:, 2 * lo + LANES:2 * lo + 2 * LANES] = rope.astype(BF16)


def mla_q(hb, w_q_a, q_norm_g, w_q_b, cm, sm):
    T = hb.shape[0]
    tm = _tile(T, 512)
    wq = w_q_b.reshape(Q_LORA, MLA_HEADS, QK_NOPE + QK_ROPE)
    w_n = wq[:, :, :QK_NOPE].reshape(Q_LORA, MLA_HEADS * QK_NOPE)
    w_r = wq[:, :, QK_NOPE:].reshape(Q_LORA, MLA_HEADS * QK_ROPE)
    wb = jnp.concatenate([w_n, _pad_rope_cols(w_r, False), _pad_rope_cols(w_r, True)], axis=1).astype(BF16)
    full = lambda shape: pl.BlockSpec(shape, lambda i: (0, 0))
    return pl.pallas_call(
        _mla_q_kernel,
        grid=(T // tm,),
        in_specs=[pl.BlockSpec((tm, D_MODEL), lambda i: (i, 0)), full((D_MODEL, Q_LORA)), full((1, Q_LORA)),
                  full(wb.shape),
                  pl.BlockSpec((tm, LANES), lambda i: (i, 0)), pl.BlockSpec((tm, LANES), lambda i: (i, 0))],
        out_specs=pl.BlockSpec((tm, MLA_HEADS * 2 * LANES), lambda i: (i, 0)),
        out_shape=jax.ShapeDtypeStruct((T, MLA_HEADS * 2 * LANES), BF16),
        compiler_params=_params("arbitrary"),
        name="mla_q",
    )(hb, w_q_a.astype(BF16), q_norm_g.reshape(1, Q_LORA), wb, cm, sm)


def _flash_kernel(q_ref, k_ref, v_ref, o_ref, m_ref, l_ref, acc_ref, *, bq, bk):
    i = pl.program_id(2)
    q = q_ref[...]
    m_ref[...] = jnp.full_like(m_ref, -jnp.inf)
    l_ref[...] = jnp.zeros_like(l_ref)
    acc_ref[...] = jnp.zeros_like(acc_ref)

    def step(j, diag):
        start = pl.multiple_of(j * bk, bk)
        kj = k_ref[pl.ds(start, bk), :]
        vj = v_ref[pl.ds(start, bk), :]
        s = lax.dot_general(q, kj, NT_DIMS, preferred_element_type=F32)
        if diag is not None:
            r = lax.broadcasted_iota(jnp.int32, (bq, bk), 0)
            c = lax.broadcasted_iota(jnp.int32, (bq, bk), 1) + diag
            s = jnp.where(c <= r, s, -jnp.inf)
        m_prev = m_ref[...]
        m_new = jnp.maximum(m_prev, jnp.max(s, axis=-1, keepdims=True))
        alpha = jnp.exp2(m_prev - m_new)
        p = jnp.exp2(s - pltpu.repeat(m_new, bk // LANES, axis=1))
        l_ref[...] = alpha * l_ref[...] + jnp.sum(p, axis=-1, keepdims=True)
        acc_ref[...] = alpha * acc_ref[...] + jnp.dot(p.astype(BF16), vj, preferred_element_type=F32)
        m_ref[...] = m_new

    def pair(j, carry):
        step(2 * j, None)
        step(2 * j + 1, None)
        return carry

    lax.fori_loop(0, i, pair, 0)
    step(2 * i, 0)
    step(2 * i + 1, bk)
    o_ref[...] = (acc_ref[...] / l_ref[...]).astype(o_ref.dtype)


def flash_attention(q, k, v, B, S):
    T = B * S
    bq = _tile(S, ATTN_BLOCK_Q)
    bk = bq // 2
    nq = S // bq
    return pl.pallas_call(
        functools.partial(_flash_kernel, bq=bq, bk=bk),
        grid=(B, MLA_HEADS, nq),
        in_specs=[pl.BlockSpec((bq, 2 * LANES), lambda b, h, i: (b * nq + i, h)),
                  pl.BlockSpec((S, 2 * LANES), lambda b, h, i: (b, h)),
                  pl.BlockSpec((S, V_HEAD), lambda b, h, i: (b, h))],
        out_specs=pl.BlockSpec((bq, V_HEAD), lambda b, h, i: (b * nq + i, h)),
        out_shape=jax.ShapeDtypeStruct((T, MLA_HEADS * V_HEAD), BF16),
        scratch_shapes=[pltpu.VMEM((bq, LANES), F32), pltpu.VMEM((bq, LANES), F32),
                        pltpu.VMEM((bq, V_HEAD), F32)],
        compiler_params=_params("arbitrary", "arbitrary", "arbitrary"),
        name="mla_flash",
    )(q, k, v)


def kernel(x, p, positions, ret_w_in, ret_gn_g, ret_gn_b, ret_w_out, mla_w_kv_a, mla_kv_norm_g, mla_w_kv_b,
           mla_w_q_a, mla_q_norm_g, mla_w_q_b, mla_w_o, ln_mix_g, ln_mix_b, ln_ffn_g, ln_ffn_b,
           moe_w_router, moe_b_router, moe_w_gate_up, moe_b_gate_up, moe_w_down, moe_b_down,
           ple_w_gate, ple_w_proj):
    B, S, D = x.shape
    T = B * S
    cos_r, sin_r, cos_m, sin_m = rope_tables(positions)
    h = x.reshape(T, D)
    hb = h.astype(BF16)
    kv = None
    for i in range(DEPTH):
        if i < N_A:
            qk, vg = ret_projections(hb, ret_w_in[i].astype(BF16), cos_r, sin_r)
            z = retention(qk, vg, ret_gn_g[i], ret_gn_b[i], B, S)
            w_out = ret_w_out[i].astype(BF16)
        else:
            if kv is None:
                kv = mla_shared_kv(hb, mla_w_kv_a, mla_kv_norm_g, mla_w_kv_b, cos_m, sin_m)
            j = i - N_A
            q = mla_q(hb, mla_w_q_a[j], mla_q_norm_g[j], mla_w_q_b[j], cos_m, sin_m)
            z = flash_attention(q, kv[0], kv[1], B, S)
            w_out = mla_w_o[j].astype(BF16)
        h1 = out_proj_ln(z, w_out, h, ln_mix_g[i], ln_mix_b[i])
        ys, dest8, gates = moe_layer(h1, i, moe_w_router[i], moe_b_router[i], moe_w_gate_up, moe_b_gate_up,
                                     moe_w_down, moe_b_down)
        h, hb = ffn_ln_ple(ys, dest8, gates, h1, p[i].reshape(T, PLE_DIM), ln_ffn_g[i], ln_ffn_b[i],
                           ple_w_gate[i].astype(BF16), ple_w_proj[i].astype(BF16))
    return h.reshape(B, S, D)
```

```python
import functools

import jax
import jax.numpy as jnp
from jax import lax
from jax.experimental import pallas as pl
from jax.experimental.pallas import tpu as pltpu

D_MODEL = 1024
DEPTH = 4
N_A = DEPTH // 2
RET_HEADS = 4
RET_QK_DIM = D_MODEL // RET_HEADS
RET_V_DIM = 2 * RET_QK_DIM
GN_EPS = 1e-6
MLA_HEADS = 8
QK_NOPE = 128
QK_ROPE = 64
V_HEAD = 128
Q_LORA = 256
KV_LORA = 256
RMS_EPS = 1e-6
ROPE_THETA = 10000.0
N_EXPERTS = 32
TOP_K = 4
D_FF = D_MODEL
SWIGLU_LIMIT = 7.0
SWIGLU_ALPHA = 1.702
PLE_DIM = 256
DN_ALPHA = (2 * DEPTH) ** 0.25
LN_EPS = 1e-5
LOG2_E = 1.4426950408889634

LANES = 128
ROW_CHUNKS = D_MODEL // LANES
VMEM_LIMIT_BYTES = 56 * 1024 * 1024

RET_CHUNK = 256
RET_BLOCK = 1024
ATTN_BLOCK_Q = 1024
MOE_BLOCK = 512
DISPATCH_TOKENS = 512
FFN_ROW_CHUNKS = 2

F32 = jnp.float32
BF16 = jnp.bfloat16
NT_DIMS = (((1,), (1,)), ((), ()))
TN_DIMS = (((0,), (0,)), ((), ()))


def _tile(n, pref):
    t = min(n, pref)
    assert n % t == 0, (n, pref)
    return t


def _params(*sem):
    return pltpu.CompilerParams(dimension_semantics=sem, vmem_limit_bytes=VMEM_LIMIT_BYTES)


def _layer_norm_rows(u, g, b):
    mu = jnp.mean(u, axis=-1, keepdims=True)
    d = u - mu
    var = jnp.mean(d * d, axis=-1, keepdims=True)
    return d * lax.rsqrt(var + LN_EPS) * g + b


def _rope_table_kernel(pos_ref, inv_r_ref, inv_m_ref, sign_m_ref, cr_ref, sr_ref, cm_ref, sm_ref):
    pos = pos_ref[...].astype(F32)
    ang_r = pos * inv_r_ref[...]
    cr_ref[...] = jnp.cos(ang_r)
    sr_ref[...] = jnp.sin(ang_r)
    ang_m = pos * inv_m_ref[...]
    cm_ref[...] = jnp.cos(ang_m)
    sm_ref[...] = jnp.sin(ang_m) * sign_m_ref[...]


def rope_tables(positions):
    T = positions.size
    tm = _tile(T, 1024)
    pos = positions.reshape(T, 1)
    half_r = RET_QK_DIM // 2
    inv_r = ROPE_THETA ** (-jnp.arange(0, RET_QK_DIM, 2, dtype=F32) / RET_QK_DIM)
    inv_m = ROPE_THETA ** (-jnp.arange(0, QK_ROPE, 2, dtype=F32) / QK_ROPE)
    half_m = QK_ROPE // 2
    lane = jnp.arange(LANES)
    inv_m = inv_m[lane % half_m]
    sign_m = jnp.where((lane % QK_ROPE) < half_m, -1.0, 1.0).astype(F32)
    assert half_r == LANES
    row = lambda v: v.reshape(1, LANES)
    vec_spec = pl.BlockSpec((1, LANES), lambda i: (0, 0))
    tab_spec = pl.BlockSpec((tm, LANES), lambda i: (i, 0))
    out = jax.ShapeDtypeStruct((T, LANES), F32)
    return pl.pallas_call(
        _rope_table_kernel,
        grid=(T // tm,),
        in_specs=[pl.BlockSpec((tm, 1), lambda i: (i, 0)), vec_spec, vec_spec, vec_spec],
        out_specs=[tab_spec] * 4,
        out_shape=[out] * 4,
        compiler_params=_params("arbitrary"),
        name="rope_tables",
    )(pos, row(inv_r), row(inv_m), row(sign_m))


def _ret_qk_kernel(x_ref, w_ref, cos_ref, sin_ref, o_ref):
    acc = jnp.dot(x_ref[...], w_ref[...], preferred_element_type=F32)
    scale = jnp.where(pl.program_id(0) == 1, RET_QK_DIM ** -0.5, 1.0).astype(F32)
    c = cos_ref[...] * scale
    s = sin_ref[...] * scale
    half = RET_QK_DIM // 2
    for h in range(RET_HEADS):
        lo = h * RET_QK_DIM
        x1 = acc[:, lo:lo + half]
        x2 = acc[:, lo + half:lo + 2 * half]
        o_ref[:, lo:lo + half] = (x1 * c - x2 * s).astype(o_ref.dtype)
        o_ref[:, lo + half:lo + 2 * half] = (x1 * s + x2 * c).astype(o_ref.dtype)


def _matmul_kernel(x_ref, w_ref, o_ref):
    o_ref[...] = jnp.dot(x_ref[...], w_ref[...], preferred_element_type=F32).astype(o_ref.dtype)


def ret_projections(hb, w_in_bf, cos_r, sin_r):
    T = hb.shape[0]
    tm = _tile(T, 1024)
    tn = D_MODEL
    n_qk = 2 * RET_HEADS * RET_QK_DIM // tn
    n_vg = 2 * RET_HEADS * RET_V_DIM // tn
    x_spec = pl.BlockSpec((tm, D_MODEL), lambda j, i: (i, 0))
    tab_spec = pl.BlockSpec((tm, LANES), lambda j, i: (i, 0))
    qk = pl.pallas_call(
        _ret_qk_kernel,
        grid=(n_qk, T // tm),
        in_specs=[x_spec, pl.BlockSpec((D_MODEL, tn), lambda j, i: (0, j)), tab_spec, tab_spec],
        out_specs=pl.BlockSpec((tm, tn), lambda j, i: (i, j)),
        out_shape=jax.ShapeDtypeStruct((T, n_qk * tn), BF16),
        compiler_params=_params("arbitrary", "arbitrary"),
        name="ret_qk_proj",
    )(hb, w_in_bf, cos_r, sin_r)
    vg = pl.pallas_call(
        _matmul_kernel,
        grid=(n_vg, T // tm),
        in_specs=[x_spec, pl.BlockSpec((D_MODEL, tn), lambda j, i: (0, j + n_qk))],
        out_specs=pl.BlockSpec((tm, tn), lambda j, i: (i, j)),
        out_shape=jax.ShapeDtypeStruct((T, n_vg * tn), BF16),
        compiler_params=_params("arbitrary", "arbitrary"),
        name="ret_vg_proj",
    )(hb, w_in_bf)
    return qk, vg


def _retention_kernel(q_ref, k_ref, v_ref, g_ref, dm_ref, dq_ref, dk_ref, dc_ref, gng_ref, gnb_ref,
                      z_ref, state_ref, *, chunk, n_chunks):
    @pl.when(pl.program_id(2) == 0)
    def _():
        state_ref[...] = jnp.zeros_like(state_ref)

    dm = dm_ref[0]
    dq = dq_ref[0]
    dk = dk_ref[0]
    dc = dc_ref[0]
    gng = gng_ref[...]
    gnb = gnb_ref[...]
    for c in range(n_chunks):
        rows = slice(c * chunk, (c + 1) * chunk)
        q = q_ref[rows, :]
        k = k_ref[rows, :]
        v = v_ref[rows, :]
        st = state_ref[...]
        inner = lax.dot_general(q, k, NT_DIMS, preferred_element_type=F32) * dm
        y = jnp.dot(inner.astype(BF16), v, preferred_element_type=F32)
        y = y + jnp.dot(q, st.astype(BF16), preferred_element_type=F32) * dq
        kd = (k.astype(F32) * dk).astype(BF16)
        state_ref[...] = st * dc + lax.dot_general(kd, v, TN_DIMS, preferred_element_type=F32)
        mu = jnp.mean(y, axis=-1, keepdims=True)
        d = y - mu
        var = jnp.mean(d * d, axis=-1, keepdims=True)
        yn = d * lax.rsqrt(var + GN_EPS) * gng + gnb
        g = g_ref[rows, :].astype(F32)
        z_ref[rows, :] = (g * jax.nn.sigmoid(g) * yn).astype(z_ref.dtype)


def retention(qk, vg, gn_g, gn_b, B, S):
    T = B * S
    H, dk, dv = RET_HEADS, RET_QK_DIM, RET_V_DIM
    C = _tile(S, RET_CHUNK)
    L = _tile(S, RET_BLOCK)
    nl = S // L
    log_g = jnp.log(1.0 - 2.0 ** (-5.0 - jnp.arange(H, dtype=F32)))
    idx = jnp.arange(C, dtype=F32)
    diff = idx[:, None] - idx[None, :]
    dm = jnp.where(diff >= 0, jnp.exp(log_g[:, None, None] * jnp.maximum(diff, 0.0)), 0.0)
    dq = jnp.exp(log_g[:, None] * (idx + 1.0))[:, :, None]
    dkk = jnp.exp(log_g[:, None] * (C - 1.0 - idx))[:, :, None]
    dc = jnp.broadcast_to(jnp.exp(log_g * C)[:, None, None], (H, 1, dv))
    kern = functools.partial(_retention_kernel, chunk=C, n_chunks=L // C)
    return pl.pallas_call(
        kern,
        grid=(B, H, nl),
        in_specs=[
            pl.BlockSpec((L, dk), lambda b, h, n: (b * nl + n, h)),
            pl.BlockSpec((L, dk), lambda b, h, n: (b * nl + n, H + h)),
            pl.BlockSpec((L, dv), lambda b, h, n: (b * nl + n, h)),
            pl.BlockSpec((L, dv), lambda b, h, n: (b * nl + n, H + h)),
            pl.BlockSpec((1, C, C), lambda b, h, n: (h, 0, 0)),
            pl.BlockSpec((1, C, 1), lambda b, h, n: (h, 0, 0)),
            pl.BlockSpec((1, C, 1), lambda b, h, n: (h, 0, 0)),
            pl.BlockSpec((1, 1, dv), lambda b, h, n: (h, 0, 0)),
            pl.BlockSpec((1, dv), lambda b, h, n: (0, h)),
            pl.BlockSpec((1, dv), lambda b, h, n: (0, h)),
        ],
        out_specs=pl.BlockSpec((L, dv), lambda b, h, n: (b * nl + n, h)),
        out_shape=jax.ShapeDtypeStruct((T, H * dv), BF16),
        scratch_shapes=[pltpu.VMEM((dk, dv), F32)],
        compiler_params=_params("arbitrary", "arbitrary", "arbitrary"),
        name="retention",
    )(qk, qk, vg, vg, dm, dq, dkk, dc, gn_g.reshape(1, H * dv), gn_b.reshape(1, H * dv))


def _out_ln_kernel(z_ref, w_ref, h_ref, g_ref, b_ref, o_ref):
    mix = jnp.dot(z_ref[...], w_ref[...], preferred_element_type=F32)
    u = DN_ALPHA * h_ref[...] + mix
    o_ref[...] = _layer_norm_rows(u, g_ref[...], b_ref[...])


def out_proj_ln(z, w_bf, h, ln_g, ln_b):
    T, K = z.shape
    tm = _tile(T, 512)
    row_spec = pl.BlockSpec((tm, D_MODEL), lambda i: (i, 0))
    vec_spec = pl.BlockSpec((1, D_MODEL), lambda i: (0, 0))
    return pl.pallas_call(
        _out_ln_kernel,
        grid=(T // tm,),
        in_specs=[pl.BlockSpec((tm, K), lambda i: (i, 0)), pl.BlockSpec((K, D_MODEL), lambda i: (0, 0)),
                  row_spec, vec_spec, vec_spec],
        out_specs=row_spec,
        out_shape=jax.ShapeDtypeStruct((T, D_MODEL), F32),
        compiler_params=_params("arbitrary"),
        name="out_proj_ln",
    )(z, w_bf, h, ln_g.reshape(1, D_MODEL), ln_b.reshape(1, D_MODEL))


def _router_kernel(h_ref, w_ref, b_ref, slab_ref, cnt_ref, carry_ref, *, tm):
    @pl.when(pl.program_id(0) == 0)
    def _():
        carry_ref[...] = jnp.zeros_like(carry_ref)

    x = h_ref[...]
    w = w_ref[...]
    x_hi = x.astype(BF16)
    x_lo = (x - x_hi.astype(F32)).astype(BF16)
    w_hi = w.astype(BF16)
    w_lo = (w - w_hi.astype(F32)).astype(BF16)
    logits = (jnp.dot(x_hi, w_hi, preferred_element_type=F32)
              + (jnp.dot(x_hi, w_lo, preferred_element_type=F32)
                 + jnp.dot(x_lo, w_hi, preferred_element_type=F32))) + b_ref[...]
    e_iota = lax.broadcasted_iota(jnp.int32, (tm, N_EXPERTS), 1).astype(F32)
    work = logits
    onehot = jnp.zeros((tm, N_EXPERTS), F32)
    top_vals, top_idx = [], []
    for _ in range(TOP_K):
        m = jnp.max(work, axis=-1, keepdims=True)
        idx = jnp.min(jnp.where(work == m, e_iota, float(N_EXPERTS)), axis=-1, keepdims=True)
        top_vals.append(m)
        top_idx.append(idx)
        hit = e_iota == idx
        work = jnp.where(hit, -jnp.inf, work)
        onehot = jnp.where(hit, 1.0, onehot)
    exps = [jnp.exp(v - top_vals[0]) for v in top_vals]
    denom = exps[0] + exps[1] + exps[2] + exps[3]
    gates = [e / denom for e in exps]

    r = lax.broadcasted_iota(jnp.int32, (tm, tm), 0)
    c = lax.broadcasted_iota(jnp.int32, (tm, tm), 1)
    tri = jnp.where(c < r, 1.0, 0.0).astype(BF16)
    before = jnp.dot(tri, onehot.astype(BF16), preferred_element_type=F32) + carry_ref[0:1, 0:N_EXPERTS]
    ranks = [jnp.sum(jnp.where(e_iota == idx, before, 0.0), axis=-1, keepdims=True) for idx in top_idx]

    lane = lax.broadcasted_iota(jnp.int32, (tm, LANES), 1)
    slab = jnp.zeros((tm, LANES), F32)
    for k in range(TOP_K):
        slab = jnp.where(lane == k, top_idx[k], slab)
        slab = jnp.where(lane == TOP_K + k, gates[k], slab)
        slab = jnp.where(lane == 2 * TOP_K + k, ranks[k], slab)
    slab_ref[...] = slab

    carry_ref[0:1, 0:N_EXPERTS] = carry_ref[0:1, 0:N_EXPERTS] + jnp.sum(onehot, axis=0, keepdims=True)
    cnt_ref[...] = carry_ref[...]


def router(h, w_router, b_router):
    T = h.shape[0]
    tm = _tile(T, 512)
    slab, cnt = pl.pallas_call(
        functools.partial(_router_kernel, tm=tm),
        grid=(T // tm,),
        in_specs=[pl.BlockSpec((tm, D_MODEL), lambda i: (i, 0)),
                  pl.BlockSpec((D_MODEL, N_EXPERTS), lambda i: (0, 0)),
                  pl.BlockSpec((1, N_EXPERTS), lambda i: (0, 0))],
        out_specs=[pl.BlockSpec((tm, LANES), lambda i: (i, 0)), pl.BlockSpec((8, LANES), lambda i: (0, 0))],
        out_shape=[jax.ShapeDtypeStruct((T, LANES), F32), jax.ShapeDtypeStruct((8, LANES), F32)],
        scratch_shapes=[pltpu.VMEM((8, LANES), F32)],
        compiler_params=_params("arbitrary"),
        name="router",
    )(h, w_router, b_router.reshape(1, N_EXPERTS))
    idx = slab[:, 0:TOP_K].astype(jnp.int32)
    gates = slab[:, TOP_K:2 * TOP_K]
    rank = slab[:, 2 * TOP_K:3 * TOP_K].astype(jnp.int32)
    counts = cnt[0, :N_EXPERTS].astype(jnp.int32)
    return idx, gates, rank, counts


def _rows_to_tiles(dst_ref, x, n):
    for c in range(ROW_CHUNKS):
        dst_ref[pl.ds(c, n, stride=ROW_CHUNKS), :] = x[:, c * LANES:(c + 1) * LANES]


def _tiles_to_rows(src_ref, n):
    return jnp.concatenate([src_ref[pl.ds(c, n, stride=ROW_CHUNKS), :] for c in range(ROW_CHUNKS)], axis=1)


def _dispatch_kernel(zfill_ref, rsrc_ref, rdst_ref, rlen_ref, q_ref, h_ref, xs_ref,
                     rows, stage, sems, zsem, *, tt, bm):
    i = pl.program_id(0)
    last = pl.num_programs(0) - 1
    slot = i % 2

    def start_runs(step):
        buf = stage.at[step % 2]
        sem = sems.at[step % 2]

        def per_expert(e, carry):
            n = rlen_ref[step * N_EXPERTS + e]
            src = rsrc_ref[step * N_EXPERTS + e]
            dst = rdst_ref[step * N_EXPERTS + e]
            done = jnp.int32(0)
            piece = tt
            while piece >= 1:
                take = (n & piece) != 0
                size = piece * ROW_CHUNKS

                @pl.when(take)
                def _(done=done, size=size):
                    s = pl.multiple_of(src + done, ROW_CHUNKS)
                    d = pl.multiple_of(dst + done, ROW_CHUNKS)
                    pltpu.make_async_copy(buf.at[pl.ds(s, size)], xs_ref.at[pl.ds(d, size)], sem).start()

                done = done + jnp.where(take, size, 0)
                piece //= 2
            return carry
        lax.fori_loop(0, N_EXPERTS, per_expert, 0)

    def wait_runs(step):
        buf = stage.at[step % 2]
        pltpu.make_async_copy(buf, xs_ref.at[pl.ds(0, tt * TOP_K * ROW_CHUNKS)], sems.at[step % 2]).wait()

    @pl.when(i == 0)
    def _():
        zeros = stage.at[0, pl.ds(0, bm * ROW_CHUNKS)]
        zeros[...] = jnp.zeros_like(zeros)

        def zero_copy(e):
            start = pl.multiple_of(zfill_ref[e], bm * ROW_CHUNKS)
            return pltpu.make_async_copy(zeros, xs_ref.at[pl.ds(start, bm * ROW_CHUNKS)], zsem)

        for e in range(N_EXPERTS):
            @pl.when(zfill_ref[e] >= 0)
            def _():
                zero_copy(e).start()
        for e in range(N_EXPERTS):
            @pl.when(zfill_ref[e] >= 0)
            def _():
                zero_copy(e).wait()

    @pl.when(i >= 2)
    def _():
        wait_runs(i - 2)

    _rows_to_tiles(rows, h_ref[...], tt)

    def sort_rows(t, carry):
        row = rows[pl.ds(pl.multiple_of(t * ROW_CHUNKS, ROW_CHUNKS), ROW_CHUNKS), :]
        for k in range(TOP_K):
            q = pl.multiple_of(q_ref[0, 0, t * TOP_K + k], ROW_CHUNKS)
            stage[slot, pl.ds(q, ROW_CHUNKS), :] = row
        return carry

    lax.fori_loop(0, tt, sort_rows, 0, unroll=8)
    start_runs(i)

    @pl.when(jnp.logical_and(i == last, i >= 1))
    def _():
        wait_runs(i - 1)

    @pl.when(i == last)
    def _():
        wait_runs(i)


def dispatch_rows(h, stage_pos8, run_src8, run_dst8, run_len, zfill8, n_slots, bm):
    T = h.shape[0]
    tt = _tile(T, DISPATCH_TOKENS)
    pos3 = stage_pos8.reshape(T // tt, 1, tt * TOP_K)
    assert tt * TOP_K >= bm
    return pl.pallas_call(
        functools.partial(_dispatch_kernel, tt=tt, bm=bm),
        grid_spec=pltpu.PrefetchScalarGridSpec(
            num_scalar_prefetch=4,
            grid=(T // tt,),
            in_specs=[pl.BlockSpec((1, 1, tt * TOP_K), lambda i, *_: (i, 0, 0), memory_space=pltpu.SMEM),
                      pl.BlockSpec((tt, D_MODEL), lambda i, *_: (i, 0))],
            out_specs=pl.BlockSpec(memory_space=pl.ANY),
            scratch_shapes=[pltpu.VMEM((tt * ROW_CHUNKS, LANES), F32),
                            pltpu.VMEM((2, tt * TOP_K * ROW_CHUNKS, LANES), F32),
                            pltpu.SemaphoreType.DMA((2,)), pltpu.SemaphoreType.DMA],
        ),
        out_shape=jax.ShapeDtypeStruct((n_slots * ROW_CHUNKS, LANES), F32),
        compiler_params=_params("arbitrary"),
        name="moe_dispatch",
    )(zfill8, run_src8, run_dst8, run_len, pos3, h)


def _moe_mlp_kernel(be_ref, na_ref, x_ref, wgu_ref, bgu_ref, wd_ref, bd_ref, y_ref, wgu_bf, wd_bf, *, bm):
    i = pl.program_id(0)
    prev = be_ref[jnp.maximum(i - 1, 0)]
    active = i < na_ref[0]

    @pl.when(jnp.logical_and(active, jnp.logical_or(i == 0, be_ref[i] != prev)))
    def _():
        wgu_bf[...] = wgu_ref[...].astype(BF16)
        wd_bf[...] = wd_ref[...].astype(BF16)

    @pl.when(active)
    def _():
        x = _tiles_to_rows(x_ref, bm).astype(BF16)
        hgu = jnp.dot(x, wgu_bf[...], preferred_element_type=F32) + bgu_ref[...]
        gate = jnp.minimum(hgu[:, :D_FF], SWIGLU_LIMIT)
        up = jnp.clip(hgu[:, D_FF:], -SWIGLU_LIMIT, SWIGLU_LIMIT)
        act = gate * jax.nn.sigmoid(SWIGLU_ALPHA * gate) * (up + 1.0)
        y = jnp.dot(act.astype(BF16), wd_bf[...], preferred_element_type=F32) + bd_ref[...]
        _rows_to_tiles(y_ref, y, bm)


def moe_mlp(xs, block_expert, n_active, layer, w_gu, b_gu, w_d, b_d, bm):
    n_blocks = xs.shape[0] // (bm * ROW_CHUNKS)

    def row_map(i, be, na):
        return (jnp.minimum(i, na[0] - 1), 0)

    def w_map(i, be, na):
        return (layer, be[jnp.minimum(i, na[0] - 1)], 0, 0)

    def b_map(i, be, na):
        return (layer * N_EXPERTS + be[jnp.minimum(i, na[0] - 1)], 0, 0)

    return pl.pallas_call(
        functools.partial(_moe_mlp_kernel, bm=bm),
        grid_spec=pltpu.PrefetchScalarGridSpec(
            num_scalar_prefetch=2,
            grid=(n_blocks,),
            in_specs=[pl.BlockSpec((bm * ROW_CHUNKS, LANES), row_map),
                      pl.BlockSpec((None, None, D_MODEL, 2 * D_FF), w_map),
                      pl.BlockSpec((None, 1, 2 * D_FF), b_map),
                      pl.BlockSpec((None, None, D_FF, D_MODEL), w_map),
                      pl.BlockSpec((None, 1, D_MODEL), b_map)],
            out_specs=pl.BlockSpec((bm * ROW_CHUNKS, LANES), row_map),
            scratch_shapes=[pltpu.VMEM((D_MODEL, 2 * D_FF), BF16), pltpu.VMEM((D_FF, D_MODEL), BF16)],
        ),
        out_shape=jax.ShapeDtypeStruct(xs.shape, F32),
        compiler_params=_params("arbitrary"),
        name="moe_mlp",
    )(block_expert, n_active, xs, w_gu, b_gu.reshape(DEPTH * N_EXPERTS, 1, 2 * D_FF), w_d,
      b_d.reshape(DEPTH * N_EXPERTS, 1, D_MODEL))


def _ffn_ln_ple_kernel(dcur_ref, dnext_ref, gates_ref, h_ref, p_ref, g_ref, b_ref, wg_ref, wp_ref, ys_ref,
                       o_ref, ob_ref, ybuf, sems, *, tm):
    i = pl.program_id(0)
    slot = i % 2

    def row_copy(dref, s, t, k):
        d = pl.multiple_of(dref[0, 0, t * TOP_K + k], ROW_CHUNKS)
        dst = ybuf.at[s, k, pl.ds(pl.multiple_of(t * ROW_CHUNKS, ROW_CHUNKS), ROW_CHUNKS)]
        return pltpu.make_async_copy(ys_ref.at[pl.ds(d, ROW_CHUNKS)], dst, sems.at[s])

    def issue(dref, s):
        def body(t, carry):
            for k in range(TOP_K):
                row_copy(dref, s, t, k).start(priority=k % 2)
            return carry
        lax.fori_loop(0, tm, body, 0, unroll=4)

    @pl.when(i == 0)
    def _():
        issue(dcur_ref, 0)

    @pl.when(i + 1 < pl.num_programs(0))
    def _():
        issue(dnext_ref, 1 - slot)

    for k in range(TOP_K):
        pltpu.make_async_copy(ys_ref.at[pl.ds(0, tm * ROW_CHUNKS)], ybuf.at[slot, k], sems.at[slot]).wait()

    sub = tm // FFN_ROW_CHUNKS
    for c in range(FFN_ROW_CHUNKS):
        rows = slice(c * sub, (c + 1) * sub)
        tiles = pl.ds(c * sub * ROW_CHUNKS, sub * ROW_CHUNKS)
        gates = gates_ref[rows, :]
        ffn = _tiles_to_rows(ybuf.at[slot, 0, tiles], sub) * gates[:, 0:1]
        for k in range(1, TOP_K):
            ffn = ffn + _tiles_to_rows(ybuf.at[slot, k, tiles], sub) * gates[:, k:k + 1]
        h2 = _layer_norm_rows(DN_ALPHA * h_ref[rows, :] + ffn, g_ref[...], b_ref[...])
        gate = jax.nn.sigmoid(jnp.dot(h2.astype(BF16), wg_ref[...], preferred_element_type=F32))
        proj = jnp.dot(p_ref[rows, :].astype(BF16), wp_ref[...], preferred_element_type=F32)
        out = h2 + gate * proj
        o_ref[rows, :] = out
        ob_ref[rows, :] = out.astype(BF16)


def ffn_ln_ple(ys, dest8, gates, h, p_all, layer, ln_g, ln_b, wg_bf, wp_bf):
    T = h.shape[0]
    tm = _tile(T, 512)
    n = T // tm
    dest3 = dest8.reshape(n, 1, tm * TOP_K)
    row_spec = pl.BlockSpec((tm, D_MODEL), lambda i: (i, 0))
    vec_spec = pl.BlockSpec((1, D_MODEL), lambda i: (0, 0))
    return pl.pallas_call(
        functools.partial(_ffn_ln_ple_kernel, tm=tm),
        grid=(n,),
        in_specs=[pl.BlockSpec((1, 1, tm * TOP_K), lambda i: (i, 0, 0), memory_space=pltpu.SMEM),
                  pl.BlockSpec((1, 1, tm * TOP_K), lambda i: (jnp.minimum(i + 1, n - 1), 0, 0),
                               memory_space=pltpu.SMEM),
                  pl.BlockSpec((tm, TOP_K), lambda i: (i, 0)),
                  row_spec,
                  pl.BlockSpec((tm, PLE_DIM), lambda i: (layer * n + i, 0)),
                  vec_spec, vec_spec,
                  pl.BlockSpec((D_MODEL, D_MODEL), lambda i: (0, 0)),
                  pl.BlockSpec((PLE_DIM, D_MODEL), lambda i: (0, 0)),
                  pl.BlockSpec(memory_space=pl.ANY)],
        out_specs=[row_spec, row_spec],
        out_shape=[jax.ShapeDtypeStruct((T, D_MODEL), F32), jax.ShapeDtypeStruct((T, D_MODEL), BF16)],
        scratch_shapes=[pltpu.VMEM((2, TOP_K, tm * ROW_CHUNKS, LANES), F32), pltpu.SemaphoreType.DMA((2,))],
        compiler_params=_params("arbitrary"),
        name="ffn_ln_ple",
    )(dest3, dest3, gates, h, p_all, ln_g.reshape(1, D_MODEL), ln_b.reshape(1, D_MODEL), wg_bf, wp_bf, ys)


def moe_layer(h1, layer, w_router, b_router, w_gu, b_gu, w_d, b_d):
    T = h1.shape[0]
    bm = MOE_BLOCK
    idx, gates, rank, counts = router(h1, w_router, b_router)
    padded = (counts + bm - 1) // bm * bm
    pend = jnp.cumsum(padded)
    pstart = pend - padded
    onehot = idx[..., None] == jnp.arange(N_EXPERTS, dtype=jnp.int32)
    dest = (jnp.sum(jnp.where(onehot, pstart, 0), axis=-1) + rank).astype(jnp.int32)
    dest8 = dest * ROW_CHUNKS
    n_slots = T * TOP_K + N_EXPERTS * bm
    n_blocks = n_slots // bm
    block_start = jnp.arange(n_blocks, dtype=jnp.int32) * bm
    block_expert = jnp.minimum(jnp.sum(block_start[:, None] >= pend[None, :], axis=1),
                               N_EXPERTS - 1).astype(jnp.int32)
    n_active = (pend[-1:] // bm).astype(jnp.int32)
    zfill8 = jnp.where(padded > 0, (pend - bm) * ROW_CHUNKS, -1).astype(jnp.int32)
    tt = _tile(T, DISPATCH_TOKENS)
    nt = T // tt
    oh_t = onehot.reshape(nt, tt * TOP_K, N_EXPERTS)
    tile_cnt = jnp.sum(oh_t, axis=1, dtype=jnp.int32)
    tile_base = jnp.cumsum(tile_cnt, axis=0) - tile_cnt
    stage_off = jnp.cumsum(tile_cnt, axis=1) - tile_cnt
    shift = jnp.sum(jnp.where(oh_t, (stage_off - tile_base)[:, None, :], 0), axis=-1)
    stage_pos8 = (shift + rank.reshape(nt, tt * TOP_K)).astype(jnp.int32) * ROW_CHUNKS
    run_src8 = (stage_off * ROW_CHUNKS).reshape(-1).astype(jnp.int32)
    run_dst8 = ((pstart[None, :] + tile_base) * ROW_CHUNKS).reshape(-1).astype(jnp.int32)
    xs = dispatch_rows(h1, stage_pos8, run_src8, run_dst8, tile_cnt.reshape(-1), zfill8, n_slots, bm)
    ys = moe_mlp(xs, block_expert, n_active, layer, w_gu, b_gu, w_d, b_d, bm)
    return ys, dest8, gates


def _rms_rows(x, g):
    return x * lax.rsqrt(jnp.mean(x * x, axis=-1, keepdims=True) + RMS_EPS) * g


def _mla_kv_kernel(x_ref, wa_ref, g_ref, wkn_ref, wv_ref, cm_ref, sm_ref, k_ref, v_ref):
    a = jnp.dot(x_ref[...], wa_ref[...], preferred_element_type=F32)
    c_kv = _rms_rows(a[:, :KV_LORA], g_ref[...]).astype(BF16)
    kr = a[:, KV_LORA:KV_LORA + LANES] * cm_ref[...] + a[:, KV_LORA + LANES:] * sm_ref[...]
    kr = kr.astype(BF16)
    kn = jnp.dot(c_kv, wkn_ref[...], preferred_element_type=F32)
    v_ref[...] = jnp.dot(c_kv, wv_ref[...], preferred_element_type=F32).astype(BF16)
    for h in range(MLA_HEADS):
        k_ref[:, 2 * h * LANES:(2 * h + 1) * LANES] = kn[:, h * QK_NOPE:(h + 1) * QK_NOPE].astype(BF16)
        k_ref[:, (2 * h + 1) * LANES:(2 * h + 2) * LANES] = kr


def _pad_rope_cols(w, swap):
    K = w.shape[0]
    w = w.reshape(K, -1, QK_ROPE)
    if swap:
        w = jnp.concatenate([w[..., QK_ROPE // 2:], w[..., :QK_ROPE // 2]], axis=-1)
    w = jnp.concatenate([w, jnp.zeros_like(w)], axis=-1)
    return w.reshape(K, -1)


def mla_shared_kv(hb, w_kv_a, kv_norm_g, w_kv_b, cm, sm):
    T = hb.shape[0]
    tm = _tile(T, 512)
    w_r = w_kv_a[:, KV_LORA:]
    wa = jnp.concatenate([w_kv_a[:, :KV_LORA], _pad_rope_cols(w_r, False), _pad_rope_cols(w_r, True)],
                         axis=1).astype(BF16)
    wb = w_kv_b.reshape(KV_LORA, MLA_HEADS, QK_NOPE + V_HEAD)
    wkn = wb[:, :, :QK_NOPE].reshape(KV_LORA, MLA_HEADS * QK_NOPE).astype(BF16)
    wv = wb[:, :, QK_NOPE:].reshape(KV_LORA, MLA_HEADS * V_HEAD).astype(BF16)
    full = lambda shape: pl.BlockSpec(shape, lambda i: (0, 0))
    return pl.pallas_call(
        _mla_kv_kernel,
        grid=(T // tm,),
        in_specs=[pl.BlockSpec((tm, D_MODEL), lambda i: (i, 0)), full(wa.shape), full((1, KV_LORA)),
                  full(wkn.shape), full(wv.shape),
                  pl.BlockSpec((tm, LANES), lambda i: (i, 0)), pl.BlockSpec((tm, LANES), lambda i: (i, 0))],
        out_specs=[pl.BlockSpec((tm, MLA_HEADS * 2 * LANES), lambda i: (i, 0)),
                   pl.BlockSpec((tm, MLA_HEADS * V_HEAD), lambda i: (i, 0))],
        out_shape=[jax.ShapeDtypeStruct((T, MLA_HEADS * 2 * LANES), BF16),
                   jax.ShapeDtypeStruct((T, MLA_HEADS * V_HEAD), BF16)],
        compiler_params=_params("arbitrary"),
        name="mla_kv",
    )(hb, wa, kv_norm_g.reshape(1, KV_LORA), wkn, wv, cm, sm)


def _mla_q_kernel(x_ref, wa_ref, g_ref, wb_ref, cm_ref, sm_ref, q_ref):
    scale = (QK_NOPE + QK_ROPE) ** -0.5 * LOG2_E
    qa = jnp.dot(x_ref[...], wa_ref[...], preferred_element_type=F32)
    qc = _rms_rows(qa, g_ref[...]).astype(BF16)
    q = jnp.dot(qc, wb_ref[...], preferred_element_type=F32)
    cm = cm_ref[...] * scale
    sm = sm_ref[...] * scale
    n = MLA_HEADS * LANES
    for h in range(MLA_HEADS):
        lo = h * LANES
        q_ref[:, 2 * lo:2 * lo + LANES] = (q[:, lo:lo + LANES] * scale).astype(BF16)
        rope = q[:, n + lo:n + lo + LANES] * cm + q[:, 2 * n + lo:2 * n + lo + LANES] * sm
        q_ref[:, 2 * lo + LANES:2 * lo + 2 * LANES] = rope.astype(BF16)


def mla_q(hb, w_q_a, q_norm_g, w_q_b, cm, sm):
    T = hb.shape[0]
    tm = _tile(T, 512)
    wq = w_q_b.reshape(Q_LORA, MLA_HEADS, QK_NOPE + QK_ROPE)
    w_n = wq[:, :, :QK_NOPE].reshape(Q_LORA, MLA_HEADS * QK_NOPE)
    w_r = wq[:, :, QK_NOPE:].reshape(Q_LORA, MLA_HEADS * QK_ROPE)
    wb = jnp.concatenate([w_n, _pad_rope_cols(w_r, False), _pad_rope_cols(w_r, True)], axis=1).astype(BF16)
    full = lambda shape: pl.BlockSpec(shape, lambda i: (0, 0))
    return pl.pallas_call(
        _mla_q_kernel,
        grid=(T // tm,),
        in_specs=[pl.BlockSpec((tm, D_MODEL), lambda i: (i, 0)), full((D_MODEL, Q_LORA)), full((1, Q_LORA)),
                  full(wb.shape),
                  pl.BlockSpec((tm, LANES), lambda i: (i, 0)), pl.BlockSpec((tm, LANES), lambda i: (i, 0))],
        out_specs=pl.BlockSpec((tm, MLA_HEADS * 2 * LANES), lambda i: (i, 0)),
        out_shape=jax.ShapeDtypeStruct((T, MLA_HEADS * 2 * LANES), BF16),
        compiler_params=_params("arbitrary"),
        name="mla_q",
    )(hb, w_q_a.astype(BF16), q_norm_g.reshape(1, Q_LORA), wb, cm, sm)


def _flash_kernel(q_ref, k_ref, v_ref, o_ref, m_ref, l_ref, acc_ref, *, bq, bk):
    i = pl.program_id(2)
    q = q_ref[...]
    m_ref[...] = jnp.full_like(m_ref, -jnp.inf)
    l_ref[...] = jnp.zeros_like(l_ref)
    acc_ref[...] = jnp.zeros_like(acc_ref)

    def step(j, rows=slice(0, bq), triangular=False):
        n = rows.stop - rows.start
        start = pl.multiple_of(j * bk, bk)
        kj = k_ref[pl.ds(start, bk), :]
        vj = v_ref[pl.ds(start, bk), :]
        s = lax.dot_general(q[rows], kj, NT_DIMS, preferred_element_type=F32)
        if triangular:
            r = lax.broadcasted_iota(jnp.int32, (n, bk), 0)
            c = lax.broadcasted_iota(jnp.int32, (n, bk), 1)
            s = jnp.where(c <= r, s, -jnp.inf)
        m_prev = m_ref[rows, :]
        m_new = jnp.maximum(m_prev, jnp.max(s, axis=-1, keepdims=True))
        alpha = jnp.exp2(m_prev - m_new)
        p = jnp.exp2(s - pltpu.repeat(m_new, bk // LANES, axis=1))
        l_ref[rows, :] = alpha * l_ref[rows, :] + jnp.sum(p, axis=-1, keepdims=True)
        acc_ref[rows, :] = alpha * acc_ref[rows, :] + jnp.dot(p.astype(BF16), vj, preferred_element_type=F32)
        m_ref[rows, :] = m_new

    def pair(j, carry):
        step(2 * j)
        step(2 * j + 1)
        return carry

    lax.fori_loop(0, i, pair, 0)
    step(2 * i, triangular=True)
    step(2 * i + 1, slice(bk, bq), triangular=True)
    o_ref[...] = (acc_ref[...] / l_ref[...]).astype(o_ref.dtype)


def flash_attention(q, k, v, B, S):
    T = B * S
    bq = _tile(S, ATTN_BLOCK_Q)
    bk = bq // 2
    nq = S // bq
    return pl.pallas_call(
        functools.partial(_flash_kernel, bq=bq, bk=bk),
        grid=(B, MLA_HEADS, nq),
        in_specs=[pl.BlockSpec((bq, 2 * LANES), lambda b, h, i: (b * nq + i, h)),
                  pl.BlockSpec((S, 2 * LANES), lambda b, h, i: (b, h)),
                  pl.BlockSpec((S, V_HEAD), lambda b, h, i: (b, h))],
        out_specs=pl.BlockSpec((bq, V_HEAD), lambda b, h, i: (b * nq + i, h)),
        out_shape=jax.ShapeDtypeStruct((T, MLA_HEADS * V_HEAD), BF16),
        scratch_shapes=[pltpu.VMEM((bq, LANES), F32), pltpu.VMEM((bq, LANES), F32),
                        pltpu.VMEM((bq, V_HEAD), F32)],
        compiler_params=_params("arbitrary", "arbitrary", "arbitrary"),
        name="mla_flash",
    )(q, k, v)


def kernel(x, p, positions, ret_w_in, ret_gn_g, ret_gn_b, ret_w_out, mla_w_kv_a, mla_kv_norm_g, mla_w_kv_b,
           mla_w_q_a, mla_q_norm_g, mla_w_q_b, mla_w_o, ln_mix_g, ln_mix_b, ln_ffn_g, ln_ffn_b,
           moe_w_router, moe_b_router, moe_w_gate_up, moe_b_gate_up, moe_w_down, moe_b_down,
           ple_w_gate, ple_w_proj):
    B, S, D = x.shape
    T = B * S
    cos_r, sin_r, cos_m, sin_m = rope_tables(positions)
    h = x.reshape(T, D)
    hb = h.astype(BF16)
    kv = None
    for i in range(DEPTH):
        if i < N_A:
            qk, vg = ret_projections(hb, ret_w_in[i].astype(BF16), cos_r, sin_r)
            z = retention(qk, vg, ret_gn_g[i], ret_gn_b[i], B, S)
            w_out = ret_w_out[i].astype(BF16)
        else:
            if kv is None:
                kv = mla_shared_kv(hb, mla_w_kv_a, mla_kv_norm_g, mla_w_kv_b, cos_m, sin_m)
            j = i - N_A
            q = mla_q(hb, mla_w_q_a[j], mla_q_norm_g[j], mla_w_q_b[j], cos_m, sin_m)
            z = flash_attention(q, kv[0], kv[1], B, S)
            w_out = mla_w_o[j].astype(BF16)
        h1 = out_proj_ln(z, w_out, h, ln_mix_g[i], ln_mix_b[i])
        ys, dest8, gates = moe_layer(h1, i, moe_w_router[i], moe_b_router[i], moe_w_gate_up, moe_b_gate_up,
                                     moe_w_down, moe_b_down)
        h, hb = ffn_ln_ple(ys, dest8, gates, h1, p.reshape(DEPTH * T, PLE_DIM), i, ln_ffn_g[i], ln_ffn_b[i],
                           ple_w_gate[i].astype(BF16), ple_w_proj[i].astype(BF16))
    return h.reshape(B, S, D)
```

```python
import functools

import jax
import jax.numpy as jnp
from jax import lax
from jax.experimental import pallas as pl
from jax.experimental.pallas import tpu as pltpu

D_MODEL = 1024
DEPTH = 4
N_A = DEPTH // 2
RET_HEADS = 4
RET_QK_DIM = D_MODEL // RET_HEADS
RET_V_DIM = 2 * RET_QK_DIM
GN_EPS = 1e-6
MLA_HEADS = 8
QK_NOPE = 128
QK_ROPE = 64
V_HEAD = 128
Q_LORA = 256
KV_LORA = 256
RMS_EPS = 1e-6
ROPE_THETA = 10000.0
N_EXPERTS = 32
TOP_K = 4
D_FF = D_MODEL
SWIGLU_LIMIT = 7.0
SWIGLU_ALPHA = 1.702
PLE_DIM = 256
DN_ALPHA = (2 * DEPTH) ** 0.25
LN_EPS = 1e-5
LOG2_E = 1.4426950408889634

LANES = 128
ROW_CHUNKS = D_MODEL // LANES
VMEM_LIMIT_BYTES = 56 * 1024 * 1024

TOKEN_ROWS = 512
PROJ_ROWS = 1024
RET_CHUNK = 256
RET_BLOCK = 1024
ATTN_BLOCK_Q = 1024
MOE_BLOCK = 512
DISPATCH_TOKENS = 512
FFN_ROW_CHUNKS = 2

F32 = jnp.float32
BF16 = jnp.bfloat16
NT_DIMS = (((1,), (1,)), ((), ()))
TN_DIMS = (((0,), (0,)), ((), ()))


def _tile(n, pref):
    t = min(n, pref)
    assert n % t == 0, (n, pref)
    return t


def _params(*sem):
    return pltpu.CompilerParams(dimension_semantics=sem, vmem_limit_bytes=VMEM_LIMIT_BYTES)


def _layer_norm_rows(u, g, b):
    mu = jnp.mean(u, axis=-1, keepdims=True)
    d = u - mu
    var = jnp.mean(d * d, axis=-1, keepdims=True)
    return d * lax.rsqrt(var + LN_EPS) * g + b


def _rope_table_kernel(pos_ref, inv_r_ref, inv_m_ref, sign_m_ref, cr_ref, sr_ref, cm_ref, sm_ref):
    pos = pos_ref[...].astype(F32)
    ang_r = pos * inv_r_ref[...]
    cr_ref[...] = jnp.cos(ang_r)
    sr_ref[...] = jnp.sin(ang_r)
    ang_m = pos * inv_m_ref[...]
    cm_ref[...] = jnp.cos(ang_m)
    sm_ref[...] = jnp.sin(ang_m) * sign_m_ref[...]


def rope_tables(positions):
    T = positions.size
    tm = _tile(T, PROJ_ROWS)
    pos = positions.reshape(T, 1)
    half_r = RET_QK_DIM // 2
    inv_r = ROPE_THETA ** (-jnp.arange(0, RET_QK_DIM, 2, dtype=F32) / RET_QK_DIM)
    inv_m = ROPE_THETA ** (-jnp.arange(0, QK_ROPE, 2, dtype=F32) / QK_ROPE)
    half_m = QK_ROPE // 2
    lane = jnp.arange(LANES)
    inv_m = inv_m[lane % half_m]
    sign_m = jnp.where((lane % QK_ROPE) < half_m, -1.0, 1.0).astype(F32)
    assert half_r == LANES
    row = lambda v: v.reshape(1, LANES)
    vec_spec = pl.BlockSpec((1, LANES), lambda i: (0, 0))
    tab_spec = pl.BlockSpec((tm, LANES), lambda i: (i, 0))
    out = jax.ShapeDtypeStruct((T, LANES), F32)
    return pl.pallas_call(
        _rope_table_kernel,
        grid=(T // tm,),
        in_specs=[pl.BlockSpec((tm, 1), lambda i: (i, 0)), vec_spec, vec_spec, vec_spec],
        out_specs=[tab_spec] * 4,
        out_shape=[out] * 4,
        compiler_params=_params("arbitrary"),
        name="rope_tables",
    )(pos, row(inv_r), row(inv_m), row(sign_m))


def _ret_qk_kernel(x_ref, w_ref, cos_ref, sin_ref, o_ref):
    acc = jnp.dot(x_ref[...], w_ref[...], preferred_element_type=F32)
    scale = jnp.where(pl.program_id(0) == 1, RET_QK_DIM ** -0.5, 1.0).astype(F32)
    c = cos_ref[...] * scale
    s = sin_ref[...] * scale
    half = RET_QK_DIM // 2
    for h in range(RET_HEADS):
        lo = h * RET_QK_DIM
        x1 = acc[:, lo:lo + half]
        x2 = acc[:, lo + half:lo + 2 * half]
        o_ref[:, lo:lo + half] = (x1 * c - x2 * s).astype(o_ref.dtype)
        o_ref[:, lo + half:lo + 2 * half] = (x1 * s + x2 * c).astype(o_ref.dtype)


def _matmul_kernel(x_ref, w_ref, o_ref):
    o_ref[...] = jnp.dot(x_ref[...], w_ref[...], preferred_element_type=F32).astype(o_ref.dtype)


def ret_projections(hb, w_in_bf, cos_r, sin_r):
    T = hb.shape[0]
    tm = _tile(T, PROJ_ROWS)
    tn = D_MODEL
    n_qk = 2 * RET_HEADS * RET_QK_DIM // tn
    n_vg = 2 * RET_HEADS * RET_V_DIM // tn
    x_spec = pl.BlockSpec((tm, D_MODEL), lambda j, i: (i, 0))
    tab_spec = pl.BlockSpec((tm, LANES), lambda j, i: (i, 0))
    qk = pl.pallas_call(
        _ret_qk_kernel,
        grid=(n_qk, T // tm),
        in_specs=[x_spec, pl.BlockSpec((D_MODEL, tn), lambda j, i: (0, j)), tab_spec, tab_spec],
        out_specs=pl.BlockSpec((tm, tn), lambda j, i: (i, j)),
        out_shape=jax.ShapeDtypeStruct((T, n_qk * tn), BF16),
        compiler_params=_params("arbitrary", "arbitrary"),
        name="ret_qk_proj",
    )(hb, w_in_bf, cos_r, sin_r)
    vg = pl.pallas_call(
        _matmul_kernel,
        grid=(n_vg, T // tm),
        in_specs=[x_spec, pl.BlockSpec((D_MODEL, tn), lambda j, i: (0, j + n_qk))],
        out_specs=pl.BlockSpec((tm, tn), lambda j, i: (i, j)),
        out_shape=jax.ShapeDtypeStruct((T, n_vg * tn), BF16),
        compiler_params=_params("arbitrary", "arbitrary"),
        name="ret_vg_proj",
    )(hb, w_in_bf)
    return qk, vg


def _retention_kernel(q_ref, k_ref, v_ref, g_ref, dm_ref, dq_ref, dk_ref, dc_ref, gng_ref, gnb_ref,
                      z_ref, state_ref, *, chunk, n_chunks):
    @pl.when(pl.program_id(2) == 0)
    def _():
        state_ref[...] = jnp.zeros_like(state_ref)

    dm = dm_ref[0]
    dq = dq_ref[0]
    dk = dk_ref[0]
    dc = dc_ref[0]
    gng = gng_ref[...]
    gnb = gnb_ref[...]
    for c in range(n_chunks):
        rows = slice(c * chunk, (c + 1) * chunk)
        q = q_ref[rows, :]
        k = k_ref[rows, :]
        v = v_ref[rows, :]
        st = state_ref[...]
        inner = lax.dot_general(q, k, NT_DIMS, preferred_element_type=F32) * dm
        y = jnp.dot(inner.astype(BF16), v, preferred_element_type=F32)
        y = y + jnp.dot(q, st.astype(BF16), preferred_element_type=F32) * dq
        kd = (k.astype(F32) * dk).astype(BF16)
        state_ref[...] = st * dc + lax.dot_general(kd, v, TN_DIMS, preferred_element_type=F32)
        mu = jnp.mean(y, axis=-1, keepdims=True)
        d = y - mu
        var = jnp.mean(d * d, axis=-1, keepdims=True)
        yn = d * lax.rsqrt(var + GN_EPS) * gng + gnb
        g = g_ref[rows, :].astype(F32)
        z_ref[rows, :] = (g * jax.nn.sigmoid(g) * yn).astype(z_ref.dtype)


def retention(qk, vg, gn_g, gn_b, B, S):
    T = B * S
    H, dk, dv = RET_HEADS, RET_QK_DIM, RET_V_DIM
    C = _tile(S, RET_CHUNK)
    L = _tile(S, RET_BLOCK)
    nl = S // L
    log_g = jnp.log(1.0 - 2.0 ** (-5.0 - jnp.arange(H, dtype=F32)))
    idx = jnp.arange(C, dtype=F32)
    diff = idx[:, None] - idx[None, :]
    dm = jnp.where(diff >= 0, jnp.exp(log_g[:, None, None] * jnp.maximum(diff, 0.0)), 0.0)
    dq = jnp.exp(log_g[:, None] * (idx + 1.0))[:, :, None]
    dkk = jnp.exp(log_g[:, None] * (C - 1.0 - idx))[:, :, None]
    dc = jnp.broadcast_to(jnp.exp(log_g * C)[:, None, None], (H, 1, dv))
    kern = functools.partial(_retention_kernel, chunk=C, n_chunks=L // C)
    return pl.pallas_call(
        kern,
        grid=(B, H, nl),
        in_specs=[
            pl.BlockSpec((L, dk), lambda b, h, n: (b * nl + n, h)),
            pl.BlockSpec((L, dk), lambda b, h, n: (b * nl + n, H + h)),
            pl.BlockSpec((L, dv), lambda b, h, n: (b * nl + n, h)),
            pl.BlockSpec((L, dv), lambda b, h, n: (b * nl + n, H + h)),
            pl.BlockSpec((1, C, C), lambda b, h, n: (h, 0, 0)),
            pl.BlockSpec((1, C, 1), lambda b, h, n: (h, 0, 0)),
            pl.BlockSpec((1, C, 1), lambda b, h, n: (h, 0, 0)),
            pl.BlockSpec((1, 1, dv), lambda b, h, n: (h, 0, 0)),
            pl.BlockSpec((1, dv), lambda b, h, n: (0, h)),
            pl.BlockSpec((1, dv), lambda b, h, n: (0, h)),
        ],
        out_specs=pl.BlockSpec((L, dv), lambda b, h, n: (b * nl + n, h)),
        out_shape=jax.ShapeDtypeStruct((T, H * dv), BF16),
        scratch_shapes=[pltpu.VMEM((dk, dv), F32)],
        compiler_params=_params("arbitrary", "arbitrary", "arbitrary"),
        name="retention",
    )(qk, qk, vg, vg, dm, dq, dkk, dc, gn_g.reshape(1, H * dv), gn_b.reshape(1, H * dv))


def _out_ln_kernel(z_ref, w_ref, h_ref, g_ref, b_ref, o_ref):
    mix = jnp.dot(z_ref[...], w_ref[...], preferred_element_type=F32)
    u = DN_ALPHA * h_ref[...] + mix
    o_ref[...] = _layer_norm_rows(u, g_ref[...], b_ref[...])


def out_proj_ln(z, w_bf, h, ln_g, ln_b):
    T, K = z.shape
    tm = _tile(T, TOKEN_ROWS)
    row_spec = pl.BlockSpec((tm, D_MODEL), lambda i: (i, 0))
    vec_spec = pl.BlockSpec((1, D_MODEL), lambda i: (0, 0))
    return pl.pallas_call(
        _out_ln_kernel,
        grid=(T // tm,),
        in_specs=[pl.BlockSpec((tm, K), lambda i: (i, 0)), pl.BlockSpec((K, D_MODEL), lambda i: (0, 0)),
                  row_spec, vec_spec, vec_spec],
        out_specs=row_spec,
        out_shape=jax.ShapeDtypeStruct((T, D_MODEL), F32),
        compiler_params=_params("arbitrary"),
        name="out_proj_ln",
    )(z, w_bf, h, ln_g.reshape(1, D_MODEL), ln_b.reshape(1, D_MODEL))


def _router_kernel(h_ref, w_ref, b_ref, slab_ref, cnt_ref, carry_ref, *, tm):
    @pl.when(pl.program_id(0) == 0)
    def _():
        carry_ref[...] = jnp.zeros_like(carry_ref)

    x = h_ref[...]
    w = w_ref[...]
    x_hi = x.astype(BF16)
    x_lo = (x - x_hi.astype(F32)).astype(BF16)
    w_hi = w.astype(BF16)
    w_lo = (w - w_hi.astype(F32)).astype(BF16)
    logits = (jnp.dot(x_hi, w_hi, preferred_element_type=F32)
              + (jnp.dot(x_hi, w_lo, preferred_element_type=F32)
                 + jnp.dot(x_lo, w_hi, preferred_element_type=F32))) + b_ref[...]
    e_iota = lax.broadcasted_iota(jnp.int32, (tm, N_EXPERTS), 1).astype(F32)
    work = logits
    onehot = jnp.zeros((tm, N_EXPERTS), F32)
    top_vals, top_idx = [], []
    for _ in range(TOP_K):
        m = jnp.max(work, axis=-1, keepdims=True)
        idx = jnp.min(jnp.where(work == m, e_iota, float(N_EXPERTS)), axis=-1, keepdims=True)
        top_vals.append(m)
        top_idx.append(idx)
        hit = e_iota == idx
        work = jnp.where(hit, -jnp.inf, work)
        onehot = jnp.where(hit, 1.0, onehot)
    exps = [jnp.exp(v - top_vals[0]) for v in top_vals]
    denom = exps[0] + exps[1] + exps[2] + exps[3]
    gates = [e / denom for e in exps]

    r = lax.broadcasted_iota(jnp.int32, (tm, tm), 0)
    c = lax.broadcasted_iota(jnp.int32, (tm, tm), 1)
    tri = jnp.where(c < r, 1.0, 0.0).astype(BF16)
    before = jnp.dot(tri, onehot.astype(BF16), preferred_element_type=F32) + carry_ref[0:1, 0:N_EXPERTS]
    ranks = [jnp.sum(jnp.where(e_iota == idx, before, 0.0), axis=-1, keepdims=True) for idx in top_idx]

    lane = lax.broadcasted_iota(jnp.int32, (tm, LANES), 1)
    slab = jnp.zeros((tm, LANES), F32)
    for k in range(TOP_K):
        slab = jnp.where(lane == k, top_idx[k], slab)
        slab = jnp.where(lane == TOP_K + k, gates[k], slab)
        slab = jnp.where(lane == 2 * TOP_K + k, ranks[k], slab)
    slab_ref[...] = slab

    carry_ref[0:1, 0:N_EXPERTS] = carry_ref[0:1, 0:N_EXPERTS] + jnp.sum(onehot, axis=0, keepdims=True)
    cnt_ref[...] = carry_ref[...]


def router(h, w_router, b_router):
    T = h.shape[0]
    tm = _tile(T, TOKEN_ROWS)
    slab, cnt = pl.pallas_call(
        functools.partial(_router_kernel, tm=tm),
        grid=(T // tm,),
        in_specs=[pl.BlockSpec((tm, D_MODEL), lambda i: (i, 0)),
                  pl.BlockSpec((D_MODEL, N_EXPERTS), lambda i: (0, 0)),
                  pl.BlockSpec((1, N_EXPERTS), lambda i: (0, 0))],
        out_specs=[pl.BlockSpec((tm, LANES), lambda i: (i, 0)), pl.BlockSpec((8, LANES), lambda i: (0, 0))],
        out_shape=[jax.ShapeDtypeStruct((T, LANES), F32), jax.ShapeDtypeStruct((8, LANES), F32)],
        scratch_shapes=[pltpu.VMEM((8, LANES), F32)],
        compiler_params=_params("arbitrary"),
        name="router",
    )(h, w_router, b_router.reshape(1, N_EXPERTS))
    idx = slab[:, 0:TOP_K].astype(jnp.int32)
    gates = slab[:, TOP_K:2 * TOP_K]
    rank = slab[:, 2 * TOP_K:3 * TOP_K].astype(jnp.int32)
    counts = cnt[0, :N_EXPERTS].astype(jnp.int32)
    return idx, gates, rank, counts


def _rows_to_tiles(dst_ref, x, n):
    for c in range(ROW_CHUNKS):
        dst_ref[pl.ds(c, n, stride=ROW_CHUNKS), :] = x[:, c * LANES:(c + 1) * LANES]


def _tiles_to_rows(src_ref, n):
    return jnp.concatenate([src_ref[pl.ds(c, n, stride=ROW_CHUNKS), :] for c in range(ROW_CHUNKS)], axis=1)


def _dispatch_kernel(zfill_ref, rsrc_ref, rdst_ref, rlen_ref, q_ref, h_ref, xs_ref,
                     rows, stage, sems, zsem, *, tt, bm):
    i = pl.program_id(0)
    last = pl.num_programs(0) - 1
    slot = i % 2

    def start_runs(step):
        buf = stage.at[step % 2]
        sem = sems.at[step % 2]

        def per_expert(e, carry):
            n = rlen_ref[step * N_EXPERTS + e]
            src = rsrc_ref[step * N_EXPERTS + e]
            dst = rdst_ref[step * N_EXPERTS + e]
            done = jnp.int32(0)
            piece = tt
            while piece >= 1:
                take = (n & piece) != 0
                size = piece * ROW_CHUNKS

                @pl.when(take)
                def _(done=done, size=size):
                    s = pl.multiple_of(src + done, ROW_CHUNKS)
                    d = pl.multiple_of(dst + done, ROW_CHUNKS)
                    pltpu.make_async_copy(buf.at[pl.ds(s, size)], xs_ref.at[pl.ds(d, size)], sem).start()

                done = done + jnp.where(take, size, 0)
                piece //= 2
            return carry
        lax.fori_loop(0, N_EXPERTS, per_expert, 0)

    def wait_runs(step):
        buf = stage.at[step % 2]
        pltpu.make_async_copy(buf, xs_ref.at[pl.ds(0, tt * TOP_K * ROW_CHUNKS)], sems.at[step % 2]).wait()

    @pl.when(i == 0)
    def _():
        zeros = stage.at[0, pl.ds(0, bm * ROW_CHUNKS)]
        zeros[...] = jnp.zeros_like(zeros)

        def zero_copy(e):
            start = pl.multiple_of(zfill_ref[e], bm * ROW_CHUNKS)
            return pltpu.make_async_copy(zeros, xs_ref.at[pl.ds(start, bm * ROW_CHUNKS)], zsem)

        for e in range(N_EXPERTS):
            @pl.when(zfill_ref[e] >= 0)
            def _():
                zero_copy(e).start()
        for e in range(N_EXPERTS):
            @pl.when(zfill_ref[e] >= 0)
            def _():
                zero_copy(e).wait()

    @pl.when(i >= 2)
    def _():
        wait_runs(i - 2)

    _rows_to_tiles(rows, h_ref[...], tt)

    def sort_rows(t, carry):
        row = rows[pl.ds(pl.multiple_of(t * ROW_CHUNKS, ROW_CHUNKS), ROW_CHUNKS), :]
        for k in range(TOP_K):
            q = pl.multiple_of(q_ref[0, 0, t * TOP_K + k], ROW_CHUNKS)
            stage[slot, pl.ds(q, ROW_CHUNKS), :] = row
        return carry

    lax.fori_loop(0, tt, sort_rows, 0, unroll=8)
    start_runs(i)

    @pl.when(jnp.logical_and(i == last, i >= 1))
    def _():
        wait_runs(i - 1)

    @pl.when(i == last)
    def _():
        wait_runs(i)


def dispatch_rows(h, stage_pos8, run_src8, run_dst8, run_len, zfill8, n_slots, bm):
    T = h.shape[0]
    tt = _tile(T, DISPATCH_TOKENS)
    pos3 = stage_pos8.reshape(T // tt, 1, tt * TOP_K)
    assert tt * TOP_K >= bm
    return pl.pallas_call(
        functools.partial(_dispatch_kernel, tt=tt, bm=bm),
        grid_spec=pltpu.PrefetchScalarGridSpec(
            num_scalar_prefetch=4,
            grid=(T // tt,),
            in_specs=[pl.BlockSpec((1, 1, tt * TOP_K), lambda i, *_: (i, 0, 0), memory_space=pltpu.SMEM),
                      pl.BlockSpec((tt, D_MODEL), lambda i, *_: (i, 0))],
            out_specs=pl.BlockSpec(memory_space=pl.ANY),
            scratch_shapes=[pltpu.VMEM((tt * ROW_CHUNKS, LANES), F32),
                            pltpu.VMEM((2, tt * TOP_K * ROW_CHUNKS, LANES), F32),
                            pltpu.SemaphoreType.DMA((2,)), pltpu.SemaphoreType.DMA],
        ),
        out_shape=jax.ShapeDtypeStruct((n_slots * ROW_CHUNKS, LANES), F32),
        compiler_params=_params("arbitrary"),
        name="moe_dispatch",
    )(zfill8, run_src8, run_dst8, run_len, pos3, h)


def _moe_mlp_kernel(be_ref, na_ref, x_ref, wgu_ref, bgu_ref, wd_ref, bd_ref, y_ref, wgu_bf, wd_bf, *, bm):
    i = pl.program_id(0)
    prev = be_ref[jnp.maximum(i - 1, 0)]
    active = i < na_ref[0]

    @pl.when(jnp.logical_and(active, jnp.logical_or(i == 0, be_ref[i] != prev)))
    def _():
        wgu_bf[...] = wgu_ref[...].astype(BF16)
        wd_bf[...] = wd_ref[...].astype(BF16)

    @pl.when(active)
    def _():
        x = _tiles_to_rows(x_ref, bm).astype(BF16)
        hgu = jnp.dot(x, wgu_bf[...], preferred_element_type=F32) + bgu_ref[...]
        gate = jnp.minimum(hgu[:, :D_FF], SWIGLU_LIMIT)
        up = jnp.clip(hgu[:, D_FF:], -SWIGLU_LIMIT, SWIGLU_LIMIT)
        act = gate * jax.nn.sigmoid(SWIGLU_ALPHA * gate) * (up + 1.0)
        y = jnp.dot(act.astype(BF16), wd_bf[...], preferred_element_type=F32) + bd_ref[...]
        _rows_to_tiles(y_ref, y, bm)


def moe_mlp(xs, block_expert, n_active, layer, w_gu, b_gu, w_d, b_d, bm):
    n_blocks = xs.shape[0] // (bm * ROW_CHUNKS)

    def row_map(i, be, na):
        return (jnp.minimum(i, na[0] - 1), 0)

    def w_map(i, be, na):
        return (layer, be[jnp.minimum(i, na[0] - 1)], 0, 0)

    def b_map(i, be, na):
        return (layer * N_EXPERTS + be[jnp.minimum(i, na[0] - 1)], 0, 0)

    return pl.pallas_call(
        functools.partial(_moe_mlp_kernel, bm=bm),
        grid_spec=pltpu.PrefetchScalarGridSpec(
            num_scalar_prefetch=2,
            grid=(n_blocks,),
            in_specs=[pl.BlockSpec((bm * ROW_CHUNKS, LANES), row_map),
                      pl.BlockSpec((None, None, D_MODEL, 2 * D_FF), w_map),
                      pl.BlockSpec((None, 1, 2 * D_FF), b_map),
                      pl.BlockSpec((None, None, D_FF, D_MODEL), w_map),
                      pl.BlockSpec((None, 1, D_MODEL), b_map)],
            out_specs=pl.BlockSpec((bm * ROW_CHUNKS, LANES), row_map),
            scratch_shapes=[pltpu.VMEM((D_MODEL, 2 * D_FF), BF16), pltpu.VMEM((D_FF, D_MODEL), BF16)],
        ),
        out_shape=jax.ShapeDtypeStruct(xs.shape, F32),
        compiler_params=_params("arbitrary"),
        name="moe_mlp",
    )(block_expert, n_active, xs, w_gu, b_gu.reshape(DEPTH * N_EXPERTS, 1, 2 * D_FF), w_d,
      b_d.reshape(DEPTH * N_EXPERTS, 1, D_MODEL))


def _ffn_ln_ple_kernel(dcur_ref, dnext_ref, gates_ref, h_ref, p_ref, g_ref, b_ref, wg_ref, wp_ref, ys_ref,
                       o_ref, ob_ref, ybuf, sems, *, tm):
    i = pl.program_id(0)
    slot = i % 2

    def row_copy(dref, s, t, k):
        d = pl.multiple_of(dref[0, 0, t * TOP_K + k], ROW_CHUNKS)
        dst = ybuf.at[s, k, pl.ds(pl.multiple_of(t * ROW_CHUNKS, ROW_CHUNKS), ROW_CHUNKS)]
        return pltpu.make_async_copy(ys_ref.at[pl.ds(d, ROW_CHUNKS)], dst, sems.at[s])

    def issue(dref, s):
        def body(t, carry):
            for k in range(TOP_K):
                row_copy(dref, s, t, k).start(priority=k % 2)
            return carry
        lax.fori_loop(0, tm, body, 0, unroll=4)

    @pl.when(i == 0)
    def _():
        issue(dcur_ref, 0)

    @pl.when(i + 1 < pl.num_programs(0))
    def _():
        issue(dnext_ref, 1 - slot)

    for k in range(TOP_K):
        pltpu.make_async_copy(ys_ref.at[pl.ds(0, tm * ROW_CHUNKS)], ybuf.at[slot, k], sems.at[slot]).wait()

    sub = tm // FFN_ROW_CHUNKS
    for c in range(FFN_ROW_CHUNKS):
        rows = slice(c * sub, (c + 1) * sub)
        tiles = pl.ds(c * sub * ROW_CHUNKS, sub * ROW_CHUNKS)
        gates = gates_ref[rows, :]
        ffn = _tiles_to_rows(ybuf.at[slot, 0, tiles], sub) * gates[:, 0:1]
        for k in range(1, TOP_K):
            ffn = ffn + _tiles_to_rows(ybuf.at[slot, k, tiles], sub) * gates[:, k:k + 1]
        h2 = _layer_norm_rows(DN_ALPHA * h_ref[rows, :] + ffn, g_ref[...], b_ref[...])
        gate = jax.nn.sigmoid(jnp.dot(h2.astype(BF16), wg_ref[...], preferred_element_type=F32))
        proj = jnp.dot(p_ref[rows, :].astype(BF16), wp_ref[...], preferred_element_type=F32)
        out = h2 + gate * proj
        o_ref[rows, :] = out
        ob_ref[rows, :] = out.astype(BF16)


def ffn_ln_ple(ys, dest8, gates, h, p_all, layer, ln_g, ln_b, wg_bf, wp_bf):
    T = h.shape[0]
    tm = _tile(T, TOKEN_ROWS)
    n = T // tm
    dest3 = dest8.reshape(n, 1, tm * TOP_K)
    row_spec = pl.BlockSpec((tm, D_MODEL), lambda i: (i, 0))
    vec_spec = pl.BlockSpec((1, D_MODEL), lambda i: (0, 0))
    return pl.pallas_call(
        functools.partial(_ffn_ln_ple_kernel, tm=tm),
        grid=(n,),
        in_specs=[pl.BlockSpec((1, 1, tm * TOP_K), lambda i: (i, 0, 0), memory_space=pltpu.SMEM),
                  pl.BlockSpec((1, 1, tm * TOP_K), lambda i: (jnp.minimum(i + 1, n - 1), 0, 0),
                               memory_space=pltpu.SMEM),
                  pl.BlockSpec((tm, TOP_K), lambda i: (i, 0)),
                  row_spec,
                  pl.BlockSpec((tm, PLE_DIM), lambda i: (layer * n + i, 0)),
                  vec_spec, vec_spec,
                  pl.BlockSpec((D_MODEL, D_MODEL), lambda i: (0, 0)),
                  pl.BlockSpec((PLE_DIM, D_MODEL), lambda i: (0, 0)),
                  pl.BlockSpec(memory_space=pl.ANY)],
        out_specs=[row_spec, row_spec],
        out_shape=[jax.ShapeDtypeStruct((T, D_MODEL), F32), jax.ShapeDtypeStruct((T, D_MODEL), BF16)],
        scratch_shapes=[pltpu.VMEM((2, TOP_K, tm * ROW_CHUNKS, LANES), F32), pltpu.SemaphoreType.DMA((2,))],
        compiler_params=_params("arbitrary"),
        name="ffn_ln_ple",
    )(dest3, dest3, gates, h, p_all, ln_g.reshape(1, D_MODEL), ln_b.reshape(1, D_MODEL), wg_bf, wp_bf, ys)


def moe_layer(h1, layer, w_router, b_router, w_gu, b_gu, w_d, b_d):
    T = h1.shape[0]
    bm = MOE_BLOCK
    idx, gates, rank, counts = router(h1, w_router, b_router)
    padded = (counts + bm - 1) // bm * bm
    pend = jnp.cumsum(padded)
    pstart = pend - padded
    onehot = idx[..., None] == jnp.arange(N_EXPERTS, dtype=jnp.int32)
    dest = (jnp.sum(jnp.where(onehot, pstart, 0), axis=-1) + rank).astype(jnp.int32)
    dest8 = dest * ROW_CHUNKS
    n_slots = T * TOP_K + N_EXPERTS * bm
    n_blocks = n_slots // bm
    block_start = jnp.arange(n_blocks, dtype=jnp.int32) * bm
    block_expert = jnp.minimum(jnp.sum(block_start[:, None] >= pend[None, :], axis=1),
                               N_EXPERTS - 1).astype(jnp.int32)
    n_active = (pend[-1:] // bm).astype(jnp.int32)
    zfill8 = jnp.where(padded > 0, (pend - bm) * ROW_CHUNKS, -1).astype(jnp.int32)
    tt = _tile(T, DISPATCH_TOKENS)
    nt = T // tt
    oh_t = onehot.reshape(nt, tt * TOP_K, N_EXPERTS)
    tile_cnt = jnp.sum(oh_t, axis=1, dtype=jnp.int32)
    tile_base = jnp.cumsum(tile_cnt, axis=0) - tile_cnt
    stage_off = jnp.cumsum(tile_cnt, axis=1) - tile_cnt
    shift = jnp.sum(jnp.where(oh_t, (stage_off - tile_base)[:, None, :], 0), axis=-1)
    stage_pos8 = (shift + rank.reshape(nt, tt * TOP_K)).astype(jnp.int32) * ROW_CHUNKS
    run_src8 = (stage_off * ROW_CHUNKS).reshape(-1).astype(jnp.int32)
    run_dst8 = ((pstart[None, :] + tile_base) * ROW_CHUNKS).reshape(-1).astype(jnp.int32)
    xs = dispatch_rows(h1, stage_pos8, run_src8, run_dst8, tile_cnt.reshape(-1), zfill8, n_slots, bm)
    ys = moe_mlp(xs, block_expert, n_active, layer, w_gu, b_gu, w_d, b_d, bm)
    return ys, dest8, gates


def _rms_rows(x, g):
    return x * lax.rsqrt(jnp.mean(x * x, axis=-1, keepdims=True) + RMS_EPS) * g


def _mla_kv_kernel(x_ref, wa_ref, g_ref, wkn_ref, wv_ref, cm_ref, sm_ref, k_ref, v_ref):
    a = jnp.dot(x_ref[...], wa_ref[...], preferred_element_type=F32)
    c_kv = _rms_rows(a[:, :KV_LORA], g_ref[...]).astype(BF16)
    kr = a[:, KV_LORA:KV_LORA + LANES] * cm_ref[...] + a[:, KV_LORA + LANES:] * sm_ref[...]
    kr = kr.astype(BF16)
    kn = jnp.dot(c_kv, wkn_ref[...], preferred_element_type=F32)
    v_ref[...] = jnp.dot(c_kv, wv_ref[...], preferred_element_type=F32).astype(BF16)
    for h in range(MLA_HEADS):
        k_ref[:, 2 * h * LANES:(2 * h + 1) * LANES] = kn[:, h * QK_NOPE:(h + 1) * QK_NOPE].astype(BF16)
        k_ref[:, (2 * h + 1) * LANES:(2 * h + 2) * LANES] = kr


def _pad_rope_cols(w, swap):
    K = w.shape[0]
    w = w.reshape(K, -1, QK_ROPE)
    if swap:
        w = jnp.concatenate([w[..., QK_ROPE // 2:], w[..., :QK_ROPE // 2]], axis=-1)
    w = jnp.concatenate([w, jnp.zeros_like(w)], axis=-1)
    return w.reshape(K, -1)


def mla_shared_kv(hb, w_kv_a, kv_norm_g, w_kv_b, cm, sm):
    T = hb.shape[0]
    tm = _tile(T, TOKEN_ROWS)
    w_r = w_kv_a[:, KV_LORA:]
    wa = jnp.concatenate([w_kv_a[:, :KV_LORA], _pad_rope_cols(w_r, False), _pad_rope_cols(w_r, True)],
                         axis=1).astype(BF16)
    wb = w_kv_b.reshape(KV_LORA, MLA_HEADS, QK_NOPE + V_HEAD)
    wkn = wb[:, :, :QK_NOPE].reshape(KV_LORA, MLA_HEADS * QK_NOPE).astype(BF16)
    wv = wb[:, :, QK_NOPE:].reshape(KV_LORA, MLA_HEADS * V_HEAD).astype(BF16)
    full = lambda shape: pl.BlockSpec(shape, lambda i: (0, 0))
    return pl.pallas_call(
        _mla_kv_kernel,
        grid=(T // tm,),
        in_specs=[pl.BlockSpec((tm, D_MODEL), lambda i: (i, 0)), full(wa.shape), full((1, KV_LORA)),
                  full(wkn.shape), full(wv.shape),
                  pl.BlockSpec((tm, LANES), lambda i: (i, 0)), pl.BlockSpec((tm, LANES), lambda i: (i, 0))],
        out_specs=[pl.BlockSpec((tm, MLA_HEADS * 2 * LANES), lambda i: (i, 0)),
                   pl.BlockSpec((tm, MLA_HEADS * V_HEAD), lambda i: (i, 0))],
        out_shape=[jax.ShapeDtypeStruct((T, MLA_HEADS * 2 * LANES), BF16),
                   jax.ShapeDtypeStruct((T, MLA_HEADS * V_HEAD), BF16)],
        compiler_params=_params("arbitrary"),
        name="mla_kv",
    )(hb, wa, kv_norm_g.reshape(1, KV_LORA), wkn, wv, cm, sm)


def _mla_q_kernel(x_ref, wa_ref, g_ref, wb_ref, cm_ref, sm_ref, q_ref):
    scale = (QK_NOPE + QK_ROPE) ** -0.5 * LOG2_E
    qa = jnp.dot(x_ref[...], wa_ref[...], preferred_element_type=F32)
    qc = _rms_rows(qa, g_ref[...]).astype(BF16)
    q = jnp.dot(qc, wb_ref[...], preferred_element_type=F32)
    cm = cm_ref[...] * scale
    sm = sm_ref[...] * scale
    n = MLA_HEADS * LANES
    for h in range(MLA_HEADS):
        lo = h * LANES
        q_ref[:, 2 * lo:2 * lo + LANES] = (q[:, lo:lo + LANES] * scale).astype(BF16)
        rope = q[:, n + lo:n + lo + LANES] * cm + q[:, 2 * n + lo:2 * n + lo + LANES] * sm
        q_ref[:, 2 * lo + LANES:2 * lo + 2 * LANES] = rope.astype(BF16)


def mla_q(hb, w_q_a, q_norm_g, w_q_b, cm, sm):
    T = hb.shape[0]
    tm = _tile(T, TOKEN_ROWS)
    wq = w_q_b.reshape(Q_LORA, MLA_HEADS, QK_NOPE + QK_ROPE)
    w_n = wq[:, :, :QK_NOPE].reshape(Q_LORA, MLA_HEADS * QK_NOPE)
    w_r = wq[:, :, QK_NOPE:].reshape(Q_LORA, MLA_HEADS * QK_ROPE)
    wb = jnp.concatenate([w_n, _pad_rope_cols(w_r, False), _pad_rope_cols(w_r, True)], axis=1).astype(BF16)
    full = lambda shape: pl.BlockSpec(shape, lambda i: (0, 0))
    return pl.pallas_call(
        _mla_q_kernel,
        grid=(T // tm,),
        in_specs=[pl.BlockSpec((tm, D_MODEL), lambda i: (i, 0)), full((D_MODEL, Q_LORA)), full((1, Q_LORA)),
                  full(wb.shape),
                  pl.BlockSpec((tm, LANES), lambda i: (i, 0)), pl.BlockSpec((tm, LANES), lambda i: (i, 0))],
        out_specs=pl.BlockSpec((tm, MLA_HEADS * 2 * LANES), lambda i: (i, 0)),
        out_shape=jax.ShapeDtypeStruct((T, MLA_HEADS * 2 * LANES), BF16),
        compiler_params=_params("arbitrary"),
        name="mla_q",
    )(hb, w_q_a.astype(BF16), q_norm_g.reshape(1, Q_LORA), wb, cm, sm)


def _flash_kernel(q_ref, k_ref, v_ref, o_ref, m_ref, l_ref, acc_ref, *, bq, bk):
    i = pl.program_id(2)
    q = q_ref[...]
    m_ref[...] = jnp.full_like(m_ref, -jnp.inf)
    l_ref[...] = jnp.zeros_like(l_ref)
    acc_ref[...] = jnp.zeros_like(acc_ref)

    def step(j, rows=slice(0, bq), triangular=False):
        n = rows.stop - rows.start
        start = pl.multiple_of(j * bk, bk)
        kj = k_ref[pl.ds(start, bk), :]
        vj = v_ref[pl.ds(start, bk), :]
        s = lax.dot_general(q[rows], kj, NT_DIMS, preferred_element_type=F32)
        if triangular:
            r = lax.broadcasted_iota(jnp.int32, (n, bk), 0)
            c = lax.broadcasted_iota(jnp.int32, (n, bk), 1)
            s = jnp.where(c <= r, s, -jnp.inf)
        m_prev = m_ref[rows, :]
        m_new = jnp.maximum(m_prev, jnp.max(s, axis=-1, keepdims=True))
        alpha = jnp.exp2(m_prev - m_new)
        p = jnp.exp2(s - pltpu.repeat(m_new, bk // LANES, axis=1))
        l_ref[rows, :] = alpha * l_ref[rows, :] + jnp.sum(p, axis=-1, keepdims=True)
        acc_ref[rows, :] = alpha * acc_ref[rows, :] + jnp.dot(p.astype(BF16), vj, preferred_element_type=F32)
        m_ref[rows, :] = m_new

    def pair(j, carry):
        step(2 * j)
        step(2 * j + 1)
        return carry

    lax.fori_loop(0, i, pair, 0)
    step(2 * i, triangular=True)
    step(2 * i + 1, slice(bk, bq), triangular=True)
    o_ref[...] = (acc_ref[...] / l_ref[...]).astype(o_ref.dtype)


def flash_attention(q, k, v, B, S):
    T = B * S
    bq = _tile(S, ATTN_BLOCK_Q)
    bk = bq // 2
    nq = S // bq
    return pl.pallas_call(
        functools.partial(_flash_kernel, bq=bq, bk=bk),
        grid=(B, MLA_HEADS, nq),
        in_specs=[pl.BlockSpec((bq, 2 * LANES), lambda b, h, i: (b * nq + i, h)),
                  pl.BlockSpec((S, 2 * LANES), lambda b, h, i: (b, h)),
                  pl.BlockSpec((S, V_HEAD), lambda b, h, i: (b, h))],
        out_specs=pl.BlockSpec((bq, V_HEAD), lambda b, h, i: (b * nq + i, h)),
        out_shape=jax.ShapeDtypeStruct((T, MLA_HEADS * V_HEAD), BF16),
        scratch_shapes=[pltpu.VMEM((bq, LANES), F32), pltpu.VMEM((bq, LANES), F32),
                        pltpu.VMEM((bq, V_HEAD), F32)],
        compiler_params=_params("arbitrary", "arbitrary", "arbitrary"),
        name="mla_flash",
    )(q, k, v)


def kernel(x, p, positions, ret_w_in, ret_gn_g, ret_gn_b, ret_w_out, mla_w_kv_a, mla_kv_norm_g, mla_w_kv_b,
           mla_w_q_a, mla_q_norm_g, mla_w_q_b, mla_w_o, ln_mix_g, ln_mix_b, ln_ffn_g, ln_ffn_b,
           moe_w_router, moe_b_router, moe_w_gate_up, moe_b_gate_up, moe_w_down, moe_b_down,
           ple_w_gate, ple_w_proj):
    B, S, D = x.shape
    T = B * S
    cos_r, sin_r, cos_m, sin_m = rope_tables(positions)
    h = x.reshape(T, D)
    hb = h.astype(BF16)
    kv = None
    for i in range(DEPTH):
        if i < N_A:
            qk, vg = ret_projections(hb, ret_w_in[i].astype(BF16), cos_r, sin_r)
            z = retention(qk, vg, ret_gn_g[i], ret_gn_b[i], B, S)
            w_out = ret_w_out[i].astype(BF16)
        else:
            if kv is None:
                kv = mla_shared_kv(hb, mla_w_kv_a, mla_kv_norm_g, mla_w_kv_b, cos_m, sin_m)
            j = i - N_A
            q = mla_q(hb, mla_w_q_a[j], mla_q_norm_g[j], mla_w_q_b[j], cos_m, sin_m)
            z = flash_attention(q, kv[0], kv[1], B, S)
            w_out = mla_w_o[j].astype(BF16)
        h1 = out_proj_ln(z, w_out, h, ln_mix_g[i], ln_mix_b[i])
        ys, dest8, gates = moe_layer(h1, i, moe_w_router[i], moe_b_router[i], moe_w_gate_up, moe_b_gate_up,
                                     moe_w_down, moe_b_down)
        h, hb = ffn_ln_ple(ys, dest8, gates, h1, p.reshape(DEPTH * T, PLE_DIM), i, ln_ffn_g[i], ln_ffn_b[i],
                           ple_w_gate[i].astype(BF16), ple_w_proj[i].astype(BF16))
    return h.reshape(B, S, D)
```

```python
import functools

import jax
import jax.numpy as jnp
from jax import lax
from jax.experimental import pallas as pl
from jax.experimental.pallas import tpu as pltpu

D_MODEL = 1024
DEPTH = 4
N_A = DEPTH // 2
RET_HEADS = 4
RET_QK_DIM = D_MODEL // RET_HEADS
RET_V_DIM = 2 * RET_QK_DIM
GN_EPS = 1e-6
MLA_HEADS = 8
QK_NOPE = 128
QK_ROPE = 64
V_HEAD = 128
Q_LORA = 256
KV_LORA = 256
RMS_EPS = 1e-6
ROPE_THETA = 10000.0
N_EXPERTS = 32
TOP_K = 4
D_FF = D_MODEL
SWIGLU_LIMIT = 7.0
SWIGLU_ALPHA = 1.702
PLE_DIM = 256
DN_ALPHA = (2 * DEPTH) ** 0.25
LN_EPS = 1e-5
LOG2_E = 1.4426950408889634

LANES = 128
ROW_CHUNKS = D_MODEL // LANES
VMEM_LIMIT_BYTES = 56 * 1024 * 1024

TOKEN_ROWS = 512
PROJ_ROWS = 1024
RET_CHUNK = 256
RET_BLOCK = 1024
ATTN_BLOCK_Q = 1024
MOE_BLOCK = 512
FFN_ROW_CHUNKS = 2

F32 = jnp.float32
BF16 = jnp.bfloat16
NT_DIMS = (((1,), (1,)), ((), ()))
TN_DIMS = (((0,), (0,)), ((), ()))


def _tile(n, pref):
    t = min(n, pref)
    assert n % t == 0, (n, pref)
    return t


def _params(*sem):
    return pltpu.CompilerParams(dimension_semantics=sem, vmem_limit_bytes=VMEM_LIMIT_BYTES)


def _layer_norm_rows(u, g, b):
    mu = jnp.mean(u, axis=-1, keepdims=True)
    d = u - mu
    var = jnp.mean(d * d, axis=-1, keepdims=True)
    return d * lax.rsqrt(var + LN_EPS) * g + b


def _rope_table_kernel(pos_ref, inv_r_ref, inv_m_ref, sign_m_ref, cr_ref, sr_ref, cm_ref, sm_ref):
    pos = pos_ref[...].astype(F32)
    ang_r = pos * inv_r_ref[...]
    cr_ref[...] = jnp.cos(ang_r)
    sr_ref[...] = jnp.sin(ang_r)
    ang_m = pos * inv_m_ref[...]
    cm_ref[...] = jnp.cos(ang_m)
    sm_ref[...] = jnp.sin(ang_m) * sign_m_ref[...]


def rope_tables(positions):
    T = positions.size
    tm = _tile(T, PROJ_ROWS)
    pos = positions.reshape(T, 1)
    half_r = RET_QK_DIM // 2
    inv_r = ROPE_THETA ** (-jnp.arange(0, RET_QK_DIM, 2, dtype=F32) / RET_QK_DIM)
    inv_m = ROPE_THETA ** (-jnp.arange(0, QK_ROPE, 2, dtype=F32) / QK_ROPE)
    half_m = QK_ROPE // 2
    lane = jnp.arange(LANES)
    inv_m = inv_m[lane % half_m]
    sign_m = jnp.where((lane % QK_ROPE) < half_m, -1.0, 1.0).astype(F32)
    assert half_r == LANES
    row = lambda v: v.reshape(1, LANES)
    vec_spec = pl.BlockSpec((1, LANES), lambda i: (0, 0))
    tab_spec = pl.BlockSpec((tm, LANES), lambda i: (i, 0))
    out = jax.ShapeDtypeStruct((T, LANES), F32)
    return pl.pallas_call(
        _rope_table_kernel,
        grid=(T // tm,),
        in_specs=[pl.BlockSpec((tm, 1), lambda i: (i, 0)), vec_spec, vec_spec, vec_spec],
        out_specs=[tab_spec] * 4,
        out_shape=[out] * 4,
        compiler_params=_params("arbitrary"),
        name="rope_tables",
    )(pos, row(inv_r), row(inv_m), row(sign_m))


def _ret_qk_kernel(x_ref, w_ref, cos_ref, sin_ref, o_ref):
    acc = jnp.dot(x_ref[...], w_ref[...], preferred_element_type=F32)
    scale = jnp.where(pl.program_id(0) == 1, RET_QK_DIM ** -0.5, 1.0).astype(F32)
    c = cos_ref[...] * scale
    s = sin_ref[...] * scale
    half = RET_QK_DIM // 2
    for h in range(RET_HEADS):
        lo = h * RET_QK_DIM
        x1 = acc[:, lo:lo + half]
        x2 = acc[:, lo + half:lo + 2 * half]
        o_ref[:, lo:lo + half] = (x1 * c - x2 * s).astype(o_ref.dtype)
        o_ref[:, lo + half:lo + 2 * half] = (x1 * s + x2 * c).astype(o_ref.dtype)


def _matmul_kernel(x_ref, w_ref, o_ref):
    o_ref[...] = jnp.dot(x_ref[...], w_ref[...], preferred_element_type=F32).astype(o_ref.dtype)


def ret_projections(hb, w_in_bf, cos_r, sin_r):
    T = hb.shape[0]
    tm = _tile(T, PROJ_ROWS)
    tn = D_MODEL
    n_qk = 2 * RET_HEADS * RET_QK_DIM // tn
    n_vg = 2 * RET_HEADS * RET_V_DIM // tn
    x_spec = pl.BlockSpec((tm, D_MODEL), lambda j, i: (i, 0))
    tab_spec = pl.BlockSpec((tm, LANES), lambda j, i: (i, 0))
    qk = pl.pallas_call(
        _ret_qk_kernel,
        grid=(n_qk, T // tm),
        in_specs=[x_spec, pl.BlockSpec((D_MODEL, tn), lambda j, i: (0, j)), tab_spec, tab_spec],
        out_specs=pl.BlockSpec((tm, tn), lambda j, i: (i, j)),
        out_shape=jax.ShapeDtypeStruct((T, n_qk * tn), BF16),
        compiler_params=_params("arbitrary", "arbitrary"),
        name="ret_qk_proj",
    )(hb, w_in_bf, cos_r, sin_r)
    vg = pl.pallas_call(
        _matmul_kernel,
        grid=(n_vg, T // tm),
        in_specs=[x_spec, pl.BlockSpec((D_MODEL, tn), lambda j, i: (0, j + n_qk))],
        out_specs=pl.BlockSpec((tm, tn), lambda j, i: (i, j)),
        out_shape=jax.ShapeDtypeStruct((T, n_vg * tn), BF16),
        compiler_params=_params("arbitrary", "arbitrary"),
        name="ret_vg_proj",
    )(hb, w_in_bf)
    return qk, vg


def _retention_kernel(q_ref, k_ref, v_ref, g_ref, dm_ref, dq_ref, dk_ref, dc_ref, gng_ref, gnb_ref,
                      z_ref, state_ref, *, chunk, n_chunks):
    @pl.when(pl.program_id(2) == 0)
    def _():
        state_ref[...] = jnp.zeros_like(state_ref)

    dm = dm_ref[0]
    dq = dq_ref[0]
    dk = dk_ref[0]
    dc = dc_ref[0]
    gng = gng_ref[...]
    gnb = gnb_ref[...]
    for c in range(n_chunks):
        rows = slice(c * chunk, (c + 1) * chunk)
        q = q_ref[rows, :]
        k = k_ref[rows, :]
        v = v_ref[rows, :]
        st = state_ref[...]
        inner = lax.dot_general(q, k, NT_DIMS, preferred_element_type=F32) * dm
        y = jnp.dot(inner.astype(BF16), v, preferred_element_type=F32)
        y = y + jnp.dot(q, st.astype(BF16), preferred_element_type=F32) * dq
        kd = (k.astype(F32) * dk).astype(BF16)
        state_ref[...] = st * dc + lax.dot_general(kd, v, TN_DIMS, preferred_element_type=F32)
        mu = jnp.mean(y, axis=-1, keepdims=True)
        d = y - mu
        var = jnp.mean(d * d, axis=-1, keepdims=True)
        yn = d * lax.rsqrt(var + GN_EPS) * gng + gnb
        g = g_ref[rows, :].astype(F32)
        z_ref[rows, :] = (g * jax.nn.sigmoid(g) * yn).astype(z_ref.dtype)


def retention(qk, vg, gn_g, gn_b, B, S):
    T = B * S
    H, dk, dv = RET_HEADS, RET_QK_DIM, RET_V_DIM
    C = _tile(S, RET_CHUNK)
    L = _tile(S, RET_BLOCK)
    nl = S // L
    log_g = jnp.log(1.0 - 2.0 ** (-5.0 - jnp.arange(H, dtype=F32)))
    idx = jnp.arange(C, dtype=F32)
    diff = idx[:, None] - idx[None, :]
    dm = jnp.where(diff >= 0, jnp.exp(log_g[:, None, None] * jnp.maximum(diff, 0.0)), 0.0)
    dq = jnp.exp(log_g[:, None] * (idx + 1.0))[:, :, None]
    dkk = jnp.exp(log_g[:, None] * (C - 1.0 - idx))[:, :, None]
    dc = jnp.broadcast_to(jnp.exp(log_g * C)[:, None, None], (H, 1, dv))
    kern = functools.partial(_retention_kernel, chunk=C, n_chunks=L // C)
    return pl.pallas_call(
        kern,
        grid=(B, H, nl),
        in_specs=[
            pl.BlockSpec((L, dk), lambda b, h, n: (b * nl + n, h)),
            pl.BlockSpec((L, dk), lambda b, h, n: (b * nl + n, H + h)),
            pl.BlockSpec((L, dv), lambda b, h, n: (b * nl + n, h)),
            pl.BlockSpec((L, dv), lambda b, h, n: (b * nl + n, H + h)),
            pl.BlockSpec((1, C, C), lambda b, h, n: (h, 0, 0)),
            pl.BlockSpec((1, C, 1), lambda b, h, n: (h, 0, 0)),
            pl.BlockSpec((1, C, 1), lambda b, h, n: (h, 0, 0)),
            pl.BlockSpec((1, 1, dv), lambda b, h, n: (h, 0, 0)),
            pl.BlockSpec((1, dv), lambda b, h, n: (0, h)),
            pl.BlockSpec((1, dv), lambda b, h, n: (0, h)),
        ],
        out_specs=pl.BlockSpec((L, dv), lambda b, h, n: (b * nl + n, h)),
        out_shape=jax.ShapeDtypeStruct((T, H * dv), BF16),
        scratch_shapes=[pltpu.VMEM((dk, dv), F32)],
        compiler_params=_params("arbitrary", "arbitrary", "arbitrary"),
        name="retention",
    )(qk, qk, vg, vg, dm, dq, dkk, dc, gn_g.reshape(1, H * dv), gn_b.reshape(1, H * dv))


def _out_ln_kernel(z_ref, w_ref, h_ref, g_ref, b_ref, o_ref):
    mix = jnp.dot(z_ref[...], w_ref[...], preferred_element_type=F32)
    u = DN_ALPHA * h_ref[...] + mix
    o_ref[...] = _layer_norm_rows(u, g_ref[...], b_ref[...])


def out_proj_ln(z, w_bf, h, ln_g, ln_b):
    T, K = z.shape
    tm = _tile(T, TOKEN_ROWS)
    row_spec = pl.BlockSpec((tm, D_MODEL), lambda i: (i, 0))
    vec_spec = pl.BlockSpec((1, D_MODEL), lambda i: (0, 0))
    return pl.pallas_call(
        _out_ln_kernel,
        grid=(T // tm,),
        in_specs=[pl.BlockSpec((tm, K), lambda i: (i, 0)), pl.BlockSpec((K, D_MODEL), lambda i: (0, 0)),
                  row_spec, vec_spec, vec_spec],
        out_specs=row_spec,
        out_shape=jax.ShapeDtypeStruct((T, D_MODEL), F32),
        compiler_params=_params("arbitrary"),
        name="out_proj_ln",
    )(z, w_bf, h, ln_g.reshape(1, D_MODEL), ln_b.reshape(1, D_MODEL))


ROUTER_FIELDS = 16


def _router_kernel(h_ref, w_ref, b_ref, out_ref, cnt_ref, carry_ref, *, tm):
    @pl.when(pl.program_id(0) == 0)
    def _():
        carry_ref[...] = jnp.zeros_like(carry_ref)

    x = h_ref[...]
    w = w_ref[...]
    x_hi = x.astype(BF16)
    x_lo = (x - x_hi.astype(F32)).astype(BF16)
    w_hi = w.astype(BF16)
    w_lo = (w - w_hi.astype(F32)).astype(BF16)
    logits = (jnp.dot(x_hi, w_hi, preferred_element_type=F32)
              + (jnp.dot(x_hi, w_lo, preferred_element_type=F32)
                 + jnp.dot(x_lo, w_hi, preferred_element_type=F32))) + b_ref[...]
    work = logits.T
    e_iota = lax.broadcasted_iota(jnp.int32, (N_EXPERTS, tm), 0).astype(F32)
    onehot = jnp.zeros((N_EXPERTS, tm), F32)
    top_vals, hits, rows = [], [], []
    for _ in range(TOP_K):
        m = jnp.max(work, axis=0, keepdims=True)
        idx = jnp.min(jnp.where(work == m, e_iota, float(N_EXPERTS)), axis=0, keepdims=True)
        hit = e_iota == idx
        top_vals.append(m)
        hits.append(hit)
        rows.append(idx)
        work = jnp.where(hit, -jnp.inf, work)
        onehot = jnp.where(hit, 1.0, onehot)
    exps = [jnp.exp(v - top_vals[0]) for v in top_vals]
    denom = exps[0] + exps[1] + exps[2] + exps[3]
    rows += [e / denom for e in exps]

    r = lax.broadcasted_iota(jnp.int32, (tm, tm), 0)
    c = lax.broadcasted_iota(jnp.int32, (tm, tm), 1)
    earlier = jnp.where(r < c, 1.0, 0.0).astype(BF16)
    before = jnp.dot(onehot.astype(BF16), earlier, preferred_element_type=F32) + carry_ref[:, 0:1]
    rows += [jnp.sum(jnp.where(hit, before, 0.0), axis=0, keepdims=True) for hit in hits]

    field = lax.broadcasted_iota(jnp.int32, (ROUTER_FIELDS, tm), 0)
    out = jnp.zeros((ROUTER_FIELDS, tm), F32)
    for f, row in enumerate(rows):
        out = jnp.where(field == f, row, out)
    out_ref[...] = out

    tile_cnt = jnp.sum(onehot, axis=1, keepdims=True)
    cnt_ref[...] = jnp.broadcast_to(tile_cnt, cnt_ref.shape)
    carry_ref[...] = carry_ref[...] + tile_cnt


def router(h, w_router, b_router):
    T = h.shape[0]
    tm = _tile(T, TOKEN_ROWS)
    n = T // tm
    out, cnt = pl.pallas_call(
        functools.partial(_router_kernel, tm=tm),
        grid=(n,),
        in_specs=[pl.BlockSpec((tm, D_MODEL), lambda i: (i, 0)),
                  pl.BlockSpec((D_MODEL, N_EXPERTS), lambda i: (0, 0)),
                  pl.BlockSpec((1, N_EXPERTS), lambda i: (0, 0))],
        out_specs=[pl.BlockSpec((ROUTER_FIELDS, tm), lambda i: (i, 0)),
                   pl.BlockSpec((N_EXPERTS, LANES), lambda i: (i, 0))],
        out_shape=[jax.ShapeDtypeStruct((n * ROUTER_FIELDS, tm), F32),
                   jax.ShapeDtypeStruct((n * N_EXPERTS, LANES), F32)],
        scratch_shapes=[pltpu.VMEM((N_EXPERTS, LANES), F32)],
        compiler_params=_params("arbitrary"),
        name="router",
    )(h, w_router, b_router.reshape(1, N_EXPERTS))
    out = out.reshape(n, ROUTER_FIELDS, tm)
    idx = out[:, 0:TOP_K].astype(jnp.int32)
    gates = out[:, TOP_K:2 * TOP_K]
    rank = out[:, 2 * TOP_K:3 * TOP_K].astype(jnp.int32)
    tile_cnt = cnt.reshape(n, N_EXPERTS, LANES)[:, :, 0].astype(jnp.int32)
    return idx, gates, rank, tile_cnt


def _rows_to_tiles(dst_ref, x, n):
    for c in range(ROW_CHUNKS):
        dst_ref[pl.ds(c, n, stride=ROW_CHUNKS), :] = x[:, c * LANES:(c + 1) * LANES]


def _tiles_to_rows(src_ref, n):
    return jnp.concatenate([src_ref[pl.ds(c, n, stride=ROW_CHUNKS), :] for c in range(ROW_CHUNKS)], axis=1)


def _dispatch_kernel(zfill_ref, rsrc_ref, rdst_ref, rlen_ref, q_ref, h_ref, xs_ref,
                     rows, stage, sems, zsem, *, tt, bm):
    i = pl.program_id(0)
    last = pl.num_programs(0) - 1
    slot = i % 2

    def start_runs(step):
        buf = stage.at[step % 2]
        sem = sems.at[step % 2]

        def per_expert(e, carry):
            n = rlen_ref[step * N_EXPERTS + e]
            src = rsrc_ref[step * N_EXPERTS + e]
            dst = rdst_ref[step * N_EXPERTS + e]
            done = jnp.int32(0)
            piece = tt
            while piece >= 1:
                take = (n & piece) != 0
                size = piece * ROW_CHUNKS

                @pl.when(take)
                def _(done=done, size=size):
                    s = pl.multiple_of(src + done, ROW_CHUNKS)
                    d = pl.multiple_of(dst + done, ROW_CHUNKS)
                    pltpu.make_async_copy(buf.at[pl.ds(s, size)], xs_ref.at[pl.ds(d, size)], sem).start()

                done = done + jnp.where(take, size, 0)
                piece //= 2
            return carry
        lax.fori_loop(0, N_EXPERTS, per_expert, 0)

    def wait_runs(step):
        buf = stage.at[step % 2]
        pltpu.make_async_copy(buf, xs_ref.at[pl.ds(0, tt * TOP_K * ROW_CHUNKS)], sems.at[step % 2]).wait()

    @pl.when(i == 0)
    def _():
        zeros = stage.at[0, pl.ds(0, bm * ROW_CHUNKS)]
        zeros[...] = jnp.zeros_like(zeros)

        def zero_copy(e):
            start = pl.multiple_of(zfill_ref[e], bm * ROW_CHUNKS)
            return pltpu.make_async_copy(zeros, xs_ref.at[pl.ds(start, bm * ROW_CHUNKS)], zsem)

        for e in range(N_EXPERTS):
            @pl.when(zfill_ref[e] >= 0)
            def _():
                zero_copy(e).start()
        for e in range(N_EXPERTS):
            @pl.when(zfill_ref[e] >= 0)
            def _():
                zero_copy(e).wait()

    @pl.when(i >= 2)
    def _():
        wait_runs(i - 2)

    _rows_to_tiles(rows, h_ref[...], tt)

    def sort_rows(t, carry):
        row = rows[pl.ds(pl.multiple_of(t * ROW_CHUNKS, ROW_CHUNKS), ROW_CHUNKS), :]
        for k in range(TOP_K):
            q = pl.multiple_of(q_ref[0, 0, k * tt + t], ROW_CHUNKS)
            stage[slot, pl.ds(q, ROW_CHUNKS), :] = row
        return carry

    lax.fori_loop(0, tt, sort_rows, 0, unroll=8)
    start_runs(i)

    @pl.when(jnp.logical_and(i == last, i >= 1))
    def _():
        wait_runs(i - 1)

    @pl.when(i == last)
    def _():
        wait_runs(i)


def dispatch_rows(h, pos3, run_src8, run_dst8, run_len, zfill8, n_slots, bm):
    T = h.shape[0]
    tt = _tile(T, TOKEN_ROWS)
    assert pos3.shape == (T // tt, 1, tt * TOP_K)
    assert tt * TOP_K >= bm
    return pl.pallas_call(
        functools.partial(_dispatch_kernel, tt=tt, bm=bm),
        grid_spec=pltpu.PrefetchScalarGridSpec(
            num_scalar_prefetch=4,
            grid=(T // tt,),
            in_specs=[pl.BlockSpec((1, 1, tt * TOP_K), lambda i, *_: (i, 0, 0), memory_space=pltpu.SMEM),
                      pl.BlockSpec((tt, D_MODEL), lambda i, *_: (i, 0))],
            out_specs=pl.BlockSpec(memory_space=pl.ANY),
            scratch_shapes=[pltpu.VMEM((tt * ROW_CHUNKS, LANES), F32),
                            pltpu.VMEM((2, tt * TOP_K * ROW_CHUNKS, LANES), F32),
                            pltpu.SemaphoreType.DMA((2,)), pltpu.SemaphoreType.DMA],
        ),
        out_shape=jax.ShapeDtypeStruct((n_slots * ROW_CHUNKS, LANES), F32),
        compiler_params=_params("arbitrary"),
        name="moe_dispatch",
    )(zfill8, run_src8, run_dst8, run_len, pos3, h)


def _moe_mlp_kernel(be_ref, na_ref, x_ref, wgu_ref, bgu_ref, wd_ref, bd_ref, y_ref, wgu_bf, wd_bf, *, bm):
    i = pl.program_id(0)
    prev = be_ref[jnp.maximum(i - 1, 0)]
    active = i < na_ref[0]

    @pl.when(jnp.logical_and(active, jnp.logical_or(i == 0, be_ref[i] != prev)))
    def _():
        wgu_bf[...] = wgu_ref[...].astype(BF16)
        wd_bf[...] = wd_ref[...].astype(BF16)

    @pl.when(active)
    def _():
        x = _tiles_to_rows(x_ref, bm).astype(BF16)
        hgu = jnp.dot(x, wgu_bf[...], preferred_element_type=F32) + bgu_ref[...]
        gate = jnp.minimum(hgu[:, :D_FF], SWIGLU_LIMIT)
        up = jnp.clip(hgu[:, D_FF:], -SWIGLU_LIMIT, SWIGLU_LIMIT)
        act = gate * jax.nn.sigmoid(SWIGLU_ALPHA * gate) * (up + 1.0)
        y = jnp.dot(act.astype(BF16), wd_bf[...], preferred_element_type=F32) + bd_ref[...]
        _rows_to_tiles(y_ref, y, bm)


def moe_mlp(xs, block_expert, n_active, layer, w_gu, b_gu, w_d, b_d, bm):
    n_blocks = xs.shape[0] // (bm * ROW_CHUNKS)

    def row_map(i, be, na):
        return (jnp.minimum(i, na[0] - 1), 0)

    def w_map(i, be, na):
        return (layer, be[jnp.minimum(i, na[0] - 1)], 0, 0)

    def b_map(i, be, na):
        return (layer * N_EXPERTS + be[jnp.minimum(i, na[0] - 1)], 0, 0)

    return pl.pallas_call(
        functools.partial(_moe_mlp_kernel, bm=bm),
        grid_spec=pltpu.PrefetchScalarGridSpec(
            num_scalar_prefetch=2,
            grid=(n_blocks,),
            in_specs=[pl.BlockSpec((bm * ROW_CHUNKS, LANES), row_map),
                      pl.BlockSpec((None, None, D_MODEL, 2 * D_FF), w_map),
                      pl.BlockSpec((None, 1, 2 * D_FF), b_map),
                      pl.BlockSpec((None, None, D_FF, D_MODEL), w_map),
                      pl.BlockSpec((None, 1, D_MODEL), b_map)],
            out_specs=pl.BlockSpec((bm * ROW_CHUNKS, LANES), row_map),
            scratch_shapes=[pltpu.VMEM((D_MODEL, 2 * D_FF), BF16), pltpu.VMEM((D_FF, D_MODEL), BF16)],
        ),
        out_shape=jax.ShapeDtypeStruct(xs.shape, F32),
        compiler_params=_params("arbitrary"),
        name="moe_mlp",
    )(block_expert, n_active, xs, w_gu, b_gu.reshape(DEPTH * N_EXPERTS, 1, 2 * D_FF), w_d,
      b_d.reshape(DEPTH * N_EXPERTS, 1, D_MODEL))


def _ffn_ln_ple_kernel(dcur_ref, dnext_ref, gates_ref, h_ref, p_ref, g_ref, b_ref, wg_ref, wp_ref, ys_ref,
                       o_ref, ob_ref, ybuf, sems, *, tm):
    i = pl.program_id(0)
    slot = i % 2

    def row_copy(dref, s, t, k):
        d = pl.multiple_of(dref[0, 0, k * tm + t], ROW_CHUNKS)
        dst = ybuf.at[s, k, pl.ds(pl.multiple_of(t * ROW_CHUNKS, ROW_CHUNKS), ROW_CHUNKS)]
        return pltpu.make_async_copy(ys_ref.at[pl.ds(d, ROW_CHUNKS)], dst, sems.at[s])

    def issue(dref, s):
        def body(t, carry):
            for k in range(TOP_K):
                row_copy(dref, s, t, k).start(priority=k % 2)
            return carry
        lax.fori_loop(0, tm, body, 0, unroll=4)

    @pl.when(i == 0)
    def _():
        issue(dcur_ref, 0)

    @pl.when(i + 1 < pl.num_programs(0))
    def _():
        issue(dnext_ref, 1 - slot)

    for k in range(TOP_K):
        pltpu.make_async_copy(ys_ref.at[pl.ds(0, tm * ROW_CHUNKS)], ybuf.at[slot, k], sems.at[slot]).wait()

    sub = tm // FFN_ROW_CHUNKS
    for c in range(FFN_ROW_CHUNKS):
        rows = slice(c * sub, (c + 1) * sub)
        tiles = pl.ds(c * sub * ROW_CHUNKS, sub * ROW_CHUNKS)
        gates = gates_ref[rows, :]
        ffn = _tiles_to_rows(ybuf.at[slot, 0, tiles], sub) * gates[:, 0:1]
        for k in range(1, TOP_K):
            ffn = ffn + _tiles_to_rows(ybuf.at[slot, k, tiles], sub) * gates[:, k:k + 1]
        h2 = _layer_norm_rows(DN_ALPHA * h_ref[rows, :] + ffn, g_ref[...], b_ref[...])
        gate = jax.nn.sigmoid(jnp.dot(h2.astype(BF16), wg_ref[...], preferred_element_type=F32))
        proj = jnp.dot(p_ref[rows, :].astype(BF16), wp_ref[...], preferred_element_type=F32)
        out = h2 + gate * proj
        o_ref[rows, :] = out
        ob_ref[rows, :] = out.astype(BF16)


def ffn_ln_ple(ys, dest3, gates, h, p_all, layer, ln_g, ln_b, wg_bf, wp_bf):
    T = h.shape[0]
    tm = _tile(T, TOKEN_ROWS)
    n = T // tm
    assert dest3.shape == (n, 1, tm * TOP_K)
    row_spec = pl.BlockSpec((tm, D_MODEL), lambda i: (i, 0))
    vec_spec = pl.BlockSpec((1, D_MODEL), lambda i: (0, 0))
    return pl.pallas_call(
        functools.partial(_ffn_ln_ple_kernel, tm=tm),
        grid=(n,),
        in_specs=[pl.BlockSpec((1, 1, tm * TOP_K), lambda i: (i, 0, 0), memory_space=pltpu.SMEM),
                  pl.BlockSpec((1, 1, tm * TOP_K), lambda i: (jnp.minimum(i + 1, n - 1), 0, 0),
                               memory_space=pltpu.SMEM),
                  pl.BlockSpec((tm, TOP_K), lambda i: (i, 0)),
                  row_spec,
                  pl.BlockSpec((tm, PLE_DIM), lambda i: (layer * n + i, 0)),
                  vec_spec, vec_spec,
                  pl.BlockSpec((D_MODEL, D_MODEL), lambda i: (0, 0)),
                  pl.BlockSpec((PLE_DIM, D_MODEL), lambda i: (0, 0)),
                  pl.BlockSpec(memory_space=pl.ANY)],
        out_specs=[row_spec, row_spec],
        out_shape=[jax.ShapeDtypeStruct((T, D_MODEL), F32), jax.ShapeDtypeStruct((T, D_MODEL), BF16)],
        scratch_shapes=[pltpu.VMEM((2, TOP_K, tm * ROW_CHUNKS, LANES), F32), pltpu.SemaphoreType.DMA((2,))],
        compiler_params=_params("arbitrary"),
        name="ffn_ln_ple",
    )(dest3, dest3, gates, h, p_all, ln_g.reshape(1, D_MODEL), ln_b.reshape(1, D_MODEL), wg_bf, wp_bf, ys)


def moe_layer(h1, layer, w_router, b_router, w_gu, b_gu, w_d, b_d):
    T = h1.shape[0]
    bm = MOE_BLOCK
    idx, gates, rank, tile_cnt = router(h1, w_router, b_router)
    nt, _, tt = idx.shape
    counts = jnp.sum(tile_cnt, axis=0)
    padded = (counts + bm - 1) // bm * bm
    pend = jnp.cumsum(padded)
    pstart = pend - padded
    n_slots = T * TOP_K + N_EXPERTS * bm
    n_blocks = n_slots // bm
    block_start = jnp.arange(n_blocks, dtype=jnp.int32) * bm
    block_expert = jnp.minimum(jnp.sum(block_start[:, None] >= pend[None, :], axis=1),
                               N_EXPERTS - 1).astype(jnp.int32)
    n_active = (pend[-1:] // bm).astype(jnp.int32)
    zfill8 = jnp.where(padded > 0, (pend - bm) * ROW_CHUNKS, -1).astype(jnp.int32)
    tile_base = jnp.cumsum(tile_cnt, axis=0) - tile_cnt
    stage_off = jnp.cumsum(tile_cnt, axis=1) - tile_cnt
    onehot = idx[:, :, None, :] == jnp.arange(N_EXPERTS, dtype=jnp.int32)[:, None]

    def per_assignment(table):
        return jnp.sum(jnp.where(onehot, table[:, None, :, None], 0), axis=2)

    dest8 = ((per_assignment(jnp.broadcast_to(pstart, (nt, N_EXPERTS))) + rank) * ROW_CHUNKS).astype(jnp.int32)
    stage_pos8 = ((per_assignment(stage_off - tile_base) + rank) * ROW_CHUNKS).astype(jnp.int32)
    run_src8 = (stage_off * ROW_CHUNKS).reshape(-1).astype(jnp.int32)
    run_dst8 = ((pstart[None, :] + tile_base) * ROW_CHUNKS).reshape(-1).astype(jnp.int32)
    xs = dispatch_rows(h1, stage_pos8.reshape(nt, 1, TOP_K * tt), run_src8, run_dst8, tile_cnt.reshape(-1),
                       zfill8, n_slots, bm)
    ys = moe_mlp(xs, block_expert, n_active, layer, w_gu, b_gu, w_d, b_d, bm)
    gates_rows = gates.transpose(0, 2, 1).reshape(T, TOP_K)
    return ys, dest8.reshape(nt, 1, TOP_K * tt), gates_rows


def _rms_rows(x, g):
    return x * lax.rsqrt(jnp.mean(x * x, axis=-1, keepdims=True) + RMS_EPS) * g


def _mla_kv_kernel(x_ref, wa_ref, g_ref, wkn_ref, wv_ref, cm_ref, sm_ref, k_ref, v_ref):
    a = jnp.dot(x_ref[...], wa_ref[...], preferred_element_type=F32)
    c_kv = _rms_rows(a[:, :KV_LORA], g_ref[...]).astype(BF16)
    kr = a[:, KV_LORA:KV_LORA + LANES] * cm_ref[...] + a[:, KV_LORA + LANES:] * sm_ref[...]
    kr = kr.astype(BF16)
    kn = jnp.dot(c_kv, wkn_ref[...], preferred_element_type=F32)
    v_ref[...] = jnp.dot(c_kv, wv_ref[...], preferred_element_type=F32).astype(BF16)
    for h in range(MLA_HEADS):
        k_ref[:, 2 * h * LANES:(2 * h + 1) * LANES] = kn[:, h * QK_NOPE:(h + 1) * QK_NOPE].astype(BF16)
        k_ref[:, (2 * h + 1) * LANES:(2 * h + 2) * LANES] = kr


def _pad_rope_cols(w, swap):
    K = w.shape[0]
    w = w.reshape(K, -1, QK_ROPE)
    if swap:
        w = jnp.concatenate([w[..., QK_ROPE // 2:], w[..., :QK_ROPE // 2]], axis=-1)
    w = jnp.concatenate([w, jnp.zeros_like(w)], axis=-1)
    return w.reshape(K, -1)


def mla_shared_kv(hb, w_kv_a, kv_norm_g, w_kv_b, cm, sm):
    T = hb.shape[0]
    tm = _tile(T, TOKEN_ROWS)
    w_r = w_kv_a[:, KV_LORA:]
    wa = jnp.concatenate([w_kv_a[:, :KV_LORA], _pad_rope_cols(w_r, False), _pad_rope_cols(w_r, True)],
                         axis=1).astype(BF16)
    wb = w_kv_b.reshape(KV_LORA, MLA_HEADS, QK_NOPE + V_HEAD)
    wkn = wb[:, :, :QK_NOPE].reshape(KV_LORA, MLA_HEADS * QK_NOPE).astype(BF16)
    wv = wb[:, :, QK_NOPE:].reshape(KV_LORA, MLA_HEADS * V_HEAD).astype(BF16)
    full = lambda shape: pl.BlockSpec(shape, lambda i: (0, 0))
    return pl.pallas_call(
        _mla_kv_kernel,
        grid=(T // tm,),
        in_specs=[pl.BlockSpec((tm, D_MODEL), lambda i: (i, 0)), full(wa.shape), full((1, KV_LORA)),
                  full(wkn.shape), full(wv.shape),
                  pl.BlockSpec((tm, LANES), lambda i: (i, 0)), pl.BlockSpec((tm, LANES), lambda i: (i, 0))],
        out_specs=[pl.BlockSpec((tm, MLA_HEADS * 2 * LANES), lambda i: (i, 0)),
                   pl.BlockSpec((tm, MLA_HEADS * V_HEAD), lambda i: (i, 0))],
        out_shape=[jax.ShapeDtypeStruct((T, MLA_HEADS * 2 * LANES), BF16),
                   jax.ShapeDtypeStruct((T, MLA_HEADS * V_HEAD), BF16)],
        compiler_params=_params("arbitrary"),
        name="mla_kv",
    )(hb, wa, kv_norm_g.reshape(1, KV_LORA), wkn, wv, cm, sm)


def _mla_q_kernel(x_ref, wa_ref, g_ref, wb_ref, cm_ref, sm_ref, q_ref):
    scale = (QK_NOPE + QK_ROPE) ** -0.5 * LOG2_E
    qa = jnp.dot(x_ref[...], wa_ref[...], preferred_element_type=F32)
    qc = _rms_rows(qa, g_ref[...]).astype(BF16)
    q = jnp.dot(qc, wb_ref[...], preferred_element_type=F32)
    cm = cm_ref[...] * scale
    sm = sm_ref[...] * scale
    n = MLA_HEADS * LANES
    for h in range(MLA_HEADS):
        lo = h * LANES
        q_ref[:, 2 * lo:2 * lo + LANES] = (q[:, lo:lo + LANES] * scale).astype(BF16)
        rope = q[:, n + lo:n + lo + LANES] * cm + q[:, 2 * n + lo:2 * n + lo + LANES] * sm
        q_ref[:, 2 * lo + LANES:2 * lo + 2 * LANES] = rope.astype(BF16)


def mla_q(hb, w_q_a, q_norm_g, w_q_b, cm, sm):
    T = hb.shape[0]
    tm = _tile(T, TOKEN_ROWS)
    wq = w_q_b.reshape(Q_LORA, MLA_HEADS, QK_NOPE + QK_ROPE)
    w_n = wq[:, :, :QK_NOPE].reshape(Q_LORA, MLA_HEADS * QK_NOPE)
    w_r = wq[:, :, QK_NOPE:].reshape(Q_LORA, MLA_HEADS * QK_ROPE)
    wb = jnp.concatenate([w_n, _pad_rope_cols(w_r, False), _pad_rope_cols(w_r, True)], axis=1).astype(BF16)
    full = lambda shape: pl.BlockSpec(shape, lambda i: (0, 0))
    return pl.pallas_call(
        _mla_q_kernel,
        grid=(T // tm,),
        in_specs=[pl.BlockSpec((tm, D_MODEL), lambda i: (i, 0)), full((D_MODEL, Q_LORA)), full((1, Q_LORA)),
                  full(wb.shape),
                  pl.BlockSpec((tm, LANES), lambda i: (i, 0)), pl.BlockSpec((tm, LANES), lambda i: (i, 0))],
        out_specs=pl.BlockSpec((tm, MLA_HEADS * 2 * LANES), lambda i: (i, 0)),
        out_shape=jax.ShapeDtypeStruct((T, MLA_HEADS * 2 * LANES), BF16),
        compiler_params=_params("arbitrary"),
        name="mla_q",
    )(hb, w_q_a.astype(BF16), q_norm_g.reshape(1, Q_LORA), wb, cm, sm)


def _flash_kernel(q_ref, k_ref, v_ref, o_ref, m_ref, l_ref, acc_ref, *, bq, bk):
    i = pl.program_id(2)
    q = q_ref[...]
    m_ref[...] = jnp.full_like(m_ref, -jnp.inf)
    l_ref[...] = jnp.zeros_like(l_ref)
    acc_ref[...] = jnp.zeros_like(acc_ref)

    def step(j, rows=slice(0, bq), triangular=False):
        n = rows.stop - rows.start
        start = pl.multiple_of(j * bk, bk)
        kj = k_ref[pl.ds(start, bk), :]
        vj = v_ref[pl.ds(start, bk), :]
        s = lax.dot_general(q[rows], kj, NT_DIMS, preferred_element_type=F32)
        if triangular:
            r = lax.broadcasted_iota(jnp.int32, (n, bk), 0)
            c = lax.broadcasted_iota(jnp.int32, (n, bk), 1)
            s = jnp.where(c <= r, s, -jnp.inf)
        m_prev = m_ref[rows, :]
        m_new = jnp.maximum(m_prev, jnp.max(s, axis=-1, keepdims=True))
        alpha = jnp.exp2(m_prev - m_new)
        p = jnp.exp2(s - pltpu.repeat(m_new, bk // LANES, axis=1))
        l_ref[rows, :] = alpha * l_ref[rows, :] + jnp.sum(p, axis=-1, keepdims=True)
        acc_ref[rows, :] = alpha * acc_ref[rows, :] + jnp.dot(p.astype(BF16), vj, preferred_element_type=F32)
        m_ref[rows, :] = m_new

    def pair(j, carry):
        step(2 * j)
        step(2 * j + 1)
        return carry

    lax.fori_loop(0, i, pair, 0)
    step(2 * i, triangular=True)
    step(2 * i + 1, slice(bk, bq), triangular=True)
    o_ref[...] = (acc_ref[...] / l_ref[...]).astype(o_ref.dtype)


def flash_attention(q, k, v, B, S):
    T = B * S
    bq = _tile(S, ATTN_BLOCK_Q)
    bk = bq // 2
    nq = S // bq
    return pl.pallas_call(
        functools.partial(_flash_kernel, bq=bq, bk=bk),
        grid=(B, MLA_HEADS, nq),
        in_specs=[pl.BlockSpec((bq, 2 * LANES), lambda b, h, i: (b * nq + i, h)),
                  pl.BlockSpec((S, 2 * LANES), lambda b, h, i: (b, h)),
                  pl.BlockSpec((S, V_HEAD), lambda b, h, i: (b, h))],
        out_specs=pl.BlockSpec((bq, V_HEAD), lambda b, h, i: (b * nq + i, h)),
        out_shape=jax.ShapeDtypeStruct((T, MLA_HEADS * V_HEAD), BF16),
        scratch_shapes=[pltpu.VMEM((bq, LANES), F32), pltpu.VMEM((bq, LANES), F32),
                        pltpu.VMEM((bq, V_HEAD), F32)],
        compiler_params=_params("arbitrary", "arbitrary", "arbitrary"),
        name="mla_flash",
    )(q, k, v)


def kernel(x, p, positions, ret_w_in, ret_gn_g, ret_gn_b, ret_w_out, mla_w_kv_a, mla_kv_norm_g, mla_w_kv_b,
           mla_w_q_a, mla_q_norm_g, mla_w_q_b, mla_w_o, ln_mix_g, ln_mix_b, ln_ffn_g, ln_ffn_b,
           moe_w_router, moe_b_router, moe_w_gate_up, moe_b_gate_up, moe_w_down, moe_b_down,
           ple_w_gate, ple_w_proj):
    B, S, D = x.shape
    T = B * S
    cos_r, sin_r, cos_m, sin_m = rope_tables(positions)
    h = x.reshape(T, D)
    hb = h.astype(BF16)
    kv = None
    for i in range(DEPTH):
        if i < N_A:
            qk, vg = ret_projections(hb, ret_w_in[i].astype(BF16), cos_r, sin_r)
            z = retention(qk, vg, ret_gn_g[i], ret_gn_b[i], B, S)
            w_out = ret_w_out[i].astype(BF16)
        else:
            if kv is None:
                kv = mla_shared_kv(hb, mla_w_kv_a, mla_kv_norm_g, mla_w_kv_b, cos_m, sin_m)
            j = i - N_A
            q = mla_q(hb, mla_w_q_a[j], mla_q_norm_g[j], mla_w_q_b[j], cos_m, sin_m)
            z = flash_attention(q, kv[0], kv[1], B, S)
            w_out = mla_w_o[j].astype(BF16)
        h1 = out_proj_ln(z, w_out, h, ln_mix_g[i], ln_mix_b[i])
        ys, dest8, gates = moe_layer(h1, i, moe_w_router[i], moe_b_router[i], moe_w_gate_up, moe_b_gate_up,
                                     moe_w_down, moe_b_down)
        h, hb = ffn_ln_ple(ys, dest8, gates, h1, p.reshape(DEPTH * T, PLE_DIM), i, ln_ffn_g[i], ln_ffn_b[i],
                           ple_w_gate[i].astype(BF16), ple_w_proj[i].astype(BF16))
    return h.reshape(B, S, D)
```

```python
import functools

import jax
import jax.numpy as jnp
from jax import lax
from jax.experimental import pallas as pl
from jax.experimental.pallas import tpu as pltpu

D_MODEL = 1024
DEPTH = 4
N_A = DEPTH // 2
RET_HEADS = 4
RET_QK_DIM = D_MODEL // RET_HEADS
RET_V_DIM = 2 * RET_QK_DIM
GN_EPS = 1e-6
MLA_HEADS = 8
QK_NOPE = 128
QK_ROPE = 64
V_HEAD = 128
Q_LORA = 256
KV_LORA = 256
RMS_EPS = 1e-6
ROPE_THETA = 10000.0
N_EXPERTS = 32
TOP_K = 4
D_FF = D_MODEL
SWIGLU_LIMIT = 7.0
SWIGLU_ALPHA = 1.702
PLE_DIM = 256
DN_ALPHA = (2 * DEPTH) ** 0.25
LN_EPS = 1e-5
LOG2_E = 1.4426950408889634

LANES = 128
ROW_CHUNKS = D_MODEL // LANES
VMEM_LIMIT_BYTES = 56 * 1024 * 1024

TOKEN_ROWS = 512
PROJ_ROWS = 1024
RET_CHUNK = 256
RET_BLOCK = 1024
ATTN_BLOCK_Q = 1024
MOE_BLOCK = 512
FFN_ROW_CHUNKS = 2

F32 = jnp.float32
BF16 = jnp.bfloat16
NT_DIMS = (((1,), (1,)), ((), ()))
TN_DIMS = (((0,), (0,)), ((), ()))


def _tile(n, pref):
    t = min(n, pref)
    assert n % t == 0, (n, pref)
    return t


def _params(*sem):
    return pltpu.CompilerParams(dimension_semantics=sem, vmem_limit_bytes=VMEM_LIMIT_BYTES)


def _layer_norm_rows(u, g, b):
    mu = jnp.mean(u, axis=-1, keepdims=True)
    d = u - mu
    var = jnp.mean(d * d, axis=-1, keepdims=True)
    return d * lax.rsqrt(var + LN_EPS) * g + b


def _rope_table_kernel(pos_ref, inv_r_ref, inv_m_ref, sign_m_ref, cr_ref, sr_ref, cm_ref, sm_ref):
    pos = pos_ref[...].astype(F32)
    ang_r = pos * inv_r_ref[...]
    cr_ref[...] = jnp.cos(ang_r)
    sr_ref[...] = jnp.sin(ang_r)
    ang_m = pos * inv_m_ref[...]
    cm_ref[...] = jnp.cos(ang_m)
    sm_ref[...] = jnp.sin(ang_m) * sign_m_ref[...]


def rope_tables(positions):
    T = positions.size
    tm = _tile(T, PROJ_ROWS)
    pos = positions.reshape(T, 1)
    half_r = RET_QK_DIM // 2
    inv_r = ROPE_THETA ** (-jnp.arange(0, RET_QK_DIM, 2, dtype=F32) / RET_QK_DIM)
    inv_m = ROPE_THETA ** (-jnp.arange(0, QK_ROPE, 2, dtype=F32) / QK_ROPE)
    half_m = QK_ROPE // 2
    lane = jnp.arange(LANES)
    inv_m = inv_m[lane % half_m]
    sign_m = jnp.where((lane % QK_ROPE) < half_m, -1.0, 1.0).astype(F32)
    assert half_r == LANES
    row = lambda v: v.reshape(1, LANES)
    vec_spec = pl.BlockSpec((1, LANES), lambda i: (0, 0))
    tab_spec = pl.BlockSpec((tm, LANES), lambda i: (i, 0))
    out = jax.ShapeDtypeStruct((T, LANES), F32)
    return pl.pallas_call(
        _rope_table_kernel,
        grid=(T // tm,),
        in_specs=[pl.BlockSpec((tm, 1), lambda i: (i, 0)), vec_spec, vec_spec, vec_spec],
        out_specs=[tab_spec] * 4,
        out_shape=[out] * 4,
        compiler_params=_params("arbitrary"),
        name="rope_tables",
    )(pos, row(inv_r), row(inv_m), row(sign_m))


def _ret_qk_kernel(x_ref, w_ref, cos_ref, sin_ref, o_ref):
    acc = jnp.dot(x_ref[...], w_ref[...], preferred_element_type=F32)
    scale = jnp.where(pl.program_id(0) == 1, RET_QK_DIM ** -0.5, 1.0).astype(F32)
    c = cos_ref[...] * scale
    s = sin_ref[...] * scale
    half = RET_QK_DIM // 2
    for h in range(RET_HEADS):
        lo = h * RET_QK_DIM
        x1 = acc[:, lo:lo + half]
        x2 = acc[:, lo + half:lo + 2 * half]
        o_ref[:, lo:lo + half] = (x1 * c - x2 * s).astype(o_ref.dtype)
        o_ref[:, lo + half:lo + 2 * half] = (x1 * s + x2 * c).astype(o_ref.dtype)


def _matmul_kernel(x_ref, w_ref, o_ref):
    o_ref[...] = jnp.dot(x_ref[...], w_ref[...], preferred_element_type=F32).astype(o_ref.dtype)


def ret_projections(hb, w_in_bf, cos_r, sin_r):
    T = hb.shape[0]
    tm = _tile(T, PROJ_ROWS)
    tn = D_MODEL
    n_qk = 2 * RET_HEADS * RET_QK_DIM // tn
    n_vg = 2 * RET_HEADS * RET_V_DIM // tn
    x_spec = pl.BlockSpec((tm, D_MODEL), lambda j, i: (i, 0))
    tab_spec = pl.BlockSpec((tm, LANES), lambda j, i: (i, 0))
    qk = pl.pallas_call(
        _ret_qk_kernel,
        grid=(n_qk, T // tm),
        in_specs=[x_spec, pl.BlockSpec((D_MODEL, tn), lambda j, i: (0, j)), tab_spec, tab_spec],
        out_specs=pl.BlockSpec((tm, tn), lambda j, i: (i, j)),
        out_shape=jax.ShapeDtypeStruct((T, n_qk * tn), BF16),
        compiler_params=_params("arbitrary", "arbitrary"),
        name="ret_qk_proj",
    )(hb, w_in_bf, cos_r, sin_r)
    vg = pl.pallas_call(
        _matmul_kernel,
        grid=(n_vg, T // tm),
        in_specs=[x_spec, pl.BlockSpec((D_MODEL, tn), lambda j, i: (0, j + n_qk))],
        out_specs=pl.BlockSpec((tm, tn), lambda j, i: (i, j)),
        out_shape=jax.ShapeDtypeStruct((T, n_vg * tn), BF16),
        compiler_params=_params("arbitrary", "arbitrary"),
        name="ret_vg_proj",
    )(hb, w_in_bf)
    return qk, vg


def _retention_kernel(q_ref, k_ref, v_ref, g_ref, dm_ref, dq_ref, dk_ref, dc_ref, gng_ref, gnb_ref,
                      z_ref, state_ref, *, chunk, n_chunks):
    @pl.when(pl.program_id(2) == 0)
    def _():
        state_ref[...] = jnp.zeros_like(state_ref)

    dm = dm_ref[0]
    dq = dq_ref[0]
    dk = dk_ref[0]
    dc = dc_ref[0]
    gng = gng_ref[...]
    gnb = gnb_ref[...]
    for c in range(n_chunks):
        rows = slice(c * chunk, (c + 1) * chunk)
        q = q_ref[rows, :]
        k = k_ref[rows, :]
        v = v_ref[rows, :]
        st = state_ref[...]
        inner = lax.dot_general(q, k, NT_DIMS, preferred_element_type=F32) * dm
        y = jnp.dot(inner.astype(BF16), v, preferred_element_type=F32)
        y = y + jnp.dot(q, st.astype(BF16), preferred_element_type=F32) * dq
        kd = (k.astype(F32) * dk).astype(BF16)
        state_ref[...] = st * dc + lax.dot_general(kd, v, TN_DIMS, preferred_element_type=F32)
        mu = jnp.mean(y, axis=-1, keepdims=True)
        d = y - mu
        var = jnp.mean(d * d, axis=-1, keepdims=True)
        yn = d * lax.rsqrt(var + GN_EPS) * gng + gnb
        g = g_ref[rows, :].astype(F32)
        z_ref[rows, :] = (g * jax.nn.sigmoid(g) * yn).astype(z_ref.dtype)


def retention(qk, vg, gn_g, gn_b, B, S):
    T = B * S
    H, dk, dv = RET_HEADS, RET_QK_DIM, RET_V_DIM
    C = _tile(S, RET_CHUNK)
    L = _tile(S, RET_BLOCK)
    nl = S // L
    log_g = jnp.log(1.0 - 2.0 ** (-5.0 - jnp.arange(H, dtype=F32)))
    idx = jnp.arange(C, dtype=F32)
    diff = idx[:, None] - idx[None, :]
    dm = jnp.where(diff >= 0, jnp.exp(log_g[:, None, None] * jnp.maximum(diff, 0.0)), 0.0)
    dq = jnp.exp(log_g[:, None] * (idx + 1.0))[:, :, None]
    dkk = jnp.exp(log_g[:, None] * (C - 1.0 - idx))[:, :, None]
    dc = jnp.broadcast_to(jnp.exp(log_g * C)[:, None, None], (H, 1, dv))
    kern = functools.partial(_retention_kernel, chunk=C, n_chunks=L // C)
    return pl.pallas_call(
        kern,
        grid=(B, H, nl),
        in_specs=[
            pl.BlockSpec((L, dk), lambda b, h, n: (b * nl + n, h)),
            pl.BlockSpec((L, dk), lambda b, h, n: (b * nl + n, H + h)),
            pl.BlockSpec((L, dv), lambda b, h, n: (b * nl + n, h)),
            pl.BlockSpec((L, dv), lambda b, h, n: (b * nl + n, H + h)),
            pl.BlockSpec((1, C, C), lambda b, h, n: (h, 0, 0)),
            pl.BlockSpec((1, C, 1), lambda b, h, n: (h, 0, 0)),
            pl.BlockSpec((1, C, 1), lambda b, h, n: (h, 0, 0)),
            pl.BlockSpec((1, 1, dv), lambda b, h, n: (h, 0, 0)),
            pl.BlockSpec((1, dv), lambda b, h, n: (0, h)),
            pl.BlockSpec((1, dv), lambda b, h, n: (0, h)),
        ],
        out_specs=pl.BlockSpec((L, dv), lambda b, h, n: (b * nl + n, h)),
        out_shape=jax.ShapeDtypeStruct((T, H * dv), BF16),
        scratch_shapes=[pltpu.VMEM((dk, dv), F32)],
        compiler_params=_params("arbitrary", "arbitrary", "arbitrary"),
        name="retention",
    )(qk, qk, vg, vg, dm, dq, dkk, dc, gn_g.reshape(1, H * dv), gn_b.reshape(1, H * dv))


def _out_ln_kernel(z_ref, w_ref, h_ref, g_ref, b_ref, o_ref):
    mix = jnp.dot(z_ref[...], w_ref[...], preferred_element_type=F32)
    u = DN_ALPHA * h_ref[...] + mix
    o_ref[...] = _layer_norm_rows(u, g_ref[...], b_ref[...])


def out_proj_ln(z, w_bf, h, ln_g, ln_b):
    T, K = z.shape
    tm = _tile(T, TOKEN_ROWS)
    row_spec = pl.BlockSpec((tm, D_MODEL), lambda i: (i, 0))
    vec_spec = pl.BlockSpec((1, D_MODEL), lambda i: (0, 0))
    return pl.pallas_call(
        _out_ln_kernel,
        grid=(T // tm,),
        in_specs=[pl.BlockSpec((tm, K), lambda i: (i, 0)), pl.BlockSpec((K, D_MODEL), lambda i: (0, 0)),
                  row_spec, vec_spec, vec_spec],
        out_specs=row_spec,
        out_shape=jax.ShapeDtypeStruct((T, D_MODEL), F32),
        compiler_params=_params("arbitrary"),
        name="out_proj_ln",
    )(z, w_bf, h, ln_g.reshape(1, D_MODEL), ln_b.reshape(1, D_MODEL))


ROUTER_FIELDS = 16


def _router_kernel(h_ref, w_ref, b_ref, out_ref, cnt_ref, carry_ref, *, tm):
    @pl.when(pl.program_id(0) == 0)
    def _():
        carry_ref[...] = jnp.zeros_like(carry_ref)

    x = h_ref[...]
    w = w_ref[...]
    x_hi = x.astype(BF16)
    x_lo = (x - x_hi.astype(F32)).astype(BF16)
    w_hi = w.astype(BF16)
    w_lo = (w - w_hi.astype(F32)).astype(BF16)
    logits = (jnp.dot(x_hi, w_hi, preferred_element_type=F32)
              + (jnp.dot(x_hi, w_lo, preferred_element_type=F32)
                 + jnp.dot(x_lo, w_hi, preferred_element_type=F32))) + b_ref[...]
    work = logits.T
    e_iota = lax.broadcasted_iota(jnp.int32, (N_EXPERTS, tm), 0).astype(F32)
    onehot = jnp.zeros((N_EXPERTS, tm), F32)
    top_vals, hits, rows = [], [], []
    for _ in range(TOP_K):
        m = jnp.max(work, axis=0, keepdims=True)
        idx = jnp.min(jnp.where(work == m, e_iota, float(N_EXPERTS)), axis=0, keepdims=True)
        hit = e_iota == idx
        top_vals.append(m)
        hits.append(hit)
        rows.append(idx)
        work = jnp.where(hit, -jnp.inf, work)
        onehot = jnp.where(hit, 1.0, onehot)
    exps = [jnp.exp(v - top_vals[0]) for v in top_vals]
    denom = exps[0] + exps[1] + exps[2] + exps[3]
    rows += [e / denom for e in exps]

    r = lax.broadcasted_iota(jnp.int32, (tm, tm), 0)
    c = lax.broadcasted_iota(jnp.int32, (tm, tm), 1)
    earlier = jnp.where(r < c, 1.0, 0.0).astype(BF16)
    before = jnp.dot(onehot.astype(BF16), earlier, preferred_element_type=F32) + carry_ref[:, 0:1]
    rows += [jnp.sum(jnp.where(hit, before, 0.0), axis=0, keepdims=True) for hit in hits]

    field = lax.broadcasted_iota(jnp.int32, (ROUTER_FIELDS, tm), 0)
    out = jnp.zeros((ROUTER_FIELDS, tm), F32)
    for f, row in enumerate(rows):
        out = jnp.where(field == f, row, out)
    out_ref[...] = out

    tile_cnt = jnp.sum(onehot, axis=1, keepdims=True)
    cnt_ref[...] = jnp.broadcast_to(tile_cnt, cnt_ref.shape)
    carry_ref[...] = carry_ref[...] + tile_cnt


def router(h, w_router, b_router):
    T = h.shape[0]
    tm = _tile(T, TOKEN_ROWS)
    n = T // tm
    out, cnt = pl.pallas_call(
        functools.partial(_router_kernel, tm=tm),
        grid=(n,),
        in_specs=[pl.BlockSpec((tm, D_MODEL), lambda i: (i, 0)),
                  pl.BlockSpec((D_MODEL, N_EXPERTS), lambda i: (0, 0)),
                  pl.BlockSpec((1, N_EXPERTS), lambda i: (0, 0))],
        out_specs=[pl.BlockSpec((ROUTER_FIELDS, tm), lambda i: (i, 0)),
                   pl.BlockSpec((N_EXPERTS, LANES), lambda i: (i, 0))],
        out_shape=[jax.ShapeDtypeStruct((n * ROUTER_FIELDS, tm), F32),
                   jax.ShapeDtypeStruct((n * N_EXPERTS, LANES), F32)],
        scratch_shapes=[pltpu.VMEM((N_EXPERTS, LANES), F32)],
        compiler_params=_params("arbitrary"),
        name="router",
    )(h, w_router, b_router.reshape(1, N_EXPERTS))
    out = out.reshape(n, ROUTER_FIELDS, tm)
    idx = out[:, 0:TOP_K].astype(jnp.int32)
    gates = out[:, TOP_K:2 * TOP_K]
    rank = out[:, 2 * TOP_K:3 * TOP_K].astype(jnp.int32)
    tile_cnt = cnt.reshape(n, N_EXPERTS, LANES)[:, :, 0].astype(jnp.int32)
    return idx, gates, rank, tile_cnt


def _rows_to_tiles(dst_ref, x, n):
    for c in range(ROW_CHUNKS):
        dst_ref[pl.ds(c, n, stride=ROW_CHUNKS), :] = x[:, c * LANES:(c + 1) * LANES]


def _tiles_to_rows(src_ref, n):
    return jnp.concatenate([src_ref[pl.ds(c, n, stride=ROW_CHUNKS), :] for c in range(ROW_CHUNKS)], axis=1)


def _dispatch_kernel(zfill_ref, rsrc_ref, rdst_ref, rlen_ref, q_ref, h_ref, xs_ref,
                     rows, stage, sems, zsem, *, tt, bm):
    i = pl.program_id(0)
    last = pl.num_programs(0) - 1
    slot = i % 2

    def start_runs(step):
        buf = stage.at[step % 2]
        sem = sems.at[step % 2]

        def per_expert(e, carry):
            n = rlen_ref[step * N_EXPERTS + e]
            src = rsrc_ref[step * N_EXPERTS + e]
            dst = rdst_ref[step * N_EXPERTS + e]
            done = jnp.int32(0)
            piece = tt
            while piece >= 1:
                take = (n & piece) != 0
                size = piece * ROW_CHUNKS

                @pl.when(take)
                def _(done=done, size=size):
                    s = pl.multiple_of(src + done, ROW_CHUNKS)
                    d = pl.multiple_of(dst + done, ROW_CHUNKS)
                    pltpu.make_async_copy(buf.at[pl.ds(s, size)], xs_ref.at[pl.ds(d, size)], sem).start()

                done = done + jnp.where(take, size, 0)
                piece //= 2
            return carry
        lax.fori_loop(0, N_EXPERTS, per_expert, 0)

    def wait_runs(step):
        buf = stage.at[step % 2]
        pltpu.make_async_copy(buf, xs_ref.at[pl.ds(0, tt * TOP_K * ROW_CHUNKS)], sems.at[step % 2]).wait()

    @pl.when(i == 0)
    def _():
        zeros = stage.at[0, pl.ds(0, bm * ROW_CHUNKS)]
        zeros[...] = jnp.zeros_like(zeros)

        def zero_copy(e):
            start = pl.multiple_of(zfill_ref[e], bm * ROW_CHUNKS)
            return pltpu.make_async_copy(zeros, xs_ref.at[pl.ds(start, bm * ROW_CHUNKS)], zsem)

        for e in range(N_EXPERTS):
            @pl.when(zfill_ref[e] >= 0)
            def _():
                zero_copy(e).start()
        for e in range(N_EXPERTS):
            @pl.when(zfill_ref[e] >= 0)
            def _():
                zero_copy(e).wait()

    @pl.when(i >= 2)
    def _():
        wait_runs(i - 2)

    _rows_to_tiles(rows, h_ref[...], tt)

    def sort_rows(t, carry):
        row = rows[pl.ds(pl.multiple_of(t * ROW_CHUNKS, ROW_CHUNKS), ROW_CHUNKS), :]
        for k in range(TOP_K):
            q = pl.multiple_of(q_ref[0, 0, k * tt + t], ROW_CHUNKS)
            stage[slot, pl.ds(q, ROW_CHUNKS), :] = row
        return carry

    lax.fori_loop(0, tt, sort_rows, 0, unroll=8)
    start_runs(i)

    @pl.when(jnp.logical_and(i == last, i >= 1))
    def _():
        wait_runs(i - 1)

    @pl.when(i == last)
    def _():
        wait_runs(i)


def dispatch_rows(h, pos3, run_src8, run_dst8, run_len, zfill8, n_slots, bm):
    T = h.shape[0]
    tt = _tile(T, TOKEN_ROWS)
    assert pos3.shape == (T // tt, 1, tt * TOP_K)
    assert tt * TOP_K >= bm
    return pl.pallas_call(
        functools.partial(_dispatch_kernel, tt=tt, bm=bm),
        grid_spec=pltpu.PrefetchScalarGridSpec(
            num_scalar_prefetch=4,
            grid=(T // tt,),
            in_specs=[pl.BlockSpec((1, 1, tt * TOP_K), lambda i, *_: (i, 0, 0), memory_space=pltpu.SMEM),
                      pl.BlockSpec((tt, D_MODEL), lambda i, *_: (i, 0))],
            out_specs=pl.BlockSpec(memory_space=pl.ANY),
            scratch_shapes=[pltpu.VMEM((tt * ROW_CHUNKS, LANES), F32),
                            pltpu.VMEM((2, tt * TOP_K * ROW_CHUNKS, LANES), F32),
                            pltpu.SemaphoreType.DMA((2,)), pltpu.SemaphoreType.DMA],
        ),
        out_shape=jax.ShapeDtypeStruct((n_slots * ROW_CHUNKS, LANES), F32),
        compiler_params=_params("arbitrary"),
        name="moe_dispatch",
    )(zfill8, run_src8, run_dst8, run_len, pos3, h)


def _moe_mlp_kernel(be_ref, na_ref, x_ref, wgu_ref, bgu_ref, wd_ref, bd_ref, y_ref, wgu_bf, wd_bf, *, bm):
    i = pl.program_id(0)
    prev = be_ref[jnp.maximum(i - 1, 0)]
    active = i < na_ref[0]

    @pl.when(jnp.logical_and(active, jnp.logical_or(i == 0, be_ref[i] != prev)))
    def _():
        wgu_bf[...] = wgu_ref[...].astype(BF16)
        wd_bf[...] = wd_ref[...].astype(BF16)

    @pl.when(active)
    def _():
        x = _tiles_to_rows(x_ref, bm).astype(BF16)
        hgu = jnp.dot(x, wgu_bf[...], preferred_element_type=F32) + bgu_ref[...]
        gate = jnp.minimum(hgu[:, :D_FF], SWIGLU_LIMIT)
        up = jnp.clip(hgu[:, D_FF:], -SWIGLU_LIMIT, SWIGLU_LIMIT)
        act = gate * jax.nn.sigmoid(SWIGLU_ALPHA * gate) * (up + 1.0)
        y = jnp.dot(act.astype(BF16), wd_bf[...], preferred_element_type=F32) + bd_ref[...]
        _rows_to_tiles(y_ref, y, bm)


def moe_mlp(xs, block_expert, n_active, layer, w_gu, b_gu, w_d, b_d, bm):
    n_blocks = xs.shape[0] // (bm * ROW_CHUNKS)

    def row_map(i, be, na):
        return (jnp.minimum(i, na[0] - 1), 0)

    def w_map(i, be, na):
        return (layer, be[jnp.minimum(i, na[0] - 1)], 0, 0)

    def b_map(i, be, na):
        return (layer * N_EXPERTS + be[jnp.minimum(i, na[0] - 1)], 0, 0)

    return pl.pallas_call(
        functools.partial(_moe_mlp_kernel, bm=bm),
        grid_spec=pltpu.PrefetchScalarGridSpec(
            num_scalar_prefetch=2,
            grid=(n_blocks,),
            in_specs=[pl.BlockSpec((bm * ROW_CHUNKS, LANES), row_map),
                      pl.BlockSpec((None, None, D_MODEL, 2 * D_FF), w_map),
                      pl.BlockSpec((None, 1, 2 * D_FF), b_map),
                      pl.BlockSpec((None, None, D_FF, D_MODEL), w_map),
                      pl.BlockSpec((None, 1, D_MODEL), b_map)],
            out_specs=pl.BlockSpec((bm * ROW_CHUNKS, LANES), row_map),
            scratch_shapes=[pltpu.VMEM((D_MODEL, 2 * D_FF), BF16), pltpu.VMEM((D_FF, D_MODEL), BF16)],
        ),
        out_shape=jax.ShapeDtypeStruct(xs.shape, F32),
        compiler_params=_params("arbitrary"),
        name="moe_mlp",
    )(block_expert, n_active, xs, w_gu, b_gu.reshape(DEPTH * N_EXPERTS, 1, 2 * D_FF), w_d,
      b_d.reshape(DEPTH * N_EXPERTS, 1, D_MODEL))


def _ffn_ln_ple_kernel(rstage_ref, rslot_ref, rlen_ref, pos_ref, gate_ref, h_ref, p_ref, g_ref, b_ref,
                       wg_ref, wp_ref, ys_ref, o_ref, ob_ref, stage, ffn_tiles, sems, *, tm):
    i = pl.program_id(0)
    slot = i % 2

    def start_runs(step):
        buf = stage.at[step % 2]
        sem = sems.at[step % 2]

        def per_expert(e, carry):
            n = rlen_ref[step * N_EXPERTS + e]
            src = rslot_ref[step * N_EXPERTS + e]
            dst = rstage_ref[step * N_EXPERTS + e]
            done = jnp.int32(0)
            piece = tm
            while piece >= 1:
                take = (n & piece) != 0
                size = piece * ROW_CHUNKS

                @pl.when(take)
                def _(done=done, size=size):
                    s = pl.multiple_of(src + done, ROW_CHUNKS)
                    d = pl.multiple_of(dst + done, ROW_CHUNKS)
                    pltpu.make_async_copy(ys_ref.at[pl.ds(s, size)], buf.at[pl.ds(d, size)], sem).start()

                done = done + jnp.where(take, size, 0)
                piece //= 2
            return carry
        lax.fori_loop(0, N_EXPERTS, per_expert, 0)

    @pl.when(i == 0)
    def _():
        start_runs(0)

    @pl.when(i + 1 < pl.num_programs(0))
    def _():
        start_runs(i + 1)

    pltpu.make_async_copy(ys_ref.at[pl.ds(0, tm * TOP_K * ROW_CHUNKS)], stage.at[slot], sems.at[slot]).wait()

    def combine(t, carry):
        acc = None
        for k in range(TOP_K):
            q = pl.multiple_of(pos_ref[0, 0, k * tm + t], ROW_CHUNKS)
            term = stage[slot, pl.ds(q, ROW_CHUNKS), :] * gate_ref[0, 0, k * tm + t]
            acc = term if acc is None else acc + term
        ffn_tiles[pl.ds(pl.multiple_of(t * ROW_CHUNKS, ROW_CHUNKS), ROW_CHUNKS), :] = acc
        return carry

    lax.fori_loop(0, tm, combine, 0, unroll=8)

    sub = tm // FFN_ROW_CHUNKS
    for c in range(FFN_ROW_CHUNKS):
        rows = slice(c * sub, (c + 1) * sub)
        ffn = _tiles_to_rows(ffn_tiles.at[pl.ds(c * sub * ROW_CHUNKS, sub * ROW_CHUNKS)], sub)
        h2 = _layer_norm_rows(DN_ALPHA * h_ref[rows, :] + ffn, g_ref[...], b_ref[...])
        gate = jax.nn.sigmoid(jnp.dot(h2.astype(BF16), wg_ref[...], preferred_element_type=F32))
        proj = jnp.dot(p_ref[rows, :].astype(BF16), wp_ref[...], preferred_element_type=F32)
        out = h2 + gate * proj
        o_ref[rows, :] = out
        ob_ref[rows, :] = out.astype(BF16)


def ffn_ln_ple(ys, pos3, gates3, run_stage8, run_slot8, run_len, h, p_all, layer, ln_g, ln_b, wg_bf, wp_bf):
    T = h.shape[0]
    tm = _tile(T, TOKEN_ROWS)
    n = T // tm
    assert pos3.shape == gates3.shape == (n, 1, tm * TOP_K)
    row_spec = pl.BlockSpec((tm, D_MODEL), lambda i, *_: (i, 0))
    vec_spec = pl.BlockSpec((1, D_MODEL), lambda i, *_: (0, 0))
    smem_spec = pl.BlockSpec((1, 1, tm * TOP_K), lambda i, *_: (i, 0, 0), memory_space=pltpu.SMEM)
    return pl.pallas_call(
        functools.partial(_ffn_ln_ple_kernel, tm=tm),
        grid_spec=pltpu.PrefetchScalarGridSpec(
            num_scalar_prefetch=3,
            grid=(n,),
            in_specs=[smem_spec, smem_spec,
                      row_spec,
                      pl.BlockSpec((tm, PLE_DIM), lambda i, *_: (layer * n + i, 0)),
                      vec_spec, vec_spec,
                      pl.BlockSpec((D_MODEL, D_MODEL), lambda i, *_: (0, 0)),
                      pl.BlockSpec((PLE_DIM, D_MODEL), lambda i, *_: (0, 0)),
                      pl.BlockSpec(memory_space=pl.ANY)],
            out_specs=[row_spec, row_spec],
            scratch_shapes=[pltpu.VMEM((2, tm * TOP_K * ROW_CHUNKS, LANES), F32),
                            pltpu.VMEM((tm * ROW_CHUNKS, LANES), F32), pltpu.SemaphoreType.DMA((2,))],
        ),
        out_shape=[jax.ShapeDtypeStruct((T, D_MODEL), F32), jax.ShapeDtypeStruct((T, D_MODEL), BF16)],
        compiler_params=_params("arbitrary"),
        name="ffn_ln_ple",
    )(run_stage8, run_slot8, run_len, pos3, gates3, h, p_all, ln_g.reshape(1, D_MODEL),
      ln_b.reshape(1, D_MODEL), wg_bf, wp_bf, ys)


def moe_layer(h1, layer, w_router, b_router, w_gu, b_gu, w_d, b_d):
    T = h1.shape[0]
    bm = MOE_BLOCK
    idx, gates, rank, tile_cnt = router(h1, w_router, b_router)
    nt, _, tt = idx.shape
    counts = jnp.sum(tile_cnt, axis=0)
    padded = (counts + bm - 1) // bm * bm
    pend = jnp.cumsum(padded)
    pstart = pend - padded
    n_slots = T * TOP_K + N_EXPERTS * bm
    n_blocks = n_slots // bm
    block_start = jnp.arange(n_blocks, dtype=jnp.int32) * bm
    block_expert = jnp.minimum(jnp.sum(block_start[:, None] >= pend[None, :], axis=1),
                               N_EXPERTS - 1).astype(jnp.int32)
    n_active = (pend[-1:] // bm).astype(jnp.int32)
    zfill8 = jnp.where(padded > 0, (pend - bm) * ROW_CHUNKS, -1).astype(jnp.int32)
    tile_base = jnp.cumsum(tile_cnt, axis=0) - tile_cnt
    stage_off = jnp.cumsum(tile_cnt, axis=1) - tile_cnt
    onehot = idx[:, :, None, :] == jnp.arange(N_EXPERTS, dtype=jnp.int32)[:, None]

    def per_assignment(table):
        return jnp.sum(jnp.where(onehot, table[:, None, :, None], 0), axis=2)

    stage_pos8 = ((per_assignment(stage_off - tile_base) + rank) * ROW_CHUNKS).astype(jnp.int32)
    run_src8 = (stage_off * ROW_CHUNKS).reshape(-1).astype(jnp.int32)
    run_dst8 = ((pstart[None, :] + tile_base) * ROW_CHUNKS).reshape(-1).astype(jnp.int32)
    pos3 = stage_pos8.reshape(nt, 1, TOP_K * tt)
    run_len = tile_cnt.reshape(-1)
    xs = dispatch_rows(h1, pos3, run_src8, run_dst8, run_len, zfill8, n_slots, bm)
    ys = moe_mlp(xs, block_expert, n_active, layer, w_gu, b_gu, w_d, b_d, bm)
    return ys, pos3, gates.reshape(nt, 1, TOP_K * tt), run_src8, run_dst8, run_len


def _rms_rows(x, g):
    return x * lax.rsqrt(jnp.mean(x * x, axis=-1, keepdims=True) + RMS_EPS) * g


def _mla_kv_kernel(x_ref, wa_ref, g_ref, wkn_ref, wv_ref, cm_ref, sm_ref, k_ref, v_ref):
    a = jnp.dot(x_ref[...], wa_ref[...], preferred_element_type=F32)
    c_kv = _rms_rows(a[:, :KV_LORA], g_ref[...]).astype(BF16)
    kr = a[:, KV_LORA:KV_LORA + LANES] * cm_ref[...] + a[:, KV_LORA + LANES:] * sm_ref[...]
    kr = kr.astype(BF16)
    kn = jnp.dot(c_kv, wkn_ref[...], preferred_element_type=F32)
    v_ref[...] = jnp.dot(c_kv, wv_ref[...], preferred_element_type=F32).astype(BF16)
    for h in range(MLA_HEADS):
        k_ref[:, 2 * h * LANES:(2 * h + 1) * LANES] = kn[:, h * QK_NOPE:(h + 1) * QK_NOPE].astype(BF16)
        k_ref[:, (2 * h + 1) * LANES:(2 * h + 2) * LANES] = kr


def _pad_rope_cols(w, swap):
    K = w.shape[0]
    w = w.reshape(K, -1, QK_ROPE)
    if swap:
        w = jnp.concatenate([w[..., QK_ROPE // 2:], w[..., :QK_ROPE // 2]], axis=-1)
    w = jnp.concatenate([w, jnp.zeros_like(w)], axis=-1)
    return w.reshape(K, -1)


def mla_shared_kv(hb, w_kv_a, kv_norm_g, w_kv_b, cm, sm):
    T = hb.shape[0]
    tm = _tile(T, TOKEN_ROWS)
    w_r = w_kv_a[:, KV_LORA:]
    wa = jnp.concatenate([w_kv_a[:, :KV_LORA], _pad_rope_cols(w_r, False), _pad_rope_cols(w_r, True)],
                         axis=1).astype(BF16)
    wb = w_kv_b.reshape(KV_LORA, MLA_HEADS, QK_NOPE + V_HEAD)
    wkn = wb[:, :, :QK_NOPE].reshape(KV_LORA, MLA_HEADS * QK_NOPE).astype(BF16)
    wv = wb[:, :, QK_NOPE:].reshape(KV_LORA, MLA_HEADS * V_HEAD).astype(BF16)
    full = lambda shape: pl.BlockSpec(shape, lambda i: (0, 0))
    return pl.pallas_call(
        _mla_kv_kernel,
        grid=(T // tm,),
        in_specs=[pl.BlockSpec((tm, D_MODEL), lambda i: (i, 0)), full(wa.shape), full((1, KV_LORA)),
                  full(wkn.shape), full(wv.shape),
                  pl.BlockSpec((tm, LANES), lambda i: (i, 0)), pl.BlockSpec((tm, LANES), lambda i: (i, 0))],
        out_specs=[pl.BlockSpec((tm, MLA_HEADS * 2 * LANES), lambda i: (i, 0)),
                   pl.BlockSpec((tm, MLA_HEADS * V_HEAD), lambda i: (i, 0))],
        out_shape=[jax.ShapeDtypeStruct((T, MLA_HEADS * 2 * LANES), BF16),
                   jax.ShapeDtypeStruct((T, MLA_HEADS * V_HEAD), BF16)],
        compiler_params=_params("arbitrary"),
        name="mla_kv",
    )(hb, wa, kv_norm_g.reshape(1, KV_LORA), wkn, wv, cm, sm)


def _mla_q_kernel(x_ref, wa_ref, g_ref, wb_ref, cm_ref, sm_ref, q_ref):
    scale = (QK_NOPE + QK_ROPE) ** -0.5 * LOG2_E
    qa = jnp.dot(x_ref[...], wa_ref[...], preferred_element_type=F32)
    qc = _rms_rows(qa, g_ref[...]).astype(BF16)
    q = jnp.dot(qc, wb_ref[...], preferred_element_type=F32)
    cm = cm_ref[...] * scale
    sm = sm_ref[...] * scale
    n = MLA_HEADS * LANES
    for h in range(MLA_HEADS):
        lo = h * LANES
        q_ref[:, 2 * lo:2 * lo + LANES] = (q[:, lo:lo + LANES] * scale).astype(BF16)
        rope = q[:, n + lo:n + lo + LANES] * cm + q[:, 2 * n + lo:2 * n + lo + LANES] * sm
        q_ref[:, 2 * lo + LANES:2 * lo + 2 * LANES] = rope.astype(BF16)


def mla_q(hb, w_q_a, q_norm_g, w_q_b, cm, sm):
    T = hb.shape[0]
    tm = _tile(T, TOKEN_ROWS)
    wq = w_q_b.reshape(Q_LORA, MLA_HEADS, QK_NOPE + QK_ROPE)
    w_n = wq[:, :, :QK_NOPE].reshape(Q_LORA, MLA_HEADS * QK_NOPE)
    w_r = wq[:, :, QK_NOPE:].reshape(Q_LORA, MLA_HEADS * QK_ROPE)
    wb = jnp.concatenate([w_n, _pad_rope_cols(w_r, False), _pad_rope_cols(w_r, True)], axis=1).astype(BF16)
    full = lambda shape: pl.BlockSpec(shape, lambda i: (0, 0))
    return pl.pallas_call(
        _mla_q_kernel,
        grid=(T // tm,),
        in_specs=[pl.BlockSpec((tm, D_MODEL), lambda i: (i, 0)), full((D_MODEL, Q_LORA)), full((1, Q_LORA)),
                  full(wb.shape),
                  pl.BlockSpec((tm, LANES), lambda i: (i, 0)), pl.BlockSpec((tm, LANES), lambda i: (i, 0))],
        out_specs=pl.BlockSpec((tm, MLA_HEADS * 2 * LANES), lambda i: (i, 0)),
        out_shape=jax.ShapeDtypeStruct((T, MLA_HEADS * 2 * LANES), BF16),
        compiler_params=_params("arbitrary"),
        name="mla_q",
    )(hb, w_q_a.astype(BF16), q_norm_g.reshape(1, Q_LORA), wb, cm, sm)


def _flash_kernel(q_ref, k_ref, v_ref, o_ref, m_ref, l_ref, acc_ref, *, bq, bk):
    i = pl.program_id(2)
    q = q_ref[...]
    m_ref[...] = jnp.full_like(m_ref, -jnp.inf)
    l_ref[...] = jnp.zeros_like(l_ref)
    acc_ref[...] = jnp.zeros_like(acc_ref)

    def step(j, rows=slice(0, bq), triangular=False):
        n = rows.stop - rows.start
        start = pl.multiple_of(j * bk, bk)
        kj = k_ref[pl.ds(start, bk), :]
        vj = v_ref[pl.ds(start, bk), :]
        s = lax.dot_general(q[rows], kj, NT_DIMS, preferred_element_type=F32)
        if triangular:
            r = lax.broadcasted_iota(jnp.int32, (n, bk), 0)
            c = lax.broadcasted_iota(jnp.int32, (n, bk), 1)
            s = jnp.where(c <= r, s, -jnp.inf)
        m_prev = m_ref[rows, :]
        m_new = jnp.maximum(m_prev, jnp.max(s, axis=-1, keepdims=True))
        alpha = jnp.exp2(m_prev - m_new)
        p = jnp.exp2(s - pltpu.repeat(m_new, bk // LANES, axis=1))
        l_ref[rows, :] = alpha * l_ref[rows, :] + jnp.sum(p, axis=-1, keepdims=True)
        acc_ref[rows, :] = alpha * acc_ref[rows, :] + jnp.dot(p.astype(BF16), vj, preferred_element_type=F32)
        m_ref[rows, :] = m_new

    def pair(j, carry):
        step(2 * j)
        step(2 * j + 1)
        return carry

    lax.fori_loop(0, i, pair, 0)
    step(2 * i, triangular=True)
    step(2 * i + 1, slice(bk, bq), triangular=True)
    o_ref[...] = (acc_ref[...] / l_ref[...]).astype(o_ref.dtype)


def flash_attention(q, k, v, B, S):
    T = B * S
    bq = _tile(S, ATTN_BLOCK_Q)
    bk = bq // 2
    nq = S // bq
    return pl.pallas_call(
        functools.partial(_flash_kernel, bq=bq, bk=bk),
        grid=(B, MLA_HEADS, nq),
        in_specs=[pl.BlockSpec((bq, 2 * LANES), lambda b, h, i: (b * nq + i, h)),
                  pl.BlockSpec((S, 2 * LANES), lambda b, h, i: (b, h)),
                  pl.BlockSpec((S, V_HEAD), lambda b, h, i: (b, h))],
        out_specs=pl.BlockSpec((bq, V_HEAD), lambda b, h, i: (b * nq + i, h)),
        out_shape=jax.ShapeDtypeStruct((T, MLA_HEADS * V_HEAD), BF16),
        scratch_shapes=[pltpu.VMEM((bq, LANES), F32), pltpu.VMEM((bq, LANES), F32),
                        pltpu.VMEM((bq, V_HEAD), F32)],
        compiler_params=_params("arbitrary", "arbitrary", "arbitrary"),
        name="mla_flash",
    )(q, k, v)


def kernel(x, p, positions, ret_w_in, ret_gn_g, ret_gn_b, ret_w_out, mla_w_kv_a, mla_kv_norm_g, mla_w_kv_b,
           mla_w_q_a, mla_q_norm_g, mla_w_q_b, mla_w_o, ln_mix_g, ln_mix_b, ln_ffn_g, ln_ffn_b,
           moe_w_router, moe_b_router, moe_w_gate_up, moe_b_gate_up, moe_w_down, moe_b_down,
           ple_w_gate, ple_w_proj):
    B, S, D = x.shape
    T = B * S
    cos_r, sin_r, cos_m, sin_m = rope_tables(positions)
    h = x.reshape(T, D)
    hb = h.astype(BF16)
    kv = None
    for i in range(DEPTH):
        if i < N_A:
            qk, vg = ret_projections(hb, ret_w_in[i].astype(BF16), cos_r, sin_r)
            z = retention(qk, vg, ret_gn_g[i], ret_gn_b[i], B, S)
            w_out = ret_w_out[i].astype(BF16)
        else:
            if kv is None:
                kv = mla_shared_kv(hb, mla_w_kv_a, mla_kv_norm_g, mla_w_kv_b, cos_m, sin_m)
            j = i - N_A
            q = mla_q(hb, mla_w_q_a[j], mla_q_norm_g[j], mla_w_q_b[j], cos_m, sin_m)
            z = flash_attention(q, kv[0], kv[1], B, S)
            w_out = mla_w_o[j].astype(BF16)
        h1 = out_proj_ln(z, w_out, h, ln_mix_g[i], ln_mix_b[i])
        ys, *tables = moe_layer(h1, i, moe_w_router[i], moe_b_router[i], moe_w_gate_up, moe_b_gate_up,
                                moe_w_down, moe_b_down)
        h, hb = ffn_ln_ple(ys, *tables, h1, p.reshape(DEPTH * T, PLE_DIM), i, ln_ffn_g[i], ln_ffn_b[i],
                           ple_w_gate[i].astype(BF16), ple_w_proj[i].astype(BF16))
    return h.reshape(B, S, D)
```

```python
import functools

import jax
import jax.numpy as jnp
from jax import lax
from jax.experimental import pallas as pl
from jax.experimental.pallas import tpu as pltpu

D_MODEL = 1024
DEPTH = 4
N_A = DEPTH // 2
RET_HEADS = 4
RET_QK_DIM = D_MODEL // RET_HEADS
RET_V_DIM = 2 * RET_QK_DIM
GN_EPS = 1e-6
MLA_HEADS = 8
QK_NOPE = 128
QK_ROPE = 64
V_HEAD = 128
Q_LORA = 256
KV_LORA = 256
RMS_EPS = 1e-6
ROPE_THETA = 10000.0
N_EXPERTS = 32
TOP_K = 4
D_FF = D_MODEL
SWIGLU_LIMIT = 7.0
SWIGLU_ALPHA = 1.702
PLE_DIM = 256
DN_ALPHA = (2 * DEPTH) ** 0.25
LN_EPS = 1e-5
LOG2_E = 1.4426950408889634

LANES = 128
ROW_CHUNKS = D_MODEL // LANES
VMEM_LIMIT_BYTES = 56 * 1024 * 1024

TOKEN_ROWS = 512
PROJ_ROWS = 1024
RET_CHUNK = 256
RET_BLOCK = 1024
ATTN_BLOCK_Q = 1024
MOE_BLOCK = 512
FFN_ROW_CHUNKS = 2

F32 = jnp.float32
BF16 = jnp.bfloat16
NT_DIMS = (((1,), (1,)), ((), ()))
TN_DIMS = (((0,), (0,)), ((), ()))


def _tile(n, pref):
    t = min(n, pref)
    assert n % t == 0, (n, pref)
    return t


def _params(*sem):
    return pltpu.CompilerParams(dimension_semantics=sem, vmem_limit_bytes=VMEM_LIMIT_BYTES)


def _layer_norm_rows(u, g, b):
    mu = jnp.mean(u, axis=-1, keepdims=True)
    d = u - mu
    var = jnp.mean(d * d, axis=-1, keepdims=True)
    return d * lax.rsqrt(var + LN_EPS) * g + b


def _rope_table_kernel(pos_ref, inv_r_ref, inv_m_ref, sign_m_ref, cr_ref, sr_ref, cm_ref, sm_ref):
    pos = pos_ref[...].astype(F32)
    ang_r = pos * inv_r_ref[...]
    cr_ref[...] = jnp.cos(ang_r)
    sr_ref[...] = jnp.sin(ang_r)
    ang_m = pos * inv_m_ref[...]
    cm_ref[...] = jnp.cos(ang_m)
    sm_ref[...] = jnp.sin(ang_m) * sign_m_ref[...]


def rope_tables(positions):
    T = positions.size
    tm = _tile(T, PROJ_ROWS)
    pos = positions.reshape(T, 1)
    half_r = RET_QK_DIM // 2
    inv_r = ROPE_THETA ** (-jnp.arange(0, RET_QK_DIM, 2, dtype=F32) / RET_QK_DIM)
    inv_m = ROPE_THETA ** (-jnp.arange(0, QK_ROPE, 2, dtype=F32) / QK_ROPE)
    half_m = QK_ROPE // 2
    lane = jnp.arange(LANES)
    inv_m = inv_m[lane % half_m]
    sign_m = jnp.where((lane % QK_ROPE) < half_m, -1.0, 1.0).astype(F32)
    assert half_r == LANES
    row = lambda v: v.reshape(1, LANES)
    vec_spec = pl.BlockSpec((1, LANES), lambda i: (0, 0))
    tab_spec = pl.BlockSpec((tm, LANES), lambda i: (i, 0))
    out = jax.ShapeDtypeStruct((T, LANES), F32)
    return pl.pallas_call(
        _rope_table_kernel,
        grid=(T // tm,),
        in_specs=[pl.BlockSpec((tm, 1), lambda i: (i, 0)), vec_spec, vec_spec, vec_spec],
        out_specs=[tab_spec] * 4,
        out_shape=[out] * 4,
        compiler_params=_params("arbitrary"),
        name="rope_tables",
    )(pos, row(inv_r), row(inv_m), row(sign_m))


def _ret_qk_kernel(x_ref, w_ref, cos_ref, sin_ref, o_ref):
    acc = jnp.dot(x_ref[...], w_ref[...], preferred_element_type=F32)
    scale = jnp.where(pl.program_id(0) == 1, RET_QK_DIM ** -0.5, 1.0).astype(F32)
    c = cos_ref[...] * scale
    s = sin_ref[...] * scale
    half = RET_QK_DIM // 2
    for h in range(RET_HEADS):
        lo = h * RET_QK_DIM
        x1 = acc[:, lo:lo + half]
        x2 = acc[:, lo + half:lo + 2 * half]
        o_ref[:, lo:lo + half] = (x1 * c - x2 * s).astype(o_ref.dtype)
        o_ref[:, lo + half:lo + 2 * half] = (x1 * s + x2 * c).astype(o_ref.dtype)


def _matmul_kernel(x_ref, w_ref, o_ref):
    o_ref[...] = jnp.dot(x_ref[...], w_ref[...], preferred_element_type=F32).astype(o_ref.dtype)


def ret_projections(hb, w_in_bf, cos_r, sin_r):
    T = hb.shape[0]
    tm = _tile(T, PROJ_ROWS)
    tn = D_MODEL
    n_qk = 2 * RET_HEADS * RET_QK_DIM // tn
    n_vg = 2 * RET_HEADS * RET_V_DIM // tn
    x_spec = pl.BlockSpec((tm, D_MODEL), lambda j, i: (i, 0))
    tab_spec = pl.BlockSpec((tm, LANES), lambda j, i: (i, 0))
    qk = pl.pallas_call(
        _ret_qk_kernel,
        grid=(n_qk, T // tm),
        in_specs=[x_spec, pl.BlockSpec((D_MODEL, tn), lambda j, i: (0, j)), tab_spec, tab_spec],
        out_specs=pl.BlockSpec((tm, tn), lambda j, i: (i, j)),
        out_shape=jax.ShapeDtypeStruct((T, n_qk * tn), BF16),
        compiler_params=_params("arbitrary", "arbitrary"),
        name="ret_qk_proj",
    )(hb, w_in_bf, cos_r, sin_r)
    vg = pl.pallas_call(
        _matmul_kernel,
        grid=(n_vg, T // tm),
        in_specs=[x_spec, pl.BlockSpec((D_MODEL, tn), lambda j, i: (0, j + n_qk))],
        out_specs=pl.BlockSpec((tm, tn), lambda j, i: (i, j)),
        out_shape=jax.ShapeDtypeStruct((T, n_vg * tn), BF16),
        compiler_params=_params("arbitrary", "arbitrary"),
        name="ret_vg_proj",
    )(hb, w_in_bf)
    return qk, vg


def _retention_kernel(q_ref, k_ref, v_ref, g_ref, dm_ref, dq_ref, dk_ref, dc_ref, gng_ref, gnb_ref,
                      z_ref, state_ref, *, chunk, n_chunks):
    @pl.when(pl.program_id(2) == 0)
    def _():
        state_ref[...] = jnp.zeros_like(state_ref)

    dm = dm_ref[0]
    dq = dq_ref[0]
    dk = dk_ref[0]
    dc = dc_ref[0]
    gng = gng_ref[...]
    gnb = gnb_ref[...]
    for c in range(n_chunks):
        rows = slice(c * chunk, (c + 1) * chunk)
        q = q_ref[rows, :]
        k = k_ref[rows, :]
        v = v_ref[rows, :]
        st = state_ref[...]
        inner = lax.dot_general(q, k, NT_DIMS, preferred_element_type=F32) * dm
        y = jnp.dot(inner.astype(BF16), v, preferred_element_type=F32)
        y = y + jnp.dot(q, st.astype(BF16), preferred_element_type=F32) * dq
        kd = (k.astype(F32) * dk).astype(BF16)
        state_ref[...] = st * dc + lax.dot_general(kd, v, TN_DIMS, preferred_element_type=F32)
        mu = jnp.mean(y, axis=-1, keepdims=True)
        d = y - mu
        var = jnp.mean(d * d, axis=-1, keepdims=True)
        yn = d * lax.rsqrt(var + GN_EPS) * gng + gnb
        g = g_ref[rows, :].astype(F32)
        z_ref[rows, :] = (g * jax.nn.sigmoid(g) * yn).astype(z_ref.dtype)


def retention(qk, vg, gn_g, gn_b, B, S):
    T = B * S
    H, dk, dv = RET_HEADS, RET_QK_DIM, RET_V_DIM
    C = _tile(S, RET_CHUNK)
    L = _tile(S, RET_BLOCK)
    nl = S // L
    log_g = jnp.log(1.0 - 2.0 ** (-5.0 - jnp.arange(H, dtype=F32)))
    idx = jnp.arange(C, dtype=F32)
    diff = idx[:, None] - idx[None, :]
    dm = jnp.where(diff >= 0, jnp.exp(log_g[:, None, None] * jnp.maximum(diff, 0.0)), 0.0)
    dq = jnp.exp(log_g[:, None] * (idx + 1.0))[:, :, None]
    dkk = jnp.exp(log_g[:, None] * (C - 1.0 - idx))[:, :, None]
    dc = jnp.broadcast_to(jnp.exp(log_g * C)[:, None, None], (H, 1, dv))
    kern = functools.partial(_retention_kernel, chunk=C, n_chunks=L // C)
    return pl.pallas_call(
        kern,
        grid=(B, H, nl),
        in_specs=[
            pl.BlockSpec((L, dk), lambda b, h, n: (b * nl + n, h)),
            pl.BlockSpec((L, dk), lambda b, h, n: (b * nl + n, H + h)),
            pl.BlockSpec((L, dv), lambda b, h, n: (b * nl + n, h)),
            pl.BlockSpec((L, dv), lambda b, h, n: (b * nl + n, H + h)),
            pl.BlockSpec((1, C, C), lambda b, h, n: (h, 0, 0)),
            pl.BlockSpec((1, C, 1), lambda b, h, n: (h, 0, 0)),
            pl.BlockSpec((1, C, 1), lambda b, h, n: (h, 0, 0)),
            pl.BlockSpec((1, 1, dv), lambda b, h, n: (h, 0, 0)),
            pl.BlockSpec((1, dv), lambda b, h, n: (0, h)),
            pl.BlockSpec((1, dv), lambda b, h, n: (0, h)),
        ],
        out_specs=pl.BlockSpec((L, dv), lambda b, h, n: (b * nl + n, h)),
        out_shape=jax.ShapeDtypeStruct((T, H * dv), BF16),
        scratch_shapes=[pltpu.VMEM((dk, dv), F32)],
        compiler_params=_params("arbitrary", "arbitrary", "arbitrary"),
        name="retention",
    )(qk, qk, vg, vg, dm, dq, dkk, dc, gn_g.reshape(1, H * dv), gn_b.reshape(1, H * dv))


def _out_ln_kernel(z_ref, w_ref, h_ref, g_ref, b_ref, o_ref):
    mix = jnp.dot(z_ref[...], w_ref[...], preferred_element_type=F32)
    u = DN_ALPHA * h_ref[...] + mix
    o_ref[...] = _layer_norm_rows(u, g_ref[...], b_ref[...])


def out_proj_ln(z, w_bf, h, ln_g, ln_b):
    T, K = z.shape
    tm = _tile(T, TOKEN_ROWS)
    row_spec = pl.BlockSpec((tm, D_MODEL), lambda i: (i, 0))
    vec_spec = pl.BlockSpec((1, D_MODEL), lambda i: (0, 0))
    return pl.pallas_call(
        _out_ln_kernel,
        grid=(T // tm,),
        in_specs=[pl.BlockSpec((tm, K), lambda i: (i, 0)), pl.BlockSpec((K, D_MODEL), lambda i: (0, 0)),
                  row_spec, vec_spec, vec_spec],
        out_specs=row_spec,
        out_shape=jax.ShapeDtypeStruct((T, D_MODEL), F32),
        compiler_params=_params("arbitrary"),
        name="out_proj_ln",
    )(z, w_bf, h, ln_g.reshape(1, D_MODEL), ln_b.reshape(1, D_MODEL))


ROUTER_FIELDS = 16


def _router_kernel(h_ref, w_ref, b_ref, out_ref, cnt_ref, carry_ref, *, tm):
    @pl.when(pl.program_id(0) == 0)
    def _():
        carry_ref[...] = jnp.zeros_like(carry_ref)

    x = h_ref[...]
    w = w_ref[...]
    x_hi = x.astype(BF16)
    x_lo = (x - x_hi.astype(F32)).astype(BF16)
    w_hi = w.astype(BF16)
    w_lo = (w - w_hi.astype(F32)).astype(BF16)
    logits = (jnp.dot(x_hi, w_hi, preferred_element_type=F32)
              + (jnp.dot(x_hi, w_lo, preferred_element_type=F32)
                 + jnp.dot(x_lo, w_hi, preferred_element_type=F32))) + b_ref[...]
    work = logits.T
    e_iota = lax.broadcasted_iota(jnp.int32, (N_EXPERTS, tm), 0).astype(F32)
    onehot = jnp.zeros((N_EXPERTS, tm), F32)
    top_vals, hits, rows = [], [], []
    for _ in range(TOP_K):
        m = jnp.max(work, axis=0, keepdims=True)
        idx = jnp.min(jnp.where(work == m, e_iota, float(N_EXPERTS)), axis=0, keepdims=True)
        hit = e_iota == idx
        top_vals.append(m)
        hits.append(hit)
        rows.append(idx)
        work = jnp.where(hit, -jnp.inf, work)
        onehot = jnp.where(hit, 1.0, onehot)
    exps = [jnp.exp(v - top_vals[0]) for v in top_vals]
    denom = exps[0] + exps[1] + exps[2] + exps[3]
    rows += [e / denom for e in exps]

    r = lax.broadcasted_iota(jnp.int32, (tm, tm), 0)
    c = lax.broadcasted_iota(jnp.int32, (tm, tm), 1)
    earlier = jnp.where(r < c, 1.0, 0.0).astype(BF16)
    before = jnp.dot(onehot.astype(BF16), earlier, preferred_element_type=F32) + carry_ref[:, 0:1]
    rows += [jnp.sum(jnp.where(hit, before, 0.0), axis=0, keepdims=True) for hit in hits]

    field = lax.broadcasted_iota(jnp.int32, (ROUTER_FIELDS, tm), 0)
    out = jnp.zeros((ROUTER_FIELDS, tm), F32)
    for f, row in enumerate(rows):
        out = jnp.where(field == f, row, out)
    out_ref[...] = out

    tile_cnt = jnp.sum(onehot, axis=1, keepdims=True)
    cnt_ref[...] = jnp.broadcast_to(tile_cnt, cnt_ref.shape)
    carry_ref[...] = carry_ref[...] + tile_cnt


def router(h, w_router, b_router):
    T = h.shape[0]
    tm = _tile(T, TOKEN_ROWS)
    n = T // tm
    out, cnt = pl.pallas_call(
        functools.partial(_router_kernel, tm=tm),
        grid=(n,),
        in_specs=[pl.BlockSpec((tm, D_MODEL), lambda i: (i, 0)),
                  pl.BlockSpec((D_MODEL, N_EXPERTS), lambda i: (0, 0)),
                  pl.BlockSpec((1, N_EXPERTS), lambda i: (0, 0))],
        out_specs=[pl.BlockSpec((ROUTER_FIELDS, tm), lambda i: (i, 0)),
                   pl.BlockSpec((N_EXPERTS, LANES), lambda i: (i, 0))],
        out_shape=[jax.ShapeDtypeStruct((n * ROUTER_FIELDS, tm), F32),
                   jax.ShapeDtypeStruct((n * N_EXPERTS, LANES), F32)],
        scratch_shapes=[pltpu.VMEM((N_EXPERTS, LANES), F32)],
        compiler_params=_params("arbitrary"),
        name="router",
    )(h, w_router, b_router.reshape(1, N_EXPERTS))
    out = out.reshape(n, ROUTER_FIELDS, tm)
    idx = out[:, 0:TOP_K].astype(jnp.int32)
    gates = out[:, TOP_K:2 * TOP_K]
    rank = out[:, 2 * TOP_K:3 * TOP_K].astype(jnp.int32)
    tile_cnt = cnt.reshape(n, N_EXPERTS, LANES)[:, :, 0].astype(jnp.int32)
    return idx, gates, rank, tile_cnt


def _rows_to_tiles(dst_ref, x, n):
    for c in range(ROW_CHUNKS):
        dst_ref[pl.ds(c, n, stride=ROW_CHUNKS), :] = x[:, c * LANES:(c + 1) * LANES]


def _tiles_to_rows(src_ref, n):
    return jnp.concatenate([src_ref[pl.ds(c, n, stride=ROW_CHUNKS), :] for c in range(ROW_CHUNKS)], axis=1)


def _dispatch_kernel(zfill_ref, rsrc_ref, rdst_ref, rlen_ref, q_ref, h_ref, xs_ref,
                     rows, stage, sems, zsem, *, tt, bm):
    i = pl.program_id(0)
    last = pl.num_programs(0) - 1
    slot = i % 2

    def start_runs(step):
        buf = stage.at[step % 2]
        sem = sems.at[step % 2]

        def per_expert(e, carry):
            n = rlen_ref[step * N_EXPERTS + e]
            src = rsrc_ref[step * N_EXPERTS + e]
            dst = rdst_ref[step * N_EXPERTS + e]
            done = jnp.int32(0)
            piece = tt
            while piece >= 1:
                take = (n & piece) != 0
                size = piece * ROW_CHUNKS

                @pl.when(take)
                def _(done=done, size=size):
                    s = pl.multiple_of(src + done, ROW_CHUNKS)
                    d = pl.multiple_of(dst + done, ROW_CHUNKS)
                    pltpu.make_async_copy(buf.at[pl.ds(s, size)], xs_ref.at[pl.ds(d, size)], sem).start()

                done = done + jnp.where(take, size, 0)
                piece //= 2
            return carry
        lax.fori_loop(0, N_EXPERTS, per_expert, 0)

    def wait_runs(step):
        buf = stage.at[step % 2]
        pltpu.make_async_copy(buf, xs_ref.at[pl.ds(0, tt * TOP_K * ROW_CHUNKS)], sems.at[step % 2]).wait()

    @pl.when(i == 0)
    def _():
        zeros = stage.at[0, pl.ds(0, bm * ROW_CHUNKS)]
        zeros[...] = jnp.zeros_like(zeros)

        def zero_copy(e):
            start = pl.multiple_of(zfill_ref[e], bm * ROW_CHUNKS)
            return pltpu.make_async_copy(zeros, xs_ref.at[pl.ds(start, bm * ROW_CHUNKS)], zsem)

        for e in range(N_EXPERTS):
            @pl.when(zfill_ref[e] >= 0)
            def _():
                zero_copy(e).start()
        for e in range(N_EXPERTS):
            @pl.when(zfill_ref[e] >= 0)
            def _():
                zero_copy(e).wait()

    @pl.when(i >= 2)
    def _():
        wait_runs(i - 2)

    _rows_to_tiles(rows, h_ref[...], tt)

    def sort_rows(t, carry):
        row = rows[pl.ds(pl.multiple_of(t * ROW_CHUNKS, ROW_CHUNKS), ROW_CHUNKS), :]
        for k in range(TOP_K):
            q = pl.multiple_of(q_ref[0, 0, k * tt + t], ROW_CHUNKS)
            stage[slot, pl.ds(q, ROW_CHUNKS), :] = row
        return carry

    lax.fori_loop(0, tt, sort_rows, 0, unroll=8)
    start_runs(i)

    @pl.when(jnp.logical_and(i == last, i >= 1))
    def _():
        wait_runs(i - 1)

    @pl.when(i == last)
    def _():
        wait_runs(i)


def dispatch_rows(h, pos3, run_src8, run_dst8, run_len, zfill8, n_slots, bm):
    T = h.shape[0]
    tt = _tile(T, TOKEN_ROWS)
    assert pos3.shape == (T // tt, 1, tt * TOP_K)
    assert tt * TOP_K >= bm
    return pl.pallas_call(
        functools.partial(_dispatch_kernel, tt=tt, bm=bm),
        grid_spec=pltpu.PrefetchScalarGridSpec(
            num_scalar_prefetch=4,
            grid=(T // tt,),
            in_specs=[pl.BlockSpec((1, 1, tt * TOP_K), lambda i, *_: (i, 0, 0), memory_space=pltpu.SMEM),
                      pl.BlockSpec((tt, D_MODEL), lambda i, *_: (i, 0))],
            out_specs=pl.BlockSpec(memory_space=pl.ANY),
            scratch_shapes=[pltpu.VMEM((tt * ROW_CHUNKS, LANES), F32),
                            pltpu.VMEM((2, tt * TOP_K * ROW_CHUNKS, LANES), F32),
                            pltpu.SemaphoreType.DMA((2,)), pltpu.SemaphoreType.DMA],
        ),
        out_shape=jax.ShapeDtypeStruct((n_slots * ROW_CHUNKS, LANES), F32),
        compiler_params=_params("arbitrary"),
        name="moe_dispatch",
    )(zfill8, run_src8, run_dst8, run_len, pos3, h)


def _moe_mlp_kernel(be_ref, na_ref, x_ref, wgu_ref, bgu_ref, wd_ref, bd_ref, y_ref, wgu_bf, wd_bf, *, bm):
    i = pl.program_id(0)
    prev = be_ref[jnp.maximum(i - 1, 0)]
    active = i < na_ref[0]

    @pl.when(jnp.logical_and(active, jnp.logical_or(i == 0, be_ref[i] != prev)))
    def _():
        wgu_bf[...] = wgu_ref[...].astype(BF16)
        wd_bf[...] = wd_ref[...].astype(BF16)

    @pl.when(active)
    def _():
        x = _tiles_to_rows(x_ref, bm).astype(BF16)
        hgu = jnp.dot(x, wgu_bf[...], preferred_element_type=F32) + bgu_ref[...]
        gate = jnp.minimum(hgu[:, :D_FF], SWIGLU_LIMIT)
        up = jnp.clip(hgu[:, D_FF:], -SWIGLU_LIMIT, SWIGLU_LIMIT)
        act = gate * jax.nn.sigmoid(SWIGLU_ALPHA * gate) * (up + 1.0)
        y = jnp.dot(act.astype(BF16), wd_bf[...], preferred_element_type=F32) + bd_ref[...]
        _rows_to_tiles(y_ref, y, bm)


def moe_mlp(xs, block_expert, n_active, layer, w_gu, b_gu, w_d, b_d, bm):
    n_blocks = xs.shape[0] // (bm * ROW_CHUNKS)

    def row_map(i, be, na):
        return (jnp.minimum(i, na[0] - 1), 0)

    def w_map(i, be, na):
        return (layer, be[jnp.minimum(i, na[0] - 1)], 0, 0)

    def b_map(i, be, na):
        return (layer * N_EXPERTS + be[jnp.minimum(i, na[0] - 1)], 0, 0)

    return pl.pallas_call(
        functools.partial(_moe_mlp_kernel, bm=bm),
        grid_spec=pltpu.PrefetchScalarGridSpec(
            num_scalar_prefetch=2,
            grid=(n_blocks,),
            in_specs=[pl.BlockSpec((bm * ROW_CHUNKS, LANES), row_map),
                      pl.BlockSpec((None, None, D_MODEL, 2 * D_FF), w_map),
                      pl.BlockSpec((None, 1, 2 * D_FF), b_map),
                      pl.BlockSpec((None, None, D_FF, D_MODEL), w_map),
                      pl.BlockSpec((None, 1, D_MODEL), b_map)],
            out_specs=pl.BlockSpec((bm * ROW_CHUNKS, LANES), row_map),
            scratch_shapes=[pltpu.VMEM((D_MODEL, 2 * D_FF), BF16), pltpu.VMEM((D_FF, D_MODEL), BF16)],
        ),
        out_shape=jax.ShapeDtypeStruct(xs.shape, F32),
        compiler_params=_params("arbitrary"),
        name="moe_mlp",
    )(block_expert, n_active, xs, w_gu, b_gu.reshape(DEPTH * N_EXPERTS, 1, 2 * D_FF), w_d,
      b_d.reshape(DEPTH * N_EXPERTS, 1, D_MODEL))


def _ffn_ln_ple_kernel(rstage_ref, rslot_ref, rlen_ref, pos_ref, gate_ref, h_ref, p_ref, g_ref, b_ref,
                       wg_ref, wp_ref, ys_ref, o_ref, ob_ref, stage, ffn_tiles, sems, *, tm):
    i = pl.program_id(0)
    slot = i % 2

    def start_runs(step):
        buf = stage.at[step % 2]
        sem = sems.at[step % 2]

        def per_expert(e, carry):
            n = rlen_ref[step * N_EXPERTS + e]
            src = rslot_ref[step * N_EXPERTS + e]
            dst = rstage_ref[step * N_EXPERTS + e]
            done = jnp.int32(0)
            piece = tm
            while piece >= 1:
                take = (n & piece) != 0
                size = piece * ROW_CHUNKS

                @pl.when(take)
                def _(done=done, size=size):
                    s = pl.multiple_of(src + done, ROW_CHUNKS)
                    d = pl.multiple_of(dst + done, ROW_CHUNKS)
                    pltpu.make_async_copy(ys_ref.at[pl.ds(s, size)], buf.at[pl.ds(d, size)], sem).start()

                done = done + jnp.where(take, size, 0)
                piece //= 2
            return carry
        lax.fori_loop(0, N_EXPERTS, per_expert, 0)

    @pl.when(i == 0)
    def _():
        start_runs(0)

    @pl.when(i + 1 < pl.num_programs(0))
    def _():
        start_runs(i + 1)

    pltpu.make_async_copy(ys_ref.at[pl.ds(0, tm * TOP_K * ROW_CHUNKS)], stage.at[slot], sems.at[slot]).wait()

    def combine(t, carry):
        acc = None
        for k in range(TOP_K):
            q = pl.multiple_of(pos_ref[0, 0, k * tm + t], ROW_CHUNKS)
            term = stage[slot, pl.ds(q, ROW_CHUNKS), :] * gate_ref[0, 0, k * tm + t]
            acc = term if acc is None else acc + term
        ffn_tiles[pl.ds(pl.multiple_of(t * ROW_CHUNKS, ROW_CHUNKS), ROW_CHUNKS), :] = acc
        return carry

    lax.fori_loop(0, tm, combine, 0, unroll=8)

    sub = tm // FFN_ROW_CHUNKS
    for c in range(FFN_ROW_CHUNKS):
        rows = slice(c * sub, (c + 1) * sub)
        ffn = _tiles_to_rows(ffn_tiles.at[pl.ds(c * sub * ROW_CHUNKS, sub * ROW_CHUNKS)], sub)
        h2 = _layer_norm_rows(DN_ALPHA * h_ref[rows, :] + ffn, g_ref[...], b_ref[...])
        gate = jax.nn.sigmoid(jnp.dot(h2.astype(BF16), wg_ref[...], preferred_element_type=F32))
        proj = jnp.dot(p_ref[rows, :].astype(BF16), wp_ref[...], preferred_element_type=F32)
        out = h2 + gate * proj
        o_ref[rows, :] = out
        ob_ref[rows, :] = out.astype(BF16)


def ffn_ln_ple(ys, pos3, gates3, run_stage8, run_slot8, run_len, h, p_all, layer, ln_g, ln_b, wg_bf, wp_bf):
    T = h.shape[0]
    tm = _tile(T, TOKEN_ROWS)
    n = T // tm
    assert pos3.shape == gates3.shape == (n, 1, tm * TOP_K)
    row_spec = pl.BlockSpec((tm, D_MODEL), lambda i, *_: (i, 0))
    vec_spec = pl.BlockSpec((1, D_MODEL), lambda i, *_: (0, 0))
    smem_spec = pl.BlockSpec((1, 1, tm * TOP_K), lambda i, *_: (i, 0, 0), memory_space=pltpu.SMEM)
    return pl.pallas_call(
        functools.partial(_ffn_ln_ple_kernel, tm=tm),
        grid_spec=pltpu.PrefetchScalarGridSpec(
            num_scalar_prefetch=3,
            grid=(n,),
            in_specs=[smem_spec, smem_spec,
                      row_spec,
                      pl.BlockSpec((tm, PLE_DIM), lambda i, *_: (layer * n + i, 0)),
                      vec_spec, vec_spec,
                      pl.BlockSpec((D_MODEL, D_MODEL), lambda i, *_: (0, 0)),
                      pl.BlockSpec((PLE_DIM, D_MODEL), lambda i, *_: (0, 0)),
                      pl.BlockSpec(memory_space=pl.ANY)],
            out_specs=[row_spec, row_spec],
            scratch_shapes=[pltpu.VMEM((2, tm * TOP_K * ROW_CHUNKS, LANES), F32),
                            pltpu.VMEM((tm * ROW_CHUNKS, LANES), F32), pltpu.SemaphoreType.DMA((2,))],
        ),
        out_shape=[jax.ShapeDtypeStruct((T, D_MODEL), F32), jax.ShapeDtypeStruct((T, D_MODEL), BF16)],
        compiler_params=_params("arbitrary"),
        name="ffn_ln_ple",
    )(run_stage8, run_slot8, run_len, pos3, gates3, h, p_all, ln_g.reshape(1, D_MODEL),
      ln_b.reshape(1, D_MODEL), wg_bf, wp_bf, ys)


def moe_layer(h1, layer, w_router, b_router, w_gu, b_gu, w_d, b_d):
    T = h1.shape[0]
    bm = MOE_BLOCK
    idx, gates, rank, tile_cnt = router(h1, w_router, b_router)
    nt, _, tt = idx.shape
    counts = jnp.sum(tile_cnt, axis=0)
    padded = (counts + bm - 1) // bm * bm
    pend = jnp.cumsum(padded)
    pstart = pend - padded
    n_slots = T * TOP_K + N_EXPERTS * bm
    n_blocks = n_slots // bm
    block_start = jnp.arange(n_blocks, dtype=jnp.int32) * bm
    block_expert = jnp.minimum(jnp.sum(block_start[:, None] >= pend[None, :], axis=1),
                               N_EXPERTS - 1).astype(jnp.int32)
    n_active = (pend[-1:] // bm).astype(jnp.int32)
    zfill8 = jnp.where(padded > 0, (pend - bm) * ROW_CHUNKS, -1).astype(jnp.int32)
    tile_base = jnp.cumsum(tile_cnt, axis=0) - tile_cnt
    stage_off = jnp.cumsum(tile_cnt, axis=1) - tile_cnt
    onehot = idx[:, :, None, :] == jnp.arange(N_EXPERTS, dtype=jnp.int32)[:, None]

    def per_assignment(table):
        return jnp.sum(jnp.where(onehot, table[:, None, :, None], 0), axis=2)

    stage_pos8 = ((per_assignment(stage_off - tile_base) + rank) * ROW_CHUNKS).astype(jnp.int32)
    run_src8 = (stage_off * ROW_CHUNKS).reshape(-1).astype(jnp.int32)
    run_dst8 = ((pstart[None, :] + tile_base) * ROW_CHUNKS).reshape(-1).astype(jnp.int32)
    pos3 = stage_pos8.reshape(nt, 1, TOP_K * tt)
    run_len = tile_cnt.reshape(-1)
    xs = dispatch_rows(h1, pos3, run_src8, run_dst8, run_len, zfill8, n_slots, bm)
    ys = moe_mlp(xs, block_expert, n_active, layer, w_gu, b_gu, w_d, b_d, bm)
    return ys, pos3, gates.reshape(nt, 1, TOP_K * tt), run_src8, run_dst8, run_len


def _rms_rows(x, g):
    return x * lax.rsqrt(jnp.mean(x * x, axis=-1, keepdims=True) + RMS_EPS) * g


def _mla_kv_kernel(x_ref, wa_ref, g_ref, wkn_ref, wv_ref, cm_ref, sm_ref, k_ref, v_ref):
    a = jnp.dot(x_ref[...], wa_ref[...], preferred_element_type=F32)
    c_kv = _rms_rows(a[:, :KV_LORA], g_ref[...]).astype(BF16)
    kr = a[:, KV_LORA:KV_LORA + LANES] * cm_ref[...] + a[:, KV_LORA + LANES:] * sm_ref[...]
    kr = kr.astype(BF16)
    kn = jnp.dot(c_kv, wkn_ref[...], preferred_element_type=F32)
    v_ref[...] = jnp.dot(c_kv, wv_ref[...], preferred_element_type=F32).astype(BF16)
    for h in range(MLA_HEADS):
        k_ref[:, 2 * h * LANES:(2 * h + 1) * LANES] = kn[:, h * QK_NOPE:(h + 1) * QK_NOPE].astype(BF16)
        k_ref[:, (2 * h + 1) * LANES:(2 * h + 2) * LANES] = kr


def _pad_rope_cols(w, swap):
    K = w.shape[0]
    w = w.reshape(K, -1, QK_ROPE)
    if swap:
        w = jnp.concatenate([w[..., QK_ROPE // 2:], w[..., :QK_ROPE // 2]], axis=-1)
    w = jnp.concatenate([w, jnp.zeros_like(w)], axis=-1)
    return w.reshape(K, -1)


def mla_shared_kv(hb, w_kv_a, kv_norm_g, w_kv_b, cm, sm):
    T = hb.shape[0]
    tm = _tile(T, TOKEN_ROWS)
    w_r = w_kv_a[:, KV_LORA:]
    wa = jnp.concatenate([w_kv_a[:, :KV_LORA], _pad_rope_cols(w_r, False), _pad_rope_cols(w_r, True)],
                         axis=1).astype(BF16)
    wb = w_kv_b.reshape(KV_LORA, MLA_HEADS, QK_NOPE + V_HEAD)
    wkn = wb[:, :, :QK_NOPE].reshape(KV_LORA, MLA_HEADS * QK_NOPE).astype(BF16)
    wv = wb[:, :, QK_NOPE:].reshape(KV_LORA, MLA_HEADS * V_HEAD).astype(BF16)
    full = lambda shape: pl.BlockSpec(shape, lambda i: (0, 0))
    return pl.pallas_call(
        _mla_kv_kernel,
        grid=(T // tm,),
        in_specs=[pl.BlockSpec((tm, D_MODEL), lambda i: (i, 0)), full(wa.shape), full((1, KV_LORA)),
                  full(wkn.shape), full(wv.shape),
                  pl.BlockSpec((tm, LANES), lambda i: (i, 0)), pl.BlockSpec((tm, LANES), lambda i: (i, 0))],
        out_specs=[pl.BlockSpec((tm, MLA_HEADS * 2 * LANES), lambda i: (i, 0)),
                   pl.BlockSpec((tm, MLA_HEADS * V_HEAD), lambda i: (i, 0))],
        out_shape=[jax.ShapeDtypeStruct((T, MLA_HEADS * 2 * LANES), BF16),
                   jax.ShapeDtypeStruct((T, MLA_HEADS * V_HEAD), BF16)],
        compiler_params=_params("arbitrary"),
        name="mla_kv",
    )(hb, wa, kv_norm_g.reshape(1, KV_LORA), wkn, wv, cm, sm)


def _mla_q_kernel(x_ref, wa_ref, g_ref, wb_ref, cm_ref, sm_ref, q_ref):
    scale = (QK_NOPE + QK_ROPE) ** -0.5 * LOG2_E
    qa = jnp.dot(x_ref[...], wa_ref[...], preferred_element_type=F32)
    qc = _rms_rows(qa, g_ref[...]).astype(BF16)
    q = jnp.dot(qc, wb_ref[...], preferred_element_type=F32)
    cm = cm_ref[...] * scale
    sm = sm_ref[...] * scale
    n = MLA_HEADS * LANES
    for h in range(MLA_HEADS):
        lo = h * LANES
        q_ref[:, 2 * lo:2 * lo + LANES] = (q[:, lo:lo + LANES] * scale).astype(BF16)
        rope = q[:, n + lo:n + lo + LANES] * cm + q[:, 2 * n + lo:2 * n + lo + LANES] * sm
        q_ref[:, 2 * lo + LANES:2 * lo + 2 * LANES] = rope.astype(BF16)


def mla_q(hb, w_q_a, q_norm_g, w_q_b, cm, sm):
    T = hb.shape[0]
    tm = _tile(T, TOKEN_ROWS)
    wq = w_q_b.reshape(Q_LORA, MLA_HEADS, QK_NOPE + QK_ROPE)
    w_n = wq[:, :, :QK_NOPE].reshape(Q_LORA, MLA_HEADS * QK_NOPE)
    w_r = wq[:, :, QK_NOPE:].reshape(Q_LORA, MLA_HEADS * QK_ROPE)
    wb = jnp.concatenate([w_n, _pad_rope_cols(w_r, False), _pad_rope_cols(w_r, True)], axis=1).astype(BF16)
    full = lambda shape: pl.BlockSpec(shape, lambda i: (0, 0))
    return pl.pallas_call(
        _mla_q_kernel,
        grid=(T // tm,),
        in_specs=[pl.BlockSpec((tm, D_MODEL), lambda i: (i, 0)), full((D_MODEL, Q_LORA)), full((1, Q_LORA)),
                  full(wb.shape),
                  pl.BlockSpec((tm, LANES), lambda i: (i, 0)), pl.BlockSpec((tm, LANES), lambda i: (i, 0))],
        out_specs=pl.BlockSpec((tm, MLA_HEADS * 2 * LANES), lambda i: (i, 0)),
        out_shape=jax.ShapeDtypeStruct((T, MLA_HEADS * 2 * LANES), BF16),
        compiler_params=_params("arbitrary"),
        name="mla_q",
    )(hb, w_q_a.astype(BF16), q_norm_g.reshape(1, Q_LORA), wb, cm, sm)


def _flash_kernel(q_ref, k_ref, v_ref, o_ref, m_ref, l_ref, acc_ref, *, bq, bk):
    i = pl.program_id(2)
    q = q_ref[...]
    m_ref[...] = jnp.full_like(m_ref, -jnp.inf)
    l_ref[...] = jnp.zeros_like(l_ref)
    acc_ref[...] = jnp.zeros_like(acc_ref)

    def step(j, rows=slice(0, bq), triangular=False):
        n = rows.stop - rows.start
        start = pl.multiple_of(j * bk, bk)
        kj = k_ref[pl.ds(start, bk), :]
        vj = v_ref[pl.ds(start, bk), :]
        s = lax.dot_general(q[rows], kj, NT_DIMS, preferred_element_type=F32)
        if triangular:
            r = lax.broadcasted_iota(jnp.int32, (n, bk), 0)
            c = lax.broadcasted_iota(jnp.int32, (n, bk), 1)
            s = jnp.where(c <= r, s, -jnp.inf)
        m_prev = m_ref[rows, :]
        m_new = jnp.maximum(m_prev, jnp.max(s, axis=-1, keepdims=True))
        alpha = jnp.exp2(m_prev - m_new)
        p = jnp.exp2(s - pltpu.repeat(m_new, bk // LANES, axis=1))
        p_lanes = p[:, 0:LANES]
        for c0 in range(LANES, bk, LANES):
            p_lanes = p_lanes + p[:, c0:c0 + LANES]
        l_ref[rows, :] = alpha * l_ref[rows, :] + p_lanes
        acc_ref[rows, :] = alpha * acc_ref[rows, :] + jnp.dot(p.astype(BF16), vj, preferred_element_type=F32)
        m_ref[rows, :] = m_new

    def pair(j, carry):
        step(2 * j)
        step(2 * j + 1)
        return carry

    lax.fori_loop(0, i, pair, 0)
    step(2 * i, triangular=True)
    step(2 * i + 1, slice(bk, bq), triangular=True)
    l = jnp.sum(l_ref[...], axis=-1, keepdims=True)
    o_ref[...] = (acc_ref[...] / l).astype(o_ref.dtype)


def flash_attention(q, k, v, B, S):
    T = B * S
    bq = _tile(S, ATTN_BLOCK_Q)
    bk = bq // 2
    nq = S // bq
    return pl.pallas_call(
        functools.partial(_flash_kernel, bq=bq, bk=bk),
        grid=(B, MLA_HEADS, nq),
        in_specs=[pl.BlockSpec((bq, 2 * LANES), lambda b, h, i: (b * nq + i, h)),
                  pl.BlockSpec((S, 2 * LANES), lambda b, h, i: (b, h)),
                  pl.BlockSpec((S, V_HEAD), lambda b, h, i: (b, h))],
        out_specs=pl.BlockSpec((bq, V_HEAD), lambda b, h, i: (b * nq + i, h)),
        out_shape=jax.ShapeDtypeStruct((T, MLA_HEADS * V_HEAD), BF16),
        scratch_shapes=[pltpu.VMEM((bq, LANES), F32), pltpu.VMEM((bq, LANES), F32),
                        pltpu.VMEM((bq, V_HEAD), F32)],
        compiler_params=_params("arbitrary", "arbitrary", "arbitrary"),
        name="mla_flash",
    )(q, k, v)


def kernel(x, p, positions, ret_w_in, ret_gn_g, ret_gn_b, ret_w_out, mla_w_kv_a, mla_kv_norm_g, mla_w_kv_b,
           mla_w_q_a, mla_q_norm_g, mla_w_q_b, mla_w_o, ln_mix_g, ln_mix_b, ln_ffn_g, ln_ffn_b,
           moe_w_router, moe_b_router, moe_w_gate_up, moe_b_gate_up, moe_w_down, moe_b_down,
           ple_w_gate, ple_w_proj):
    B, S, D = x.shape
    T = B * S
    cos_r, sin_r, cos_m, sin_m = rope_tables(positions)
    h = x.reshape(T, D)
    hb = h.astype(BF16)
    kv = None
    for i in range(DEPTH):
        if i < N_A:
            qk, vg = ret_projections(hb, ret_w_in[i].astype(BF16), cos_r, sin_r)
            z = retention(qk, vg, ret_gn_g[i], ret_gn_b[i], B, S)
            w_out = ret_w_out[i].astype(BF16)
        else:
            if kv is None:
                kv = mla_shared_kv(hb, mla_w_kv_a, mla_kv_norm_g, mla_w_kv_b, cos_m, sin_m)
            j = i - N_A
            q = mla_q(hb, mla_w_q_a[j], mla_q_norm_g[j], mla_w_q_b[j], cos_m, sin_m)
            z = flash_attention(q, kv[0], kv[1], B, S)
            w_out = mla_w_o[j].astype(BF16)
        h1 = out_proj_ln(z, w_out, h, ln_mix_g[i], ln_mix_b[i])
        ys, *tables = moe_layer(h1, i, moe_w_router[i], moe_b_router[i], moe_w_gate_up, moe_b_gate_up,
                                moe_w_down, moe_b_down)
        h, hb = ffn_ln_ple(ys, *tables, h1, p.reshape(DEPTH * T, PLE_DIM), i, ln_ffn_g[i], ln_ffn_b[i],
                           ple_w_gate[i].astype(BF16), ple_w_proj[i].astype(BF16))
    return h.reshape(B, S, D)
```

```python
import functools

import jax
import jax.numpy as jnp
from jax import lax
from jax.experimental import pallas as pl
from jax.experimental.pallas import tpu as pltpu

D_MODEL = 1024
DEPTH = 4
N_A = DEPTH // 2
RET_HEADS = 4
RET_QK_DIM = D_MODEL // RET_HEADS
RET_V_DIM = 2 * RET_QK_DIM
GN_EPS = 1e-6
MLA_HEADS = 8
QK_NOPE = 128
QK_ROPE = 64
V_HEAD = 128
Q_LORA = 256
KV_LORA = 256
RMS_EPS = 1e-6
ROPE_THETA = 10000.0
N_EXPERTS = 32
TOP_K = 4
D_FF = D_MODEL
SWIGLU_LIMIT = 7.0
SWIGLU_ALPHA = 1.702
PLE_DIM = 256
DN_ALPHA = (2 * DEPTH) ** 0.25
LN_EPS = 1e-5
LOG2_E = 1.4426950408889634

LANES = 128
ROW_CHUNKS = D_MODEL // LANES
VMEM_LIMIT_BYTES = 56 * 1024 * 1024

TOKEN_ROWS = 512
PROJ_ROWS = 1024
RET_CHUNK = 256
RET_BLOCK = 1024
ATTN_BLOCK_Q = 1024
MOE_BLOCK = 512
FFN_ROW_CHUNKS = 2

F32 = jnp.float32
BF16 = jnp.bfloat16
NT_DIMS = (((1,), (1,)), ((), ()))
TN_DIMS = (((0,), (0,)), ((), ()))


def _tile(n, pref):
    t = min(n, pref)
    assert n % t == 0, (n, pref)
    return t


def _params(*sem):
    return pltpu.CompilerParams(dimension_semantics=sem, vmem_limit_bytes=VMEM_LIMIT_BYTES)


def _layer_norm_rows(u, g, b):
    mu = jnp.mean(u, axis=-1, keepdims=True)
    d = u - mu
    var = jnp.mean(d * d, axis=-1, keepdims=True)
    return d * lax.rsqrt(var + LN_EPS) * g + b


def _rope_table_kernel(pos_ref, inv_r_ref, inv_m_ref, sign_m_ref, cr_ref, sr_ref, cm_ref, sm_ref):
    pos = pos_ref[...].astype(F32)
    ang_r = pos * inv_r_ref[...]
    cr_ref[...] = jnp.cos(ang_r)
    sr_ref[...] = jnp.sin(ang_r)
    ang_m = pos * inv_m_ref[...]
    cm_ref[...] = jnp.cos(ang_m)
    sm_ref[...] = jnp.sin(ang_m) * sign_m_ref[...]


def rope_tables(positions):
    T = positions.size
    tm = _tile(T, PROJ_ROWS)
    pos = positions.reshape(T, 1)
    half_r = RET_QK_DIM // 2
    inv_r = ROPE_THETA ** (-jnp.arange(0, RET_QK_DIM, 2, dtype=F32) / RET_QK_DIM)
    inv_m = ROPE_THETA ** (-jnp.arange(0, QK_ROPE, 2, dtype=F32) / QK_ROPE)
    half_m = QK_ROPE // 2
    lane = jnp.arange(LANES)
    inv_m = inv_m[lane % half_m]
    sign_m = jnp.where((lane % QK_ROPE) < half_m, -1.0, 1.0).astype(F32)
    assert half_r == LANES
    row = lambda v: v.reshape(1, LANES)
    vec_spec = pl.BlockSpec((1, LANES), lambda i: (0, 0))
    tab_spec = pl.BlockSpec((tm, LANES), lambda i: (i, 0))
    out = jax.ShapeDtypeStruct((T, LANES), F32)
    return pl.pallas_call(
        _rope_table_kernel,
        grid=(T // tm,),
        in_specs=[pl.BlockSpec((tm, 1), lambda i: (i, 0)), vec_spec, vec_spec, vec_spec],
        out_specs=[tab_spec] * 4,
        out_shape=[out] * 4,
        compiler_params=_params("arbitrary"),
        name="rope_tables",
    )(pos, row(inv_r), row(inv_m), row(sign_m))


def _ret_qk_kernel(x_ref, w_ref, cos_ref, sin_ref, o_ref):
    acc = jnp.dot(x_ref[...], w_ref[...], preferred_element_type=F32)
    scale = jnp.where(pl.program_id(0) == 1, RET_QK_DIM ** -0.5, 1.0).astype(F32)
    c = cos_ref[...] * scale
    s = sin_ref[...] * scale
    half = RET_QK_DIM // 2
    for h in range(RET_HEADS):
        lo = h * RET_QK_DIM
        x1 = acc[:, lo:lo + half]
        x2 = acc[:, lo + half:lo + 2 * half]
        o_ref[:, lo:lo + half] = (x1 * c - x2 * s).astype(o_ref.dtype)
        o_ref[:, lo + half:lo + 2 * half] = (x1 * s + x2 * c).astype(o_ref.dtype)


def _matmul_kernel(x_ref, w_ref, o_ref):
    o_ref[...] = jnp.dot(x_ref[...], w_ref[...], preferred_element_type=F32).astype(o_ref.dtype)


def ret_projections(hb, w_in_bf, cos_r, sin_r):
    T = hb.shape[0]
    tm = _tile(T, PROJ_ROWS)
    tn = D_MODEL
    n_qk = 2 * RET_HEADS * RET_QK_DIM // tn
    n_vg = 2 * RET_HEADS * RET_V_DIM // tn
    x_spec = pl.BlockSpec((tm, D_MODEL), lambda j, i: (i, 0))
    tab_spec = pl.BlockSpec((tm, LANES), lambda j, i: (i, 0))
    qk = pl.pallas_call(
        _ret_qk_kernel,
        grid=(n_qk, T // tm),
        in_specs=[x_spec, pl.BlockSpec((D_MODEL, tn), lambda j, i: (0, j)), tab_spec, tab_spec],
        out_specs=pl.BlockSpec((tm, tn), lambda j, i: (i, j)),
        out_shape=jax.ShapeDtypeStruct((T, n_qk * tn), BF16),
        compiler_params=_params("arbitrary", "arbitrary"),
        name="ret_qk_proj",
    )(hb, w_in_bf, cos_r, sin_r)
    vg = pl.pallas_call(
        _matmul_kernel,
        grid=(n_vg, T // tm),
        in_specs=[x_spec, pl.BlockSpec((D_MODEL, tn), lambda j, i: (0, j + n_qk))],
        out_specs=pl.BlockSpec((tm, tn), lambda j, i: (i, j)),
        out_shape=jax.ShapeDtypeStruct((T, n_vg * tn), BF16),
        compiler_params=_params("arbitrary", "arbitrary"),
        name="ret_vg_proj",
    )(hb, w_in_bf)
    return qk, vg


def _retention_kernel(q_ref, k_ref, v_ref, g_ref, dm_ref, dq_ref, dk_ref, dc_ref, gng_ref, gnb_ref,
                      z_ref, state_ref, *, chunk, n_chunks):
    @pl.when(pl.program_id(2) == 0)
    def _():
        state_ref[...] = jnp.zeros_like(state_ref)

    dm = dm_ref[0]
    dq = dq_ref[0]
    dk = dk_ref[0]
    dc = dc_ref[0]
    gng = gng_ref[...]
    gnb = gnb_ref[...]
    for c in range(n_chunks):
        rows = slice(c * chunk, (c + 1) * chunk)
        q = q_ref[rows, :]
        k = k_ref[rows, :]
        v = v_ref[rows, :]
        st = state_ref[...]
        inner = lax.dot_general(q, k, NT_DIMS, preferred_element_type=F32) * dm
        y = jnp.dot(inner.astype(BF16), v, preferred_element_type=F32)
        y = y + jnp.dot(q, st.astype(BF16), preferred_element_type=F32) * dq
        kd = (k.astype(F32) * dk).astype(BF16)
        state_ref[...] = st * dc + lax.dot_general(kd, v, TN_DIMS, preferred_element_type=F32)
        mu = jnp.mean(y, axis=-1, keepdims=True)
        d = y - mu
        var = jnp.mean(d * d, axis=-1, keepdims=True)
        yn = d * lax.rsqrt(var + GN_EPS) * gng + gnb
        g = g_ref[rows, :].astype(F32)
        z_ref[rows, :] = (g * jax.nn.sigmoid(g) * yn).astype(z_ref.dtype)


def retention(qk, vg, gn_g, gn_b, B, S):
    T = B * S
    H, dk, dv = RET_HEADS, RET_QK_DIM, RET_V_DIM
    C = _tile(S, RET_CHUNK)
    L = _tile(S, RET_BLOCK)
    nl = S // L
    log_g = jnp.log(1.0 - 2.0 ** (-5.0 - jnp.arange(H, dtype=F32)))
    idx = jnp.arange(C, dtype=F32)
    diff = idx[:, None] - idx[None, :]
    dm = jnp.where(diff >= 0, jnp.exp(log_g[:, None, None] * jnp.maximum(diff, 0.0)), 0.0)
    dq = jnp.exp(log_g[:, None] * (idx + 1.0))[:, :, None]
    dkk = jnp.exp(log_g[:, None] * (C - 1.0 - idx))[:, :, None]
    dc = jnp.broadcast_to(jnp.exp(log_g * C)[:, None, None], (H, 1, dv))
    kern = functools.partial(_retention_kernel, chunk=C, n_chunks=L // C)
    return pl.pallas_call(
        kern,
        grid=(B, H, nl),
        in_specs=[
            pl.BlockSpec((L, dk), lambda b, h, n: (b * nl + n, h)),
            pl.BlockSpec((L, dk), lambda b, h, n: (b * nl + n, H + h)),
            pl.BlockSpec((L, dv), lambda b, h, n: (b * nl + n, h)),
            pl.BlockSpec((L, dv), lambda b, h, n: (b * nl + n, H + h)),
            pl.BlockSpec((1, C, C), lambda b, h, n: (h, 0, 0)),
            pl.BlockSpec((1, C, 1), lambda b, h, n: (h, 0, 0)),
            pl.BlockSpec((1, C, 1), lambda b, h, n: (h, 0, 0)),
            pl.BlockSpec((1, 1, dv), lambda b, h, n: (h, 0, 0)),
            pl.BlockSpec((1, dv), lambda b, h, n: (0, h)),
            pl.BlockSpec((1, dv), lambda b, h, n: (0, h)),
        ],
        out_specs=pl.BlockSpec((L, dv), lambda b, h, n: (b * nl + n, h)),
        out_shape=jax.ShapeDtypeStruct((T, H * dv), BF16),
        scratch_shapes=[pltpu.VMEM((dk, dv), F32)],
        compiler_params=_params("arbitrary", "arbitrary", "arbitrary"),
        name="retention",
    )(qk, qk, vg, vg, dm, dq, dkk, dc, gn_g.reshape(1, H * dv), gn_b.reshape(1, H * dv))


def _out_ln_kernel(z_ref, w_ref, h_ref, g_ref, b_ref, o_ref):
    mix = jnp.dot(z_ref[...], w_ref[...], preferred_element_type=F32)
    u = DN_ALPHA * h_ref[...] + mix
    o_ref[...] = _layer_norm_rows(u, g_ref[...], b_ref[...])


def out_proj_ln(z, w_bf, h, ln_g, ln_b):
    T, K = z.shape
    tm = _tile(T, TOKEN_ROWS)
    row_spec = pl.BlockSpec((tm, D_MODEL), lambda i: (i, 0))
    vec_spec = pl.BlockSpec((1, D_MODEL), lambda i: (0, 0))
    return pl.pallas_call(
        _out_ln_kernel,
        grid=(T // tm,),
        in_specs=[pl.BlockSpec((tm, K), lambda i: (i, 0)), pl.BlockSpec((K, D_MODEL), lambda i: (0, 0)),
                  row_spec, vec_spec, vec_spec],
        out_specs=row_spec,
        out_shape=jax.ShapeDtypeStruct((T, D_MODEL), F32),
        compiler_params=_params("arbitrary"),
        name="out_proj_ln",
    )(z, w_bf, h, ln_g.reshape(1, D_MODEL), ln_b.reshape(1, D_MODEL))


ROUTER_FIELDS = 16


def _router_kernel(h_ref, w_ref, b_ref, out_ref, cnt_ref, carry_ref, *, tm):
    @pl.when(pl.program_id(0) == 0)
    def _():
        carry_ref[...] = jnp.zeros_like(carry_ref)

    x = h_ref[...]
    w = w_ref[...]
    x_hi = x.astype(BF16)
    x_lo = (x - x_hi.astype(F32)).astype(BF16)
    w_hi = w.astype(BF16)
    w_lo = (w - w_hi.astype(F32)).astype(BF16)
    logits = (jnp.dot(x_hi, w_hi, preferred_element_type=F32)
              + (jnp.dot(x_hi, w_lo, preferred_element_type=F32)
                 + jnp.dot(x_lo, w_hi, preferred_element_type=F32))) + b_ref[...]
    work = logits.T
    e_iota = lax.broadcasted_iota(jnp.int32, (N_EXPERTS, tm), 0).astype(F32)
    onehot = jnp.zeros((N_EXPERTS, tm), F32)
    top_vals, hits, rows = [], [], []
    for _ in range(TOP_K):
        m = jnp.max(work, axis=0, keepdims=True)
        idx = jnp.min(jnp.where(work == m, e_iota, float(N_EXPERTS)), axis=0, keepdims=True)
        hit = e_iota == idx
        top_vals.append(m)
        hits.append(hit)
        rows.append(idx)
        work = jnp.where(hit, -jnp.inf, work)
        onehot = jnp.where(hit, 1.0, onehot)
    exps = [jnp.exp(v - top_vals[0]) for v in top_vals]
    denom = exps[0] + exps[1] + exps[2] + exps[3]
    rows += [e / denom for e in exps]

    r = lax.broadcasted_iota(jnp.int32, (tm, tm), 0)
    c = lax.broadcasted_iota(jnp.int32, (tm, tm), 1)
    earlier = jnp.where(r < c, 1.0, 0.0).astype(BF16)
    before = jnp.dot(onehot.astype(BF16), earlier, preferred_element_type=F32) + carry_ref[:, 0:1]
    rows += [jnp.sum(jnp.where(hit, before, 0.0), axis=0, keepdims=True) for hit in hits]

    field = lax.broadcasted_iota(jnp.int32, (ROUTER_FIELDS, tm), 0)
    out = jnp.zeros((ROUTER_FIELDS, tm), F32)
    for f, row in enumerate(rows):
        out = jnp.where(field == f, row, out)
    out_ref[...] = out

    tile_cnt = jnp.sum(onehot, axis=1, keepdims=True)
    cnt_ref[...] = jnp.broadcast_to(tile_cnt, cnt_ref.shape)
    carry_ref[...] = carry_ref[...] + tile_cnt


def router(h, w_router, b_router):
    T = h.shape[0]
    tm = _tile(T, TOKEN_ROWS)
    n = T // tm
    out, cnt = pl.pallas_call(
        functools.partial(_router_kernel, tm=tm),
        grid=(n,),
        in_specs=[pl.BlockSpec((tm, D_MODEL), lambda i: (i, 0)),
                  pl.BlockSpec((D_MODEL, N_EXPERTS), lambda i: (0, 0)),
                  pl.BlockSpec((1, N_EXPERTS), lambda i: (0, 0))],
        out_specs=[pl.BlockSpec((ROUTER_FIELDS, tm), lambda i: (i, 0)),
                   pl.BlockSpec((N_EXPERTS, LANES), lambda i: (i, 0))],
        out_shape=[jax.ShapeDtypeStruct((n * ROUTER_FIELDS, tm), F32),
                   jax.ShapeDtypeStruct((n * N_EXPERTS, LANES), F32)],
        scratch_shapes=[pltpu.VMEM((N_EXPERTS, LANES), F32)],
        compiler_params=_params("arbitrary"),
        name="router",
    )(h, w_router, b_router.reshape(1, N_EXPERTS))
    out = out.reshape(n, ROUTER_FIELDS, tm)
    idx = out[:, 0:TOP_K].astype(jnp.int32)
    gates = out[:, TOP_K:2 * TOP_K]
    rank = out[:, 2 * TOP_K:3 * TOP_K].astype(jnp.int32)
    tile_cnt = cnt.reshape(n, N_EXPERTS, LANES)[:, :, 0].astype(jnp.int32)
    return idx, gates, rank, tile_cnt


def _rows_to_tiles(dst_ref, x, n):
    for c in range(ROW_CHUNKS):
        dst_ref[pl.ds(c, n, stride=ROW_CHUNKS), :] = x[:, c * LANES:(c + 1) * LANES]


def _tiles_to_rows(src_ref, n):
    return jnp.concatenate([src_ref[pl.ds(c, n, stride=ROW_CHUNKS), :] for c in range(ROW_CHUNKS)], axis=1)


def _dispatch_kernel(zfill_ref, rsrc_ref, rdst_ref, rlen_ref, q_ref, h_ref, xs_ref,
                     rows, stage, sems, zsem, *, tt, bm):
    i = pl.program_id(0)
    last = pl.num_programs(0) - 1
    slot = i % 2

    def start_runs(step):
        buf = stage.at[step % 2]
        sem = sems.at[step % 2]

        def per_expert(e, carry):
            n = rlen_ref[step * N_EXPERTS + e]
            src = rsrc_ref[step * N_EXPERTS + e]
            dst = rdst_ref[step * N_EXPERTS + e]
            done = jnp.int32(0)
            piece = tt
            while piece >= 1:
                take = (n & piece) != 0
                size = piece * ROW_CHUNKS

                @pl.when(take)
                def _(done=done, size=size):
                    s = pl.multiple_of(src + done, ROW_CHUNKS)
                    d = pl.multiple_of(dst + done, ROW_CHUNKS)
                    pltpu.make_async_copy(buf.at[pl.ds(s, size)], xs_ref.at[pl.ds(d, size)], sem).start()

                done = done + jnp.where(take, size, 0)
                piece //= 2
            return carry
        lax.fori_loop(0, N_EXPERTS, per_expert, 0)

    def wait_runs(step):
        buf = stage.at[step % 2]
        pltpu.make_async_copy(buf, xs_ref.at[pl.ds(0, tt * TOP_K * ROW_CHUNKS)], sems.at[step % 2]).wait()

    @pl.when(i == 0)
    def _():
        zeros = stage.at[0, pl.ds(0, bm * ROW_CHUNKS)]
        zeros[...] = jnp.zeros_like(zeros)

        def zero_copy(e):
            start = pl.multiple_of(zfill_ref[e], bm * ROW_CHUNKS)
            return pltpu.make_async_copy(zeros, xs_ref.at[pl.ds(start, bm * ROW_CHUNKS)], zsem)

        for e in range(N_EXPERTS):
            @pl.when(zfill_ref[e] >= 0)
            def _():
                zero_copy(e).start()
        for e in range(N_EXPERTS):
            @pl.when(zfill_ref[e] >= 0)
            def _():
                zero_copy(e).wait()

    @pl.when(i >= 2)
    def _():
        wait_runs(i - 2)

    _rows_to_tiles(rows, h_ref[...], tt)

    def sort_rows(t, carry):
        row = rows[pl.ds(pl.multiple_of(t * ROW_CHUNKS, ROW_CHUNKS), ROW_CHUNKS), :]
        for k in range(TOP_K):
            q = pl.multiple_of(q_ref[0, 0, k * tt + t], ROW_CHUNKS)
            stage[slot, pl.ds(q, ROW_CHUNKS), :] = row
        return carry

    lax.fori_loop(0, tt, sort_rows, 0, unroll=16)
    start_runs(i)

    @pl.when(jnp.logical_and(i == last, i >= 1))
    def _():
        wait_runs(i - 1)

    @pl.when(i == last)
    def _():
        wait_runs(i)


def dispatch_rows(h, pos3, run_src8, run_dst8, run_len, zfill8, n_slots, bm):
    T = h.shape[0]
    tt = _tile(T, TOKEN_ROWS)
    assert pos3.shape == (T // tt, 1, tt * TOP_K)
    assert tt * TOP_K >= bm
    return pl.pallas_call(
        functools.partial(_dispatch_kernel, tt=tt, bm=bm),
        grid_spec=pltpu.PrefetchScalarGridSpec(
            num_scalar_prefetch=4,
            grid=(T // tt,),
            in_specs=[pl.BlockSpec((1, 1, tt * TOP_K), lambda i, *_: (i, 0, 0), memory_space=pltpu.SMEM),
                      pl.BlockSpec((tt, D_MODEL), lambda i, *_: (i, 0))],
            out_specs=pl.BlockSpec(memory_space=pl.ANY),
            scratch_shapes=[pltpu.VMEM((tt * ROW_CHUNKS, LANES), F32),
                            pltpu.VMEM((2, tt * TOP_K * ROW_CHUNKS, LANES), F32),
                            pltpu.SemaphoreType.DMA((2,)), pltpu.SemaphoreType.DMA],
        ),
        out_shape=jax.ShapeDtypeStruct((n_slots * ROW_CHUNKS, LANES), F32),
        compiler_params=_params("arbitrary"),
        name="moe_dispatch",
    )(zfill8, run_src8, run_dst8, run_len, pos3, h)


def _moe_mlp_kernel(be_ref, na_ref, x_ref, wgu_ref, bgu_ref, wd_ref, bd_ref, y_ref, wgu_bf, wd_bf, *, bm):
    i = pl.program_id(0)
    prev = be_ref[jnp.maximum(i - 1, 0)]
    active = i < na_ref[0]

    @pl.when(jnp.logical_and(active, jnp.logical_or(i == 0, be_ref[i] != prev)))
    def _():
        wgu_bf[...] = wgu_ref[...].astype(BF16)
        wd_bf[...] = wd_ref[...].astype(BF16)

    @pl.when(active)
    def _():
        x = _tiles_to_rows(x_ref, bm).astype(BF16)
        hgu = jnp.dot(x, wgu_bf[...], preferred_element_type=F32) + bgu_ref[...]
        gate = jnp.minimum(hgu[:, :D_FF], SWIGLU_LIMIT)
        up = jnp.clip(hgu[:, D_FF:], -SWIGLU_LIMIT, SWIGLU_LIMIT)
        act = gate * jax.nn.sigmoid(SWIGLU_ALPHA * gate) * (up + 1.0)
        y = jnp.dot(act.astype(BF16), wd_bf[...], preferred_element_type=F32) + bd_ref[...]
        _rows_to_tiles(y_ref, y, bm)


def moe_mlp(xs, block_expert, n_active, layer, w_gu, b_gu, w_d, b_d, bm):
    n_blocks = xs.shape[0] // (bm * ROW_CHUNKS)

    def row_map(i, be, na):
        return (jnp.minimum(i, na[0] - 1), 0)

    def w_map(i, be, na):
        return (layer, be[jnp.minimum(i, na[0] - 1)], 0, 0)

    def b_map(i, be, na):
        return (layer * N_EXPERTS + be[jnp.minimum(i, na[0] - 1)], 0, 0)

    return pl.pallas_call(
        functools.partial(_moe_mlp_kernel, bm=bm),
        grid_spec=pltpu.PrefetchScalarGridSpec(
            num_scalar_prefetch=2,
            grid=(n_blocks,),
            in_specs=[pl.BlockSpec((bm * ROW_CHUNKS, LANES), row_map),
                      pl.BlockSpec((None, None, D_MODEL, 2 * D_FF), w_map),
                      pl.BlockSpec((None, 1, 2 * D_FF), b_map),
                      pl.BlockSpec((None, None, D_FF, D_MODEL), w_map),
                      pl.BlockSpec((None, 1, D_MODEL), b_map)],
            out_specs=pl.BlockSpec((bm * ROW_CHUNKS, LANES), row_map),
            scratch_shapes=[pltpu.VMEM((D_MODEL, 2 * D_FF), BF16), pltpu.VMEM((D_FF, D_MODEL), BF16)],
        ),
        out_shape=jax.ShapeDtypeStruct(xs.shape, F32),
        compiler_params=_params("arbitrary"),
        name="moe_mlp",
    )(block_expert, n_active, xs, w_gu, b_gu.reshape(DEPTH * N_EXPERTS, 1, 2 * D_FF), w_d,
      b_d.reshape(DEPTH * N_EXPERTS, 1, D_MODEL))


def _ffn_ln_ple_kernel(rstage_ref, rslot_ref, rlen_ref, pos_ref, gate_ref, h_ref, p_ref, g_ref, b_ref,
                       wg_ref, wp_ref, ys_ref, o_ref, ob_ref, stage, ffn_tiles, sems, *, tm):
    i = pl.program_id(0)
    slot = i % 2

    def start_runs(step):
        buf = stage.at[step % 2]
        sem = sems.at[step % 2]

        def per_expert(e, carry):
            n = rlen_ref[step * N_EXPERTS + e]
            src = rslot_ref[step * N_EXPERTS + e]
            dst = rstage_ref[step * N_EXPERTS + e]
            done = jnp.int32(0)
            piece = tm
            while piece >= 1:
                take = (n & piece) != 0
                size = piece * ROW_CHUNKS

                @pl.when(take)
                def _(done=done, size=size):
                    s = pl.multiple_of(src + done, ROW_CHUNKS)
                    d = pl.multiple_of(dst + done, ROW_CHUNKS)
                    pltpu.make_async_copy(ys_ref.at[pl.ds(s, size)], buf.at[pl.ds(d, size)], sem).start()

                done = done + jnp.where(take, size, 0)
                piece //= 2
            return carry
        lax.fori_loop(0, N_EXPERTS, per_expert, 0)

    @pl.when(i == 0)
    def _():
        start_runs(0)

    @pl.when(i + 1 < pl.num_programs(0))
    def _():
        start_runs(i + 1)

    pltpu.make_async_copy(ys_ref.at[pl.ds(0, tm * TOP_K * ROW_CHUNKS)], stage.at[slot], sems.at[slot]).wait()

    def combine(t, carry):
        acc = None
        for k in range(TOP_K):
            q = pl.multiple_of(pos_ref[0, 0, k * tm + t], ROW_CHUNKS)
            term = stage[slot, pl.ds(q, ROW_CHUNKS), :] * gate_ref[0, 0, k * tm + t]
            acc = term if acc is None else acc + term
        ffn_tiles[pl.ds(pl.multiple_of(t * ROW_CHUNKS, ROW_CHUNKS), ROW_CHUNKS), :] = acc
        return carry

    lax.fori_loop(0, tm, combine, 0, unroll=16)

    sub = tm // FFN_ROW_CHUNKS
    for c in range(FFN_ROW_CHUNKS):
        rows = slice(c * sub, (c + 1) * sub)
        ffn = _tiles_to_rows(ffn_tiles.at[pl.ds(c * sub * ROW_CHUNKS, sub * ROW_CHUNKS)], sub)
        h2 = _layer_norm_rows(DN_ALPHA * h_ref[rows, :] + ffn, g_ref[...], b_ref[...])
        gate = jax.nn.sigmoid(jnp.dot(h2.astype(BF16), wg_ref[...], preferred_element_type=F32))
        proj = jnp.dot(p_ref[rows, :].astype(BF16), wp_ref[...], preferred_element_type=F32)
        out = h2 + gate * proj
        o_ref[rows, :] = out
        ob_ref[rows, :] = out.astype(BF16)


def ffn_ln_ple(ys, pos3, gates3, run_stage8, run_slot8, run_len, h, p_all, layer, ln_g, ln_b, wg_bf, wp_bf):
    T = h.shape[0]
    tm = _tile(T, TOKEN_ROWS)
    n = T // tm
    assert pos3.shape == gates3.shape == (n, 1, tm * TOP_K)
    row_spec = pl.BlockSpec((tm, D_MODEL), lambda i, *_: (i, 0))
    vec_spec = pl.BlockSpec((1, D_MODEL), lambda i, *_: (0, 0))
    smem_spec = pl.BlockSpec((1, 1, tm * TOP_K), lambda i, *_: (i, 0, 0), memory_space=pltpu.SMEM)
    return pl.pallas_call(
        functools.partial(_ffn_ln_ple_kernel, tm=tm),
        grid_spec=pltpu.PrefetchScalarGridSpec(
            num_scalar_prefetch=3,
            grid=(n,),
            in_specs=[smem_spec, smem_spec,
                      row_spec,
                      pl.BlockSpec((tm, PLE_DIM), lambda i, *_: (layer * n + i, 0)),
                      vec_spec, vec_spec,
                      pl.BlockSpec((D_MODEL, D_MODEL), lambda i, *_: (0, 0)),
                      pl.BlockSpec((PLE_DIM, D_MODEL), lambda i, *_: (0, 0)),
                      pl.BlockSpec(memory_space=pl.ANY)],
            out_specs=[row_spec, row_spec],
            scratch_shapes=[pltpu.VMEM((2, tm * TOP_K * ROW_CHUNKS, LANES), F32),
                            pltpu.VMEM((tm * ROW_CHUNKS, LANES), F32), pltpu.SemaphoreType.DMA((2,))],
        ),
        out_shape=[jax.ShapeDtypeStruct((T, D_MODEL), F32), jax.ShapeDtypeStruct((T, D_MODEL), BF16)],
        compiler_params=_params("arbitrary"),
        name="ffn_ln_ple",
    )(run_stage8, run_slot8, run_len, pos3, gates3, h, p_all, ln_g.reshape(1, D_MODEL),
      ln_b.reshape(1, D_MODEL), wg_bf, wp_bf, ys)


def moe_layer(h1, layer, w_router, b_router, w_gu, b_gu, w_d, b_d):
    T = h1.shape[0]
    bm = MOE_BLOCK
    idx, gates, rank, tile_cnt = router(h1, w_router, b_router)
    nt, _, tt = idx.shape
    counts = jnp.sum(tile_cnt, axis=0)
    padded = (counts + bm - 1) // bm * bm
    pend = jnp.cumsum(padded)
    pstart = pend - padded
    n_slots = T * TOP_K + N_EXPERTS * bm
    n_blocks = n_slots // bm
    block_start = jnp.arange(n_blocks, dtype=jnp.int32) * bm
    block_expert = jnp.minimum(jnp.sum(block_start[:, None] >= pend[None, :], axis=1),
                               N_EXPERTS - 1).astype(jnp.int32)
    n_active = (pend[-1:] // bm).astype(jnp.int32)
    zfill8 = jnp.where(padded > 0, (pend - bm) * ROW_CHUNKS, -1).astype(jnp.int32)
    tile_base = jnp.cumsum(tile_cnt, axis=0) - tile_cnt
    stage_off = jnp.cumsum(tile_cnt, axis=1) - tile_cnt
    onehot = idx[:, :, None, :] == jnp.arange(N_EXPERTS, dtype=jnp.int32)[:, None]

    def per_assignment(table):
        return jnp.sum(jnp.where(onehot, table[:, None, :, None], 0), axis=2)

    stage_pos8 = ((per_assignment(stage_off - tile_base) + rank) * ROW_CHUNKS).astype(jnp.int32)
    run_src8 = (stage_off * ROW_CHUNKS).reshape(-1).astype(jnp.int32)
    run_dst8 = ((pstart[None, :] + tile_base) * ROW_CHUNKS).reshape(-1).astype(jnp.int32)
    pos3 = stage_pos8.reshape(nt, 1, TOP_K * tt)
    run_len = tile_cnt.reshape(-1)
    xs = dispatch_rows(h1, pos3, run_src8, run_dst8, run_len, zfill8, n_slots, bm)
    ys = moe_mlp(xs, block_expert, n_active, layer, w_gu, b_gu, w_d, b_d, bm)
    return ys, pos3, gates.reshape(nt, 1, TOP_K * tt), run_src8, run_dst8, run_len


def _rms_rows(x, g):
    return x * lax.rsqrt(jnp.mean(x * x, axis=-1, keepdims=True) + RMS_EPS) * g


def _mla_kv_kernel(x_ref, wa_ref, g_ref, wkn_ref, wv_ref, cm_ref, sm_ref, k_ref, v_ref):
    a = jnp.dot(x_ref[...], wa_ref[...], preferred_element_type=F32)
    c_kv = _rms_rows(a[:, :KV_LORA], g_ref[...]).astype(BF16)
    kr = a[:, KV_LORA:KV_LORA + LANES] * cm_ref[...] + a[:, KV_LORA + LANES:] * sm_ref[...]
    kr = kr.astype(BF16)
    kn = jnp.dot(c_kv, wkn_ref[...], preferred_element_type=F32)
    v_ref[...] = jnp.dot(c_kv, wv_ref[...], preferred_element_type=F32).astype(BF16)
    for h in range(MLA_HEADS):
        k_ref[:, 2 * h * LANES:(2 * h + 1) * LANES] = kn[:, h * QK_NOPE:(h + 1) * QK_NOPE].astype(BF16)
        k_ref[:, (2 * h + 1) * LANES:(2 * h + 2) * LANES] = kr


def _pad_rope_cols(w, swap):
    K = w.shape[0]
    w = w.reshape(K, -1, QK_ROPE)
    if swap:
        w = jnp.concatenate([w[..., QK_ROPE // 2:], w[..., :QK_ROPE // 2]], axis=-1)
    w = jnp.concatenate([w, jnp.zeros_like(w)], axis=-1)
    return w.reshape(K, -1)


def mla_shared_kv(hb, w_kv_a, kv_norm_g, w_kv_b, cm, sm):
    T = hb.shape[0]
    tm = _tile(T, TOKEN_ROWS)
    w_r = w_kv_a[:, KV_LORA:]
    wa = jnp.concatenate([w_kv_a[:, :KV_LORA], _pad_rope_cols(w_r, False), _pad_rope_cols(w_r, True)],
                         axis=1).astype(BF16)
    wb = w_kv_b.reshape(KV_LORA, MLA_HEADS, QK_NOPE + V_HEAD)
    wkn = wb[:, :, :QK_NOPE].reshape(KV_LORA, MLA_HEADS * QK_NOPE).astype(BF16)
    wv = wb[:, :, QK_NOPE:].reshape(KV_LORA, MLA_HEADS * V_HEAD).astype(BF16)
    full = lambda shape: pl.BlockSpec(shape, lambda i: (0, 0))
    return pl.pallas_call(
        _mla_kv_kernel,
        grid=(T // tm,),
        in_specs=[pl.BlockSpec((tm, D_MODEL), lambda i: (i, 0)), full(wa.shape), full((1, KV_LORA)),
                  full(wkn.shape), full(wv.shape),
                  pl.BlockSpec((tm, LANES), lambda i: (i, 0)), pl.BlockSpec((tm, LANES), lambda i: (i, 0))],
        out_specs=[pl.BlockSpec((tm, MLA_HEADS * 2 * LANES), lambda i: (i, 0)),
                   pl.BlockSpec((tm, MLA_HEADS * V_HEAD), lambda i: (i, 0))],
        out_shape=[jax.ShapeDtypeStruct((T, MLA_HEADS * 2 * LANES), BF16),
                   jax.ShapeDtypeStruct((T, MLA_HEADS * V_HEAD), BF16)],
        compiler_params=_params("arbitrary"),
        name="mla_kv",
    )(hb, wa, kv_norm_g.reshape(1, KV_LORA), wkn, wv, cm, sm)


def _mla_q_kernel(x_ref, wa_ref, g_ref, wb_ref, cm_ref, sm_ref, q_ref):
    scale = (QK_NOPE + QK_ROPE) ** -0.5 * LOG2_E
    qa = jnp.dot(x_ref[...], wa_ref[...], preferred_element_type=F32)
    qc = _rms_rows(qa, g_ref[...]).astype(BF16)
    q = jnp.dot(qc, wb_ref[...], preferred_element_type=F32)
    cm = cm_ref[...] * scale
    sm = sm_ref[...] * scale
    n = MLA_HEADS * LANES
    for h in range(MLA_HEADS):
        lo = h * LANES
        q_ref[:, 2 * lo:2 * lo + LANES] = (q[:, lo:lo + LANES] * scale).astype(BF16)
        rope = q[:, n + lo:n + lo + LANES] * cm + q[:, 2 * n + lo:2 * n + lo + LANES] * sm
        q_ref[:, 2 * lo + LANES:2 * lo + 2 * LANES] = rope.astype(BF16)


def mla_q(hb, w_q_a, q_norm_g, w_q_b, cm, sm):
    T = hb.shape[0]
    tm = _tile(T, TOKEN_ROWS)
    wq = w_q_b.reshape(Q_LORA, MLA_HEADS, QK_NOPE + QK_ROPE)
    w_n = wq[:, :, :QK_NOPE].reshape(Q_LORA, MLA_HEADS * QK_NOPE)
    w_r = wq[:, :, QK_NOPE:].reshape(Q_LORA, MLA_HEADS * QK_ROPE)
    wb = jnp.concatenate([w_n, _pad_rope_cols(w_r, False), _pad_rope_cols(w_r, True)], axis=1).astype(BF16)
    full = lambda shape: pl.BlockSpec(shape, lambda i: (0, 0))
    return pl.pallas_call(
        _mla_q_kernel,
        grid=(T // tm,),
        in_specs=[pl.BlockSpec((tm, D_MODEL), lambda i: (i, 0)), full((D_MODEL, Q_LORA)), full((1, Q_LORA)),
                  full(wb.shape),
                  pl.BlockSpec((tm, LANES), lambda i: (i, 0)), pl.BlockSpec((tm, LANES), lambda i: (i, 0))],
        out_specs=pl.BlockSpec((tm, MLA_HEADS * 2 * LANES), lambda i: (i, 0)),
        out_shape=jax.ShapeDtypeStruct((T, MLA_HEADS * 2 * LANES), BF16),
        compiler_params=_params("arbitrary"),
        name="mla_q",
    )(hb, w_q_a.astype(BF16), q_norm_g.reshape(1, Q_LORA), wb, cm, sm)


def _flash_kernel(q_ref, k_ref, v_ref, o_ref, m_ref, l_ref, acc_ref, *, bq, bk):
    i = pl.program_id(2)
    q = q_ref[...]
    m_ref[...] = jnp.full_like(m_ref, -jnp.inf)
    l_ref[...] = jnp.zeros_like(l_ref)
    acc_ref[...] = jnp.zeros_like(acc_ref)

    def step(j, rows=slice(0, bq), triangular=False):
        n = rows.stop - rows.start
        start = pl.multiple_of(j * bk, bk)
        kj = k_ref[pl.ds(start, bk), :]
        vj = v_ref[pl.ds(start, bk), :]
        s = lax.dot_general(q[rows], kj, NT_DIMS, preferred_element_type=F32)
        if triangular:
            r = lax.broadcasted_iota(jnp.int32, (n, bk), 0)
            c = lax.broadcasted_iota(jnp.int32, (n, bk), 1)
            s = jnp.where(c <= r, s, -jnp.inf)
        m_prev = m_ref[rows, :]
        m_new = jnp.maximum(m_prev, jnp.max(s, axis=-1, keepdims=True))
        alpha = jnp.exp2(m_prev - m_new)
        p = jnp.exp2(s - pltpu.repeat(m_new, bk // LANES, axis=1))
        p_lanes = p[:, 0:LANES]
        for c0 in range(LANES, bk, LANES):
            p_lanes = p_lanes + p[:, c0:c0 + LANES]
        l_ref[rows, :] = alpha * l_ref[rows, :] + p_lanes
        acc_ref[rows, :] = alpha * acc_ref[rows, :] + jnp.dot(p.astype(BF16), vj, preferred_element_type=F32)
        m_ref[rows, :] = m_new

    def pair(j, carry):
        step(2 * j)
        step(2 * j + 1)
        return carry

    lax.fori_loop(0, i, pair, 0)
    step(2 * i, triangular=True)
    step(2 * i + 1, slice(bk, bq), triangular=True)
    l = jnp.sum(l_ref[...], axis=-1, keepdims=True)
    o_ref[...] = (acc_ref[...] / l).astype(o_ref.dtype)


def flash_attention(q, k, v, B, S):
    T = B * S
    bq = _tile(S, ATTN_BLOCK_Q)
    bk = bq // 2
    nq = S // bq
    return pl.pallas_call(
        functools.partial(_flash_kernel, bq=bq, bk=bk),
        grid=(B, MLA_HEADS, nq),
        in_specs=[pl.BlockSpec((bq, 2 * LANES), lambda b, h, i: (b * nq + i, h)),
                  pl.BlockSpec((S, 2 * LANES), lambda b, h, i: (b, h)),
                  pl.BlockSpec((S, V_HEAD), lambda b, h, i: (b, h))],
        out_specs=pl.BlockSpec((bq, V_HEAD), lambda b, h, i: (b * nq + i, h)),
        out_shape=jax.ShapeDtypeStruct((T, MLA_HEADS * V_HEAD), BF16),
        scratch_shapes=[pltpu.VMEM((bq, LANES), F32), pltpu.VMEM((bq, LANES), F32),
                        pltpu.VMEM((bq, V_HEAD), F32)],
        compiler_params=_params("arbitrary", "arbitrary", "arbitrary"),
        name="mla_flash",
    )(q, k, v)


def kernel(x, p, positions, ret_w_in, ret_gn_g, ret_gn_b, ret_w_out, mla_w_kv_a, mla_kv_norm_g, mla_w_kv_b,
           mla_w_q_a, mla_q_norm_g, mla_w_q_b, mla_w_o, ln_mix_g, ln_mix_b, ln_ffn_g, ln_ffn_b,
           moe_w_router, moe_b_router, moe_w_gate_up, moe_b_gate_up, moe_w_down, moe_b_down,
           ple_w_gate, ple_w_proj):
    B, S, D = x.shape
    T = B * S
    cos_r, sin_r, cos_m, sin_m = rope_tables(positions)
    h = x.reshape(T, D)
    hb = h.astype(BF16)
    kv = None
    for i in range(DEPTH):
        if i < N_A:
            qk, vg = ret_projections(hb, ret_w_in[i].astype(BF16), cos_r, sin_r)
            z = retention(qk, vg, ret_gn_g[i], ret_gn_b[i], B, S)
            w_out = ret_w_out[i].astype(BF16)
        else:
            if kv is None:
                kv = mla_shared_kv(hb, mla_w_kv_a, mla_kv_norm_g, mla_w_kv_b, cos_m, sin_m)
            j = i - N_A
            q = mla_q(hb, mla_w_q_a[j], mla_q_norm_g[j], mla_w_q_b[j], cos_m, sin_m)
            z = flash_attention(q, kv[0], kv[1], B, S)
            w_out = mla_w_o[j].astype(BF16)
        h1 = out_proj_ln(z, w_out, h, ln_mix_g[i], ln_mix_b[i])
        ys, *tables = moe_layer(h1, i, moe_w_router[i], moe_b_router[i], moe_w_gate_up, moe_b_gate_up,
                                moe_w_down, moe_b_down)
        h, hb = ffn_ln_ple(ys, *tables, h1, p.reshape(DEPTH * T, PLE_DIM), i, ln_ffn_g[i], ln_ffn_b[i],
                           ple_w_gate[i].astype(BF16), ple_w_proj[i].astype(BF16))
    return h.reshape(B, S, D)
```
